```python
import jax, jax.numpy as jnp
from jax import lax
import numpy as np

D_MODEL = 1024
BATCH = 4
SEQ = 4096
DEPTH = 2

EXPAND = 2
D_INNER = EXPAND * D_MODEL
BLOCK = 128
A_WIDTH = D_INNER // 2
A_GROUPS = 8
A_GROUP_DIM = A_WIDTH // A_GROUPS
B_HEADS = 8
B_HEAD_DIM = (D_INNER - A_WIDTH) // B_HEADS
B_WIDTH = B_HEADS * B_HEAD_DIM
C_WIDTH = D_INNER // 2
POOL_WINDOWS = (2, 4, 8, 16)
C_GROUPS = len(POOL_WINDOWS)
C_GROUP_DIM = C_WIDTH // C_GROUPS
D_HEADS = 8
D_HEAD_DIM = (D_INNER - C_WIDTH) // D_HEADS
D_WIDTH = D_HEADS * D_HEAD_DIM
ROPE_BASE = 10000.0
EPS = 1e-6

EVEN_SPLITS = (A_WIDTH, 2 * A_WIDTH, 2 * A_WIDTH + B_WIDTH, 2 * A_WIDTH + 2 * B_WIDTH, 2 * A_WIDTH + 3 * B_WIDTH)
ODD_SPLITS = (C_WIDTH, C_WIDTH + D_WIDTH, C_WIDTH + 2 * D_WIDTH, C_WIDTH + 3 * D_WIDTH)
EVEN_IN = EVEN_SPLITS[-1] + D_INNER
ODD_IN = ODD_SPLITS[-1] + D_INNER

kernel_name = "hybrid_gmlp_stickbreak_pool_retention_adaln"


def rms_norm(x, g):
    xf = x.astype(jnp.float32)
    y = xf * lax.rsqrt(jnp.mean(xf * xf, axis=-1, keepdims=True) + EPS)
    return (y * g.astype(jnp.float32)).astype(x.dtype)


def ada_modulation(c, w_mod, b_mod):
    m = jax.nn.silu(c) @ w_mod + b_mod
    shift, scale, gate = jnp.split(m, 3, axis=-1)
    return shift[:, None], scale[:, None], gate[:, None]


def to_heads(t, h, dh):
    b, s, _ = t.shape
    return t.reshape(b, s, h, dh).transpose(0, 2, 1, 3)


def from_heads(t):
    b, h, s, dh = t.shape
    return t.transpose(0, 2, 1, 3).reshape(b, s, h * dh)


def rotary(t, positions):
    half = t.shape[-1] // 2
    inv_freq = ROPE_BASE ** (-jnp.arange(half, dtype=jnp.float32) / half)
    ang = positions.astype(jnp.float32)[:, None, :, None] * inv_freq
    cos, sin = jnp.cos(ang), jnp.sin(ang)
    t1, t2 = t[..., :half].astype(jnp.float32), t[..., half:].astype(jnp.float32)
    out = jnp.concatenate([t1 * cos - t2 * sin, t1 * sin + t2 * cos], axis=-1)
    return out.astype(t.dtype)


def chunked_spatial_gating(u, v, v_norm_g, w_s, b_s):
    b, s, _ = u.shape
    nc = s // BLOCK
    vg = rms_norm(v.reshape(b, nc, BLOCK, A_GROUPS, A_GROUP_DIM), v_norm_g)
    causal = jnp.tril(jnp.ones((BLOCK, BLOCK), dtype=bool))
    w = jnp.where(causal[None], w_s, 0.0)
    mixed = jnp.einsum('gts,bnsgc->bntgc', w, vg) + b_s.T[:, :, None]
    return u * mixed.reshape(b, s, A_WIDTH)


def stick_breaking_attention(q, k, v):
    b, h, s, dh = q.shape
    nb = s // BLOCK
    scale = dh ** -0.5
    key_pos = jnp.arange(s)
    q_blocks = q.reshape(b, h, nb, BLOCK, dh).transpose(2, 0, 1, 3, 4)

    def one_block(args):
        q_blk, start = args
        z = jnp.einsum('bhtd,bhsd->bhts', q_blk, k).astype(jnp.float32) * scale
        q_pos = start + jnp.arange(BLOCK)
        before = key_pos[None, :] < q_pos[:, None]
        log_1m = jnp.where(before, jax.nn.log_sigmoid(-z), 0.0)
        between = lax.cumsum(log_1m, axis=3, reverse=True) - log_1m
        w = jnp.where(before, jnp.exp(jax.nn.log_sigmoid(z) + between), 0.0)
        return jnp.einsum('bhts,bhsd->bhtd', w.astype(v.dtype), v)

    starts = jnp.arange(nb, dtype=jnp.int32) * BLOCK
    out = lax.map(one_block, (q_blocks, starts))
    return out.transpose(1, 2, 0, 3, 4).reshape(b, h, s, dh)


def multiscale_pool(x, w_group, scale):
    b, s, _ = x.shape
    xg = x.reshape(b, s, C_GROUPS, C_GROUP_DIM)
    cs = jnp.cumsum(xg.astype(jnp.float32), axis=1)
    t = jnp.arange(s)
    pooled = []
    for gi, win in enumerate(POOL_WINDOWS):
        cg = cs[:, :, gi]
        lagged = jnp.pad(cg, ((0, 0), (win, 0), (0, 0)))[:, :s]
        count = jnp.minimum(t + 1, win).astype(jnp.float32)[None, :, None]
        pooled.append((cg - lagged) / count)
    pooled = (jnp.stack(pooled, axis=2) - xg.astype(jnp.float32)).astype(x.dtype)
    mixed = jnp.einsum('bsgc,gce->bsge', pooled, w_group)
    return mixed.reshape(b, s, C_WIDTH) * scale


def retention_chunkwise(q, k, v):
    b, h, s, dh = q.shape
    nc = s // BLOCK
    f32 = jnp.float32
    log_gamma = jnp.log1p(-jnp.exp2(-5.0 - jnp.arange(h, dtype=f32)))
    idx = jnp.arange(BLOCK, dtype=f32)
    diff = idx[:, None] - idx[None, :]
    intra_decay = jnp.where(diff >= 0, jnp.exp(log_gamma[:, None, None] * jnp.maximum(diff, 0.0)), 0.0)
    q_decay = jnp.exp(log_gamma[:, None] * (idx + 1.0))
    k_decay = jnp.exp(log_gamma[:, None] * (BLOCK - 1.0 - idx))
    chunk_decay = jnp.exp(log_gamma * BLOCK)
    qc = q.astype(f32).reshape(b, h, nc, BLOCK, dh)
    kc = k.astype(f32).reshape(b, h, nc, BLOCK, dh) * (dh ** -0.5)
    vc = v.astype(f32).reshape(b, h, nc, BLOCK, dh)
    scores = jnp.einsum('bhntd,bhnsd->bhnts', qc, kc) * intra_decay[None, :, None]
    intra = jnp.einsum('bhnts,bhnse->bhnte', scores, vc)
    kv = jnp.einsum('bhnsd,bhnse->nbhde', kc * k_decay[None, :, None, :, None], vc)

    def step(state, kv_n):
        return state * chunk_decay[None, :, None, None] + kv_n, state

    _, prev = lax.scan(step, jnp.zeros((b, h, dh, dh), f32), kv)
    cross = jnp.einsum('bhntd,nbhde->bhnte', qc * q_decay[None, :, None, :, None], prev)
    out = (intra + cross).reshape(b, h, s, dh)
    out = out * lax.rsqrt(jnp.mean(out * out, axis=-1, keepdims=True) + EPS)
    return out.astype(q.dtype)


def even_layer(x, c, norm_g, w_mod, b_mod, w_in, a_vnorm_g, a_ws, a_bs, b_qnorm_g, b_knorm_g, w_out):
    shift, scale, gate = ada_modulation(c, w_mod, b_mod)
    hdn = rms_norm(x, norm_g) * (1.0 + scale) + shift
    p = hdn @ w_in
    u, v, q, k, val, z = jnp.split(p, list(EVEN_SPLITS), axis=-1)
    a_out = chunked_spatial_gating(u, v, a_vnorm_g, a_ws, a_bs)
    qh = rms_norm(to_heads(q, B_HEADS, B_HEAD_DIM), b_qnorm_g)
    kh = rms_norm(to_heads(k, B_HEADS, B_HEAD_DIM), b_knorm_g)
    b_out = from_heads(stick_breaking_attention(qh, kh, to_heads(val, B_HEADS, B_HEAD_DIM)))
    y = jnp.concatenate([a_out, b_out], axis=-1) * jax.nn.silu(z)
    return x + gate * (y @ w_out)


def odd_layer(x, c, positions, norm_g, w_mod, b_mod, w_in, c_w, c_scale, d_qnorm_g, d_knorm_g, w_out):
    shift, scale, gate = ada_modulation(c, w_mod, b_mod)
    hdn = rms_norm(x, norm_g) * (1.0 + scale) + shift
    p = hdn @ w_in
    pc, q, k, val, z = jnp.split(p, list(ODD_SPLITS), axis=-1)
    c_out = multiscale_pool(pc, c_w, c_scale)
    qh = rotary(rms_norm(to_heads(q, D_HEADS, D_HEAD_DIM), d_qnorm_g), positions)
    kh = rotary(rms_norm(to_heads(k, D_HEADS, D_HEAD_DIM), d_knorm_g), positions)
    d_out = from_heads(retention_chunkwise(qh, kh, to_heads(val, D_HEADS, D_HEAD_DIM)))
    y = jnp.concatenate([c_out, d_out], axis=-1) * jax.nn.silu(z)
    return x + gate * (y @ w_out)


def setup_inputs(seed: int = 0) -> dict:
    key = jax.random.key(seed)
    ks = iter(jax.random.split(key, 32))
    f32 = jnp.float32
    ne = (DEPTH + 1) // 2
    no = DEPTH // 2

    def nrm(shape, s):
        return jax.random.normal(next(ks), shape, f32) * s

    x = nrm((BATCH, SEQ, D_MODEL), 1.0)
    c = nrm((BATCH, D_MODEL), 1.0)
    offsets = jax.random.randint(next(ks), (BATCH, 1), 0, 1024, dtype=jnp.int32)
    positions = jnp.arange(SEQ, dtype=jnp.int32)[None, :] + offsets
    return {
        "x": x, "c": c, "positions": positions,
        "even_norm_g": 1.0 + nrm((ne, D_MODEL), 0.02),
        "even_w_mod": nrm((ne, D_MODEL, 3 * D_MODEL), 0.5 * D_MODEL ** -0.5),
        "even_b_mod": nrm((ne, 3 * D_MODEL), 0.02),
        "even_w_in": nrm((ne, D_MODEL, EVEN_IN), D_MODEL ** -0.5),
        "even_a_vnorm_g": 1.0 + nrm((ne, A_GROUPS, A_GROUP_DIM), 0.02),
        "even_a_ws": nrm((ne, A_GROUPS, BLOCK, BLOCK), BLOCK ** -0.5),
        "even_a_bs": 1.0 + nrm((ne, A_GROUPS, BLOCK), 0.1),
        "even_b_qnorm_g": 1.0 + nrm((ne, B_HEAD_DIM), 0.02),
        "even_b_knorm_g": 1.0 + nrm((ne, B_HEAD_DIM), 0.02),
        "even_w_out": nrm((ne, D_INNER, D_MODEL), D_INNER ** -0.5),
        "odd_norm_g": 1.0 + nrm((no, D_MODEL), 0.02),
        "odd_w_mod": nrm((no, D_MODEL, 3 * D_MODEL), 0.5 * D_MODEL ** -0.5),
        "odd_b_mod": nrm((no, 3 * D_MODEL), 0.02),
        "odd_w_in": nrm((no, D_MODEL, ODD_IN), D_MODEL ** -0.5),
        "odd_c_w": nrm((no, C_GROUPS, C_GROUP_DIM, C_GROUP_DIM), C_GROUP_DIM ** -0.5),
        "odd_c_scale": 1.0 + nrm((no, C_WIDTH), 0.1),
        "odd_d_qnorm_g": 1.0 + nrm((no, D_HEAD_DIM), 0.02),
        "odd_d_knorm_g": 1.0 + nrm((no, D_HEAD_DIM), 0.02),
        "odd_w_out": nrm((no, D_INNER, D_MODEL), D_INNER ** -0.5),
    }


def reference(x, c, positions,
              even_norm_g, even_w_mod, even_b_mod, even_w_in, even_a_vnorm_g, even_a_ws, even_a_bs,
              even_b_qnorm_g, even_b_knorm_g, even_w_out,
              odd_norm_g, odd_w_mod, odd_b_mod, odd_w_in, odd_c_w, odd_c_scale,
              odd_d_qnorm_g, odd_d_knorm_g, odd_w_out):
    for layer in range(DEPTH):
        i = layer // 2
        if layer % 2 == 0:
            x = even_layer(x, c, even_norm_g[i], even_w_mod[i], even_b_mod[i], even_w_in[i],
                           even_a_vnorm_g[i], even_a_ws[i], even_a_bs[i],
                           even_b_qnorm_g[i], even_b_knorm_g[i], even_w_out[i])
        else:
            x = odd_layer(x, c, positions, odd_norm_g[i], odd_w_mod[i], odd_b_mod[i], odd_w_in[i],
                          odd_c_w[i], odd_c_scale[i], odd_d_qnorm_g[i], odd_d_knorm_g[i], odd_w_out[i])
    return x
```

```python
import functools

import jax
import jax.numpy as jnp
from jax import lax
from jax.experimental import pallas as pl
from jax.experimental.pallas import tpu as pltpu

F32 = jnp.float32
BF16 = jnp.bfloat16

EPS = 1e-6
ROPE_BASE = 10000.0
BLOCK = 128
HEAD_DIM = 128
N_HEADS = 8
POOL_WINDOWS = (2, 4, 8, 16)
POOL_GROUP_DIM = 256
POOL_HALO = 16

LANES = 128
MOD_ROWS = 8

ROW_TILE = 512
COL_TILE = 1024
RET_ROWS = 512

ATT_Q = 128
ATT_WIN = 384
ATT_HEADS_PER_STEP = 4
ATT_SKIP_BELOW = -104.0

VMEM_LIMIT = 56 * 1024 * 1024


def _silu(v):
    return v * jax.nn.sigmoid(v)


def _dot(a, b):
    return jnp.dot(a, b, preferred_element_type=F32)


def _dot_nt(a, b):
    return lax.dot_general(a, b, (((1,), (1,)), ((), ())), preferred_element_type=F32)


def _dot_tn(a, b):
    return lax.dot_general(a, b, (((0,), (0,)), ((), ())), preferred_element_type=F32)


def _group_rms(a, gain):
    ms = jnp.mean(a * a, axis=-1, keepdims=True)
    return a * lax.rsqrt(ms + EPS) * gain


def _mod_kernel(c_ref, w_ref, b_ref, o_ref):
    a = _silu(c_ref[...])
    o_ref[...] = _dot(a.astype(BF16), w_ref[...].astype(BF16)) + b_ref[...]


def _modulation(c, w_mod, b_mod):
    bsz, d = c.shape
    c8 = jnp.pad(c, ((0, MOD_ROWS - bsz), (0, 0)))
    m = pl.pallas_call(
        _mod_kernel,
        grid=(3,),
        in_specs=[pl.BlockSpec((MOD_ROWS, d), lambda j: (0, 0)),
                  pl.BlockSpec((d, d), lambda j: (0, j)),
                  pl.BlockSpec((1, d), lambda j: (0, j))],
        out_specs=pl.BlockSpec((MOD_ROWS, d), lambda j: (0, j)),
        out_shape=jax.ShapeDtypeStruct((MOD_ROWS, 3 * d), F32),
        name="modulation",
    )(c8, w_mod, b_mod.reshape(1, 3 * d))
    m = m[:bsz]
    return (m[:, None, :d], m[:, None, d:2 * d], m[:, None, 2 * d:])


def _inproj_kernel(*refs, modes, post_scale, use_rope):
    if use_rope:
        (x_ref, g_ref, sc_ref, sh_ref, w_ref, eg_ref, cos_ref, sin_ref,
         of_ref, ob_ref, h_scr) = refs
    else:
        x_ref, g_ref, sc_ref, sh_ref, w_ref, eg_ref, of_ref, ob_ref, h_scr = refs
    j = pl.program_id(1)

    @pl.when(j == 0)
    def _():
        x = x_ref[...]
        ms = jnp.mean(x * x, axis=-1, keepdims=True)
        y = x * lax.rsqrt(ms + EPS) * g_ref[...]
        h_scr[...] = (y * (1.0 + sc_ref[...]) + sh_ref[...]).astype(BF16)

    acc = _dot(h_scr[...], w_ref[...])
    n_groups = acc.shape[1] // HEAD_DIM

    def is_mode(name):
        hit = None
        for jj, m in enumerate(modes):
            if m == name:
                hit = (j == jj) if hit is None else (hit | (j == jj))
        return hit

    if "f32" in modes:
        @pl.when(is_mode("f32"))
        def _():
            of_ref[...] = acc

    if "bf16" in modes:
        @pl.when(is_mode("bf16"))
        def _():
            ob_ref[...] = acc.astype(BF16)

    for name in ("norm", "rope", "rope_scaled"):
        if name not in modes:
            continue

        @pl.when(is_mode(name))
        def _(name=name):
            for g in range(n_groups):
                cols = slice(g * HEAD_DIM, (g + 1) * HEAD_DIM)
                t = _group_rms(acc[:, cols], eg_ref[:, cols])
                if name != "norm":
                    t = t * cos_ref[...] + pltpu.roll(t, HEAD_DIM // 2, axis=1) * sin_ref[...]
                if name == "rope_scaled":
                    t = t * post_scale
                ob_ref[:, cols] = t.astype(BF16)


def _in_projection(x2, seq, norm_g, scale, shift, w_bf, ep_gain, modes, rope=None, post_scale=1.0):
    n, d = x2.shape
    tm, tn = ROW_TILE, COL_TILE
    n_f32 = sum(m == "f32" for m in modes)
    n_bf = len(modes) - n_f32
    f32_blocks = [max(sum(m == "f32" for m in modes[:jj + 1]) - 1, 0) for jj in range(len(modes))]
    bf_blocks = [max(sum(m != "f32" for m in modes[:jj + 1]) - 1, 0) for jj in range(len(modes))]

    def lookup(blocks, j):
        idx = jnp.int32(blocks[0])
        for jj in range(1, len(blocks)):
            if blocks[jj] != blocks[jj - 1]:
                idx = jnp.where(j >= jj, jnp.int32(blocks[jj]), idx)
        return idx

    use_rope = rope is not None
    in_specs = [
        pl.BlockSpec((tm, d), lambda i, j: (i, 0)),
        pl.BlockSpec((1, d), lambda i, j: (0, 0)),
        pl.BlockSpec((None, 1, d), lambda i, j: ((i * tm) // seq, 0, 0)),
        pl.BlockSpec((None, 1, d), lambda i, j: ((i * tm) // seq, 0, 0)),
        pl.BlockSpec((d, tn), lambda i, j: (0, j)),
        pl.BlockSpec((1, tn), lambda i, j: (0, j)),
    ]
    args = [x2, norm_g.reshape(1, d), scale, shift, w_bf, ep_gain]
    if use_rope:
        in_specs += [pl.BlockSpec((tm, HEAD_DIM), lambda i, j: (i, 0)),
                     pl.BlockSpec((tm, HEAD_DIM), lambda i, j: (i, 0))]
        args += list(rope)
    return pl.pallas_call(
        functools.partial(_inproj_kernel, modes=tuple(modes), post_scale=post_scale,
                          use_rope=use_rope),
        grid=(n // tm, len(modes)),
        in_specs=in_specs,
        out_specs=[pl.BlockSpec((tm, tn), lambda i, j: (i, lookup(f32_blocks, j))),
                   pl.BlockSpec((tm, tn), lambda i, j: (i, lookup(bf_blocks, j)))],
        out_shape=[jax.ShapeDtypeStruct((n, n_f32 * tn), F32),
                   jax.ShapeDtypeStruct((n, n_bf * tn), BF16)],
        scratch_shapes=[pltpu.VMEM((tm, d), BF16)],
        compiler_params=pltpu.CompilerParams(
            dimension_semantics=("parallel", "arbitrary"), vmem_limit_bytes=VMEM_LIMIT),
        name="in_projection",
    )(*args)


def _attn_kernel(q_ref, k_ref, v_ref, o_ref, uw_scr, u1_scr, *, n_qblocks, scale):
    hb = ATT_HEADS_PER_STEP
    tq, win = ATT_Q, ATT_WIN

    def suffix_operator(width):
        r = lax.broadcasted_iota(jnp.int32, (width, width + LANES), 0)
        c = lax.broadcasted_iota(jnp.int32, (width, width + LANES), 1)
        return jnp.where((r > c) | (c >= width), 1.0, 0.0).astype(BF16)

    uw = suffix_operator(win)
    uw_scr[0:win, :] = uw
    uw_scr[win:2 * win, :] = uw
    u1 = suffix_operator(tq)
    u1_scr[0:tq, :] = u1
    u1_scr[tq:2 * tq, :] = u1

    key_minus_query = (lax.broadcasted_iota(jnp.int32, (tq, win), 1)
                       - lax.broadcasted_iota(jnp.int32, (tq, win), 0))

    def log_terms(q, k):
        z = _dot_nt(q, k) * scale
        e = jnp.exp(-jnp.abs(z))
        lsm = jnp.minimum(-z, 0.0) - jnp.log(1.0 + e)
        return z + lsm, lsm

    def split_bf16(a):
        hi = a.astype(BF16)
        lo = (a - hi.astype(F32)).astype(BF16)
        return jnp.concatenate([hi, lo], axis=1)

    def qblock(i, carry):
        r0 = pl.multiple_of(i * tq, tq)
        first = jnp.maximum(i - (win // tq - 1), 0)
        ws = pl.multiple_of(first * tq, tq)
        before = key_minus_query < (r0 - ws)

        log_beta, lhs = [], []
        for h in range(hb):
            cols = slice(h * HEAD_DIM, (h + 1) * HEAD_DIM)
            lb, l1m = log_terms(q_ref[pl.ds(r0, tq), cols], k_ref[pl.ds(ws, win), cols])
            log_beta.append(lb)
            lhs.append(split_bf16(jnp.where(before, l1m, 0.0)))
        sums = _dot(jnp.concatenate(lhs, axis=0), uw_scr[...])

        accs, totals = [], []
        for h in range(hb):
            cols = slice(h * HEAD_DIM, (h + 1) * HEAD_DIM)
            blk = sums[h * tq:(h + 1) * tq]
            w = jnp.where(before, jnp.exp(log_beta[h] + blk[:, :win]), 0.0)
            accs.append(_dot(w.astype(BF16), v_ref[pl.ds(ws, win), cols]))
            totals.append(blk[:, win:])

        def worst(ts):
            m = ts[0]
            for t in ts[1:]:
                m = jnp.maximum(m, t)
            return jnp.max(m)

        def more(st):
            j, m, _, _ = st
            return (j >= 0) & (m > ATT_SKIP_BELOW)

        def older_tile(st):
            j, _, totals, accs = st
            k0 = pl.multiple_of(j * tq, tq)
            log_beta, lhs = [], []
            for h in range(hb):
                cols = slice(h * HEAD_DIM, (h + 1) * HEAD_DIM)
                lb, l1m = log_terms(q_ref[pl.ds(r0, tq), cols], k_ref[pl.ds(k0, tq), cols])
                log_beta.append(lb)
                lhs.append(split_bf16(l1m))
            sums = _dot(jnp.concatenate(lhs, axis=0), u1_scr[...])
            new_totals, new_accs = [], []
            for h in range(hb):
                cols = slice(h * HEAD_DIM, (h + 1) * HEAD_DIM)
                blk = sums[h * tq:(h + 1) * tq]
                w = jnp.exp(log_beta[h] + blk[:, :tq] + totals[h])
                new_accs.append(accs[h] + _dot(w.astype(BF16), v_ref[pl.ds(k0, tq), cols]))
                new_totals.append(totals[h] + blk[:, tq:])
            return (j - 1, worst(new_totals), tuple(new_totals), tuple(new_accs))

        _, _, _, accs = lax.while_loop(
            more, older_tile, (first - 1, worst(totals), tuple(totals), tuple(accs)))
        for h in range(hb):
            o_ref[pl.ds(r0, tq), h * HEAD_DIM:(h + 1) * HEAD_DIM] = accs[h]
        return carry

    lax.fori_loop(0, n_qblocks, qblock, 0)


def _stick_breaking(pb, batch, seq, q_col, k_col, v_col):
    n = pb.shape[0]
    width = ATT_HEADS_PER_STEP * HEAD_DIM
    n_hg = N_HEADS // ATT_HEADS_PER_STEP

    def spec(col):
        return pl.BlockSpec((seq, width), lambda b, hg: (b, col // width + hg))

    return pl.pallas_call(
        functools.partial(_attn_kernel, n_qblocks=seq // ATT_Q, scale=HEAD_DIM ** -0.5),
        grid=(batch, n_hg),
        in_specs=[spec(q_col), spec(k_col), spec(v_col)],
        out_specs=pl.BlockSpec((seq, width), lambda b, hg: (b, hg)),
        out_shape=jax.ShapeDtypeStruct((n, N_HEADS * HEAD_DIM), F32),
        scratch_shapes=[pltpu.VMEM((2 * ATT_WIN, ATT_WIN + LANES), BF16),
                        pltpu.VMEM((2 * ATT_Q, ATT_Q + LANES), BF16)],
        compiler_params=pltpu.CompilerParams(
            dimension_semantics=("parallel", "parallel"), vmem_limit_bytes=VMEM_LIMIT),
        name="stick_breaking",
    )(pb, pb, pb)


def _out_even_kernel(x_ref, gate_ref, u_ref, vg_ref, za_ref, zb_ref, bo_ref, ws_ref, bias_ref,
                     wout_ref, o_ref, y_scr):
    tm = x_ref.shape[0]
    a_width = u_ref.shape[1]
    row = lax.broadcasted_iota(jnp.int32, (BLOCK, BLOCK), 0)
    col = lax.broadcasted_iota(jnp.int32, (BLOCK, BLOCK), 1)
    for g in range(a_width // HEAD_DIM):
        cols = slice(g * HEAD_DIM, (g + 1) * HEAD_DIM)
        wg = jnp.where(col <= row, ws_ref[g], 0.0).astype(BF16)
        for c in range(tm // BLOCK):
            rows = slice(c * BLOCK, (c + 1) * BLOCK)
            mixed = _dot(wg, vg_ref[rows, cols]) + bias_ref[:, cols]
            y_scr[rows, cols] = (u_ref[rows, cols] * mixed * _silu(za_ref[rows, cols])).astype(BF16)
    y_scr[:, a_width:] = (bo_ref[...] * _silu(zb_ref[...])).astype(BF16)
    o_ref[...] = x_ref[...] + gate_ref[...] * _dot(y_scr[...], wout_ref[...])


def _out_even(x2, seq, gate, pf, pb, b_out, a_ws, bias_full, w_out_bf):
    n, d = x2.shape
    tm = ROW_TILE
    aw = b_out.shape[1]
    row = lambda c: pl.BlockSpec((tm, aw), lambda i: (i, c))
    return pl.pallas_call(
        _out_even_kernel,
        grid=(n // tm,),
        in_specs=[pl.BlockSpec((tm, d), lambda i: (i, 0)),
                  pl.BlockSpec((None, 1, d), lambda i: ((i * tm) // seq, 0, 0)),
                  row(0),
                  row(0),
                  row(1), row(2),
                  row(0),
                  pl.BlockSpec(a_ws.shape, lambda i: (0, 0, 0)),
                  pl.BlockSpec(bias_full.shape, lambda i: (0, 0)),
                  pl.BlockSpec(w_out_bf.shape, lambda i: (0, 0))],
        out_specs=pl.BlockSpec((tm, d), lambda i: (i, 0)),
        out_shape=jax.ShapeDtypeStruct((n, d), F32),
        scratch_shapes=[pltpu.VMEM((tm, 2 * aw), BF16)],
        compiler_params=pltpu.CompilerParams(
            dimension_semantics=("parallel",), vmem_limit_bytes=VMEM_LIMIT),
        name="out_even",
    )(x2, gate, pf, pb, pf, pf, b_out, a_ws, bias_full, w_out_bf)


def _rope_kernel(pos_ref, freq_ref, sign_ref, cos_ref, sin_ref):
    ang = pos_ref[...] * freq_ref[...]
    cos_ref[...] = jnp.cos(ang)
    sin_ref[...] = jnp.sin(ang) * sign_ref[...]


def _rope_tables(positions):
    n = positions.size
    half = HEAD_DIM // 2
    inv_freq = ROPE_BASE ** (-jnp.arange(half, dtype=F32) / half)
    freq = jnp.concatenate([inv_freq, inv_freq]).reshape(1, HEAD_DIM)
    sign = jnp.concatenate([-jnp.ones((half,), F32), jnp.ones((half,), F32)]).reshape(1, HEAD_DIM)
    pos = jnp.broadcast_to(positions.reshape(n, 1).astype(F32), (n, HEAD_DIM))
    tr = 2048
    spec = pl.BlockSpec((tr, HEAD_DIM), lambda i: (i, 0))
    vec = pl.BlockSpec((1, HEAD_DIM), lambda i: (0, 0))
    return pl.pallas_call(
        _rope_kernel,
        grid=(n // tr,),
        in_specs=[spec, vec, vec],
        out_specs=[spec, spec],
        out_shape=[jax.ShapeDtypeStruct((n, HEAD_DIM), F32)] * 2,
        name="rope_tables",
    )(pos, freq, sign)


def _retention_kernel(q_ref, k_ref, v_ref, idec_ref, qdec_ref, kdec_ref, cdec_ref, o_ref, st_scr):
    @pl.when(pl.program_id(1) == 0)
    def _():
        st_scr[...] = jnp.zeros_like(st_scr)

    rows_per_step = q_ref.shape[0]
    for h in range(N_HEADS):
        cols = slice(h * HEAD_DIM, (h + 1) * HEAD_DIM)
        for c in range(rows_per_step // BLOCK):
            rows = slice(c * BLOCK, (c + 1) * BLOCK)
            q, k, v = q_ref[rows, cols], k_ref[rows, cols], v_ref[rows, cols]
            state = st_scr[h]
            scores = _dot_nt(q, k) * idec_ref[h]
            out = _dot(scores.astype(BF16), v) + _dot(q, state.astype(BF16)) * qdec_ref[h]
            ms = jnp.mean(out * out, axis=-1, keepdims=True)
            o_ref[rows, cols] = out * lax.rsqrt(ms + EPS)
            k_decayed = (k.astype(F32) * kdec_ref[h]).astype(BF16)
            st_scr[h] = state * cdec_ref[h] + _dot_tn(k_decayed, v)


def _retention_tables():
    log_gamma = jnp.log1p(-jnp.exp2(-5.0 - jnp.arange(N_HEADS, dtype=F32)))
    idx = jnp.arange(BLOCK, dtype=F32)
    diff = idx[:, None] - idx[None, :]
    intra = jnp.where(diff >= 0, jnp.exp(log_gamma[:, None, None] * jnp.maximum(diff, 0.0)), 0.0)
    q_decay = jnp.exp(log_gamma[:, None] * (idx + 1.0))
    k_decay = jnp.exp(log_gamma[:, None] * (BLOCK - 1.0 - idx))
    chunk_decay = jnp.exp(log_gamma * BLOCK)
    shape = (N_HEADS, BLOCK, HEAD_DIM)
    return (intra,
            jnp.broadcast_to(q_decay[:, :, None], shape),
            jnp.broadcast_to(k_decay[:, :, None], shape),
            jnp.broadcast_to(chunk_decay[:, None, None], shape))


def _retention(pb, batch, seq):
    n = pb.shape[0]
    width = N_HEADS * HEAD_DIM
    steps = seq // RET_ROWS
    tables = _retention_tables()
    row = lambda c: pl.BlockSpec((RET_ROWS, width), lambda b, s: (b * steps + s, c))
    tab = pl.BlockSpec((N_HEADS, BLOCK, HEAD_DIM), lambda b, s: (0, 0, 0))
    return pl.pallas_call(
        _retention_kernel,
        grid=(batch, steps),
        in_specs=[row(0), row(1), row(2), tab, tab, tab, tab],
        out_specs=row(0),
        out_shape=jax.ShapeDtypeStruct((n, width), F32),
        scratch_shapes=[pltpu.VMEM((N_HEADS, HEAD_DIM, HEAD_DIM), F32)],
        compiler_params=pltpu.CompilerParams(
            dimension_semantics=("parallel", "arbitrary"), vmem_limit_bytes=VMEM_LIMIT),
        name="retention",
    )(pb, pb, pb, *tables)


def _out_odd_kernel(x_ref, gate_ref, pc_ref, halo_ref, za_ref, zb_ref, do_ref, cw_ref, cs_ref,
                    wout_ref, o_ref, ext_scr, y_scr, *, seq):
    tm = x_ref.shape[0]
    c_width = pc_ref.shape[1]
    seq_row0 = (pl.program_id(0) * tm) % seq
    ext_scr[0:POOL_HALO, :] = jnp.where(seq_row0 == 0, 0.0, halo_ref[...])
    ext_scr[POOL_HALO:, :] = pc_ref[...]
    t_seq = seq_row0 + lax.broadcasted_iota(jnp.int32, (tm, POOL_GROUP_DIM), 0)
    for gi, win in enumerate(POOL_WINDOWS):
        cols = slice(gi * POOL_GROUP_DIM, (gi + 1) * POOL_GROUP_DIM)
        cur = pc_ref[:, cols]
        wsum = cur
        for lag in range(1, win):
            wsum = wsum + ext_scr[POOL_HALO - lag:POOL_HALO - lag + tm, cols]
        count = jnp.minimum(t_seq + 1, win).astype(F32)
        pooled = wsum / count - cur
        mixed = _dot(pooled.astype(BF16), cw_ref[gi]) * cs_ref[:, cols]
        y_scr[:, cols] = (mixed * _silu(za_ref[:, cols])).astype(BF16)
    y_scr[:, c_width:] = (do_ref[...] * _silu(zb_ref[...])).astype(BF16)
    o_ref[...] = x_ref[...] + gate_ref[...] * _dot(y_scr[...], wout_ref[...])


def _out_odd(x2, seq, gate, pf, d_out, c_w_bf, c_scale, w_out_bf):
    n, d = x2.shape
    tm = ROW_TILE
    cw = d_out.shape[1]
    row = lambda c: pl.BlockSpec((tm, cw), lambda i: (i, c))
    halo_blocks = tm // POOL_HALO
    return pl.pallas_call(
        functools.partial(_out_odd_kernel, seq=seq),
        grid=(n // tm,),
        in_specs=[pl.BlockSpec((tm, d), lambda i: (i, 0)),
                  pl.BlockSpec((None, 1, d), lambda i: ((i * tm) // seq, 0, 0)),
                  row(0),
                  pl.BlockSpec((POOL_HALO, cw), lambda i: (jnp.maximum(i * halo_blocks - 1, 0), 0)),
                  row(1), row(2),
                  row(0),
                  pl.BlockSpec(c_w_bf.shape, lambda i: (0, 0, 0)),
                  pl.BlockSpec((1, cw), lambda i: (0, 0)),
                  pl.BlockSpec(w_out_bf.shape, lambda i: (0, 0))],
        out_specs=pl.BlockSpec((tm, d), lambda i: (i, 0)),
        out_shape=jax.ShapeDtypeStruct((n, d), F32),
        scratch_shapes=[pltpu.VMEM((POOL_HALO + tm, cw), F32),
                        pltpu.VMEM((tm, 2 * cw), BF16)],
        compiler_params=pltpu.CompilerParams(
            dimension_semantics=("parallel",), vmem_limit_bytes=VMEM_LIMIT),
        name="out_odd",
    )(x2, gate, pf, pf, pf, pf, d_out, c_w_bf, c_scale.reshape(1, cw), w_out_bf)


def _even_layer(x2, batch, seq, c, norm_g, w_mod, b_mod, w_in, a_vnorm_g, a_ws, a_bs,
                b_qnorm_g, b_knorm_g, w_out):
    d = x2.shape[1]
    shift, scale, gate = _modulation(c, w_mod, b_mod)
    ones = jnp.ones((d,), F32)
    ep_gain = jnp.concatenate([
        ones, a_vnorm_g.reshape(-1), jnp.tile(b_qnorm_g, N_HEADS), jnp.tile(b_knorm_g, N_HEADS),
        ones, ones, ones]).reshape(1, -1)
    modes = ("f32", "norm", "norm", "norm", "bf16", "f32", "f32")
    pf, pb = _in_projection(x2, seq, norm_g, scale, shift, w_in.astype(BF16), ep_gain, modes)
    b_out = _stick_breaking(pb, batch, seq, q_col=d, k_col=2 * d, v_col=3 * d)
    bias_full = jnp.repeat(a_bs.T, HEAD_DIM, axis=1)
    return _out_even(x2, seq, gate, pf, pb, b_out, a_ws, bias_full, w_out.astype(BF16))


def _odd_layer(x2, batch, seq, c, positions, norm_g, w_mod, b_mod, w_in, c_w, c_scale,
               d_qnorm_g, d_knorm_g, w_out):
    d = x2.shape[1]
    shift, scale, gate = _modulation(c, w_mod, b_mod)
    ones = jnp.ones((d,), F32)
    ep_gain = jnp.concatenate([
        ones, jnp.tile(d_qnorm_g, N_HEADS), jnp.tile(d_knorm_g, N_HEADS),
        ones, ones, ones]).reshape(1, -1)
    modes = ("f32", "rope", "rope_scaled", "bf16", "f32", "f32")
    pf, pb = _in_projection(x2, seq, norm_g, scale, shift, w_in.astype(BF16), ep_gain, modes,
                            rope=_rope_tables(positions), post_scale=HEAD_DIM ** -0.5)
    d_out = _retention(pb, batch, seq)
    return _out_odd(x2, seq, gate, pf, d_out, c_w.astype(BF16), c_scale, w_out.astype(BF16))


def kernel(x, c, positions, even_norm_g, even_w_mod, even_b_mod, even_w_in, even_a_vnorm_g, even_a_ws, even_a_bs, even_b_qnorm_g, even_b_knorm_g, even_w_out, odd_norm_g, odd_w_mod, odd_b_mod, odd_w_in, odd_c_w, odd_c_scale, odd_d_qnorm_g, odd_d_knorm_g, odd_w_out):
    batch, seq, d = x.shape
    depth = even_norm_g.shape[0] + odd_norm_g.shape[0]
    x2 = x.reshape(batch * seq, d)
    for layer in range(depth):
        i = layer // 2
        if layer % 2 == 0:
            x2 = _even_layer(x2, batch, seq, c, even_norm_g[i], even_w_mod[i], even_b_mod[i],
                             even_w_in[i], even_a_vnorm_g[i], even_a_ws[i], even_a_bs[i],
                             even_b_qnorm_g[i], even_b_knorm_g[i], even_w_out[i])
        else:
            x2 = _odd_layer(x2, batch, seq, c, positions, odd_norm_g[i], odd_w_mod[i], odd_b_mod[i],
                            odd_w_in[i], odd_c_w[i], odd_c_scale[i], odd_d_qnorm_g[i],
                            odd_d_knorm_g[i], odd_w_out[i])
    return x2.reshape(batch, seq, d)
```

```python
import functools

import jax
import jax.numpy as jnp
from jax import lax
from jax.experimental import pallas as pl
from jax.experimental.pallas import tpu as pltpu

F32 = jnp.float32
BF16 = jnp.bfloat16

EPS = 1e-6
ROPE_BASE = 10000.0
BLOCK = 128
HEAD_DIM = 128
N_HEADS = 8
POOL_WINDOWS = (2, 4, 8, 16)
POOL_GROUP_DIM = 256
POOL_HALO = 16

MOD_ROWS = 8

ROW_TILE = 512
COL_TILE = 1024
RET_ROWS = 512

ATT_Q = 64
ATT_WIN = 256
ATT_SUB = 4
ATT_HEADS_PER_STEP = 4
ATT_SKIP_BELOW = -104.0
ATT_NO_LIMIT = 1 << 30

VMEM_LIMIT = 56 * 1024 * 1024


def _silu(v):
    return v * jax.nn.sigmoid(v)


def _dot(a, b):
    return jnp.dot(a, b, preferred_element_type=F32)


def _dot_nt(a, b):
    return lax.dot_general(a, b, (((1,), (1,)), ((), ())), preferred_element_type=F32)


def _dot_tn(a, b):
    return lax.dot_general(a, b, (((0,), (0,)), ((), ())), preferred_element_type=F32)


def _group_rms(a, gain):
    ms = jnp.mean(a * a, axis=-1, keepdims=True)
    return a * lax.rsqrt(ms + EPS) * gain


def _mod_kernel(c_ref, w_ref, b_ref, o_ref):
    a = _silu(c_ref[...])
    o_ref[...] = _dot(a.astype(BF16), w_ref[...].astype(BF16)) + b_ref[...]


def _modulation(c, w_mod, b_mod):
    bsz, d = c.shape
    c8 = jnp.pad(c, ((0, MOD_ROWS - bsz), (0, 0)))
    m = pl.pallas_call(
        _mod_kernel,
        grid=(3,),
        in_specs=[pl.BlockSpec((MOD_ROWS, d), lambda j: (0, 0)),
                  pl.BlockSpec((d, d), lambda j: (0, j)),
                  pl.BlockSpec((1, d), lambda j: (0, j))],
        out_specs=pl.BlockSpec((MOD_ROWS, d), lambda j: (0, j)),
        out_shape=jax.ShapeDtypeStruct((MOD_ROWS, 3 * d), F32),
        name="modulation",
    )(c8, w_mod, b_mod.reshape(1, 3 * d))
    m = m[:bsz]
    return (m[:, None, :d], m[:, None, d:2 * d], m[:, None, 2 * d:])


def _inproj_kernel(*refs, modes, post_scale, use_rope):
    if use_rope:
        x_ref, g_ref, sc_ref, sh_ref, w_ref, eg_ref, cos_ref, sin_ref, o_ref, h_scr = refs
    else:
        x_ref, g_ref, sc_ref, sh_ref, w_ref, eg_ref, o_ref, h_scr = refs
    tn = COL_TILE
    x = x_ref[...]
    ms = jnp.mean(x * x, axis=-1, keepdims=True)
    y = x * lax.rsqrt(ms + EPS) * g_ref[...]
    h_scr[...] = (y * (1.0 + sc_ref[...]) + sh_ref[...]).astype(BF16)

    for jj, mode in enumerate(modes):
        acc = _dot(h_scr[...], w_ref[:, jj * tn:(jj + 1) * tn])
        if mode == "raw":
            o_ref[:, jj * tn:(jj + 1) * tn] = acc.astype(BF16)
            continue
        for g in range(tn // HEAD_DIM):
            cols = slice(jj * tn + g * HEAD_DIM, jj * tn + (g + 1) * HEAD_DIM)
            t = _group_rms(acc[:, g * HEAD_DIM:(g + 1) * HEAD_DIM], eg_ref[:, cols])
            if mode != "norm":
                t = t * cos_ref[...] + pltpu.roll(t, HEAD_DIM // 2, axis=1) * sin_ref[...]
            if mode == "rope_scaled":
                t = t * post_scale
            o_ref[:, cols] = t.astype(BF16)


def _in_projection(x2, seq, norm_g, scale, shift, w_bf, ep_gain, modes, rope=None, post_scale=1.0):
    n, d = x2.shape
    tm, tn = ROW_TILE, COL_TILE
    ncols = len(modes) * tn
    use_rope = rope is not None
    in_specs = [
        pl.BlockSpec((tm, d), lambda i: (i, 0)),
        pl.BlockSpec((1, d), lambda i: (0, 0)),
        pl.BlockSpec((None, 1, d), lambda i: ((i * tm) // seq, 0, 0)),
        pl.BlockSpec((None, 1, d), lambda i: ((i * tm) // seq, 0, 0)),
        pl.BlockSpec((d, ncols), lambda i: (0, 0), pipeline_mode=pl.Buffered(1)),
        pl.BlockSpec((1, ncols), lambda i: (0, 0)),
    ]
    args = [x2, norm_g.reshape(1, d), scale, shift, w_bf, ep_gain]
    if use_rope:
        in_specs += [pl.BlockSpec((tm, HEAD_DIM), lambda i: (i, 0)),
                     pl.BlockSpec((tm, HEAD_DIM), lambda i: (i, 0))]
        args += list(rope)
    return pl.pallas_call(
        functools.partial(_inproj_kernel, modes=tuple(modes), post_scale=post_scale,
                          use_rope=use_rope),
        grid=(n // tm,),
        in_specs=in_specs,
        out_specs=pl.BlockSpec((tm, ncols), lambda i: (i, 0)),
        out_shape=jax.ShapeDtypeStruct((n, ncols), BF16),
        scratch_shapes=[pltpu.VMEM((tm, d), BF16)],
        compiler_params=pltpu.CompilerParams(
            dimension_semantics=("parallel",), vmem_limit_bytes=VMEM_LIMIT),
        name="in_projection",
    )(*args)


def _attn_kernel(q_ref, k_ref, v_ref, o_ref, u_scr, *, seq, scale):
    hb, tq, win, nsub = ATT_HEADS_PER_STEP, ATT_Q, ATT_WIN, ATT_SUB

    u_scr[...] = jnp.where(lax.broadcasted_iota(jnp.int32, (win, win), 0)
                           > lax.broadcasted_iota(jnp.int32, (win, win), 1), 1.0, 0.0).astype(BF16)
    key_idx = lax.broadcasted_iota(jnp.int32, (tq, win), 1)
    row_idx = lax.broadcasted_iota(jnp.int32, (tq, win), 0)

    def head_cols(h):
        return slice(h * HEAD_DIM, (h + 1) * HEAD_DIM)

    def key_tiles(rows0, keys0, limits, totals):
        log_beta, terms, first_term, valid = [], [], [], []
        for a in range(nsub):
            ok = key_idx < (jnp.minimum(rows0[a] + row_idx, limits[a]) - keys0[a])
            valid.append(ok)
            for h in range(hb):
                z = _dot_nt(q_ref[pl.ds(rows0[a], tq), head_cols(h)],
                            k_ref[pl.ds(keys0[a], win), head_cols(h)]) * scale
                lsm = jnp.minimum(-z, 0.0) - jnp.log(1.0 + jnp.exp(-jnp.abs(z)))
                log_beta.append(z + lsm)
                counted = jnp.where(ok, lsm, 0.0)
                terms.append(counted.astype(BF16))
                first_term.append(counted[:, 0:1])
        newer = _dot(jnp.concatenate(terms, axis=0), u_scr[...])
        outs, tile_totals = [], []
        for a in range(nsub):
            for h in range(hb):
                c = a * hb + h
                within = newer[c * tq:(c + 1) * tq]
                expo = log_beta[c] + within
                if totals is not None:
                    expo = expo + totals[c]
                w = jnp.where(valid[a], jnp.exp(expo), 0.0)
                outs.append(_dot(w.astype(BF16), v_ref[pl.ds(keys0[a], win), head_cols(h)]))
                tile_totals.append(within[:, 0:1] + first_term[c])
        return outs, tile_totals

    def slowest(totals):
        m = totals[0]
        for t in totals[1:]:
            m = jnp.maximum(m, t)
        return jnp.max(m)

    def qblock(i, carry):
        r0 = i * (tq * nsub)
        rows0 = [pl.multiple_of(r0 + a * tq, tq) for a in range(nsub)]
        starts = [jnp.maximum(r - (win - tq), 0) for r in rows0]
        keys0 = [pl.multiple_of(s, tq) for s in starts]
        accs, totals = key_tiles(rows0, keys0, [ATT_NO_LIMIT] * nsub, None)

        def more(st):
            m, worst, _, _ = st
            return (starts[-1] - m * win > 0) & (worst > ATT_SKIP_BELOW)

        def older(st):
            m, _, totals, accs = st
            limits = [s - m * win for s in starts]
            keys0 = [pl.multiple_of(jnp.maximum(lim - win, 0), tq) for lim in limits]
            outs, tile_totals = key_tiles(rows0, keys0, limits, totals)
            totals = tuple(t + d for t, d in zip(totals, tile_totals))
            accs = tuple(acc + o for acc, o in zip(accs, outs))
            return (m + 1, slowest(totals), totals, accs)

        _, _, _, accs = lax.while_loop(
            more, older, (jnp.int32(0), slowest(totals), tuple(totals), tuple(accs)))
        for a in range(nsub):
            for h in range(hb):
                o_ref[pl.ds(rows0[a], tq), head_cols(h)] = accs[a * hb + h].astype(BF16)
        return carry

    lax.fori_loop(0, seq // (tq * nsub), qblock, 0)


def _stick_breaking(p, batch, seq, q_col, k_col, v_col):
    n = p.shape[0]
    width = ATT_HEADS_PER_STEP * HEAD_DIM
    n_hg = N_HEADS // ATT_HEADS_PER_STEP

    def spec(col):
        return pl.BlockSpec((seq, width), lambda b, hg: (b, col // width + hg))

    return pl.pallas_call(
        functools.partial(_attn_kernel, seq=seq, scale=HEAD_DIM ** -0.5),
        grid=(batch, n_hg),
        in_specs=[spec(q_col), spec(k_col), spec(v_col)],
        out_specs=pl.BlockSpec((seq, width), lambda b, hg: (b, hg)),
        out_shape=jax.ShapeDtypeStruct((n, N_HEADS * HEAD_DIM), BF16),
        scratch_shapes=[pltpu.VMEM((ATT_WIN, ATT_WIN), BF16)],
        compiler_params=pltpu.CompilerParams(
            dimension_semantics=("parallel", "parallel"), vmem_limit_bytes=VMEM_LIMIT),
        name="stick_breaking",
    )(p, p, p)


def _out_even_kernel(x_ref, gate_ref, u_ref, vg_ref, za_ref, zb_ref, bo_ref, ws_ref, bias_ref,
                     wout_ref, o_ref, y_scr):
    tm = x_ref.shape[0]
    a_width = u_ref.shape[1]
    row = lax.broadcasted_iota(jnp.int32, (BLOCK, BLOCK), 0)
    col = lax.broadcasted_iota(jnp.int32, (BLOCK, BLOCK), 1)
    for g in range(a_width // HEAD_DIM):
        cols = slice(g * HEAD_DIM, (g + 1) * HEAD_DIM)
        wg = jnp.where(col <= row, ws_ref[g], 0.0).astype(BF16)
        for c in range(tm // BLOCK):
            rows = slice(c * BLOCK, (c + 1) * BLOCK)
            mixed = _dot(wg, vg_ref[rows, cols]) + bias_ref[:, cols]
            gated = u_ref[rows, cols].astype(F32) * mixed * _silu(za_ref[rows, cols].astype(F32))
            y_scr[rows, cols] = gated.astype(BF16)
    y_scr[:, a_width:] = (bo_ref[...].astype(F32) * _silu(zb_ref[...].astype(F32))).astype(BF16)
    o_ref[...] = x_ref[...] + gate_ref[...] * _dot(y_scr[...], wout_ref[...])


def _out_even(x2, seq, gate, p, b_out, a_ws, bias_full, w_out_bf):
    n, d = x2.shape
    tm = ROW_TILE
    aw = b_out.shape[1]
    row = lambda c: pl.BlockSpec((tm, aw), lambda i: (i, c))
    return pl.pallas_call(
        _out_even_kernel,
        grid=(n // tm,),
        in_specs=[pl.BlockSpec((tm, d), lambda i: (i, 0)),
                  pl.BlockSpec((None, 1, d), lambda i: ((i * tm) // seq, 0, 0)),
                  row(0), row(1), row(5), row(6),
                  row(0),
                  pl.BlockSpec(a_ws.shape, lambda i: (0, 0, 0)),
                  pl.BlockSpec(bias_full.shape, lambda i: (0, 0)),
                  pl.BlockSpec(w_out_bf.shape, lambda i: (0, 0))],
        out_specs=pl.BlockSpec((tm, d), lambda i: (i, 0)),
        out_shape=jax.ShapeDtypeStruct((n, d), F32),
        scratch_shapes=[pltpu.VMEM((tm, 2 * aw), BF16)],
        compiler_params=pltpu.CompilerParams(
            dimension_semantics=("parallel",), vmem_limit_bytes=VMEM_LIMIT),
        name="out_even",
    )(x2, gate, p, p, p, p, b_out, a_ws, bias_full, w_out_bf)


def _rope_kernel(pos_ref, freq_ref, sign_ref, cos_ref, sin_ref):
    ang = pos_ref[...] * freq_ref[...]
    cos_ref[...] = jnp.cos(ang)
    sin_ref[...] = jnp.sin(ang) * sign_ref[...]


def _rope_tables(positions):
    n = positions.size
    half = HEAD_DIM // 2
    inv_freq = ROPE_BASE ** (-jnp.arange(half, dtype=F32) / half)
    freq = jnp.concatenate([inv_freq, inv_freq]).reshape(1, HEAD_DIM)
    sign = jnp.concatenate([-jnp.ones((half,), F32), jnp.ones((half,), F32)]).reshape(1, HEAD_DIM)
    pos = jnp.broadcast_to(positions.reshape(n, 1).astype(F32), (n, HEAD_DIM))
    tr = 2048
    spec = pl.BlockSpec((tr, HEAD_DIM), lambda i: (i, 0))
    vec = pl.BlockSpec((1, HEAD_DIM), lambda i: (0, 0))
    return pl.pallas_call(
        _rope_kernel,
        grid=(n // tr,),
        in_specs=[spec, vec, vec],
        out_specs=[spec, spec],
        out_shape=[jax.ShapeDtypeStruct((n, HEAD_DIM), F32)] * 2,
        name="rope_tables",
    )(pos, freq, sign)


def _retention_kernel(q_ref, k_ref, v_ref, idec_ref, qdec_ref, kdec_ref, cdec_ref, o_ref, st_scr):
    @pl.when(pl.program_id(1) == 0)
    def _():
        st_scr[...] = jnp.zeros_like(st_scr)

    rows_per_step = q_ref.shape[0]
    for h in range(N_HEADS):
        cols = slice(h * HEAD_DIM, (h + 1) * HEAD_DIM)
        for c in range(rows_per_step // BLOCK):
            rows = slice(c * BLOCK, (c + 1) * BLOCK)
            q, k, v = q_ref[rows, cols], k_ref[rows, cols], v_ref[rows, cols]
            state = st_scr[h]
            scores = _dot_nt(q, k) * idec_ref[h]
            out = _dot(scores.astype(BF16), v) + _dot(q, state.astype(BF16)) * qdec_ref[h]
            ms = jnp.mean(out * out, axis=-1, keepdims=True)
            o_ref[rows, cols] = (out * lax.rsqrt(ms + EPS)).astype(BF16)
            k_decayed = (k.astype(F32) * kdec_ref[h]).astype(BF16)
            st_scr[h] = state * cdec_ref[h] + _dot_tn(k_decayed, v)


def _retention_tables():
    log_gamma = jnp.log1p(-jnp.exp2(-5.0 - jnp.arange(N_HEADS, dtype=F32)))
    idx = jnp.arange(BLOCK, dtype=F32)
    diff = idx[:, None] - idx[None, :]
    intra = jnp.where(diff >= 0, jnp.exp(log_gamma[:, None, None] * jnp.maximum(diff, 0.0)), 0.0)
    q_decay = jnp.exp(log_gamma[:, None] * (idx + 1.0))
    k_decay = jnp.exp(log_gamma[:, None] * (BLOCK - 1.0 - idx))
    chunk_decay = jnp.exp(log_gamma * BLOCK)
    shape = (N_HEADS, BLOCK, HEAD_DIM)
    return (intra,
            jnp.broadcast_to(q_decay[:, :, None], shape),
            jnp.broadcast_to(k_decay[:, :, None], shape),
            jnp.broadcast_to(chunk_decay[:, None, None], shape))


def _retention(p, batch, seq, q_blk, k_blk, v_blk):
    n = p.shape[0]
    width = N_HEADS * HEAD_DIM
    steps = seq // RET_ROWS
    tables = _retention_tables()
    row = lambda c: pl.BlockSpec((RET_ROWS, width), lambda b, s: (b * steps + s, c))
    tab = pl.BlockSpec((N_HEADS, BLOCK, HEAD_DIM), lambda b, s: (0, 0, 0))
    return pl.pallas_call(
        _retention_kernel,
        grid=(batch, steps),
        in_specs=[row(q_blk), row(k_blk), row(v_blk), tab, tab, tab, tab],
        out_specs=row(0),
        out_shape=jax.ShapeDtypeStruct((n, width), BF16),
        scratch_shapes=[pltpu.VMEM((N_HEADS, HEAD_DIM, HEAD_DIM), F32)],
        compiler_params=pltpu.CompilerParams(
            dimension_semantics=("parallel", "arbitrary"), vmem_limit_bytes=VMEM_LIMIT),
        name="retention",
    )(p, p, p, *tables)


def _out_odd_kernel(x_ref, gate_ref, pc_ref, halo_ref, za_ref, zb_ref, do_ref, cw_ref, cs_ref,
                    wout_ref, o_ref, ext_scr, y_scr, *, seq):
    tm = x_ref.shape[0]
    c_width = pc_ref.shape[1]
    seq_row0 = (pl.program_id(0) * tm) % seq
    ext_scr[0:POOL_HALO, :] = jnp.where(seq_row0 == 0, 0.0, halo_ref[...].astype(F32))
    ext_scr[POOL_HALO:, :] = pc_ref[...].astype(F32)
    t_seq = seq_row0 + lax.broadcasted_iota(jnp.int32, (tm, POOL_GROUP_DIM), 0)
    for gi, win in enumerate(POOL_WINDOWS):
        cols = slice(gi * POOL_GROUP_DIM, (gi + 1) * POOL_GROUP_DIM)
        cur = ext_scr[POOL_HALO:, cols]
        wsum = cur
        for lag in range(1, win):
            wsum = wsum + ext_scr[POOL_HALO - lag:POOL_HALO - lag + tm, cols]
        count = jnp.minimum(t_seq + 1, win).astype(F32)
        pooled = wsum / count - cur
        mixed = _dot(pooled.astype(BF16), cw_ref[gi]) * cs_ref[:, cols]
        y_scr[:, cols] = (mixed * _silu(za_ref[:, cols].astype(F32))).astype(BF16)
    y_scr[:, c_width:] = (do_ref[...].astype(F32) * _silu(zb_ref[...].astype(F32))).astype(BF16)
    o_ref[...] = x_ref[...] + gate_ref[...] * _dot(y_scr[...], wout_ref[...])


def _out_odd(x2, seq, gate, p, d_out, c_w_bf, c_scale, w_out_bf):
    n, d = x2.shape
    tm = ROW_TILE
    cw = d_out.shape[1]
    row = lambda c: pl.BlockSpec((tm, cw), lambda i: (i, c))
    halo_blocks = tm // POOL_HALO
    return pl.pallas_call(
        functools.partial(_out_odd_kernel, seq=seq),
        grid=(n // tm,),
        in_specs=[pl.BlockSpec((tm, d), lambda i: (i, 0)),
                  pl.BlockSpec((None, 1, d), lambda i: ((i * tm) // seq, 0, 0)),
                  row(0),
                  pl.BlockSpec((POOL_HALO, cw), lambda i: (jnp.maximum(i * halo_blocks - 1, 0), 0)),
                  row(4), row(5),
                  row(0),
                  pl.BlockSpec(c_w_bf.shape, lambda i: (0, 0, 0)),
                  pl.BlockSpec((1, cw), lambda i: (0, 0)),
                  pl.BlockSpec(w_out_bf.shape, lambda i: (0, 0))],
        out_specs=pl.BlockSpec((tm, d), lambda i: (i, 0)),
        out_shape=jax.ShapeDtypeStruct((n, d), F32),
        scratch_shapes=[pltpu.VMEM((POOL_HALO + tm, cw), F32),
                        pltpu.VMEM((tm, 2 * cw), BF16)],
        compiler_params=pltpu.CompilerParams(
            dimension_semantics=("parallel",), vmem_limit_bytes=VMEM_LIMIT),
        name="out_odd",
    )(x2, gate, p, p, p, p, d_out, c_w_bf, c_scale.reshape(1, cw), w_out_bf)


def _even_layer(x2, batch, seq, c, norm_g, w_mod, b_mod, w_in, a_vnorm_g, a_ws, a_bs,
                b_qnorm_g, b_knorm_g, w_out):
    d = x2.shape[1]
    shift, scale, gate = _modulation(c, w_mod, b_mod)
    ones = jnp.ones((d,), F32)
    ep_gain = jnp.concatenate([
        ones, a_vnorm_g.reshape(-1), jnp.tile(b_qnorm_g, N_HEADS), jnp.tile(b_knorm_g, N_HEADS),
        ones, ones, ones]).reshape(1, -1)
    modes = ("raw", "norm", "norm", "norm", "raw", "raw", "raw")
    p = _in_projection(x2, seq, norm_g, scale, shift, w_in.astype(BF16), ep_gain, modes)
    b_out = _stick_breaking(p, batch, seq, q_col=2 * d, k_col=3 * d, v_col=4 * d)
    bias_full = jnp.repeat(a_bs.T, HEAD_DIM, axis=1)
    return _out_even(x2, seq, gate, p, b_out, a_ws, bias_full, w_out.astype(BF16))


def _odd_layer(x2, batch, seq, c, positions, norm_g, w_mod, b_mod, w_in, c_w, c_scale,
               d_qnorm_g, d_knorm_g, w_out):
    d = x2.shape[1]
    shift, scale, gate = _modulation(c, w_mod, b_mod)
    ones = jnp.ones((d,), F32)
    ep_gain = jnp.concatenate([
        ones, jnp.tile(d_qnorm_g, N_HEADS), jnp.tile(d_knorm_g, N_HEADS),
        ones, ones, ones]).reshape(1, -1)
    modes = ("raw", "rope", "rope_scaled", "raw", "raw", "raw")
    p = _in_projection(x2, seq, norm_g, scale, shift, w_in.astype(BF16), ep_gain, modes,
                       rope=_rope_tables(positions), post_scale=HEAD_DIM ** -0.5)
    d_out = _retention(p, batch, seq, q_blk=1, k_blk=2, v_blk=3)
    return _out_odd(x2, seq, gate, p, d_out, c_w.astype(BF16), c_scale, w_out.astype(BF16))


def kernel(x, c, positions, even_norm_g, even_w_mod, even_b_mod, even_w_in, even_a_vnorm_g, even_a_ws, even_a_bs, even_b_qnorm_g, even_b_knorm_g, even_w_out, odd_norm_g, odd_w_mod, odd_b_mod, odd_w_in, odd_c_w, odd_c_scale, odd_d_qnorm_g, odd_d_knorm_g, odd_w_out):
    batch, seq, d = x.shape
    depth = even_norm_g.shape[0] + odd_norm_g.shape[0]
    x2 = x.reshape(batch * seq, d)
    for layer in range(depth):
        i = layer // 2
        if layer % 2 == 0:
            x2 = _even_layer(x2, batch, seq, c, even_norm_g[i], even_w_mod[i], even_b_mod[i],
                             even_w_in[i], even_a_vnorm_g[i], even_a_ws[i], even_a_bs[i],
                             even_b_qnorm_g[i], even_b_knorm_g[i], even_w_out[i])
        else:
            x2 = _odd_layer(x2, batch, seq, c, positions, odd_norm_g[i], odd_w_mod[i], odd_b_mod[i],
                            odd_w_in[i], odd_c_w[i], odd_c_scale[i], odd_d_qnorm_g[i],
                            odd_d_knorm_g[i], odd_w_out[i])
    return x2.reshape(batch, seq, d)
```

```python
import functools

import jax
import jax.numpy as jnp
from jax import lax
from jax.experimental import pallas as pl
from jax.experimental.pallas import tpu as pltpu

F32 = jnp.float32
BF16 = jnp.bfloat16

EPS = 1e-6
ROPE_BASE = 10000.0
BLOCK = 128
HEAD_DIM = 128
N_HEADS = 8
POOL_WINDOWS = (2, 4, 8, 16)
POOL_GROUP_DIM = 256
POOL_HALO = 16

MOD_ROWS = 8

ROW_TILE = 512
COL_TILE = 1024
RET_ROWS = 512
RET_CHUNK = 128

ATT_Q = 64
ATT_WIN = 384
ATT_NEW = 256
ATT_EXCLUDED = -1e30
LANES = 128
ATT_SUB = 2
ATT_HEADS_PER_STEP = 4
ATT_SKIP_BELOW = -151.0
LOG2E = 1.4426950408889634
ATT_NO_LIMIT = 1 << 30

VMEM_LIMIT = 56 * 1024 * 1024


def _silu(v):
    return v * jax.nn.sigmoid(v)


def _dot(a, b):
    return jnp.dot(a, b, preferred_element_type=F32)


def _dot_nt(a, b):
    return lax.dot_general(a, b, (((1,), (1,)), ((), ())), preferred_element_type=F32)


def _dot_tn(a, b):
    return lax.dot_general(a, b, (((0,), (0,)), ((), ())), preferred_element_type=F32)


def _group_rms(a, gain):
    ms = jnp.mean(a * a, axis=-1, keepdims=True)
    return a * lax.rsqrt(ms + EPS) * gain


def _mod_kernel(c_ref, w_ref, b_ref, o_ref):
    a = _silu(c_ref[...])
    o_ref[...] = _dot(a.astype(BF16), w_ref[...].astype(BF16)) + b_ref[...]


def _modulation(c, w_mod, b_mod):
    bsz, d = c.shape
    c8 = jnp.pad(c, ((0, MOD_ROWS - bsz), (0, 0)))
    m = pl.pallas_call(
        _mod_kernel,
        grid=(3,),
        in_specs=[pl.BlockSpec((MOD_ROWS, d), lambda j: (0, 0)),
                  pl.BlockSpec((d, d), lambda j: (0, j)),
                  pl.BlockSpec((1, d), lambda j: (0, j))],
        out_specs=pl.BlockSpec((MOD_ROWS, d), lambda j: (0, j)),
        out_shape=jax.ShapeDtypeStruct((MOD_ROWS, 3 * d), F32),
        name="modulation",
    )(c8, w_mod, b_mod.reshape(1, 3 * d))
    m = m[:bsz]
    return (m[:, None, :d], m[:, None, d:2 * d], m[:, None, 2 * d:])


def _inproj_kernel(*refs, modes, post_scale, use_rope):
    if use_rope:
        x_ref, g_ref, sc_ref, sh_ref, w_ref, eg_ref, cos_ref, sin_ref, o_ref, h_scr = refs
    else:
        x_ref, g_ref, sc_ref, sh_ref, w_ref, eg_ref, o_ref, h_scr = refs
    tn = COL_TILE
    x = x_ref[...]
    ms = jnp.mean(x * x, axis=-1, keepdims=True)
    y = x * lax.rsqrt(ms + EPS) * g_ref[...]
    h_scr[...] = (y * (1.0 + sc_ref[...]) + sh_ref[...]).astype(BF16)

    for jj, mode in enumerate(modes):
        acc = _dot(h_scr[...], w_ref[:, jj * tn:(jj + 1) * tn])
        if mode == "raw":
            o_ref[:, jj * tn:(jj + 1) * tn] = acc.astype(BF16)
            continue
        for g in range(tn // HEAD_DIM):
            cols = slice(jj * tn + g * HEAD_DIM, jj * tn + (g + 1) * HEAD_DIM)
            t = _group_rms(acc[:, g * HEAD_DIM:(g + 1) * HEAD_DIM], eg_ref[:, cols])
            if mode != "norm":
                t = t * cos_ref[...] + pltpu.roll(t, HEAD_DIM // 2, axis=1) * sin_ref[...]
            if mode == "rope_scaled":
                t = t * post_scale
            o_ref[:, cols] = t.astype(BF16)


def _in_projection(x2, seq, norm_g, scale, shift, w_bf, ep_gain, modes, rope=None, post_scale=1.0):
    n, d = x2.shape
    tm, tn = ROW_TILE, COL_TILE
    ncols = len(modes) * tn
    use_rope = rope is not None
    in_specs = [
        pl.BlockSpec((tm, d), lambda i: (i, 0)),
        pl.BlockSpec((1, d), lambda i: (0, 0)),
        pl.BlockSpec((None, 1, d), lambda i: ((i * tm) // seq, 0, 0)),
        pl.BlockSpec((None, 1, d), lambda i: ((i * tm) // seq, 0, 0)),
        pl.BlockSpec((d, ncols), lambda i: (0, 0), pipeline_mode=pl.Buffered(1)),
        pl.BlockSpec((1, ncols), lambda i: (0, 0)),
    ]
    args = [x2, norm_g.reshape(1, d), scale, shift, w_bf, ep_gain]
    if use_rope:
        in_specs += [pl.BlockSpec((tm, HEAD_DIM), lambda i: (i, 0)),
                     pl.BlockSpec((tm, HEAD_DIM), lambda i: (i, 0))]
        args += list(rope)
    return pl.pallas_call(
        functools.partial(_inproj_kernel, modes=tuple(modes), post_scale=post_scale,
                          use_rope=use_rope),
        grid=(n // tm,),
        in_specs=in_specs,
        out_specs=pl.BlockSpec((tm, ncols), lambda i: (i, 0)),
        out_shape=jax.ShapeDtypeStruct((n, ncols), BF16),
        scratch_shapes=[pltpu.VMEM((tm, d), BF16)],
        compiler_params=pltpu.CompilerParams(
            dimension_semantics=("parallel",), vmem_limit_bytes=VMEM_LIMIT),
        name="in_projection",
    )(*args)


def _attn_kernel(q_ref, k_ref, v_ref, o_ref, u_scr, *, seq, scale):
    hb, tq, win, nsub = ATT_HEADS_PER_STEP, ATT_Q, ATT_WIN, ATT_SUB

    u_scr[...] = jnp.where(lax.broadcasted_iota(jnp.int32, (ATT_NEW, ATT_NEW), 0)
                           > lax.broadcasted_iota(jnp.int32, (ATT_NEW, ATT_NEW), 1),
                           1.0, 0.0).astype(BF16)
    old = win - ATT_NEW
    key_idx = lax.broadcasted_iota(jnp.int32, (tq, win), 1)
    row_idx = lax.broadcasted_iota(jnp.int32, (tq, win), 0)
    tail = win - LANES
    tail_ok = (lax.broadcasted_iota(jnp.int32, (tq, LANES), 1) + tail
               < lax.broadcasted_iota(jnp.int32, (tq, LANES), 0) + (win - tq))

    def head_cols(h):
        return slice(h * HEAD_DIM, (h + 1) * HEAD_DIM)

    def key_tiles(rows0, keys0, limits, totals):
        log_beta, terms = [], []
        for a in range(nsub):
            if limits is not None:
                ok = key_idx < (jnp.minimum(rows0[a] + row_idx, limits[a]) - keys0[a])
            for h in range(hb):
                z = _dot_nt(q_ref[pl.ds(rows0[a], tq), head_cols(h)],
                            k_ref[pl.ds(keys0[a], win), head_cols(h)]) * (scale * LOG2E)
                if limits is None:
                    z = jnp.concatenate(
                        [z[:, :tail], jnp.where(tail_ok, z[:, tail:], ATT_EXCLUDED)], axis=1)
                else:
                    z = jnp.where(ok, z, ATT_EXCLUDED)
                nz = -z
                lsm = (jnp.minimum(nz, 0.0)
                       - jnp.log(1.0 + jnp.exp2(jnp.minimum(z, nz))) * LOG2E)
                log_beta.append(z + lsm)
                terms.append(lsm)
        sums_new = _dot(jnp.concatenate([t[:, old:].astype(BF16) for t in terms], axis=0),
                        u_scr[...])
        sums_old = _dot(jnp.concatenate([t[:, :old].astype(BF16) for t in terms], axis=0),
                        u_scr[0:old, 0:old])
        outs, tile_totals = [], []
        for a in range(nsub):
            for h in range(hb):
                c = a * hb + h
                newer = sums_new[c * tq:(c + 1) * tq]
                older = sums_old[c * tq:(c + 1) * tq]
                total_new = newer[:, 0:1] + terms[c][:, old:old + 1]
                expo = log_beta[c] + jnp.concatenate([older + total_new, newer], axis=1)
                if totals is not None:
                    expo = expo + totals[c]
                w = jnp.exp2(expo).astype(BF16)
                outs.append(_dot(w, v_ref[pl.ds(keys0[a], win), head_cols(h)]))
                tile_totals.append(total_new + older[:, 0:1] + terms[c][:, 0:1])
        return outs, tile_totals

    def slowest(totals):
        m = totals[0]
        for t in totals[1:]:
            m = jnp.maximum(m, t)
        return jnp.max(m)

    def qblock(i, carry, *, near_start):
        r0 = i * (tq * nsub)
        rows0 = [pl.multiple_of(r0 + a * tq, tq) for a in range(nsub)]
        if near_start:
            starts = [jnp.maximum(r - (win - tq), 0) for r in rows0]
        else:
            starts = [r - (win - tq) for r in rows0]
        keys0 = [pl.multiple_of(s, tq) for s in starts]
        accs, totals = key_tiles(rows0, keys0, [ATT_NO_LIMIT] * nsub if near_start else None, None)

        def more(st):
            m, worst, _, _ = st
            return (starts[-1] - m * win > 0) & (worst > ATT_SKIP_BELOW)

        def older(st):
            m, _, totals, accs = st
            limits = [s - m * win for s in starts]
            keys0 = [pl.multiple_of(jnp.maximum(lim - win, 0), tq) for lim in limits]
            outs, tile_totals = key_tiles(rows0, keys0, limits, totals)
            totals = tuple(t + d for t, d in zip(totals, tile_totals))
            accs = tuple(acc + o for acc, o in zip(accs, outs))
            return (m + 1, slowest(totals), totals, accs)

        _, _, _, accs = lax.while_loop(
            more, older, (jnp.int32(0), slowest(totals), tuple(totals), tuple(accs)))
        for a in range(nsub):
            for h in range(hb):
                o_ref[pl.ds(rows0[a], tq), head_cols(h)] = accs[a * hb + h].astype(BF16)
        return carry

    n_blocks = seq // (tq * nsub)
    n_near = min(-(-(win - tq) // (tq * nsub)), n_blocks)
    lax.fori_loop(0, n_near, functools.partial(qblock, near_start=True), 0)
    lax.fori_loop(n_near, n_blocks, functools.partial(qblock, near_start=False), 0)


def _stick_breaking(p, batch, seq, q_col, k_col, v_col):
    n = p.shape[0]
    width = ATT_HEADS_PER_STEP * HEAD_DIM
    n_hg = N_HEADS // ATT_HEADS_PER_STEP

    def spec(col):
        return pl.BlockSpec((seq, width), lambda b, hg: (b, col // width + hg))

    return pl.pallas_call(
        functools.partial(_attn_kernel, seq=seq, scale=HEAD_DIM ** -0.5),
        grid=(batch, n_hg),
        in_specs=[spec(q_col), spec(k_col), spec(v_col)],
        out_specs=pl.BlockSpec((seq, width), lambda b, hg: (b, hg)),
        out_shape=jax.ShapeDtypeStruct((n, N_HEADS * HEAD_DIM), BF16),
        scratch_shapes=[pltpu.VMEM((ATT_NEW, ATT_NEW), BF16)],
        compiler_params=pltpu.CompilerParams(
            dimension_semantics=("parallel", "parallel"), vmem_limit_bytes=VMEM_LIMIT),
        name="stick_breaking",
    )(p, p, p)


def _out_even_kernel(x_ref, gate_ref, u_ref, vg_ref, za_ref, zb_ref, bo_ref, ws_ref, bias_ref,
                     wout_ref, o_ref, y_scr):
    tm = x_ref.shape[0]
    a_width = u_ref.shape[1]
    row = lax.broadcasted_iota(jnp.int32, (BLOCK, BLOCK), 0)
    col = lax.broadcasted_iota(jnp.int32, (BLOCK, BLOCK), 1)
    for g in range(a_width // HEAD_DIM):
        cols = slice(g * HEAD_DIM, (g + 1) * HEAD_DIM)
        wg = jnp.where(col <= row, ws_ref[g], 0.0).astype(BF16)
        for c in range(tm // BLOCK):
            rows = slice(c * BLOCK, (c + 1) * BLOCK)
            mixed = _dot(wg, vg_ref[rows, cols]) + bias_ref[:, cols]
            gated = u_ref[rows, cols].astype(F32) * mixed * _silu(za_ref[rows, cols].astype(F32))
            y_scr[rows, cols] = gated.astype(BF16)
    y_scr[:, a_width:] = (bo_ref[...].astype(F32) * _silu(zb_ref[...].astype(F32))).astype(BF16)
    o_ref[...] = x_ref[...] + gate_ref[...] * _dot(y_scr[...], wout_ref[...])


def _out_even(x2, seq, gate, p, b_out, a_ws, bias_full, w_out_bf):
    n, d = x2.shape
    tm = ROW_TILE
    aw = b_out.shape[1]
    row = lambda c: pl.BlockSpec((tm, aw), lambda i: (i, c))
    return pl.pallas_call(
        _out_even_kernel,
        grid=(n // tm,),
        in_specs=[pl.BlockSpec((tm, d), lambda i: (i, 0)),
                  pl.BlockSpec((None, 1, d), lambda i: ((i * tm) // seq, 0, 0)),
                  row(0), row(1), row(5), row(6),
                  row(0),
                  pl.BlockSpec(a_ws.shape, lambda i: (0, 0, 0)),
                  pl.BlockSpec(bias_full.shape, lambda i: (0, 0)),
                  pl.BlockSpec(w_out_bf.shape, lambda i: (0, 0))],
        out_specs=pl.BlockSpec((tm, d), lambda i: (i, 0)),
        out_shape=jax.ShapeDtypeStruct((n, d), F32),
        scratch_shapes=[pltpu.VMEM((tm, 2 * aw), BF16)],
        compiler_params=pltpu.CompilerParams(
            dimension_semantics=("parallel",), vmem_limit_bytes=VMEM_LIMIT),
        name="out_even",
    )(x2, gate, p, p, p, p, b_out, a_ws, bias_full, w_out_bf)


def _rope_kernel(pos_ref, freq_ref, sign_ref, cos_ref, sin_ref):
    ang = pos_ref[...] * freq_ref[...]
    cos_ref[...] = jnp.cos(ang)
    sin_ref[...] = jnp.sin(ang) * sign_ref[...]


def _rope_tables(positions):
    n = positions.size
    half = HEAD_DIM // 2
    inv_freq = ROPE_BASE ** (-jnp.arange(half, dtype=F32) / half)
    freq = jnp.concatenate([inv_freq, inv_freq]).reshape(1, HEAD_DIM)
    sign = jnp.concatenate([-jnp.ones((half,), F32), jnp.ones((half,), F32)]).reshape(1, HEAD_DIM)
    pos = jnp.broadcast_to(positions.reshape(n, 1).astype(F32), (n, HEAD_DIM))
    tr = 2048
    spec = pl.BlockSpec((tr, HEAD_DIM), lambda i: (i, 0))
    vec = pl.BlockSpec((1, HEAD_DIM), lambda i: (0, 0))
    return pl.pallas_call(
        _rope_kernel,
        grid=(n // tr,),
        in_specs=[spec, vec, vec],
        out_specs=[spec, spec],
        out_shape=[jax.ShapeDtypeStruct((n, HEAD_DIM), F32)] * 2,
        name="rope_tables",
    )(pos, freq, sign)


def _retention_kernel(q_ref, k_ref, v_ref, idec_ref, qdec_ref, kdec_ref, cdec_ref, o_ref, st_scr):
    @pl.when(pl.program_id(1) == 0)
    def _():
        st_scr[...] = jnp.zeros_like(st_scr)

    chunk = RET_CHUNK
    n_chunks = q_ref.shape[0] // chunk
    cols = [slice(h * HEAD_DIM, (h + 1) * HEAD_DIM) for h in range(N_HEADS)]
    rows = [slice(c * chunk, (c + 1) * chunk) for c in range(n_chunks)]

    kv = {}
    for c in range(n_chunks):
        for h in range(N_HEADS):
            k_decayed = (k_ref[rows[c], cols[h]].astype(F32) * kdec_ref[h]).astype(BF16)
            kv[h, c] = _dot_tn(k_decayed, v_ref[rows[c], cols[h]])
    state = {}
    for h in range(N_HEADS):
        s = st_scr[h]
        for c in range(n_chunks):
            state[h, c] = s.astype(BF16)
            s = s * cdec_ref[h] + kv[h, c]
        st_scr[h] = s
    for c in range(n_chunks):
        for h in range(N_HEADS):
            q = q_ref[rows[c], cols[h]]
            scores = _dot_nt(q, k_ref[rows[c], cols[h]]) * idec_ref[h]
            out = (_dot(scores.astype(BF16), v_ref[rows[c], cols[h]])
                   + _dot(q, state[h, c]) * qdec_ref[h])
            ms = jnp.mean(out * out, axis=-1, keepdims=True)
            o_ref[rows[c], cols[h]] = (out * lax.rsqrt(ms + EPS)).astype(BF16)


def _retention_tables():
    chunk = RET_CHUNK
    log_gamma = jnp.log1p(-jnp.exp2(-5.0 - jnp.arange(N_HEADS, dtype=F32)))
    idx = jnp.arange(chunk, dtype=F32)
    diff = idx[:, None] - idx[None, :]
    intra = jnp.where(diff >= 0, jnp.exp(log_gamma[:, None, None] * jnp.maximum(diff, 0.0)), 0.0)
    q_decay = jnp.exp(log_gamma[:, None] * (idx + 1.0))
    k_decay = jnp.exp(log_gamma[:, None] * (chunk - 1.0 - idx))
    chunk_decay = jnp.exp(log_gamma * chunk)
    return (intra,
            jnp.broadcast_to(q_decay[:, :, None], (N_HEADS, chunk, HEAD_DIM)),
            jnp.broadcast_to(k_decay[:, :, None], (N_HEADS, chunk, HEAD_DIM)),
            jnp.broadcast_to(chunk_decay[:, None, None], (N_HEADS, HEAD_DIM, HEAD_DIM)))


def _retention(p, batch, seq, q_blk, k_blk, v_blk):
    n = p.shape[0]
    width = N_HEADS * HEAD_DIM
    steps = seq // RET_ROWS
    tables = _retention_tables()
    row = lambda c: pl.BlockSpec((RET_ROWS, width), lambda b, s: (b * steps + s, c))
    tabs = [pl.BlockSpec(t.shape, lambda b, s: (0, 0, 0)) for t in tables]
    return pl.pallas_call(
        _retention_kernel,
        grid=(batch, steps),
        in_specs=[row(q_blk), row(k_blk), row(v_blk)] + tabs,
        out_specs=row(0),
        out_shape=jax.ShapeDtypeStruct((n, width), BF16),
        scratch_shapes=[pltpu.VMEM((N_HEADS, HEAD_DIM, HEAD_DIM), F32)],
        compiler_params=pltpu.CompilerParams(
            dimension_semantics=("parallel", "arbitrary"), vmem_limit_bytes=VMEM_LIMIT),
        name="retention",
    )(p, p, p, *tables)


def _out_odd_kernel(x_ref, gate_ref, pc_ref, halo_ref, za_ref, zb_ref, do_ref, cw_ref, cs_ref,
                    wout_ref, o_ref, ext_scr, y_scr, *, seq):
    tm = x_ref.shape[0]
    c_width = pc_ref.shape[1]
    seq_row0 = (pl.program_id(0) * tm) % seq
    ext_scr[0:POOL_HALO, :] = jnp.where(seq_row0 == 0, jnp.zeros_like(halo_ref), halo_ref[...])
    ext_scr[POOL_HALO:, :] = pc_ref[...]
    src_rows = BLOCK + POOL_HALO
    lag = (lax.broadcasted_iota(jnp.int32, (BLOCK, src_rows), 0) + POOL_HALO
           - lax.broadcasted_iota(jnp.int32, (BLOCK, src_rows), 1))
    t_seq = seq_row0 + lax.broadcasted_iota(jnp.int32, (BLOCK, POOL_GROUP_DIM), 0)
    for gi, win in enumerate(POOL_WINDOWS):
        cols = slice(gi * POOL_GROUP_DIM, (gi + 1) * POOL_GROUP_DIM)
        band = jnp.where((lag >= 0) & (lag < win), 1.0 / win, 0.0).astype(BF16)
        short = jnp.where(t_seq + 1 < win, win / (t_seq + 1).astype(F32), 1.0)
        pooled = []
        for c in range(tm // BLOCK):
            mean = _dot(band, ext_scr[c * BLOCK:c * BLOCK + src_rows, cols])
            if c == 0:
                mean = mean * short
            cur = pc_ref[c * BLOCK:(c + 1) * BLOCK, cols].astype(F32)
            pooled.append((mean - cur).astype(BF16))
        mixed = _dot(jnp.concatenate(pooled, axis=0), cw_ref[gi]) * cs_ref[:, cols]
        y_scr[:, cols] = (mixed * _silu(za_ref[:, cols].astype(F32))).astype(BF16)
    y_scr[:, c_width:] = (do_ref[...].astype(F32) * _silu(zb_ref[...].astype(F32))).astype(BF16)
    o_ref[...] = x_ref[...] + gate_ref[...] * _dot(y_scr[...], wout_ref[...])


def _out_odd(x2, seq, gate, p, d_out, c_w_bf, c_scale, w_out_bf):
    n, d = x2.shape
    tm = ROW_TILE
    cw = d_out.shape[1]
    row = lambda c: pl.BlockSpec((tm, cw), lambda i: (i, c))
    halo_blocks = tm // POOL_HALO
    return pl.pallas_call(
        functools.partial(_out_odd_kernel, seq=seq),
        grid=(n // tm,),
        in_specs=[pl.BlockSpec((tm, d), lambda i: (i, 0)),
                  pl.BlockSpec((None, 1, d), lambda i: ((i * tm) // seq, 0, 0)),
                  row(0),
                  pl.BlockSpec((POOL_HALO, cw), lambda i: (jnp.maximum(i * halo_blocks - 1, 0), 0)),
                  row(4), row(5),
                  row(0),
                  pl.BlockSpec(c_w_bf.shape, lambda i: (0, 0, 0)),
                  pl.BlockSpec((1, cw), lambda i: (0, 0)),
                  pl.BlockSpec(w_out_bf.shape, lambda i: (0, 0))],
        out_specs=pl.BlockSpec((tm, d), lambda i: (i, 0)),
        out_shape=jax.ShapeDtypeStruct((n, d), F32),
        scratch_shapes=[pltpu.VMEM((POOL_HALO + tm, cw), BF16),
                        pltpu.VMEM((tm, 2 * cw), BF16)],
        compiler_params=pltpu.CompilerParams(
            dimension_semantics=("parallel",), vmem_limit_bytes=VMEM_LIMIT),
        name="out_odd",
    )(x2, gate, p, p, p, p, d_out, c_w_bf, c_scale.reshape(1, cw), w_out_bf)


def _even_layer(x2, batch, seq, c, norm_g, w_mod, b_mod, w_in, a_vnorm_g, a_ws, a_bs,
                b_qnorm_g, b_knorm_g, w_out):
    d = x2.shape[1]
    shift, scale, gate = _modulation(c, w_mod, b_mod)
    ones = jnp.ones((d,), F32)
    ep_gain = jnp.concatenate([
        ones, a_vnorm_g.reshape(-1), jnp.tile(b_qnorm_g, N_HEADS), jnp.tile(b_knorm_g, N_HEADS),
        ones, ones, ones]).reshape(1, -1)
    modes = ("raw", "norm", "norm", "norm", "raw", "raw", "raw")
    p = _in_projection(x2, seq, norm_g, scale, shift, w_in.astype(BF16), ep_gain, modes)
    b_out = _stick_breaking(p, batch, seq, q_col=2 * d, k_col=3 * d, v_col=4 * d)
    bias_full = jnp.repeat(a_bs.T, HEAD_DIM, axis=1)
    return _out_even(x2, seq, gate, p, b_out, a_ws, bias_full, w_out.astype(BF16))


def _odd_layer(x2, batch, seq, c, positions, norm_g, w_mod, b_mod, w_in, c_w, c_scale,
               d_qnorm_g, d_knorm_g, w_out):
    d = x2.shape[1]
    shift, scale, gate = _modulation(c, w_mod, b_mod)
    ones = jnp.ones((d,), F32)
    ep_gain = jnp.concatenate([
        ones, jnp.tile(d_qnorm_g, N_HEADS), jnp.tile(d_knorm_g, N_HEADS),
        ones, ones, ones]).reshape(1, -1)
    modes = ("raw", "rope", "rope_scaled", "raw", "raw", "raw")
    p = _in_projection(x2, seq, norm_g, scale, shift, w_in.astype(BF16), ep_gain, modes,
                       rope=_rope_tables(positions), post_scale=HEAD_DIM ** -0.5)
    d_out = _retention(p, batch, seq, q_blk=1, k_blk=2, v_blk=3)
    return _out_odd(x2, seq, gate, p, d_out, c_w.astype(BF16), c_scale, w_out.astype(BF16))


def kernel(x, c, positions, even_norm_g, even_w_mod, even_b_mod, even_w_in, even_a_vnorm_g, even_a_ws, even_a_bs, even_b_qnorm_g, even_b_knorm_g, even_w_out, odd_norm_g, odd_w_mod, odd_b_mod, odd_w_in, odd_c_w, odd_c_scale, odd_d_qnorm_g, odd_d_knorm_g, odd_w_out):
    batch, seq, d = x.shape
    depth = even_norm_g.shape[0] + odd_norm_g.shape[0]
    x2 = x.reshape(batch * seq, d)
    for layer in range(depth):
        i = layer // 2
        if layer % 2 == 0:
            x2 = _even_layer(x2, batch, seq, c, even_norm_g[i], even_w_mod[i], even_b_mod[i],
                             even_w_in[i], even_a_vnorm_g[i], even_a_ws[i], even_a_bs[i],
                             even_b_qnorm_g[i], even_b_knorm_g[i], even_w_out[i])
        else:
            x2 = _odd_layer(x2, batch, seq, c, positions, odd_norm_g[i], odd_w_mod[i], odd_b_mod[i],
                            odd_w_in[i], odd_c_w[i], odd_c_scale[i], odd_d_qnorm_g[i],
                            odd_d_knorm_g[i], odd_w_out[i])
    return x2.reshape(batch, seq, d)
```

```python
import functools

import jax
import jax.numpy as jnp
from jax import lax
from jax.experimental import pallas as pl
from jax.experimental.pallas import tpu as pltpu

F32 = jnp.float32
BF16 = jnp.bfloat16

EPS = 1e-6
ROPE_BASE = 10000.0
BLOCK = 128
HEAD_DIM = 128
N_HEADS = 8
POOL_WINDOWS = (2, 4, 8, 16)
POOL_GROUP_DIM = 256
POOL_HALO = 16

MOD_ROWS = 8

ROW_TILE = 512
COL_TILE = 1024
RET_CHUNK = 128

ATT_Q = 64
ATT_WIN = 384
ATT_NEW = 256
ATT_EXCLUDED = -1e30
LANES = 128
ATT_SUB = 2
ATT_HEADS_PER_STEP = 4
ATT_SKIP_BELOW = -151.0
LOG2E = 1.4426950408889634
ATT_NO_LIMIT = 1 << 30

VMEM_LIMIT = 56 * 1024 * 1024


def _silu(v):
    return v * jax.nn.sigmoid(v)


def _dot(a, b):
    return jnp.dot(a, b, preferred_element_type=F32)


def _dot_nt(a, b):
    return lax.dot_general(a, b, (((1,), (1,)), ((), ())), preferred_element_type=F32)


def _dot_tn(a, b):
    return lax.dot_general(a, b, (((0,), (0,)), ((), ())), preferred_element_type=F32)


def _group_rms(a, gain):
    ms = jnp.mean(a * a, axis=-1, keepdims=True)
    return a * lax.rsqrt(ms + EPS) * gain


def _mod_kernel(c_ref, w_ref, b_ref, o_ref):
    a = _silu(c_ref[...])
    o_ref[...] = _dot(a.astype(BF16), w_ref[...].astype(BF16)) + b_ref[...]


def _modulation(c, w_mod, b_mod):
    bsz, d = c.shape
    c8 = jnp.pad(c, ((0, MOD_ROWS - bsz), (0, 0)))
    m = pl.pallas_call(
        _mod_kernel,
        grid=(3,),
        in_specs=[pl.BlockSpec((MOD_ROWS, d), lambda j: (0, 0)),
                  pl.BlockSpec((d, d), lambda j: (0, j)),
                  pl.BlockSpec((1, d), lambda j: (0, j))],
        out_specs=pl.BlockSpec((MOD_ROWS, d), lambda j: (0, j)),
        out_shape=jax.ShapeDtypeStruct((MOD_ROWS, 3 * d), F32),
        name="modulation",
    )(c8, w_mod, b_mod.reshape(1, 3 * d))
    m = m[:bsz]
    return (m[:, None, :d], m[:, None, d:2 * d], m[:, None, 2 * d:])


def _inproj_kernel(*refs, modes, post_scale, use_rope):
    if use_rope:
        x_ref, g_ref, sc_ref, sh_ref, w_ref, eg_ref, pos_ref, freq_ref, sign_ref, o_ref, h_scr = refs
        ang = pos_ref[...] * freq_ref[...]
        cos_t = jnp.cos(ang)
        sin_t = jnp.sin(ang) * sign_ref[...]
    else:
        x_ref, g_ref, sc_ref, sh_ref, w_ref, eg_ref, o_ref, h_scr = refs
    tn = COL_TILE
    x = x_ref[...]
    ms = jnp.mean(x * x, axis=-1, keepdims=True)
    y = x * lax.rsqrt(ms + EPS) * g_ref[...]
    h_scr[...] = (y * (1.0 + sc_ref[...]) + sh_ref[...]).astype(BF16)

    for jj, mode in enumerate(modes):
        acc = _dot(h_scr[...], w_ref[:, jj * tn:(jj + 1) * tn])
        if mode == "raw":
            o_ref[:, jj * tn:(jj + 1) * tn] = acc.astype(BF16)
            continue
        for g in range(tn // HEAD_DIM):
            cols = slice(jj * tn + g * HEAD_DIM, jj * tn + (g + 1) * HEAD_DIM)
            t = _group_rms(acc[:, g * HEAD_DIM:(g + 1) * HEAD_DIM], eg_ref[:, cols])
            if mode != "norm":
                t = t * cos_t + pltpu.roll(t, HEAD_DIM // 2, axis=1) * sin_t
            if mode == "rope_scaled":
                t = t * post_scale
            o_ref[:, cols] = t.astype(BF16)


def _in_projection(x2, seq, norm_g, scale, shift, w_bf, ep_gain, modes, rope=None, post_scale=1.0):
    n, d = x2.shape
    tm, tn = ROW_TILE, COL_TILE
    ncols = len(modes) * tn
    use_rope = rope is not None
    in_specs = [
        pl.BlockSpec((tm, d), lambda i: (i, 0)),
        pl.BlockSpec((1, d), lambda i: (0, 0)),
        pl.BlockSpec((None, 1, d), lambda i: ((i * tm) // seq, 0, 0)),
        pl.BlockSpec((None, 1, d), lambda i: ((i * tm) // seq, 0, 0)),
        pl.BlockSpec((d, ncols), lambda i: (0, 0), pipeline_mode=pl.Buffered(1)),
        pl.BlockSpec((1, ncols), lambda i: (0, 0)),
    ]
    args = [x2, norm_g.reshape(1, d), scale, shift, w_bf, ep_gain]
    if use_rope:
        vec = pl.BlockSpec((1, HEAD_DIM), lambda i: (0, 0))
        in_specs += [pl.BlockSpec((tm, 1), lambda i: (i, 0)), vec, vec]
        args += list(rope)
    return pl.pallas_call(
        functools.partial(_inproj_kernel, modes=tuple(modes), post_scale=post_scale,
                          use_rope=use_rope),
        grid=(n // tm,),
        in_specs=in_specs,
        out_specs=pl.BlockSpec((tm, ncols), lambda i: (i, 0)),
        out_shape=jax.ShapeDtypeStruct((n, ncols), BF16),
        scratch_shapes=[pltpu.VMEM((tm, d), BF16)],
        compiler_params=pltpu.CompilerParams(
            dimension_semantics=("parallel",), vmem_limit_bytes=VMEM_LIMIT),
        name="in_projection",
    )(*args)


def _attn_kernel(q_ref, k_ref, v_ref, o_ref, u_scr, *, seq, scale):
    hb, tq, win, nsub = ATT_HEADS_PER_STEP, ATT_Q, ATT_WIN, ATT_SUB
    chains = nsub * hb

    u_scr[...] = jnp.where(lax.broadcasted_iota(jnp.int32, (ATT_NEW, ATT_NEW), 0)
                           > lax.broadcasted_iota(jnp.int32, (ATT_NEW, ATT_NEW), 1),
                           1.0, 0.0).astype(BF16)
    old = win - ATT_NEW
    key_idx = lax.broadcasted_iota(jnp.int32, (tq, win), 1)
    row_idx = lax.broadcasted_iota(jnp.int32, (tq, win), 0)
    tail = win - LANES
    tail_ok = (lax.broadcasted_iota(jnp.int32, (tq, LANES), 1) + tail
               < lax.broadcasted_iota(jnp.int32, (tq, LANES), 0) + (win - tq))

    def head_cols(h):
        return slice(h * HEAD_DIM, (h + 1) * HEAD_DIM)

    def tile_logits(rows0, keys0):
        return [_dot_nt(q_ref[pl.ds(rows0[a], tq), head_cols(h)],
                        k_ref[pl.ds(keys0[a], win), head_cols(h)]) * (scale * LOG2E)
                for a in range(nsub) for h in range(hb)]

    def tile_weights(logits, rows0, keys0, limits, totals):
        log_beta, terms = [], []
        for a in range(nsub):
            if limits is not None:
                ok = key_idx < (jnp.minimum(rows0[a] + row_idx, limits[a]) - keys0[a])
            for h in range(hb):
                z = logits[a * hb + h]
                if limits is None:
                    z = jnp.concatenate(
                        [z[:, :tail], jnp.where(tail_ok, z[:, tail:], ATT_EXCLUDED)], axis=1)
                else:
                    z = jnp.where(ok, z, ATT_EXCLUDED)
                nz = -z
                lsm = (jnp.minimum(nz, 0.0)
                       - jnp.log(1.0 + jnp.exp2(jnp.minimum(z, nz))) * LOG2E)
                log_beta.append(z + lsm)
                terms.append(lsm)
        sums_new = _dot(jnp.concatenate([t[:, old:].astype(BF16) for t in terms], axis=0),
                        u_scr[...])
        sums_old = _dot(jnp.concatenate([t[:, :old].astype(BF16) for t in terms], axis=0),
                        u_scr[0:old, 0:old])
        weights, tile_totals = [], []
        for c in range(chains):
            newer = sums_new[c * tq:(c + 1) * tq]
            older = sums_old[c * tq:(c + 1) * tq]
            total_new = newer[:, 0:1] + terms[c][:, old:old + 1]
            expo = log_beta[c] + jnp.concatenate([older + total_new, newer], axis=1)
            if totals is not None:
                expo = expo + totals[c]
            weights.append(jnp.exp2(expo).astype(BF16))
            tile_totals.append(total_new + older[:, 0:1] + terms[c][:, 0:1])
        return weights, tile_totals

    def tile_values(weights, keys0):
        return [_dot(weights[a * hb + h], v_ref[pl.ds(keys0[a], win), head_cols(h)])
                for a in range(nsub) for h in range(hb)]

    def slowest(totals):
        m = totals[0]
        for t in totals[1:]:
            m = jnp.maximum(m, t)
        return jnp.max(m)

    def qblock(i, carry, *, near_start):
        r0 = i * (tq * nsub)
        rows0 = [pl.multiple_of(r0 + a * tq, tq) for a in range(nsub)]
        if near_start:
            starts = [jnp.maximum(r - (win - tq), 0) for r in rows0]
        else:
            starts = [r - (win - tq) for r in rows0]
        keys0 = [pl.multiple_of(s, tq) for s in starts]
        weights, totals = tile_weights(tile_logits(rows0, keys0), rows0, keys0,
                                       [ATT_NO_LIMIT] * nsub if near_start else None, None)
        accs = tile_values(weights, keys0)

        def more(st):
            m, worst, _, _ = st
            return (starts[-1] - m * win > 0) & (worst > ATT_SKIP_BELOW)

        def older(st):
            m, _, totals, accs = st
            limits = [s - m * win for s in starts]
            keys0 = [pl.multiple_of(jnp.maximum(lim - win, 0), tq) for lim in limits]
            weights, tile_totals = tile_weights(tile_logits(rows0, keys0), rows0, keys0,
                                                limits, totals)
            outs = tile_values(weights, keys0)
            totals = tuple(t + d for t, d in zip(totals, tile_totals))
            accs = tuple(acc + o for acc, o in zip(accs, outs))
            return (m + 1, slowest(totals), totals, accs)

        _, _, _, accs = lax.while_loop(
            more, older, (jnp.int32(0), slowest(totals), tuple(totals), tuple(accs)))
        for a in range(nsub):
            for h in range(hb):
                o_ref[pl.ds(rows0[a], tq), head_cols(h)] = accs[a * hb + h].astype(BF16)
        return carry

    n_blocks = seq // (tq * nsub)
    n_clipped = min(-(-(win - tq) // (tq * nsub)), n_blocks)
    lax.fori_loop(0, n_clipped, functools.partial(qblock, near_start=True), 0)
    lax.fori_loop(n_clipped, n_blocks, functools.partial(qblock, near_start=False), 0)


def _stick_breaking(p, batch, seq, q_col, k_col, v_col):
    n = p.shape[0]
    width = ATT_HEADS_PER_STEP * HEAD_DIM
    n_hg = N_HEADS // ATT_HEADS_PER_STEP

    def spec(col):
        return pl.BlockSpec((seq, width), lambda b, hg: (b, col // width + hg))

    return pl.pallas_call(
        functools.partial(_attn_kernel, seq=seq, scale=HEAD_DIM ** -0.5),
        grid=(batch, n_hg),
        in_specs=[spec(q_col), spec(k_col), spec(v_col)],
        out_specs=pl.BlockSpec((seq, width), lambda b, hg: (b, hg)),
        out_shape=jax.ShapeDtypeStruct((n, N_HEADS * HEAD_DIM), BF16),
        scratch_shapes=[pltpu.VMEM((ATT_NEW, ATT_NEW), BF16)],
        compiler_params=pltpu.CompilerParams(
            dimension_semantics=("parallel", "parallel"), vmem_limit_bytes=VMEM_LIMIT),
        name="stick_breaking",
    )(p, p, p)


def _out_even_kernel(x_ref, gate_ref, u_ref, vg_ref, za_ref, zb_ref, bo_ref, ws_ref, bias_ref,
                     wout_ref, o_ref, y_scr):
    tm = x_ref.shape[0]
    a_width = u_ref.shape[1]
    row = lax.broadcasted_iota(jnp.int32, (BLOCK, BLOCK), 0)
    col = lax.broadcasted_iota(jnp.int32, (BLOCK, BLOCK), 1)
    for g in range(a_width // HEAD_DIM):
        cols = slice(g * HEAD_DIM, (g + 1) * HEAD_DIM)
        wg = jnp.where(col <= row, ws_ref[g], 0.0).astype(BF16)
        for c in range(tm // BLOCK):
            rows = slice(c * BLOCK, (c + 1) * BLOCK)
            mixed = _dot(wg, vg_ref[rows, cols]) + bias_ref[:, cols]
            gated = u_ref[rows, cols].astype(F32) * mixed * _silu(za_ref[rows, cols].astype(F32))
            y_scr[rows, cols] = gated.astype(BF16)
    y_scr[:, a_width:] = (bo_ref[...].astype(F32) * _silu(zb_ref[...].astype(F32))).astype(BF16)
    o_ref[...] = x_ref[...] + gate_ref[...] * _dot(y_scr[...], wout_ref[...])


def _out_even(x2, seq, gate, p, b_out, a_ws, bias_full, w_out_bf):
    n, d = x2.shape
    tm = ROW_TILE
    aw = b_out.shape[1]
    row = lambda c: pl.BlockSpec((tm, aw), lambda i: (i, c))
    return pl.pallas_call(
        _out_even_kernel,
        grid=(n // tm,),
        in_specs=[pl.BlockSpec((tm, d), lambda i: (i, 0)),
                  pl.BlockSpec((None, 1, d), lambda i: ((i * tm) // seq, 0, 0)),
                  row(0), row(1), row(5), row(6),
                  row(0),
                  pl.BlockSpec(a_ws.shape, lambda i: (0, 0, 0)),
                  pl.BlockSpec(bias_full.shape, lambda i: (0, 0)),
                  pl.BlockSpec(w_out_bf.shape, lambda i: (0, 0))],
        out_specs=pl.BlockSpec((tm, d), lambda i: (i, 0)),
        out_shape=jax.ShapeDtypeStruct((n, d), F32),
        scratch_shapes=[pltpu.VMEM((tm, 2 * aw), BF16)],
        compiler_params=pltpu.CompilerParams(
            dimension_semantics=("parallel",), vmem_limit_bytes=VMEM_LIMIT),
        name="out_even",
    )(x2, gate, p, p, p, p, b_out, a_ws, bias_full, w_out_bf)


def _rope_operands(positions):
    half = HEAD_DIM // 2
    inv_freq = ROPE_BASE ** (-jnp.arange(half, dtype=F32) / half)
    freq = jnp.concatenate([inv_freq, inv_freq]).reshape(1, HEAD_DIM)
    sign = jnp.concatenate([-jnp.ones((half,), F32), jnp.ones((half,), F32)]).reshape(1, HEAD_DIM)
    return positions.reshape(-1, 1).astype(F32), freq, sign


def _retention_tables():
    chunk = RET_CHUNK
    log_gamma = jnp.log1p(-jnp.exp2(-5.0 - jnp.arange(N_HEADS, dtype=F32)))
    idx = jnp.arange(chunk, dtype=F32)
    diff = idx[:, None] - idx[None, :]
    intra = jnp.where(diff >= 0, jnp.exp(log_gamma[:, None, None] * jnp.maximum(diff, 0.0)), 0.0)
    q_decay = jnp.exp(log_gamma[:, None] * (idx + 1.0))
    k_decay = jnp.exp(log_gamma[:, None] * (chunk - 1.0 - idx))
    chunk_decay = jnp.exp(log_gamma * chunk)
    return (intra,
            jnp.broadcast_to(q_decay[:, :, None], (N_HEADS, chunk, HEAD_DIM)),
            jnp.broadcast_to(k_decay[:, :, None], (N_HEADS, chunk, HEAD_DIM)),
            jnp.broadcast_to(chunk_decay[:, None, None], (N_HEADS, HEAD_DIM, HEAD_DIM)))


def _retention_rows(q_ref, k_ref, v_ref, idec_ref, qdec_ref, kdec_ref, cdec_ref, st_scr, emit):
    chunk = RET_CHUNK
    n_chunks = q_ref.shape[0] // chunk
    cols = [slice(h * HEAD_DIM, (h + 1) * HEAD_DIM) for h in range(N_HEADS)]
    rows = [slice(c * chunk, (c + 1) * chunk) for c in range(n_chunks)]

    kv = {}
    for c in range(n_chunks):
        for h in range(N_HEADS):
            k_decayed = (k_ref[rows[c], cols[h]].astype(F32) * kdec_ref[h]).astype(BF16)
            kv[h, c] = _dot_tn(k_decayed, v_ref[rows[c], cols[h]])
    state = {}
    for h in range(N_HEADS):
        s = st_scr[h]
        for c in range(n_chunks):
            state[h, c] = s.astype(BF16)
            s = s * cdec_ref[h] + kv[h, c]
        st_scr[h] = s
    for c in range(n_chunks):
        for h in range(N_HEADS):
            q = q_ref[rows[c], cols[h]]
            scores = _dot_nt(q, k_ref[rows[c], cols[h]]) * idec_ref[h]
            out = (_dot(scores.astype(BF16), v_ref[rows[c], cols[h]])
                   + _dot(q, state[h, c]) * qdec_ref[h])
            ms = jnp.mean(out * out, axis=-1, keepdims=True)
            emit(rows[c], h, out * lax.rsqrt(ms + EPS))


def _pooled_rows(pc_ref, halo_ref, cw_ref, cs_ref, ext_scr, seq_row0, emit):
    tm = pc_ref.shape[0]
    ext_scr[0:POOL_HALO, :] = jnp.where(seq_row0 == 0, jnp.zeros_like(halo_ref), halo_ref[...])
    ext_scr[POOL_HALO:, :] = pc_ref[...]
    src_rows = BLOCK + POOL_HALO
    lag = (lax.broadcasted_iota(jnp.int32, (BLOCK, src_rows), 0) + POOL_HALO
           - lax.broadcasted_iota(jnp.int32, (BLOCK, src_rows), 1))
    t_seq = seq_row0 + lax.broadcasted_iota(jnp.int32, (BLOCK, POOL_GROUP_DIM), 0)
    for gi, win in enumerate(POOL_WINDOWS):
        cols = slice(gi * POOL_GROUP_DIM, (gi + 1) * POOL_GROUP_DIM)
        band = jnp.where((lag >= 0) & (lag < win), 1.0 / win, 0.0).astype(BF16)
        short = jnp.where(t_seq + 1 < win, win / (t_seq + 1).astype(F32), 1.0)
        pooled = []
        for c in range(tm // BLOCK):
            mean = _dot(band, ext_scr[c * BLOCK:c * BLOCK + src_rows, cols])
            if c == 0:
                mean = mean * short
            cur = pc_ref[c * BLOCK:(c + 1) * BLOCK, cols].astype(F32)
            pooled.append((mean - cur).astype(BF16))
        emit(gi, _dot(jnp.concatenate(pooled, axis=0), cw_ref[gi]) * cs_ref[:, cols])


def _odd_mixers_kernel(q_ref, k_ref, v_ref, idec_ref, qdec_ref, kdec_ref, cdec_ref,
                       x_ref, gate_ref, pc_ref, halo_ref, za_ref, zb_ref, cw_ref, cs_ref, wout_ref,
                       o_ref, st_scr, ext_scr, y_scr):
    tm = x_ref.shape[0]
    c_width = pc_ref.shape[1]

    @pl.when(pl.program_id(1) == 0)
    def _():
        st_scr[...] = jnp.zeros_like(st_scr)

    def gated_retention(rows, h, out):
        cols = slice(h * HEAD_DIM, (h + 1) * HEAD_DIM)
        ycols = slice(c_width + h * HEAD_DIM, c_width + (h + 1) * HEAD_DIM)
        y_scr[rows, ycols] = (out * _silu(zb_ref[rows, cols].astype(F32))).astype(BF16)

    def gated_pool(gi, mixed):
        cols = slice(gi * POOL_GROUP_DIM, (gi + 1) * POOL_GROUP_DIM)
        y_scr[:, cols] = (mixed * _silu(za_ref[:, cols].astype(F32))).astype(BF16)

    _retention_rows(q_ref, k_ref, v_ref, idec_ref, qdec_ref, kdec_ref, cdec_ref, st_scr,
                    gated_retention)
    _pooled_rows(pc_ref, halo_ref, cw_ref, cs_ref, ext_scr, pl.program_id(1) * tm, gated_pool)
    o_ref[...] = x_ref[...] + gate_ref[...] * _dot(y_scr[...], wout_ref[...])


def _odd_mixers(x2, batch, seq, gate, p, c_w_bf, c_scale, w_out_bf):
    n, d = x2.shape
    tm = ROW_TILE
    cw = N_HEADS * HEAD_DIM
    steps = seq // tm
    tables = _retention_tables()
    row = lambda c: pl.BlockSpec((tm, cw), lambda b, s: (b * steps + s, c))
    tabs = [pl.BlockSpec(t.shape, lambda b, s: (0, 0, 0)) for t in tables]
    halo_blocks = tm // POOL_HALO
    halo = pl.BlockSpec(
        (POOL_HALO, cw), lambda b, s: (jnp.maximum((b * steps + s) * halo_blocks - 1, 0), 0))
    return pl.pallas_call(
        _odd_mixers_kernel,
        grid=(batch, steps),
        in_specs=[row(1), row(2), row(3)] + tabs + [
            pl.BlockSpec((tm, d), lambda b, s: (b * steps + s, 0)),
            pl.BlockSpec((None, 1, d), lambda b, s: (b, 0, 0)),
            row(0), halo, row(4), row(5),
            pl.BlockSpec(c_w_bf.shape, lambda b, s: (0, 0, 0)),
            pl.BlockSpec((1, cw), lambda b, s: (0, 0)),
            pl.BlockSpec(w_out_bf.shape, lambda b, s: (0, 0))],
        out_specs=pl.BlockSpec((tm, d), lambda b, s: (b * steps + s, 0)),
        out_shape=jax.ShapeDtypeStruct((n, d), F32),
        scratch_shapes=[pltpu.VMEM((N_HEADS, HEAD_DIM, HEAD_DIM), F32),
                        pltpu.VMEM((POOL_HALO + tm, cw), BF16),
                        pltpu.VMEM((tm, 2 * cw), BF16)],
        compiler_params=pltpu.CompilerParams(
            dimension_semantics=("parallel", "arbitrary"), vmem_limit_bytes=VMEM_LIMIT),
        name="odd_mixers",
    )(p, p, p, *tables, x2, gate, p, p, p, p, c_w_bf, c_scale.reshape(1, cw), w_out_bf)


def _even_layer(x2, batch, seq, c, norm_g, w_mod, b_mod, w_in, a_vnorm_g, a_ws, a_bs,
                b_qnorm_g, b_knorm_g, w_out):
    d = x2.shape[1]
    shift, scale, gate = _modulation(c, w_mod, b_mod)
    ones = jnp.ones((d,), F32)
    ep_gain = jnp.concatenate([
        ones, a_vnorm_g.reshape(-1), jnp.tile(b_qnorm_g, N_HEADS), jnp.tile(b_knorm_g, N_HEADS),
        ones, ones, ones]).reshape(1, -1)
    modes = ("raw", "norm", "norm", "norm", "raw", "raw", "raw")
    p = _in_projection(x2, seq, norm_g, scale, shift, w_in.astype(BF16), ep_gain, modes)
    b_out = _stick_breaking(p, batch, seq, q_col=2 * d, k_col=3 * d, v_col=4 * d)
    bias_full = jnp.repeat(a_bs.T, HEAD_DIM, axis=1)
    return _out_even(x2, seq, gate, p, b_out, a_ws, bias_full, w_out.astype(BF16))


def _odd_layer(x2, batch, seq, c, positions, norm_g, w_mod, b_mod, w_in, c_w, c_scale,
               d_qnorm_g, d_knorm_g, w_out):
    d = x2.shape[1]
    shift, scale, gate = _modulation(c, w_mod, b_mod)
    ones = jnp.ones((d,), F32)
    ep_gain = jnp.concatenate([
        ones, jnp.tile(d_qnorm_g, N_HEADS), jnp.tile(d_knorm_g, N_HEADS),
        ones, ones, ones]).reshape(1, -1)
    modes = ("raw", "rope", "rope_scaled", "raw", "raw", "raw")
    p = _in_projection(x2, seq, norm_g, scale, shift, w_in.astype(BF16), ep_gain, modes,
                       rope=_rope_operands(positions), post_scale=HEAD_DIM ** -0.5)
    return _odd_mixers(x2, batch, seq, gate, p, c_w.astype(BF16), c_scale, w_out.astype(BF16))


def kernel(x, c, positions, even_norm_g, even_w_mod, even_b_mod, even_w_in, even_a_vnorm_g, even_a_ws, even_a_bs, even_b_qnorm_g, even_b_knorm_g, even_w_out, odd_norm_g, odd_w_mod, odd_b_mod, odd_w_in, odd_c_w, odd_c_scale, odd_d_qnorm_g, odd_d_knorm_g, odd_w_out):
    batch, seq, d = x.shape
    depth = even_norm_g.shape[0] + odd_norm_g.shape[0]
    x2 = x.reshape(batch * seq, d)
    for layer in range(depth):
        i = layer // 2
        if layer % 2 == 0:
            x2 = _even_layer(x2, batch, seq, c, even_norm_g[i], even_w_mod[i], even_b_mod[i],
                             even_w_in[i], even_a_vnorm_g[i], even_a_ws[i], even_a_bs[i],
                             even_b_qnorm_g[i], even_b_knorm_g[i], even_w_out[i])
        else:
            x2 = _odd_layer(x2, batch, seq, c, positions, odd_norm_g[i], odd_w_mod[i], odd_b_mod[i],
                            odd_w_in[i], odd_c_w[i], odd_c_scale[i], odd_d_qnorm_g[i],
                            odd_d_knorm_g[i], odd_w_out[i])
    return x2.reshape(batch, seq, d)
```

```python
import functools

import jax
import jax.numpy as jnp
from jax import lax
from jax.experimental import pallas as pl
from jax.experimental.pallas import tpu as pltpu

F32 = jnp.float32
BF16 = jnp.bfloat16

EPS = 1e-6
ROPE_BASE = 10000.0
BLOCK = 128
HEAD_DIM = 128
N_HEADS = 8
POOL_WINDOWS = (2, 4, 8, 16)
POOL_GROUP_DIM = 256
POOL_HALO = 16

MOD_ROWS = 8

ROW_TILE = 512
COL_TILE = 1024
RET_CHUNK = 128

ATT_Q = 64
ATT_WIN = 384
ATT_NEW = 256
ATT_EXCLUDED = -1e30
LANES = 128
ATT_SUB = 2
ATT_HEADS_PER_STEP = 4
ATT_SKIP_BELOW = -151.0
LOG2E = 1.4426950408889634
ATT_NO_LIMIT = 1 << 30

VMEM_LIMIT = 56 * 1024 * 1024


def _silu(v):
    return v * jax.nn.sigmoid(v)


def _dot(a, b):
    return jnp.dot(a, b, preferred_element_type=F32)


def _dot_nt(a, b):
    return lax.dot_general(a, b, (((1,), (1,)), ((), ())), preferred_element_type=F32)


def _dot_tn(a, b):
    return lax.dot_general(a, b, (((0,), (0,)), ((), ())), preferred_element_type=F32)


def _group_rms(a, gain):
    ms = jnp.mean(a * a, axis=-1, keepdims=True)
    return a * lax.rsqrt(ms + EPS) * gain


def _mod_kernel(c_ref, w_ref, b_ref, o_ref):
    a = _silu(c_ref[...])
    o_ref[...] = _dot(a.astype(BF16), w_ref[...].astype(BF16)) + b_ref[...]


def _modulation(c, w_mod, b_mod):
    bsz, d = c.shape
    c8 = jnp.pad(c, ((0, MOD_ROWS - bsz), (0, 0)))
    m = pl.pallas_call(
        _mod_kernel,
        grid=(3,),
        in_specs=[pl.BlockSpec((MOD_ROWS, d), lambda j: (0, 0)),
                  pl.BlockSpec((d, d), lambda j: (0, j)),
                  pl.BlockSpec((1, d), lambda j: (0, j))],
        out_specs=pl.BlockSpec((MOD_ROWS, d), lambda j: (0, j)),
        out_shape=jax.ShapeDtypeStruct((MOD_ROWS, 3 * d), F32),
        name="modulation",
    )(c8, w_mod, b_mod.reshape(1, 3 * d))
    m = m[:bsz]
    return (m[:, None, :d], m[:, None, d:2 * d], m[:, None, 2 * d:])


def _inproj_kernel(*refs, modes, post_scale, use_rope):
    if use_rope:
        x_ref, g_ref, sc_ref, sh_ref, w_ref, eg_ref, pos_ref, freq_ref, sign_ref, o_ref, h_scr = refs
        ang = pos_ref[...] * freq_ref[...]
        cos_t = jnp.cos(ang)
        sin_t = jnp.sin(ang) * sign_ref[...]
    else:
        x_ref, g_ref, sc_ref, sh_ref, w_ref, eg_ref, o_ref, h_scr = refs
    tn = COL_TILE
    x = x_ref[...]
    ms = jnp.mean(x * x, axis=-1, keepdims=True)
    y = x * lax.rsqrt(ms + EPS) * g_ref[...]
    h_scr[...] = (y * (1.0 + sc_ref[...]) + sh_ref[...]).astype(BF16)

    for jj, mode in enumerate(modes):
        acc = _dot(h_scr[...], w_ref[:, jj * tn:(jj + 1) * tn])
        if mode == "raw":
            o_ref[:, jj * tn:(jj + 1) * tn] = acc.astype(BF16)
            continue
        for g in range(tn // HEAD_DIM):
            cols = slice(jj * tn + g * HEAD_DIM, jj * tn + (g + 1) * HEAD_DIM)
            t = _group_rms(acc[:, g * HEAD_DIM:(g + 1) * HEAD_DIM], eg_ref[:, cols])
            if mode != "norm":
                t = t * cos_t + pltpu.roll(t, HEAD_DIM // 2, axis=1) * sin_t
            if mode == "rope_scaled":
                t = t * post_scale
            o_ref[:, cols] = t.astype(BF16)


def _in_projection(x2, seq, norm_g, scale, shift, w_bf, ep_gain, modes, rope=None, post_scale=1.0):
    n, d = x2.shape
    tm, tn = ROW_TILE, COL_TILE
    ncols = len(modes) * tn
    use_rope = rope is not None
    in_specs = [
        pl.BlockSpec((tm, d), lambda i: (i, 0)),
        pl.BlockSpec((1, d), lambda i: (0, 0)),
        pl.BlockSpec((None, 1, d), lambda i: ((i * tm) // seq, 0, 0)),
        pl.BlockSpec((None, 1, d), lambda i: ((i * tm) // seq, 0, 0)),
        pl.BlockSpec((d, ncols), lambda i: (0, 0), pipeline_mode=pl.Buffered(1)),
        pl.BlockSpec((1, ncols), lambda i: (0, 0)),
    ]
    args = [x2, norm_g.reshape(1, d), scale, shift, w_bf, ep_gain]
    if use_rope:
        vec = pl.BlockSpec((1, HEAD_DIM), lambda i: (0, 0))
        in_specs += [pl.BlockSpec((tm, 1), lambda i: (i, 0)), vec, vec]
        args += list(rope)
    return pl.pallas_call(
        functools.partial(_inproj_kernel, modes=tuple(modes), post_scale=post_scale,
                          use_rope=use_rope),
        grid=(n // tm,),
        in_specs=in_specs,
        out_specs=pl.BlockSpec((tm, ncols), lambda i: (i, 0)),
        out_shape=jax.ShapeDtypeStruct((n, ncols), BF16),
        scratch_shapes=[pltpu.VMEM((tm, d), BF16)],
        compiler_params=pltpu.CompilerParams(
            dimension_semantics=("parallel",), vmem_limit_bytes=VMEM_LIMIT),
        name="in_projection",
    )(*args)


def _attn_kernel(q_ref, k_ref, v_ref, o_ref, u_scr, *, seq, scale):
    hb, tq, win, nsub = ATT_HEADS_PER_STEP, ATT_Q, ATT_WIN, ATT_SUB
    chains = nsub * hb

    u_scr[...] = jnp.where(lax.broadcasted_iota(jnp.int32, (ATT_NEW, ATT_NEW), 0)
                           > lax.broadcasted_iota(jnp.int32, (ATT_NEW, ATT_NEW), 1),
                           1.0, 0.0).astype(BF16)
    old = win - ATT_NEW
    key_idx = lax.broadcasted_iota(jnp.int32, (tq, win), 1)
    row_idx = lax.broadcasted_iota(jnp.int32, (tq, win), 0)
    tail = win - LANES
    tail_ok = (lax.broadcasted_iota(jnp.int32, (tq, LANES), 1) + tail
               < lax.broadcasted_iota(jnp.int32, (tq, LANES), 0) + (win - tq))

    def head_cols(h):
        return slice(h * HEAD_DIM, (h + 1) * HEAD_DIM)

    def tile_logits(rows0, keys0):
        return [_dot_nt(q_ref[pl.ds(rows0[a], tq), head_cols(h)],
                        k_ref[pl.ds(keys0[a], win), head_cols(h)]) * (scale * LOG2E)
                for a in range(nsub) for h in range(hb)]

    def tile_weights(logits, rows0, keys0, limits, totals):
        log_beta, terms = [], []
        for a in range(nsub):
            if limits is not None:
                ok = key_idx < (jnp.minimum(rows0[a] + row_idx, limits[a]) - keys0[a])
            for h in range(hb):
                z = logits[a * hb + h]
                if limits is None:
                    z = jnp.concatenate(
                        [z[:, :tail], jnp.where(tail_ok, z[:, tail:], ATT_EXCLUDED)], axis=1)
                else:
                    z = jnp.where(ok, z, ATT_EXCLUDED)
                nz = -z
                lsm = (jnp.minimum(nz, 0.0)
                       - jnp.log(1.0 + jnp.exp2(jnp.minimum(z, nz))) * LOG2E)
                log_beta.append(z + lsm)
                terms.append(lsm)
        sums_new = _dot(jnp.concatenate([t[:, old:].astype(BF16) for t in terms], axis=0),
                        u_scr[...])
        sums_old = _dot(jnp.concatenate([t[:, :old].astype(BF16) for t in terms], axis=0),
                        u_scr[0:old, 0:old])
        weights, tile_totals = [], []
        for c in range(chains):
            newer = sums_new[c * tq:(c + 1) * tq]
            older = sums_old[c * tq:(c + 1) * tq]
            total_new = newer[:, 0:1] + terms[c][:, old:old + 1]
            expo = log_beta[c] + jnp.concatenate([older + total_new, newer], axis=1)
            if totals is not None:
                expo = expo + totals[c]
            weights.append(jnp.exp2(expo).astype(BF16))
            tile_totals.append(total_new + older[:, 0:1] + terms[c][:, 0:1])
        return weights, tile_totals

    def tile_values(weights, keys0):
        return [_dot(weights[a * hb + h], v_ref[pl.ds(keys0[a], win), head_cols(h)])
                for a in range(nsub) for h in range(hb)]

    def slowest(totals):
        m = totals[0]
        for t in totals[1:]:
            m = jnp.maximum(m, t)
        return jnp.max(m)

    def qblock(i, carry, *, near_start):
        r0 = i * (tq * nsub)
        rows0 = [pl.multiple_of(r0 + a * tq, tq) for a in range(nsub)]
        if near_start:
            starts = [jnp.maximum(r - (win - tq), 0) for r in rows0]
        else:
            starts = [r - (win - tq) for r in rows0]
        keys0 = [pl.multiple_of(s, tq) for s in starts]
        weights, totals = tile_weights(tile_logits(rows0, keys0), rows0, keys0,
                                       [ATT_NO_LIMIT] * nsub if near_start else None, None)
        accs = tile_values(weights, keys0)

        def more(st):
            m, worst, _, _ = st
            return (starts[-1] - m * win > 0) & (worst > ATT_SKIP_BELOW)

        def older(st):
            m, _, totals, accs = st
            limits = [s - m * win for s in starts]
            keys0 = [pl.multiple_of(jnp.maximum(lim - win, 0), tq) for lim in limits]
            weights, tile_totals = tile_weights(tile_logits(rows0, keys0), rows0, keys0,
                                                limits, totals)
            outs = tile_values(weights, keys0)
            totals = tuple(t + d for t, d in zip(totals, tile_totals))
            accs = tuple(acc + o for acc, o in zip(accs, outs))
            return (m + 1, slowest(totals), totals, accs)

        _, _, _, accs = lax.while_loop(
            more, older, (jnp.int32(0), slowest(totals), tuple(totals), tuple(accs)))
        for a in range(nsub):
            for h in range(hb):
                o_ref[pl.ds(rows0[a], tq), head_cols(h)] = accs[a * hb + h].astype(BF16)
        return carry

    n_blocks = seq // (tq * nsub)
    n_clipped = min(-(-(win - tq) // (tq * nsub)), n_blocks)
    lax.fori_loop(0, n_clipped, functools.partial(qblock, near_start=True), 0)
    lax.fori_loop(n_clipped, n_blocks, functools.partial(qblock, near_start=False), 0)


def _stick_breaking(p, batch, seq, q_col, k_col, v_col):
    n = p.shape[0]
    width = ATT_HEADS_PER_STEP * HEAD_DIM
    n_hg = N_HEADS // ATT_HEADS_PER_STEP

    def spec(col):
        return pl.BlockSpec((seq, width), lambda b, hg: (b, col // width + hg))

    return pl.pallas_call(
        functools.partial(_attn_kernel, seq=seq, scale=HEAD_DIM ** -0.5),
        grid=(batch, n_hg),
        in_specs=[spec(q_col), spec(k_col), spec(v_col)],
        out_specs=pl.BlockSpec((seq, width), lambda b, hg: (b, hg)),
        out_shape=jax.ShapeDtypeStruct((n, N_HEADS * HEAD_DIM), BF16),
        scratch_shapes=[pltpu.VMEM((ATT_NEW, ATT_NEW), BF16)],
        compiler_params=pltpu.CompilerParams(
            dimension_semantics=("parallel", "parallel"), vmem_limit_bytes=VMEM_LIMIT),
        name="stick_breaking",
    )(p, p, p)


def _out_even_kernel(x_ref, gate_ref, uv_ref, z_ref, bo_ref, ws_ref, bias_ref, wout_ref,
                     o_ref, y_scr):
    tm = x_ref.shape[0]
    a_width = bo_ref.shape[1]
    row = lax.broadcasted_iota(jnp.int32, (BLOCK, BLOCK), 0)
    col = lax.broadcasted_iota(jnp.int32, (BLOCK, BLOCK), 1)
    for g in range(a_width // HEAD_DIM):
        cols = slice(g * HEAD_DIM, (g + 1) * HEAD_DIM)
        vcols = slice(a_width + g * HEAD_DIM, a_width + (g + 1) * HEAD_DIM)
        wg = jnp.where(col <= row, ws_ref[g], 0.0).astype(BF16)
        for c in range(tm // BLOCK):
            rows = slice(c * BLOCK, (c + 1) * BLOCK)
            mixed = _dot(wg, uv_ref[rows, vcols]) + bias_ref[:, cols]
            gated = uv_ref[rows, cols].astype(F32) * mixed * _silu(z_ref[rows, cols].astype(F32))
            y_scr[rows, cols] = gated.astype(BF16)
    y_scr[:, a_width:] = (bo_ref[...].astype(F32)
                          * _silu(z_ref[:, a_width:].astype(F32))).astype(BF16)
    o_ref[...] = x_ref[...] + gate_ref[...] * _dot(y_scr[...], wout_ref[...])


def _out_even(x2, seq, gate, p, b_out, a_ws, bias_full, w_out_bf):
    n, d = x2.shape
    tm = ROW_TILE
    aw = b_out.shape[1]
    return pl.pallas_call(
        _out_even_kernel,
        grid=(n // tm,),
        in_specs=[pl.BlockSpec((tm, d), lambda i: (i, 0)),
                  pl.BlockSpec((None, 1, d), lambda i: ((i * tm) // seq, 0, 0)),
                  pl.BlockSpec((tm, 2 * aw), lambda i: (i, 0)),
                  pl.BlockSpec((tm, 2 * aw), lambda i: (i, 1)),
                  pl.BlockSpec((tm, aw), lambda i: (i, 0)),
                  pl.BlockSpec(a_ws.shape, lambda i: (0, 0, 0)),
                  pl.BlockSpec(bias_full.shape, lambda i: (0, 0)),
                  pl.BlockSpec(w_out_bf.shape, lambda i: (0, 0))],
        out_specs=pl.BlockSpec((tm, d), lambda i: (i, 0)),
        out_shape=jax.ShapeDtypeStruct((n, d), F32),
        scratch_shapes=[pltpu.VMEM((tm, 2 * aw), BF16)],
        compiler_params=pltpu.CompilerParams(
            dimension_semantics=("parallel",), vmem_limit_bytes=VMEM_LIMIT),
        name="out_even",
    )(x2, gate, p, p, b_out, a_ws, bias_full, w_out_bf)


def _rope_operands(positions):
    half = HEAD_DIM // 2
    inv_freq = ROPE_BASE ** (-jnp.arange(half, dtype=F32) / half)
    freq = jnp.concatenate([inv_freq, inv_freq]).reshape(1, HEAD_DIM)
    sign = jnp.concatenate([-jnp.ones((half,), F32), jnp.ones((half,), F32)]).reshape(1, HEAD_DIM)
    return positions.reshape(-1, 1).astype(F32), freq, sign


def _retention_tables():
    chunk = RET_CHUNK
    log_gamma = jnp.log1p(-jnp.exp2(-5.0 - jnp.arange(N_HEADS, dtype=F32)))
    idx = jnp.arange(chunk, dtype=F32)
    diff = idx[:, None] - idx[None, :]
    intra = jnp.where(diff >= 0, jnp.exp(log_gamma[:, None, None] * jnp.maximum(diff, 0.0)), 0.0)
    q_decay = jnp.exp(log_gamma[:, None] * (idx + 1.0))
    k_decay = jnp.exp(log_gamma[:, None] * (chunk - 1.0 - idx))
    chunk_decay = jnp.exp(log_gamma * chunk)
    return (intra,
            jnp.broadcast_to(q_decay[:, :, None], (N_HEADS, chunk, HEAD_DIM)),
            jnp.broadcast_to(k_decay[:, :, None], (N_HEADS, chunk, HEAD_DIM)),
            jnp.broadcast_to(chunk_decay[:, None, None], (N_HEADS, HEAD_DIM, HEAD_DIM)))


def _retention_rows(q_ref, k_ref, v_ref, idec_ref, qdec_ref, kdec_ref, cdec_ref, st_scr, emit):
    chunk = RET_CHUNK
    n_chunks = q_ref.shape[0] // chunk
    cols = [slice(h * HEAD_DIM, (h + 1) * HEAD_DIM) for h in range(N_HEADS)]
    rows = [slice(c * chunk, (c + 1) * chunk) for c in range(n_chunks)]

    kv = {}
    for c in range(n_chunks):
        for h in range(N_HEADS):
            k_decayed = (k_ref[rows[c], cols[h]].astype(F32) * kdec_ref[h]).astype(BF16)
            kv[h, c] = _dot_tn(k_decayed, v_ref[rows[c], cols[h]])
    state = {}
    for h in range(N_HEADS):
        s = st_scr[h]
        for c in range(n_chunks):
            state[h, c] = s.astype(BF16)
            s = s * cdec_ref[h] + kv[h, c]
        st_scr[h] = s
    for c in range(n_chunks):
        for h in range(N_HEADS):
            q = q_ref[rows[c], cols[h]]
            scores = _dot_nt(q, k_ref[rows[c], cols[h]]) * idec_ref[h]
            out = (_dot(scores.astype(BF16), v_ref[rows[c], cols[h]])
                   + _dot(q, state[h, c]) * qdec_ref[h])
            ms = jnp.mean(out * out, axis=-1, keepdims=True)
            emit(rows[c], h, out * lax.rsqrt(ms + EPS))


def _pooled_rows(pc_ref, halo_ref, cw_ref, cs_ref, ext_scr, seq_row0, emit):
    tm = pc_ref.shape[0]
    ext_scr[0:POOL_HALO, :] = jnp.where(seq_row0 == 0, jnp.zeros_like(halo_ref), halo_ref[...])
    ext_scr[POOL_HALO:, :] = pc_ref[...]
    src_rows = BLOCK + POOL_HALO
    lag = (lax.broadcasted_iota(jnp.int32, (BLOCK, src_rows), 0) + POOL_HALO
           - lax.broadcasted_iota(jnp.int32, (BLOCK, src_rows), 1))
    t_seq = seq_row0 + lax.broadcasted_iota(jnp.int32, (BLOCK, POOL_GROUP_DIM), 0)
    for gi, win in enumerate(POOL_WINDOWS):
        cols = slice(gi * POOL_GROUP_DIM, (gi + 1) * POOL_GROUP_DIM)
        band = jnp.where((lag >= 0) & (lag < win), 1.0 / win, 0.0).astype(BF16)
        short = jnp.where(t_seq + 1 < win, win / (t_seq + 1).astype(F32), 1.0)
        pooled = []
        for c in range(tm // BLOCK):
            mean = _dot(band, ext_scr[c * BLOCK:c * BLOCK + src_rows, cols])
            if c == 0:
                mean = mean * short
            cur = pc_ref[c * BLOCK:(c + 1) * BLOCK, cols].astype(F32)
            pooled.append((mean - cur).astype(BF16))
        emit(gi, _dot(jnp.concatenate(pooled, axis=0), cw_ref[gi]) * cs_ref[:, cols])


def _odd_mixers_kernel(p_ref, idec_ref, qdec_ref, kdec_ref, cdec_ref,
                       x_ref, gate_ref, halo_ref, cw_ref, cs_ref, wout_ref,
                       o_ref, st_scr, ext_scr, y_scr):
    tm = x_ref.shape[0]
    c_width = halo_ref.shape[1]
    pc_ref, q_ref, k_ref, v_ref, za_ref, zb_ref = (
        p_ref.at[:, pl.ds(j * c_width, c_width)] for j in range(6))

    @pl.when(pl.program_id(1) == 0)
    def _():
        st_scr[...] = jnp.zeros_like(st_scr)

    def gated_retention(rows, h, out):
        cols = slice(h * HEAD_DIM, (h + 1) * HEAD_DIM)
        ycols = slice(c_width + h * HEAD_DIM, c_width + (h + 1) * HEAD_DIM)
        y_scr[rows, ycols] = (out * _silu(zb_ref[rows, cols].astype(F32))).astype(BF16)

    def gated_pool(gi, mixed):
        cols = slice(gi * POOL_GROUP_DIM, (gi + 1) * POOL_GROUP_DIM)
        y_scr[:, cols] = (mixed * _silu(za_ref[:, cols].astype(F32))).astype(BF16)

    _retention_rows(q_ref, k_ref, v_ref, idec_ref, qdec_ref, kdec_ref, cdec_ref, st_scr,
                    gated_retention)
    _pooled_rows(pc_ref, halo_ref, cw_ref, cs_ref, ext_scr, pl.program_id(1) * tm, gated_pool)
    o_ref[...] = x_ref[...] + gate_ref[...] * _dot(y_scr[...], wout_ref[...])


def _odd_mixers(x2, batch, seq, gate, p, c_w_bf, c_scale, w_out_bf):
    n, d = x2.shape
    tm = ROW_TILE
    cw = N_HEADS * HEAD_DIM
    steps = seq // tm
    tables = _retention_tables()
    tabs = [pl.BlockSpec(t.shape, lambda b, s: (0, 0, 0)) for t in tables]
    halo_blocks = tm // POOL_HALO
    halo = pl.BlockSpec(
        (POOL_HALO, cw), lambda b, s: (jnp.maximum((b * steps + s) * halo_blocks - 1, 0), 0))
    return pl.pallas_call(
        _odd_mixers_kernel,
        grid=(batch, steps),
        in_specs=[pl.BlockSpec((tm, p.shape[1]), lambda b, s: (b * steps + s, 0))] + tabs + [
            pl.BlockSpec((tm, d), lambda b, s: (b * steps + s, 0)),
            pl.BlockSpec((None, 1, d), lambda b, s: (b, 0, 0)),
            halo,
            pl.BlockSpec(c_w_bf.shape, lambda b, s: (0, 0, 0)),
            pl.BlockSpec((1, cw), lambda b, s: (0, 0)),
            pl.BlockSpec(w_out_bf.shape, lambda b, s: (0, 0))],
        out_specs=pl.BlockSpec((tm, d), lambda b, s: (b * steps + s, 0)),
        out_shape=jax.ShapeDtypeStruct((n, d), F32),
        scratch_shapes=[pltpu.VMEM((N_HEADS, HEAD_DIM, HEAD_DIM), F32),
                        pltpu.VMEM((POOL_HALO + tm, cw), BF16),
                        pltpu.VMEM((tm, 2 * cw), BF16)],
        compiler_params=pltpu.CompilerParams(
            dimension_semantics=("parallel", "arbitrary"), vmem_limit_bytes=VMEM_LIMIT),
        name="odd_mixers",
    )(p, *tables, x2, gate, p, c_w_bf, c_scale.reshape(1, cw), w_out_bf)


def _even_layer(x2, batch, seq, c, norm_g, w_mod, b_mod, w_in, a_vnorm_g, a_ws, a_bs,
                b_qnorm_g, b_knorm_g, w_out):
    d = x2.shape[1]
    shift, scale, gate = _modulation(c, w_mod, b_mod)
    ones = jnp.ones((d,), F32)
    w_bf = jnp.concatenate([w_in[:, :2 * d], w_in[:, 5 * d:], w_in[:, 2 * d:5 * d]],
                           axis=1).astype(BF16)
    ep_gain = jnp.concatenate([
        ones, a_vnorm_g.reshape(-1), ones, ones,
        jnp.tile(b_qnorm_g, N_HEADS), jnp.tile(b_knorm_g, N_HEADS), ones]).reshape(1, -1)
    modes = ("raw", "norm", "raw", "raw", "norm", "norm", "raw")
    p = _in_projection(x2, seq, norm_g, scale, shift, w_bf, ep_gain, modes)
    b_out = _stick_breaking(p, batch, seq, q_col=4 * d, k_col=5 * d, v_col=6 * d)
    bias_full = jnp.repeat(a_bs.T, HEAD_DIM, axis=1)
    return _out_even(x2, seq, gate, p, b_out, a_ws, bias_full, w_out.astype(BF16))


def _odd_layer(x2, batch, seq, c, positions, norm_g, w_mod, b_mod, w_in, c_w, c_scale,
               d_qnorm_g, d_knorm_g, w_out):
    d = x2.shape[1]
    shift, scale, gate = _modulation(c, w_mod, b_mod)
    ones = jnp.ones((d,), F32)
    ep_gain = jnp.concatenate([
        ones, jnp.tile(d_qnorm_g, N_HEADS), jnp.tile(d_knorm_g, N_HEADS),
        ones, ones, ones]).reshape(1, -1)
    modes = ("raw", "rope", "rope_scaled", "raw", "raw", "raw")
    p = _in_projection(x2, seq, norm_g, scale, shift, w_in.astype(BF16), ep_gain, modes,
                       rope=_rope_operands(positions), post_scale=HEAD_DIM ** -0.5)
    return _odd_mixers(x2, batch, seq, gate, p, c_w.astype(BF16), c_scale, w_out.astype(BF16))


def kernel(x, c, positions, even_norm_g, even_w_mod, even_b_mod, even_w_in, even_a_vnorm_g, even_a_ws, even_a_bs, even_b_qnorm_g, even_b_knorm_g, even_w_out, odd_norm_g, odd_w_mod, odd_b_mod, odd_w_in, odd_c_w, odd_c_scale, odd_d_qnorm_g, odd_d_knorm_g, odd_w_out):
    batch, seq, d = x.shape
    depth = even_norm_g.shape[0] + odd_norm_g.shape[0]
    x2 = x.reshape(batch * seq, d)
    for layer in range(depth):
        i = layer // 2
        if layer % 2 == 0:
            x2 = _even_layer(x2, batch, seq, c, even_norm_g[i], even_w_mod[i], even_b_mod[i],
                             even_w_in[i], even_a_vnorm_g[i], even_a_ws[i], even_a_bs[i],
                             even_b_qnorm_g[i], even_b_knorm_g[i], even_w_out[i])
        else:
            x2 = _odd_layer(x2, batch, seq, c, positions, odd_norm_g[i], odd_w_mod[i], odd_b_mod[i],
                            odd_w_in[i], odd_c_w[i], odd_c_scale[i], odd_d_qnorm_g[i],
                            odd_d_knorm_g[i], odd_w_out[i])
    return x2.reshape(batch, seq, d)
```

```python
import functools

import jax
import jax.numpy as jnp
from jax import lax
from jax.experimental import pallas as pl
from jax.experimental.pallas import tpu as pltpu

F32 = jnp.float32
BF16 = jnp.bfloat16

EPS = 1e-6
ROPE_BASE = 10000.0
BLOCK = 128
HEAD_DIM = 128
N_HEADS = 8
POOL_WINDOWS = (2, 4, 8, 16)
POOL_GROUP_DIM = 256
POOL_HALO = 16

MOD_ROWS = 8

ROW_TILE = 512
OUT_EVEN_ROWS = 1024
COL_TILE = 1024
RET_CHUNK = 128

ATT_Q = 64
ATT_WIN = 384
ATT_NEW = 256
ATT_EXCLUDED = -1e30
LANES = 128
ATT_SUB = 2
ATT_HEADS_PER_STEP = 4
ATT_SKIP_BELOW = -151.0
LOG2E = 1.4426950408889634
ATT_NO_LIMIT = 1 << 30

VMEM_LIMIT = 56 * 1024 * 1024


def _silu(v):
    return v * jax.nn.sigmoid(v)


def _dot(a, b):
    return jnp.dot(a, b, preferred_element_type=F32)


def _dot_nt(a, b):
    return lax.dot_general(a, b, (((1,), (1,)), ((), ())), preferred_element_type=F32)


def _dot_tn(a, b):
    return lax.dot_general(a, b, (((0,), (0,)), ((), ())), preferred_element_type=F32)


def _group_rms(a, gain):
    ms = jnp.mean(a * a, axis=-1, keepdims=True)
    return a * lax.rsqrt(ms + EPS) * gain


def _mod_kernel(c_ref, w_ref, b_ref, o_ref):
    a = _silu(c_ref[...])
    o_ref[...] = _dot(a.astype(BF16), w_ref[...].astype(BF16)) + b_ref[...]


def _modulation(c, w_mod, b_mod):
    bsz, d = c.shape
    c8 = jnp.pad(c, ((0, MOD_ROWS - bsz), (0, 0)))
    m = pl.pallas_call(
        _mod_kernel,
        grid=(3,),
        in_specs=[pl.BlockSpec((MOD_ROWS, d), lambda j: (0, 0)),
                  pl.BlockSpec((d, d), lambda j: (0, j)),
                  pl.BlockSpec((1, d), lambda j: (0, j))],
        out_specs=pl.BlockSpec((MOD_ROWS, d), lambda j: (0, j)),
        out_shape=jax.ShapeDtypeStruct((MOD_ROWS, 3 * d), F32),
        name="modulation",
    )(c8, w_mod, b_mod.reshape(1, 3 * d))
    m = m[:bsz]
    return (m[:, None, :d], m[:, None, d:2 * d], m[:, None, 2 * d:])


def _inproj_kernel(*refs, modes, post_scale, use_rope):
    if use_rope:
        x_ref, g_ref, sc_ref, sh_ref, w_ref, eg_ref, pos_ref, freq_ref, sign_ref, o_ref, h_scr = refs
        ang = pos_ref[...] * freq_ref[...]
        cos_t = jnp.cos(ang)
        sin_t = jnp.sin(ang) * sign_ref[...]
    else:
        x_ref, g_ref, sc_ref, sh_ref, w_ref, eg_ref, o_ref, h_scr = refs
    tn = COL_TILE
    x = x_ref[...]
    ms = jnp.mean(x * x, axis=-1, keepdims=True)
    y = x * lax.rsqrt(ms + EPS) * g_ref[...]
    h_scr[...] = (y * (1.0 + sc_ref[...]) + sh_ref[...]).astype(BF16)

    for jj, mode in enumerate(modes):
        acc = _dot(h_scr[...], w_ref[:, jj * tn:(jj + 1) * tn])
        if mode == "raw":
            o_ref[:, jj * tn:(jj + 1) * tn] = acc.astype(BF16)
            continue
        for g in range(tn // HEAD_DIM):
            cols = slice(jj * tn + g * HEAD_DIM, jj * tn + (g + 1) * HEAD_DIM)
            t = _group_rms(acc[:, g * HEAD_DIM:(g + 1) * HEAD_DIM], eg_ref[:, cols])
            if mode != "norm":
                t = t * cos_t + pltpu.roll(t, HEAD_DIM // 2, axis=1) * sin_t
            if mode == "rope_scaled":
                t = t * post_scale
            o_ref[:, cols] = t.astype(BF16)


def _in_projection(x2, seq, norm_g, scale, shift, w_bf, ep_gain, modes, rope=None, post_scale=1.0):
    n, d = x2.shape
    tm, tn = ROW_TILE, COL_TILE
    ncols = len(modes) * tn
    use_rope = rope is not None
    in_specs = [
        pl.BlockSpec((tm, d), lambda i: (i, 0)),
        pl.BlockSpec((1, d), lambda i: (0, 0)),
        pl.BlockSpec((None, 1, d), lambda i: ((i * tm) // seq, 0, 0)),
        pl.BlockSpec((None, 1, d), lambda i: ((i * tm) // seq, 0, 0)),
        pl.BlockSpec((d, ncols), lambda i: (0, 0), pipeline_mode=pl.Buffered(1)),
        pl.BlockSpec((1, ncols), lambda i: (0, 0)),
    ]
    args = [x2, norm_g.reshape(1, d), scale, shift, w_bf, ep_gain]
    if use_rope:
        vec = pl.BlockSpec((1, HEAD_DIM), lambda i: (0, 0))
        in_specs += [pl.BlockSpec((tm, 1), lambda i: (i, 0)), vec, vec]
        args += list(rope)
    return pl.pallas_call(
        functools.partial(_inproj_kernel, modes=tuple(modes), post_scale=post_scale,
                          use_rope=use_rope),
        grid=(n // tm,),
        in_specs=in_specs,
        out_specs=pl.BlockSpec((tm, ncols), lambda i: (i, 0)),
        out_shape=jax.ShapeDtypeStruct((n, ncols), BF16),
        scratch_shapes=[pltpu.VMEM((tm, d), BF16)],
        compiler_params=pltpu.CompilerParams(
            dimension_semantics=("parallel",), vmem_limit_bytes=VMEM_LIMIT),
        name="in_projection",
    )(*args)


def _attn_kernel(q_ref, k_ref, v_ref, o_ref, u_scr, *, seq, scale):
    hb, tq, win, nsub = ATT_HEADS_PER_STEP, ATT_Q, ATT_WIN, ATT_SUB
    chains = nsub * hb

    u_scr[...] = jnp.where(lax.broadcasted_iota(jnp.int32, (ATT_NEW, ATT_NEW), 0)
                           > lax.broadcasted_iota(jnp.int32, (ATT_NEW, ATT_NEW), 1),
                           1.0, 0.0).astype(BF16)
    old = win - ATT_NEW
    key_idx = lax.broadcasted_iota(jnp.int32, (tq, win), 1)
    row_idx = lax.broadcasted_iota(jnp.int32, (tq, win), 0)
    tail = win - LANES
    tail_ok = (lax.broadcasted_iota(jnp.int32, (tq, LANES), 1) + tail
               < lax.broadcasted_iota(jnp.int32, (tq, LANES), 0) + (win - tq))

    def head_cols(h):
        return slice(h * HEAD_DIM, (h + 1) * HEAD_DIM)

    def tile_logits(rows0, keys0):
        return [_dot_nt(q_ref[pl.ds(rows0[a], tq), head_cols(h)],
                        k_ref[pl.ds(keys0[a], win), head_cols(h)]) * (scale * LOG2E)
                for a in range(nsub) for h in range(hb)]

    def tile_weights(logits, rows0, keys0, limits, totals):
        log_beta, terms = [], []
        for a in range(nsub):
            if limits is not None:
                ok = key_idx < (jnp.minimum(rows0[a] + row_idx, limits[a]) - keys0[a])
            for h in range(hb):
                z = logits[a * hb + h]
                if limits is None:
                    z = jnp.concatenate(
                        [z[:, :tail], jnp.where(tail_ok, z[:, tail:], ATT_EXCLUDED)], axis=1)
                else:
                    z = jnp.where(ok, z, ATT_EXCLUDED)
                nz = -z
                lsm = (jnp.minimum(nz, 0.0)
                       - jnp.log(1.0 + jnp.exp2(jnp.minimum(z, nz))) * LOG2E)
                log_beta.append(z + lsm)
                terms.append(lsm)
        sums_new = _dot(jnp.concatenate([t[:, old:].astype(BF16) for t in terms], axis=0),
                        u_scr[...])
        sums_old = _dot(jnp.concatenate([t[:, :old].astype(BF16) for t in terms], axis=0),
                        u_scr[0:old, 0:old])
        weights, tile_totals = [], []
        for c in range(chains):
            newer = sums_new[c * tq:(c + 1) * tq]
            older = sums_old[c * tq:(c + 1) * tq]
            total_new = newer[:, 0:1] + terms[c][:, old:old + 1]
            expo = log_beta[c] + jnp.concatenate([older + total_new, newer], axis=1)
            if totals is not None:
                expo = expo + totals[c]
            weights.append(jnp.exp2(expo).astype(BF16))
            tile_totals.append(total_new + older[:, 0:1] + terms[c][:, 0:1])
        return weights, tile_totals

    def tile_values(weights, keys0):
        return [_dot(weights[a * hb + h], v_ref[pl.ds(keys0[a], win), head_cols(h)])
                for a in range(nsub) for h in range(hb)]

    def slowest(totals):
        m = totals[0]
        for t in totals[1:]:
            m = jnp.maximum(m, t)
        return jnp.max(m)

    def qblock(i, carry, *, near_start):
        r0 = i * (tq * nsub)
        rows0 = [pl.multiple_of(r0 + a * tq, tq) for a in range(nsub)]
        if near_start:
            starts = [jnp.maximum(r - (win - tq), 0) for r in rows0]
        else:
            starts = [r - (win - tq) for r in rows0]
        keys0 = [pl.multiple_of(s, tq) for s in starts]
        weights, totals = tile_weights(tile_logits(rows0, keys0), rows0, keys0,
                                       [ATT_NO_LIMIT] * nsub if near_start else None, None)
        accs = tile_values(weights, keys0)

        def more(st):
            m, worst, _, _ = st
            return (starts[-1] - m * win > 0) & (worst > ATT_SKIP_BELOW)

        def older(st):
            m, _, totals, accs = st
            limits = [s - m * win for s in starts]
            keys0 = [pl.multiple_of(jnp.maximum(lim - win, 0), tq) for lim in limits]
            weights, tile_totals = tile_weights(tile_logits(rows0, keys0), rows0, keys0,
                                                limits, totals)
            outs = tile_values(weights, keys0)
            totals = tuple(t + d for t, d in zip(totals, tile_totals))
            accs = tuple(acc + o for acc, o in zip(accs, outs))
            return (m + 1, slowest(totals), totals, accs)

        _, _, _, accs = lax.while_loop(
            more, older, (jnp.int32(0), slowest(totals), tuple(totals), tuple(accs)))
        for a in range(nsub):
            for h in range(hb):
                o_ref[pl.ds(rows0[a], tq), head_cols(h)] = accs[a * hb + h].astype(BF16)
        return carry

    n_blocks = seq // (tq * nsub)
    n_clipped = min(-(-(win - tq) // (tq * nsub)), n_blocks)
    lax.fori_loop(0, n_clipped, functools.partial(qblock, near_start=True), 0)
    lax.fori_loop(n_clipped, n_blocks, functools.partial(qblock, near_start=False), 0)


def _stick_breaking(p, batch, seq, q_col, k_col, v_col):
    n = p.shape[0]
    width = ATT_HEADS_PER_STEP * HEAD_DIM
    n_hg = N_HEADS // ATT_HEADS_PER_STEP

    def spec(col):
        return pl.BlockSpec((seq, width), lambda b, hg: (b, col // width + hg))

    return pl.pallas_call(
        functools.partial(_attn_kernel, seq=seq, scale=HEAD_DIM ** -0.5),
        grid=(batch, n_hg),
        in_specs=[spec(q_col), spec(k_col), spec(v_col)],
        out_specs=pl.BlockSpec((seq, width), lambda b, hg: (b, hg)),
        out_shape=jax.ShapeDtypeStruct((n, N_HEADS * HEAD_DIM), BF16),
        scratch_shapes=[pltpu.VMEM((ATT_NEW, ATT_NEW), BF16)],
        compiler_params=pltpu.CompilerParams(
            dimension_semantics=("parallel", "parallel"), vmem_limit_bytes=VMEM_LIMIT),
        name="stick_breaking",
    )(p, p, p)


def _out_even_kernel(x_ref, gate_ref, uv_ref, za_ref, zb_ref, bo_ref, ws_ref, bias_ref, wout_ref,
                     o_ref, y_scr):
    tm = x_ref.shape[0]
    a_width = bo_ref.shape[1]
    row = lax.broadcasted_iota(jnp.int32, (BLOCK, BLOCK), 0)
    col = lax.broadcasted_iota(jnp.int32, (BLOCK, BLOCK), 1)
    for g in range(a_width // HEAD_DIM):
        cols = slice(g * HEAD_DIM, (g + 1) * HEAD_DIM)
        vcols = slice(a_width + g * HEAD_DIM, a_width + (g + 1) * HEAD_DIM)
        wg = jnp.where(col <= row, ws_ref[g], 0.0).astype(BF16)
        for c in range(tm // BLOCK):
            rows = slice(c * BLOCK, (c + 1) * BLOCK)
            mixed = _dot(wg, uv_ref[rows, vcols]) + bias_ref[:, cols]
            gated = uv_ref[rows, cols].astype(F32) * mixed * _silu(za_ref[rows, cols].astype(F32))
            y_scr[rows, cols] = gated.astype(BF16)
    y_scr[:, a_width:] = (bo_ref[...].astype(F32) * _silu(zb_ref[...].astype(F32))).astype(BF16)
    o_ref[...] = x_ref[...] + gate_ref[...] * _dot(y_scr[...], wout_ref[...])


def _out_even(x2, seq, gate, p, b_out, a_ws, bias_full, w_out_bf):
    n, d = x2.shape
    tm = OUT_EVEN_ROWS
    aw = b_out.shape[1]
    row = lambda c: pl.BlockSpec((tm, aw), lambda i: (i, c))
    const = lambda a: pl.BlockSpec(a.shape, lambda i: (0,) * a.ndim, pipeline_mode=pl.Buffered(1))
    return pl.pallas_call(
        _out_even_kernel,
        grid=(n // tm,),
        in_specs=[pl.BlockSpec((tm, d), lambda i: (i, 0)),
                  pl.BlockSpec((None, 1, d), lambda i: ((i * tm) // seq, 0, 0)),
                  pl.BlockSpec((tm, 2 * aw), lambda i: (i, 0)),
                  row(5), row(6),
                  row(0),
                  const(a_ws), const(bias_full), const(w_out_bf)],
        out_specs=pl.BlockSpec((tm, d), lambda i: (i, 0)),
        out_shape=jax.ShapeDtypeStruct((n, d), F32),
        scratch_shapes=[pltpu.VMEM((tm, 2 * aw), BF16)],
        compiler_params=pltpu.CompilerParams(
            dimension_semantics=("parallel",), vmem_limit_bytes=VMEM_LIMIT),
        name="out_even",
    )(x2, gate, p, p, p, b_out, a_ws, bias_full, w_out_bf)


def _rope_operands(positions):
    half = HEAD_DIM // 2
    inv_freq = ROPE_BASE ** (-jnp.arange(half, dtype=F32) / half)
    freq = jnp.concatenate([inv_freq, inv_freq]).reshape(1, HEAD_DIM)
    sign = jnp.concatenate([-jnp.ones((half,), F32), jnp.ones((half,), F32)]).reshape(1, HEAD_DIM)
    return positions.reshape(-1, 1).astype(F32), freq, sign


def _retention_tables():
    chunk = RET_CHUNK
    log_gamma = jnp.log1p(-jnp.exp2(-5.0 - jnp.arange(N_HEADS, dtype=F32)))
    idx = jnp.arange(chunk, dtype=F32)
    diff = idx[:, None] - idx[None, :]
    intra = jnp.where(diff >= 0, jnp.exp(log_gamma[:, None, None] * jnp.maximum(diff, 0.0)), 0.0)
    q_decay = jnp.exp(log_gamma[:, None] * (idx + 1.0))
    k_decay = jnp.exp(log_gamma[:, None] * (chunk - 1.0 - idx))
    chunk_decay = jnp.exp(log_gamma * chunk)
    return (intra,
            jnp.broadcast_to(q_decay[:, :, None], (N_HEADS, chunk, HEAD_DIM)),
            jnp.broadcast_to(k_decay[:, :, None], (N_HEADS, chunk, HEAD_DIM)),
            jnp.broadcast_to(chunk_decay[:, None, None], (N_HEADS, HEAD_DIM, HEAD_DIM)))


def _retention_rows(q_ref, k_ref, v_ref, idec_ref, qdec_ref, kdec_ref, cdec_ref, st_scr, emit):
    chunk = RET_CHUNK
    n_chunks = q_ref.shape[0] // chunk
    cols = [slice(h * HEAD_DIM, (h + 1) * HEAD_DIM) for h in range(N_HEADS)]
    rows = [slice(c * chunk, (c + 1) * chunk) for c in range(n_chunks)]

    kv = {}
    for c in range(n_chunks):
        for h in range(N_HEADS):
            k_decayed = (k_ref[rows[c], cols[h]].astype(F32) * kdec_ref[h]).astype(BF16)
            kv[h, c] = _dot_tn(k_decayed, v_ref[rows[c], cols[h]])
    state = {}
    for h in range(N_HEADS):
        s = st_scr[h]
        for c in range(n_chunks):
            state[h, c] = s.astype(BF16)
            s = s * cdec_ref[h] + kv[h, c]
        st_scr[h] = s
    for c in range(n_chunks):
        for h in range(N_HEADS):
            q = q_ref[rows[c], cols[h]]
            scores = _dot_nt(q, k_ref[rows[c], cols[h]]) * idec_ref[h]
            out = (_dot(scores.astype(BF16), v_ref[rows[c], cols[h]])
                   + _dot(q, state[h, c]) * qdec_ref[h])
            ms = jnp.mean(out * out, axis=-1, keepdims=True)
            emit(rows[c], h, out * lax.rsqrt(ms + EPS))


def _pooled_rows(pc_ref, halo_ref, cw_ref, cs_ref, ext_scr, seq_row0, emit):
    tm = pc_ref.shape[0]
    ext_scr[0:POOL_HALO, :] = jnp.where(seq_row0 == 0, jnp.zeros_like(halo_ref), halo_ref[...])
    ext_scr[POOL_HALO:, :] = pc_ref[...]
    src_rows = BLOCK + POOL_HALO
    lag = (lax.broadcasted_iota(jnp.int32, (BLOCK, src_rows), 0) + POOL_HALO
           - lax.broadcasted_iota(jnp.int32, (BLOCK, src_rows), 1))
    t_seq = seq_row0 + lax.broadcasted_iota(jnp.int32, (BLOCK, POOL_GROUP_DIM), 0)
    for gi, win in enumerate(POOL_WINDOWS):
        cols = slice(gi * POOL_GROUP_DIM, (gi + 1) * POOL_GROUP_DIM)
        band = jnp.where((lag >= 0) & (lag < win), 1.0 / win, 0.0).astype(BF16)
        short = jnp.where(t_seq + 1 < win, win / (t_seq + 1).astype(F32), 1.0)
        pooled = []
        for c in range(tm // BLOCK):
            mean = _dot(band, ext_scr[c * BLOCK:c * BLOCK + src_rows, cols])
            if c == 0:
                mean = mean * short
            cur = pc_ref[c * BLOCK:(c + 1) * BLOCK, cols].astype(F32)
            pooled.append((mean - cur).astype(BF16))
        emit(gi, _dot(jnp.concatenate(pooled, axis=0), cw_ref[gi]) * cs_ref[:, cols])


def _odd_mixers_kernel(p_ref, idec_ref, qdec_ref, kdec_ref, cdec_ref,
                       x_ref, gate_ref, halo_ref, cw_ref, cs_ref, wout_ref,
                       o_ref, st_scr, ext_scr, y_scr):
    tm = x_ref.shape[0]
    c_width = halo_ref.shape[1]
    pc_ref, q_ref, k_ref, v_ref, za_ref, zb_ref = (
        p_ref.at[:, pl.ds(j * c_width, c_width)] for j in range(6))

    @pl.when(pl.program_id(1) == 0)
    def _():
        st_scr[...] = jnp.zeros_like(st_scr)

    def gated_retention(rows, h, out):
        cols = slice(h * HEAD_DIM, (h + 1) * HEAD_DIM)
        ycols = slice(c_width + h * HEAD_DIM, c_width + (h + 1) * HEAD_DIM)
        y_scr[rows, ycols] = (out * _silu(zb_ref[rows, cols].astype(F32))).astype(BF16)

    def gated_pool(gi, mixed):
        cols = slice(gi * POOL_GROUP_DIM, (gi + 1) * POOL_GROUP_DIM)
        y_scr[:, cols] = (mixed * _silu(za_ref[:, cols].astype(F32))).astype(BF16)

    _retention_rows(q_ref, k_ref, v_ref, idec_ref, qdec_ref, kdec_ref, cdec_ref, st_scr,
                    gated_retention)
    _pooled_rows(pc_ref, halo_ref, cw_ref, cs_ref, ext_scr, pl.program_id(1) * tm, gated_pool)
    o_ref[...] = x_ref[...] + gate_ref[...] * _dot(y_scr[...], wout_ref[...])


def _odd_mixers(x2, batch, seq, gate, p, c_w_bf, c_scale, w_out_bf):
    n, d = x2.shape
    tm = ROW_TILE
    cw = N_HEADS * HEAD_DIM
    steps = seq // tm
    tables = _retention_tables()
    tabs = [pl.BlockSpec(t.shape, lambda b, s: (0, 0, 0)) for t in tables]
    halo_blocks = tm // POOL_HALO
    halo = pl.BlockSpec(
        (POOL_HALO, cw), lambda b, s: (jnp.maximum((b * steps + s) * halo_blocks - 1, 0), 0))
    return pl.pallas_call(
        _odd_mixers_kernel,
        grid=(batch, steps),
        in_specs=[pl.BlockSpec((tm, p.shape[1]), lambda b, s: (b * steps + s, 0))] + tabs + [
            pl.BlockSpec((tm, d), lambda b, s: (b * steps + s, 0)),
            pl.BlockSpec((None, 1, d), lambda b, s: (b, 0, 0)),
            halo,
            pl.BlockSpec(c_w_bf.shape, lambda b, s: (0, 0, 0)),
            pl.BlockSpec((1, cw), lambda b, s: (0, 0)),
            pl.BlockSpec(w_out_bf.shape, lambda b, s: (0, 0))],
        out_specs=pl.BlockSpec((tm, d), lambda b, s: (b * steps + s, 0)),
        out_shape=jax.ShapeDtypeStruct((n, d), F32),
        scratch_shapes=[pltpu.VMEM((N_HEADS, HEAD_DIM, HEAD_DIM), F32),
                        pltpu.VMEM((POOL_HALO + tm, cw), BF16),
                        pltpu.VMEM((tm, 2 * cw), BF16)],
        compiler_params=pltpu.CompilerParams(
            dimension_semantics=("parallel", "arbitrary"), vmem_limit_bytes=VMEM_LIMIT),
        name="odd_mixers",
    )(p, *tables, x2, gate, p, c_w_bf, c_scale.reshape(1, cw), w_out_bf)


def _even_layer(x2, batch, seq, c, norm_g, w_mod, b_mod, w_in, a_vnorm_g, a_ws, a_bs,
                b_qnorm_g, b_knorm_g, w_out):
    d = x2.shape[1]
    shift, scale, gate = _modulation(c, w_mod, b_mod)
    ones = jnp.ones((d,), F32)
    ep_gain = jnp.concatenate([
        ones, a_vnorm_g.reshape(-1), jnp.tile(b_qnorm_g, N_HEADS), jnp.tile(b_knorm_g, N_HEADS),
        ones, ones, ones]).reshape(1, -1)
    modes = ("raw", "norm", "norm", "norm", "raw", "raw", "raw")
    p = _in_projection(x2, seq, norm_g, scale, shift, w_in.astype(BF16), ep_gain, modes)
    b_out = _stick_breaking(p, batch, seq, q_col=2 * d, k_col=3 * d, v_col=4 * d)
    bias_full = jnp.repeat(a_bs.T, HEAD_DIM, axis=1)
    return _out_even(x2, seq, gate, p, b_out, a_ws, bias_full, w_out.astype(BF16))


def _odd_layer(x2, batch, seq, c, positions, norm_g, w_mod, b_mod, w_in, c_w, c_scale,
               d_qnorm_g, d_knorm_g, w_out):
    d = x2.shape[1]
    shift, scale, gate = _modulation(c, w_mod, b_mod)
    ones = jnp.ones((d,), F32)
    ep_gain = jnp.concatenate([
        ones, jnp.tile(d_qnorm_g, N_HEADS), jnp.tile(d_knorm_g, N_HEADS),
        ones, ones, ones]).reshape(1, -1)
    modes = ("raw", "rope", "rope_scaled", "raw", "raw", "raw")
    p = _in_projection(x2, seq, norm_g, scale, shift, w_in.astype(BF16), ep_gain, modes,
                       rope=_rope_operands(positions), post_scale=HEAD_DIM ** -0.5)
    return _odd_mixers(x2, batch, seq, gate, p, c_w.astype(BF16), c_scale, w_out.astype(BF16))


def kernel(x, c, positions, even_norm_g, even_w_mod, even_b_mod, even_w_in, even_a_vnorm_g, even_a_ws, even_a_bs, even_b_qnorm_g, even_b_knorm_g, even_w_out, odd_norm_g, odd_w_mod, odd_b_mod, odd_w_in, odd_c_w, odd_c_scale, odd_d_qnorm_g, odd_d_knorm_g, odd_w_out):
    batch, seq, d = x.shape
    depth = even_norm_g.shape[0] + odd_norm_g.shape[0]
    x2 = x.reshape(batch * seq, d)
    for layer in range(depth):
        i = layer // 2
        if layer % 2 == 0:
            x2 = _even_layer(x2, batch, seq, c, even_norm_g[i], even_w_mod[i], even_b_mod[i],
                             even_w_in[i], even_a_vnorm_g[i], even_a_ws[i], even_a_bs[i],
                             even_b_qnorm_g[i], even_b_knorm_g[i], even_w_out[i])
        else:
            x2 = _odd_layer(x2, batch, seq, c, positions, odd_norm_g[i], odd_w_mod[i], odd_b_mod[i],
                            odd_w_in[i], odd_c_w[i], odd_c_scale[i], odd_d_qnorm_g[i],
                            odd_d_knorm_g[i], odd_w_out[i])
    return x2.reshape(batch, seq, d)
```

```python
import functools

import jax
import jax.numpy as jnp
from jax import lax
from jax.experimental import pallas as pl
from jax.experimental.pallas import tpu as pltpu

F32 = jnp.float32
BF16 = jnp.bfloat16

EPS = 1e-6
ROPE_BASE = 10000.0
BLOCK = 128
HEAD_DIM = 128
N_HEADS = 8
POOL_WINDOWS = (2, 4, 8, 16)
POOL_GROUP_DIM = 256
POOL_HALO = 16

MOD_ROWS = 8

ROW_TILE = 512
OUT_EVEN_ROWS = 1024
COL_TILE = 1024
RET_CHUNK = 128

ATT_Q = 64
ATT_WIN = 384
ATT_NEW = 256
ATT_EXCLUDED = -1e30
LANES = 128
ATT_SUB = 2
ATT_HEADS_PER_STEP = 4
ATT_SKIP_BELOW = -151.0
LOG2E = 1.4426950408889634
ATT_NO_LIMIT = 1 << 30

VMEM_LIMIT = 56 * 1024 * 1024


def _silu(v):
    return v * jax.nn.sigmoid(v)


def _dot(a, b):
    return jnp.dot(a, b, preferred_element_type=F32)


def _dot_nt(a, b):
    return lax.dot_general(a, b, (((1,), (1,)), ((), ())), preferred_element_type=F32)


def _dot_tn(a, b):
    return lax.dot_general(a, b, (((0,), (0,)), ((), ())), preferred_element_type=F32)


def _group_rms(a, gain):
    ms = jnp.mean(a * a, axis=-1, keepdims=True)
    return a * lax.rsqrt(ms + EPS) * gain


def _mod_kernel(c_ref, w_ref, b_ref, o_ref):
    a = _silu(c_ref[...])
    o_ref[...] = _dot(a.astype(BF16), w_ref[...].astype(BF16)) + b_ref[...]


def _modulation(c, w_mod, b_mod):
    bsz, d = c.shape
    c8 = jnp.pad(c, ((0, MOD_ROWS - bsz), (0, 0)))
    m = pl.pallas_call(
        _mod_kernel,
        grid=(3,),
        in_specs=[pl.BlockSpec((MOD_ROWS, d), lambda j: (0, 0)),
                  pl.BlockSpec((d, d), lambda j: (0, j)),
                  pl.BlockSpec((1, d), lambda j: (0, j))],
        out_specs=pl.BlockSpec((MOD_ROWS, d), lambda j: (0, j)),
        out_shape=jax.ShapeDtypeStruct((MOD_ROWS, 3 * d), F32),
        name="modulation",
    )(c8, w_mod, b_mod.reshape(1, 3 * d))
    m = m[:bsz]
    return (m[:, None, :d], m[:, None, d:2 * d], m[:, None, 2 * d:])


def _inproj_kernel(*refs, modes, post_scale, use_rope):
    *refs, h_scr, w_ref, stage, sem = refs
    w_hbm = refs[4]
    tn = COL_TILE

    def weight_copy(j):
        return pltpu.make_async_copy(w_hbm.at[:, pl.ds(j * tn, tn)], stage.at[j % 2], sem.at[j % 2])

    @pl.when(pl.program_id(0) == 0)
    def _():
        weight_copy(0).start()
        for j in range(len(modes)):
            if j + 1 < len(modes):
                weight_copy(j + 1).start()
            weight_copy(j).wait()
            w_ref[:, j * tn:(j + 1) * tn] = stage[j % 2].astype(BF16)

    if use_rope:
        x_ref, g_ref, sc_ref, sh_ref, _, eg_ref, pos_ref, freq_ref, sign_ref, o_ref = refs
        ang = pos_ref[...] * freq_ref[...]
        cos_t = jnp.cos(ang)
        sin_t = jnp.sin(ang) * sign_ref[...]
    else:
        x_ref, g_ref, sc_ref, sh_ref, _, eg_ref, o_ref = refs
    x = x_ref[...]
    ms = jnp.mean(x * x, axis=-1, keepdims=True)
    y = x * lax.rsqrt(ms + EPS) * g_ref[...]
    h_scr[...] = (y * (1.0 + sc_ref[...]) + sh_ref[...]).astype(BF16)

    for jj, mode in enumerate(modes):
        acc = _dot(h_scr[...], w_ref[:, jj * tn:(jj + 1) * tn])
        if mode == "raw":
            o_ref[:, jj * tn:(jj + 1) * tn] = acc.astype(BF16)
            continue
        for g in range(tn // HEAD_DIM):
            cols = slice(jj * tn + g * HEAD_DIM, jj * tn + (g + 1) * HEAD_DIM)
            t = _group_rms(acc[:, g * HEAD_DIM:(g + 1) * HEAD_DIM], eg_ref[:, cols])
            if mode != "norm":
                t = t * cos_t + pltpu.roll(t, HEAD_DIM // 2, axis=1) * sin_t
            if mode == "rope_scaled":
                t = t * post_scale
            o_ref[:, cols] = t.astype(BF16)


def _in_projection(x2, seq, norm_g, scale, shift, w_in, ep_gain, modes, rope=None, post_scale=1.0):
    n, d = x2.shape
    tm, tn = ROW_TILE, COL_TILE
    ncols = len(modes) * tn
    use_rope = rope is not None
    in_specs = [
        pl.BlockSpec((tm, d), lambda i: (i, 0)),
        pl.BlockSpec((1, d), lambda i: (0, 0)),
        pl.BlockSpec((None, 1, d), lambda i: ((i * tm) // seq, 0, 0)),
        pl.BlockSpec((None, 1, d), lambda i: ((i * tm) // seq, 0, 0)),
        pl.BlockSpec(memory_space=pl.ANY),
        pl.BlockSpec((1, ncols), lambda i: (0, 0)),
    ]
    args = [x2, norm_g.reshape(1, d), scale, shift, w_in, ep_gain]
    if use_rope:
        vec = pl.BlockSpec((1, HEAD_DIM), lambda i: (0, 0))
        in_specs += [pl.BlockSpec((tm, 1), lambda i: (i, 0)), vec, vec]
        args += list(rope)
    return pl.pallas_call(
        functools.partial(_inproj_kernel, modes=tuple(modes), post_scale=post_scale,
                          use_rope=use_rope),
        grid=(n // tm,),
        in_specs=in_specs,
        out_specs=pl.BlockSpec((tm, ncols), lambda i: (i, 0)),
        out_shape=jax.ShapeDtypeStruct((n, ncols), BF16),
        scratch_shapes=[pltpu.VMEM((tm, d), BF16),
                        pltpu.VMEM((d, ncols), BF16),
                        pltpu.VMEM((2, d, tn), F32),
                        pltpu.SemaphoreType.DMA((2,))],
        compiler_params=pltpu.CompilerParams(
            dimension_semantics=("arbitrary",), vmem_limit_bytes=VMEM_LIMIT),
        name="in_projection",
    )(*args)


def _attn_kernel(q_ref, k_ref, v_ref, o_ref, u_scr, *, seq, scale):
    hb, tq, win, nsub = ATT_HEADS_PER_STEP, ATT_Q, ATT_WIN, ATT_SUB
    chains = nsub * hb

    u_scr[...] = jnp.where(lax.broadcasted_iota(jnp.int32, (ATT_NEW, ATT_NEW), 0)
                           > lax.broadcasted_iota(jnp.int32, (ATT_NEW, ATT_NEW), 1),
                           1.0, 0.0).astype(BF16)
    old = win - ATT_NEW
    key_idx = lax.broadcasted_iota(jnp.int32, (tq, win), 1)
    row_idx = lax.broadcasted_iota(jnp.int32, (tq, win), 0)
    tail = win - LANES
    tail_ok = (lax.broadcasted_iota(jnp.int32, (tq, LANES), 1) + tail
               < lax.broadcasted_iota(jnp.int32, (tq, LANES), 0) + (win - tq))

    def head_cols(h):
        return slice(h * HEAD_DIM, (h + 1) * HEAD_DIM)

    def tile_logits(rows0, keys0):
        return [_dot_nt(q_ref[pl.ds(rows0[a], tq), head_cols(h)],
                        k_ref[pl.ds(keys0[a], win), head_cols(h)]) * (scale * LOG2E)
                for a in range(nsub) for h in range(hb)]

    def tile_weights(logits, rows0, keys0, limits, totals):
        log_beta, terms = [], []
        for a in range(nsub):
            if limits is not None:
                ok = key_idx < (jnp.minimum(rows0[a] + row_idx, limits[a]) - keys0[a])
            for h in range(hb):
                z = logits[a * hb + h]
                if limits is None:
                    z = jnp.concatenate(
                        [z[:, :tail], jnp.where(tail_ok, z[:, tail:], ATT_EXCLUDED)], axis=1)
                else:
                    z = jnp.where(ok, z, ATT_EXCLUDED)
                nz = -z
                lsm = (jnp.minimum(nz, 0.0)
                       - jnp.log(1.0 + jnp.exp2(jnp.minimum(z, nz))) * LOG2E)
                log_beta.append(z + lsm)
                terms.append(lsm)
        sums_new = _dot(jnp.concatenate([t[:, old:].astype(BF16) for t in terms], axis=0),
                        u_scr[...])
        sums_old = _dot(jnp.concatenate([t[:, :old].astype(BF16) for t in terms], axis=0),
                        u_scr[0:old, 0:old])
        weights, tile_totals = [], []
        for c in range(chains):
            newer = sums_new[c * tq:(c + 1) * tq]
            older = sums_old[c * tq:(c + 1) * tq]
            total_new = newer[:, 0:1] + terms[c][:, old:old + 1]
            expo = log_beta[c] + jnp.concatenate([older + total_new, newer], axis=1)
            if totals is not None:
                expo = expo + totals[c]
            weights.append(jnp.exp2(expo).astype(BF16))
            tile_totals.append(total_new + older[:, 0:1] + terms[c][:, 0:1])
        return weights, tile_totals

    def tile_values(weights, keys0):
        return [_dot(weights[a * hb + h], v_ref[pl.ds(keys0[a], win), head_cols(h)])
                for a in range(nsub) for h in range(hb)]

    def slowest(totals):
        m = totals[0]
        for t in totals[1:]:
            m = jnp.maximum(m, t)
        return jnp.max(m)

    def qblock(i, carry, *, near_start):
        r0 = i * (tq * nsub)
        rows0 = [pl.multiple_of(r0 + a * tq, tq) for a in range(nsub)]
        if near_start:
            starts = [jnp.maximum(r - (win - tq), 0) for r in rows0]
        else:
            starts = [r - (win - tq) for r in rows0]
        keys0 = [pl.multiple_of(s, tq) for s in starts]
        weights, totals = tile_weights(tile_logits(rows0, keys0), rows0, keys0,
                                       [ATT_NO_LIMIT] * nsub if near_start else None, None)
        accs = tile_values(weights, keys0)

        def more(st):
            m, worst, _, _ = st
            return (starts[-1] - m * win > 0) & (worst > ATT_SKIP_BELOW)

        def older(st):
            m, _, totals, accs = st
            limits = [s - m * win for s in starts]
            keys0 = [pl.multiple_of(jnp.maximum(lim - win, 0), tq) for lim in limits]
            weights, tile_totals = tile_weights(tile_logits(rows0, keys0), rows0, keys0,
                                                limits, totals)
            outs = tile_values(weights, keys0)
            totals = tuple(t + d for t, d in zip(totals, tile_totals))
            accs = tuple(acc + o for acc, o in zip(accs, outs))
            return (m + 1, slowest(totals), totals, accs)

        _, _, _, accs = lax.while_loop(
            more, older, (jnp.int32(0), slowest(totals), tuple(totals), tuple(accs)))
        for a in range(nsub):
            for h in range(hb):
                o_ref[pl.ds(rows0[a], tq), head_cols(h)] = accs[a * hb + h].astype(BF16)
        return carry

    n_blocks = seq // (tq * nsub)
    n_clipped = min(-(-(win - tq) // (tq * nsub)), n_blocks)
    lax.fori_loop(0, n_clipped, functools.partial(qblock, near_start=True), 0)
    lax.fori_loop(n_clipped, n_blocks, functools.partial(qblock, near_start=False), 0)


def _stick_breaking(p, batch, seq, q_col, k_col, v_col):
    n = p.shape[0]
    width = ATT_HEADS_PER_STEP * HEAD_DIM
    n_hg = N_HEADS // ATT_HEADS_PER_STEP

    def spec(col):
        return pl.BlockSpec((seq, width), lambda b, hg: (b, col // width + hg))

    return pl.pallas_call(
        functools.partial(_attn_kernel, seq=seq, scale=HEAD_DIM ** -0.5),
        grid=(batch, n_hg),
        in_specs=[spec(q_col), spec(k_col), spec(v_col)],
        out_specs=pl.BlockSpec((seq, width), lambda b, hg: (b, hg)),
        out_shape=jax.ShapeDtypeStruct((n, N_HEADS * HEAD_DIM), BF16),
        scratch_shapes=[pltpu.VMEM((ATT_NEW, ATT_NEW), BF16)],
        compiler_params=pltpu.CompilerParams(
            dimension_semantics=("parallel", "parallel"), vmem_limit_bytes=VMEM_LIMIT),
        name="stick_breaking",
    )(p, p, p)


def _out_even_kernel(x_ref, gate_ref, uv_ref, za_ref, zb_ref, bo_ref, ws_ref, bias_ref, wout_ref,
                     o_ref, y_scr):
    tm = x_ref.shape[0]
    a_width = bo_ref.shape[1]
    row = lax.broadcasted_iota(jnp.int32, (BLOCK, BLOCK), 0)
    col = lax.broadcasted_iota(jnp.int32, (BLOCK, BLOCK), 1)
    for g in range(a_width // HEAD_DIM):
        cols = slice(g * HEAD_DIM, (g + 1) * HEAD_DIM)
        vcols = slice(a_width + g * HEAD_DIM, a_width + (g + 1) * HEAD_DIM)
        wg = jnp.where(col <= row, ws_ref[g], 0.0).astype(BF16)
        for c in range(tm // BLOCK):
            rows = slice(c * BLOCK, (c + 1) * BLOCK)
            mixed = _dot(wg, uv_ref[rows, vcols]) + bias_ref[:, cols]
            gated = uv_ref[rows, cols].astype(F32) * mixed * _silu(za_ref[rows, cols].astype(F32))
            y_scr[rows, cols] = gated.astype(BF16)
    y_scr[:, a_width:] = (bo_ref[...].astype(F32) * _silu(zb_ref[...].astype(F32))).astype(BF16)
    o_ref[...] = x_ref[...] + gate_ref[...] * _dot(y_scr[...], wout_ref[...])


def _out_even(x2, seq, gate, p, b_out, a_ws, bias_full, w_out_bf):
    n, d = x2.shape
    tm = OUT_EVEN_ROWS
    aw = b_out.shape[1]
    row = lambda c: pl.BlockSpec((tm, aw), lambda i: (i, c))
    const = lambda a: pl.BlockSpec(a.shape, lambda i: (0,) * a.ndim, pipeline_mode=pl.Buffered(1))
    return pl.pallas_call(
        _out_even_kernel,
        grid=(n // tm,),
        in_specs=[pl.BlockSpec((tm, d), lambda i: (i, 0)),
                  pl.BlockSpec((None, 1, d), lambda i: ((i * tm) // seq, 0, 0)),
                  pl.BlockSpec((tm, 2 * aw), lambda i: (i, 0)),
                  row(5), row(6),
                  row(0),
                  const(a_ws), const(bias_full), const(w_out_bf)],
        out_specs=pl.BlockSpec((tm, d), lambda i: (i, 0)),
        out_shape=jax.ShapeDtypeStruct((n, d), F32),
        scratch_shapes=[pltpu.VMEM((tm, 2 * aw), BF16)],
        compiler_params=pltpu.CompilerParams(
            dimension_semantics=("parallel",), vmem_limit_bytes=VMEM_LIMIT),
        name="out_even",
    )(x2, gate, p, p, p, b_out, a_ws, bias_full, w_out_bf)


def _rope_operands(positions):
    half = HEAD_DIM // 2
    inv_freq = ROPE_BASE ** (-jnp.arange(half, dtype=F32) / half)
    freq = jnp.concatenate([inv_freq, inv_freq]).reshape(1, HEAD_DIM)
    sign = jnp.concatenate([-jnp.ones((half,), F32), jnp.ones((half,), F32)]).reshape(1, HEAD_DIM)
    return positions.reshape(-1, 1).astype(F32), freq, sign


def _retention_tables():
    chunk = RET_CHUNK
    log_gamma = jnp.log1p(-jnp.exp2(-5.0 - jnp.arange(N_HEADS, dtype=F32)))
    idx = jnp.arange(chunk, dtype=F32)
    diff = idx[:, None] - idx[None, :]
    intra = jnp.where(diff >= 0, jnp.exp(log_gamma[:, None, None] * jnp.maximum(diff, 0.0)), 0.0)
    q_decay = jnp.exp(log_gamma[:, None] * (idx + 1.0))
    k_decay = jnp.exp(log_gamma[:, None] * (chunk - 1.0 - idx))
    chunk_decay = jnp.exp(log_gamma * chunk)
    return (intra,
            jnp.broadcast_to(q_decay[:, :, None], (N_HEADS, chunk, HEAD_DIM)),
            jnp.broadcast_to(k_decay[:, :, None], (N_HEADS, chunk, HEAD_DIM)),
            jnp.broadcast_to(chunk_decay[:, None, None], (N_HEADS, HEAD_DIM, HEAD_DIM)))


def _retention_rows(q_ref, k_ref, v_ref, idec_ref, qdec_ref, kdec_ref, cdec_ref, st_scr, emit):
    chunk = RET_CHUNK
    n_chunks = q_ref.shape[0] // chunk
    cols = [slice(h * HEAD_DIM, (h + 1) * HEAD_DIM) for h in range(N_HEADS)]
    rows = [slice(c * chunk, (c + 1) * chunk) for c in range(n_chunks)]

    kv = {}
    for c in range(n_chunks):
        for h in range(N_HEADS):
            k_decayed = (k_ref[rows[c], cols[h]].astype(F32) * kdec_ref[h]).astype(BF16)
            kv[h, c] = _dot_tn(k_decayed, v_ref[rows[c], cols[h]])
    state = {}
    for h in range(N_HEADS):
        s = st_scr[h]
        for c in range(n_chunks):
            state[h, c] = s.astype(BF16)
            s = s * cdec_ref[h] + kv[h, c]
        st_scr[h] = s
    for c in range(n_chunks):
        for h in range(N_HEADS):
            q = q_ref[rows[c], cols[h]]
            scores = _dot_nt(q, k_ref[rows[c], cols[h]]) * idec_ref[h]
            out = (_dot(scores.astype(BF16), v_ref[rows[c], cols[h]])
                   + _dot(q, state[h, c]) * qdec_ref[h])
            ms = jnp.mean(out * out, axis=-1, keepdims=True)
            emit(rows[c], h, out * lax.rsqrt(ms + EPS))


def _pooled_rows(pc_ref, halo_ref, cw_ref, cs_ref, ext_scr, seq_row0, emit):
    tm = pc_ref.shape[0]
    ext_scr[0:POOL_HALO, :] = jnp.where(seq_row0 == 0, jnp.zeros_like(halo_ref), halo_ref[...])
    ext_scr[POOL_HALO:, :] = pc_ref[...]
    src_rows = BLOCK + POOL_HALO
    lag = (lax.broadcasted_iota(jnp.int32, (BLOCK, src_rows), 0) + POOL_HALO
           - lax.broadcasted_iota(jnp.int32, (BLOCK, src_rows), 1))
    t_seq = seq_row0 + lax.broadcasted_iota(jnp.int32, (BLOCK, POOL_GROUP_DIM), 0)
    for gi, win in enumerate(POOL_WINDOWS):
        cols = slice(gi * POOL_GROUP_DIM, (gi + 1) * POOL_GROUP_DIM)
        band = jnp.where((lag >= 0) & (lag < win), 1.0 / win, 0.0).astype(BF16)
        short = jnp.where(t_seq + 1 < win, win / (t_seq + 1).astype(F32), 1.0)
        pooled = []
        for c in range(tm // BLOCK):
            mean = _dot(band, ext_scr[c * BLOCK:c * BLOCK + src_rows, cols])
            if c == 0:
                mean = mean * short
            cur = pc_ref[c * BLOCK:(c + 1) * BLOCK, cols].astype(F32)
            pooled.append((mean - cur).astype(BF16))
        emit(gi, _dot(jnp.concatenate(pooled, axis=0), cw_ref[gi]) * cs_ref[:, cols])


def _odd_mixers_kernel(p_ref, idec_ref, qdec_ref, kdec_ref, cdec_ref,
                       x_ref, gate_ref, halo_ref, cw_ref, cs_ref, wout_ref,
                       o_ref, st_scr, ext_scr, y_scr):
    tm = x_ref.shape[0]
    c_width = halo_ref.shape[1]
    pc_ref, q_ref, k_ref, v_ref, za_ref, zb_ref = (
        p_ref.at[:, pl.ds(j * c_width, c_width)] for j in range(6))

    @pl.when(pl.program_id(1) == 0)
    def _():
        st_scr[...] = jnp.zeros_like(st_scr)

    def gated_retention(rows, h, out):
        cols = slice(h * HEAD_DIM, (h + 1) * HEAD_DIM)
        ycols = slice(c_width + h * HEAD_DIM, c_width + (h + 1) * HEAD_DIM)
        y_scr[rows, ycols] = (out * _silu(zb_ref[rows, cols].astype(F32))).astype(BF16)

    def gated_pool(gi, mixed):
        cols = slice(gi * POOL_GROUP_DIM, (gi + 1) * POOL_GROUP_DIM)
        y_scr[:, cols] = (mixed * _silu(za_ref[:, cols].astype(F32))).astype(BF16)

    _retention_rows(q_ref, k_ref, v_ref, idec_ref, qdec_ref, kdec_ref, cdec_ref, st_scr,
                    gated_retention)
    _pooled_rows(pc_ref, halo_ref, cw_ref, cs_ref, ext_scr, pl.program_id(1) * tm, gated_pool)
    o_ref[...] = x_ref[...] + gate_ref[...] * _dot(y_scr[...], wout_ref[...])


def _odd_mixers(x2, batch, seq, gate, p, c_w_bf, c_scale, w_out_bf):
    n, d = x2.shape
    tm = ROW_TILE
    cw = N_HEADS * HEAD_DIM
    steps = seq // tm
    tables = _retention_tables()
    tabs = [pl.BlockSpec(t.shape, lambda b, s: (0, 0, 0)) for t in tables]
    halo_blocks = tm // POOL_HALO
    halo = pl.BlockSpec(
        (POOL_HALO, cw), lambda b, s: (jnp.maximum((b * steps + s) * halo_blocks - 1, 0), 0))
    return pl.pallas_call(
        _odd_mixers_kernel,
        grid=(batch, steps),
        in_specs=[pl.BlockSpec((tm, p.shape[1]), lambda b, s: (b * steps + s, 0))] + tabs + [
            pl.BlockSpec((tm, d), lambda b, s: (b * steps + s, 0)),
            pl.BlockSpec((None, 1, d), lambda b, s: (b, 0, 0)),
            halo,
            pl.BlockSpec(c_w_bf.shape, lambda b, s: (0, 0, 0)),
            pl.BlockSpec((1, cw), lambda b, s: (0, 0)),
            pl.BlockSpec(w_out_bf.shape, lambda b, s: (0, 0))],
        out_specs=pl.BlockSpec((tm, d), lambda b, s: (b * steps + s, 0)),
        out_shape=jax.ShapeDtypeStruct((n, d), F32),
        scratch_shapes=[pltpu.VMEM((N_HEADS, HEAD_DIM, HEAD_DIM), F32),
                        pltpu.VMEM((POOL_HALO + tm, cw), BF16),
                        pltpu.VMEM((tm, 2 * cw), BF16)],
        compiler_params=pltpu.CompilerParams(
            dimension_semantics=("parallel", "arbitrary"), vmem_limit_bytes=VMEM_LIMIT),
        name="odd_mixers",
    )(p, *tables, x2, gate, p, c_w_bf, c_scale.reshape(1, cw), w_out_bf)


def _even_layer(x2, batch, seq, c, norm_g, w_mod, b_mod, w_in, a_vnorm_g, a_ws, a_bs,
                b_qnorm_g, b_knorm_g, w_out):
    d = x2.shape[1]
    shift, scale, gate = _modulation(c, w_mod, b_mod)
    ones = jnp.ones((d,), F32)
    ep_gain = jnp.concatenate([
        ones, a_vnorm_g.reshape(-1), jnp.tile(b_qnorm_g, N_HEADS), jnp.tile(b_knorm_g, N_HEADS),
        ones, ones, ones]).reshape(1, -1)
    modes = ("raw", "norm", "norm", "norm", "raw", "raw", "raw")
    p = _in_projection(x2, seq, norm_g, scale, shift, w_in, ep_gain, modes)
    b_out = _stick_breaking(p, batch, seq, q_col=2 * d, k_col=3 * d, v_col=4 * d)
    bias_full = jnp.repeat(a_bs.T, HEAD_DIM, axis=1)
    return _out_even(x2, seq, gate, p, b_out, a_ws, bias_full, w_out.astype(BF16))


def _odd_layer(x2, batch, seq, c, positions, norm_g, w_mod, b_mod, w_in, c_w, c_scale,
               d_qnorm_g, d_knorm_g, w_out):
    d = x2.shape[1]
    shift, scale, gate = _modulation(c, w_mod, b_mod)
    ones = jnp.ones((d,), F32)
    ep_gain = jnp.concatenate([
        ones, jnp.tile(d_qnorm_g, N_HEADS), jnp.tile(d_knorm_g, N_HEADS),
        ones, ones, ones]).reshape(1, -1)
    modes = ("raw", "rope", "rope_scaled", "raw", "raw", "raw")
    p = _in_projection(x2, seq, norm_g, scale, shift, w_in, ep_gain, modes,
                       rope=_rope_operands(positions), post_scale=HEAD_DIM ** -0.5)
    return _odd_mixers(x2, batch, seq, gate, p, c_w.astype(BF16), c_scale, w_out.astype(BF16))


def kernel(x, c, positions, even_norm_g, even_w_mod, even_b_mod, even_w_in, even_a_vnorm_g, even_a_ws, even_a_bs, even_b_qnorm_g, even_b_knorm_g, even_w_out, odd_norm_g, odd_w_mod, odd_b_mod, odd_w_in, odd_c_w, odd_c_scale, odd_d_qnorm_g, odd_d_knorm_g, odd_w_out):
    batch, seq, d = x.shape
    depth = even_norm_g.shape[0] + odd_norm_g.shape[0]
    x2 = x.reshape(batch * seq, d)
    for layer in range(depth):
        i = layer // 2
        if layer % 2 == 0:
            x2 = _even_layer(x2, batch, seq, c, even_norm_g[i], even_w_mod[i], even_b_mod[i],
                             even_w_in[i], even_a_vnorm_g[i], even_a_ws[i], even_a_bs[i],
                             even_b_qnorm_g[i], even_b_knorm_g[i], even_w_out[i])
        else:
            x2 = _odd_layer(x2, batch, seq, c, positions, odd_norm_g[i], odd_w_mod[i], odd_b_mod[i],
                            odd_w_in[i], odd_c_w[i], odd_c_scale[i], odd_d_qnorm_g[i],
                            odd_d_knorm_g[i], odd_w_out[i])
    return x2.reshape(batch, seq, d)
```

```python
import functools

import jax
import jax.numpy as jnp
from jax import lax
from jax.experimental import pallas as pl
from jax.experimental.pallas import tpu as pltpu

F32 = jnp.float32
BF16 = jnp.bfloat16

EPS = 1e-6
ROPE_BASE = 10000.0
BLOCK = 128
HEAD_DIM = 128
N_HEADS = 8
POOL_WINDOWS = (2, 4, 8, 16)
POOL_GROUP_DIM = 256
POOL_HALO = 16

MOD_ROWS = 8

ROW_TILE = 512
OUT_EVEN_ROWS = 1024
COL_TILE = 1024
RET_CHUNK = 128

ATT_Q = 64
ATT_WIN = 384
ATT_NEW = 256
ATT_EXCLUDED = -1e30
LANES = 128
ATT_SUB = 2
ATT_HEADS_PER_STEP = 4
ATT_SKIP_BELOW = -151.0
LOG2E = 1.4426950408889634
ATT_NO_LIMIT = 1 << 30

VMEM_LIMIT = 56 * 1024 * 1024


def _silu(v):
    return v * jax.nn.sigmoid(v)


def _dot(a, b):
    return jnp.dot(a, b, preferred_element_type=F32)


def _dot_nt(a, b):
    return lax.dot_general(a, b, (((1,), (1,)), ((), ())), preferred_element_type=F32)


def _dot_tn(a, b):
    return lax.dot_general(a, b, (((0,), (0,)), ((), ())), preferred_element_type=F32)


def _group_rms(a, gain):
    ms = jnp.mean(a * a, axis=-1, keepdims=True)
    return a * lax.rsqrt(ms + EPS) * gain


def _mod_kernel(c_ref, w_ref, b_ref, o_ref):
    a = _silu(c_ref[...])
    o_ref[...] = _dot(a.astype(BF16), w_ref[...].astype(BF16)) + b_ref[...]


def _modulation(c, w_mod, b_mod):
    bsz, d = c.shape
    c8 = jnp.pad(c, ((0, MOD_ROWS - bsz), (0, 0)))
    m = pl.pallas_call(
        _mod_kernel,
        grid=(3,),
        in_specs=[pl.BlockSpec((MOD_ROWS, d), lambda j: (0, 0)),
                  pl.BlockSpec((d, d), lambda j: (0, j)),
                  pl.BlockSpec((1, d), lambda j: (0, j))],
        out_specs=pl.BlockSpec((MOD_ROWS, d), lambda j: (0, j)),
        out_shape=jax.ShapeDtypeStruct((MOD_ROWS, 3 * d), F32),
        name="modulation",
    )(c8, w_mod, b_mod.reshape(1, 3 * d))
    m = m[:bsz]
    return (m[:, None, :d], m[:, None, d:2 * d], m[:, None, 2 * d:])


def _inproj_kernel(*refs, modes, out_tiles, post_scale, use_rope):
    *refs, h_scr, w_ref, stage, sem, held = refs
    w_hbm = refs[4]
    tn = COL_TILE

    def weight_copy(j):
        return pltpu.make_async_copy(w_hbm.at[:, pl.ds(j * tn, tn)], stage.at[j % 2], sem.at[j % 2])

    @pl.when(pl.program_id(0) == 0)
    def _():
        weight_copy(0).start()
        for j in range(len(modes)):
            if j + 1 < len(modes):
                weight_copy(j + 1).start()
            weight_copy(j).wait()
            w_ref[:, j * tn:(j + 1) * tn] = stage[j % 2].astype(BF16)

    if use_rope:
        x_ref, g_ref, sc_ref, sh_ref, _, eg_ref, pos_ref, freq_ref, sign_ref, o_ref = refs
        ang = pos_ref[...] * freq_ref[...]
        cos_t = jnp.cos(ang)
        sin_t = jnp.sin(ang) * sign_ref[...]
    else:
        x_ref, g_ref, sc_ref, sh_ref, _, eg_ref, o_ref = refs
    x = x_ref[...]
    ms = jnp.mean(x * x, axis=-1, keepdims=True)
    y = x * lax.rsqrt(ms + EPS) * g_ref[...]
    h_scr[...] = (y * (1.0 + sc_ref[...]) + sh_ref[...]).astype(BF16)

    cross_lane = ("norm", "rope", "rope_scaled")
    heavy = [jj for jj, m in enumerate(modes) if m in cross_lane]
    plain = [jj for jj, m in enumerate(modes) if m not in cross_lane]
    order = []
    while heavy or plain:
        order += heavy[:1] + plain[:1]
        heavy, plain = heavy[1:], plain[1:]
    for jj in order:
        mode = modes[jj]
        out0 = None if out_tiles[jj] is None else out_tiles[jj] * tn
        acc = _dot(h_scr[...], w_ref[:, jj * tn:(jj + 1) * tn])
        if mode == "hold":
            held[...] = acc
            continue
        if mode == "raw":
            o_ref[:, out0:out0 + tn] = acc.astype(BF16)
            continue
        if mode == "silu_gate":
            o_ref[:, out0:out0 + tn] = (held[...] * _silu(acc)).astype(BF16)
            continue
        for g in range(tn // HEAD_DIM):
            cols = slice(jj * tn + g * HEAD_DIM, jj * tn + (g + 1) * HEAD_DIM)
            t = _group_rms(acc[:, g * HEAD_DIM:(g + 1) * HEAD_DIM], eg_ref[:, cols])
            if mode != "norm":
                t = t * cos_t + pltpu.roll(t, HEAD_DIM // 2, axis=1) * sin_t
            if mode == "rope_scaled":
                t = t * post_scale
            o_ref[:, out0 + g * HEAD_DIM:out0 + (g + 1) * HEAD_DIM] = t.astype(BF16)


def _in_projection(x2, seq, norm_g, scale, shift, w_in, ep_gain, modes, rope=None, post_scale=1.0):
    n, d = x2.shape
    tm, tn = ROW_TILE, COL_TILE
    ncols = len(modes) * tn
    out_tiles, n_out = [], 0
    for m in modes:
        out_tiles.append(None if m == "hold" else n_out)
        n_out += m != "hold"
    assert modes.count("hold") == modes.count("silu_gate") <= 1
    assert "hold" not in modes or modes.index("hold") < modes.index("silu_gate")
    use_rope = rope is not None
    in_specs = [
        pl.BlockSpec((tm, d), lambda i: (i, 0)),
        pl.BlockSpec((1, d), lambda i: (0, 0)),
        pl.BlockSpec((None, 1, d), lambda i: ((i * tm) // seq, 0, 0)),
        pl.BlockSpec((None, 1, d), lambda i: ((i * tm) // seq, 0, 0)),
        pl.BlockSpec(memory_space=pl.ANY),
        pl.BlockSpec((1, ncols), lambda i: (0, 0)),
    ]
    args = [x2, norm_g.reshape(1, d), scale, shift, w_in, ep_gain]
    if use_rope:
        vec = pl.BlockSpec((1, HEAD_DIM), lambda i: (0, 0))
        in_specs += [pl.BlockSpec((tm, 1), lambda i: (i, 0)), vec, vec]
        args += list(rope)
    return pl.pallas_call(
        functools.partial(_inproj_kernel, modes=tuple(modes), out_tiles=tuple(out_tiles),
                          post_scale=post_scale, use_rope=use_rope),
        grid=(n // tm,),
        in_specs=in_specs,
        out_specs=pl.BlockSpec((tm, n_out * tn), lambda i: (i, 0)),
        out_shape=jax.ShapeDtypeStruct((n, n_out * tn), BF16),
        scratch_shapes=[pltpu.VMEM((tm, d), BF16),
                        pltpu.VMEM((d, ncols), BF16),
                        pltpu.VMEM((2, d, tn), F32),
                        pltpu.SemaphoreType.DMA((2,)),
                        pltpu.VMEM((tm, tn), F32)],
        compiler_params=pltpu.CompilerParams(
            dimension_semantics=("arbitrary",), vmem_limit_bytes=VMEM_LIMIT),
        name="in_projection",
    )(*args)


def _attn_kernel(q_ref, k_ref, v_ref, o_ref, u_scr, *, seq, scale):
    hb, tq, win, nsub = ATT_HEADS_PER_STEP, ATT_Q, ATT_WIN, ATT_SUB
    chains = nsub * hb

    u_scr[...] = jnp.where(lax.broadcasted_iota(jnp.int32, (ATT_NEW, ATT_NEW), 0)
                           > lax.broadcasted_iota(jnp.int32, (ATT_NEW, ATT_NEW), 1),
                           1.0, 0.0).astype(BF16)
    old = win - ATT_NEW
    key_idx = lax.broadcasted_iota(jnp.int32, (tq, win), 1)
    row_idx = lax.broadcasted_iota(jnp.int32, (tq, win), 0)
    tail = win - LANES
    tail_ok = (lax.broadcasted_iota(jnp.int32, (tq, LANES), 1) + tail
               < lax.broadcasted_iota(jnp.int32, (tq, LANES), 0) + (win - tq))

    def head_cols(h):
        return slice(h * HEAD_DIM, (h + 1) * HEAD_DIM)

    def tile_logits(rows0, keys0):
        return [_dot_nt(q_ref[pl.ds(rows0[a], tq), head_cols(h)],
                        k_ref[pl.ds(keys0[a], win), head_cols(h)]) * (scale * LOG2E)
                for a in range(nsub) for h in range(hb)]

    def tile_weights(logits, rows0, keys0, limits, totals):
        log_beta, terms = [], []
        for a in range(nsub):
            if limits is not None:
                ok = key_idx < (jnp.minimum(rows0[a] + row_idx, limits[a]) - keys0[a])
            for h in range(hb):
                z = logits[a * hb + h]
                if limits is None:
                    z = jnp.concatenate(
                        [z[:, :tail], jnp.where(tail_ok, z[:, tail:], ATT_EXCLUDED)], axis=1)
                else:
                    z = jnp.where(ok, z, ATT_EXCLUDED)
                nz = -z
                lsm = (jnp.minimum(nz, 0.0)
                       - jnp.log(1.0 + jnp.exp2(jnp.minimum(z, nz))) * LOG2E)
                log_beta.append(z + lsm)
                terms.append(lsm)
        sums_new = _dot(jnp.concatenate([t[:, old:].astype(BF16) for t in terms], axis=0),
                        u_scr[...])
        sums_old = _dot(jnp.concatenate([t[:, :old].astype(BF16) for t in terms], axis=0),
                        u_scr[0:old, 0:old])
        weights, tile_totals = [], []
        for c in range(chains):
            newer = sums_new[c * tq:(c + 1) * tq]
            older = sums_old[c * tq:(c + 1) * tq]
            total_new = newer[:, 0:1] + terms[c][:, old:old + 1]
            expo = log_beta[c] + jnp.concatenate([older + total_new, newer], axis=1)
            if totals is not None:
                expo = expo + totals[c]
            weights.append(jnp.exp2(expo).astype(BF16))
            tile_totals.append(total_new + older[:, 0:1] + terms[c][:, 0:1])
        return weights, tile_totals

    def tile_values(weights, keys0):
        return [_dot(weights[a * hb + h], v_ref[pl.ds(keys0[a], win), head_cols(h)])
                for a in range(nsub) for h in range(hb)]

    def slowest(totals):
        m = totals[0]
        for t in totals[1:]:
            m = jnp.maximum(m, t)
        return jnp.max(m)

    def qblock(i, carry, *, near_start):
        r0 = i * (tq * nsub)
        rows0 = [pl.multiple_of(r0 + a * tq, tq) for a in range(nsub)]
        if near_start:
            starts = [jnp.maximum(r - (win - tq), 0) for r in rows0]
        else:
            starts = [r - (win - tq) for r in rows0]
        keys0 = [pl.multiple_of(s, tq) for s in starts]
        weights, totals = tile_weights(tile_logits(rows0, keys0), rows0, keys0,
                                       [ATT_NO_LIMIT] * nsub if near_start else None, None)
        accs = tile_values(weights, keys0)

        def more(st):
            m, worst, _, _ = st
            return (starts[-1] - m * win > 0) & (worst > ATT_SKIP_BELOW)

        def older(st):
            m, _, totals, accs = st
            limits = [s - m * win for s in starts]
            keys0 = [pl.multiple_of(jnp.maximum(lim - win, 0), tq) for lim in limits]
            weights, tile_totals = tile_weights(tile_logits(rows0, keys0), rows0, keys0,
                                                limits, totals)
            outs = tile_values(weights, keys0)
            totals = tuple(t + d for t, d in zip(totals, tile_totals))
            accs = tuple(acc + o for acc, o in zip(accs, outs))
            return (m + 1, slowest(totals), totals, accs)

        _, _, _, accs = lax.while_loop(
            more, older, (jnp.int32(0), slowest(totals), tuple(totals), tuple(accs)))
        for a in range(nsub):
            for h in range(hb):
                o_ref[pl.ds(rows0[a], tq), head_cols(h)] = accs[a * hb + h].astype(BF16)
        return carry

    n_blocks = seq // (tq * nsub)
    n_clipped = min(-(-(win - tq) // (tq * nsub)), n_blocks)
    lax.fori_loop(0, n_clipped, functools.partial(qblock, near_start=True), 0)
    lax.fori_loop(n_clipped, n_blocks, functools.partial(qblock, near_start=False), 0)


def _stick_breaking(p, batch, seq, q_col, k_col, v_col):
    n = p.shape[0]
    width = ATT_HEADS_PER_STEP * HEAD_DIM
    n_hg = N_HEADS // ATT_HEADS_PER_STEP

    def spec(col):
        return pl.BlockSpec((seq, width), lambda b, hg: (b, col // width + hg))

    return pl.pallas_call(
        functools.partial(_attn_kernel, seq=seq, scale=HEAD_DIM ** -0.5),
        grid=(batch, n_hg),
        in_specs=[spec(q_col), spec(k_col), spec(v_col)],
        out_specs=pl.BlockSpec((seq, width), lambda b, hg: (b, hg)),
        out_shape=jax.ShapeDtypeStruct((n, N_HEADS * HEAD_DIM), BF16),
        scratch_shapes=[pltpu.VMEM((ATT_NEW, ATT_NEW), BF16)],
        compiler_params=pltpu.CompilerParams(
            dimension_semantics=("parallel", "parallel"), vmem_limit_bytes=VMEM_LIMIT),
        name="stick_breaking",
    )(p, p, p)


def _out_even_kernel(x_ref, gate_ref, ug_ref, vg_ref, zb_ref, bo_ref, ws_ref, bias_ref, wout_ref,
                     o_ref, y_scr):
    tm = x_ref.shape[0]
    a_width = bo_ref.shape[1]
    row = lax.broadcasted_iota(jnp.int32, (BLOCK, BLOCK), 0)
    col = lax.broadcasted_iota(jnp.int32, (BLOCK, BLOCK), 1)
    for g in range(a_width // HEAD_DIM):
        cols = slice(g * HEAD_DIM, (g + 1) * HEAD_DIM)
        wg = jnp.where(col <= row, ws_ref[g], 0.0).astype(BF16)
        for c in range(tm // BLOCK):
            rows = slice(c * BLOCK, (c + 1) * BLOCK)
            mixed = _dot(wg, vg_ref[rows, cols]) + bias_ref[:, cols]
            y_scr[rows, cols] = (ug_ref[rows, cols].astype(F32) * mixed).astype(BF16)
    y_scr[:, a_width:] = (bo_ref[...].astype(F32) * _silu(zb_ref[...].astype(F32))).astype(BF16)
    o_ref[...] = x_ref[...] + gate_ref[...] * _dot(y_scr[...], wout_ref[...])


def _out_even(x2, seq, gate, p, b_out, a_ws, bias_full, w_out_bf):
    n, d = x2.shape
    tm = OUT_EVEN_ROWS
    aw = b_out.shape[1]
    row = lambda c: pl.BlockSpec((tm, aw), lambda i: (i, c))
    const = lambda a: pl.BlockSpec(a.shape, lambda i: (0,) * a.ndim, pipeline_mode=pl.Buffered(1))
    return pl.pallas_call(
        _out_even_kernel,
        grid=(n // tm,),
        in_specs=[pl.BlockSpec((tm, d), lambda i: (i, 0)),
                  pl.BlockSpec((None, 1, d), lambda i: ((i * tm) // seq, 0, 0)),
                  row(4), row(0), row(5),
                  row(0),
                  const(a_ws), const(bias_full), const(w_out_bf)],
        out_specs=pl.BlockSpec((tm, d), lambda i: (i, 0)),
        out_shape=jax.ShapeDtypeStruct((n, d), F32),
        scratch_shapes=[pltpu.VMEM((tm, 2 * aw), BF16)],
        compiler_params=pltpu.CompilerParams(
            dimension_semantics=("parallel",), vmem_limit_bytes=VMEM_LIMIT),
        name="out_even",
    )(x2, gate, p, p, p, b_out, a_ws, bias_full, w_out_bf)


def _rope_operands(positions):
    half = HEAD_DIM // 2
    inv_freq = ROPE_BASE ** (-jnp.arange(half, dtype=F32) / half)
    freq = jnp.concatenate([inv_freq, inv_freq]).reshape(1, HEAD_DIM)
    sign = jnp.concatenate([-jnp.ones((half,), F32), jnp.ones((half,), F32)]).reshape(1, HEAD_DIM)
    return positions.reshape(-1, 1).astype(F32), freq, sign


def _retention_tables():
    chunk = RET_CHUNK
    log_gamma = jnp.log1p(-jnp.exp2(-5.0 - jnp.arange(N_HEADS, dtype=F32)))
    idx = jnp.arange(chunk, dtype=F32)
    diff = idx[:, None] - idx[None, :]
    intra = jnp.where(diff >= 0, jnp.exp(log_gamma[:, None, None] * jnp.maximum(diff, 0.0)), 0.0)
    q_decay = jnp.exp(log_gamma[:, None] * (idx + 1.0))
    k_decay = jnp.exp(log_gamma[:, None] * (chunk - 1.0 - idx))
    chunk_decay = jnp.exp(log_gamma * chunk)
    return (intra,
            jnp.broadcast_to(q_decay[:, :, None], (N_HEADS, chunk, HEAD_DIM)),
            jnp.broadcast_to(k_decay[:, :, None], (N_HEADS, chunk, HEAD_DIM)),
            jnp.broadcast_to(chunk_decay[:, None, None], (N_HEADS, HEAD_DIM, HEAD_DIM)))


def _retention_rows(q_ref, k_ref, v_ref, idec_ref, qdec_ref, kdec_ref, cdec_ref, st_scr, emit):
    chunk = RET_CHUNK
    n_chunks = q_ref.shape[0] // chunk
    cols = [slice(h * HEAD_DIM, (h + 1) * HEAD_DIM) for h in range(N_HEADS)]
    rows = [slice(c * chunk, (c + 1) * chunk) for c in range(n_chunks)]

    kv = {}
    for c in range(n_chunks):
        for h in range(N_HEADS):
            k_decayed = (k_ref[rows[c], cols[h]].astype(F32) * kdec_ref[h]).astype(BF16)
            kv[h, c] = _dot_tn(k_decayed, v_ref[rows[c], cols[h]])
    state = {}
    for h in range(N_HEADS):
        s = st_scr[h]
        for c in range(n_chunks):
            state[h, c] = s.astype(BF16)
            s = s * cdec_ref[h] + kv[h, c]
        st_scr[h] = s
    for c in range(n_chunks):
        for h in range(N_HEADS):
            q = q_ref[rows[c], cols[h]]
            scores = _dot_nt(q, k_ref[rows[c], cols[h]]) * idec_ref[h]
            out = (_dot(scores.astype(BF16), v_ref[rows[c], cols[h]])
                   + _dot(q, state[h, c]) * qdec_ref[h])
            ms = jnp.mean(out * out, axis=-1, keepdims=True)
            emit(rows[c], h, out * lax.rsqrt(ms + EPS))


def _pooled_rows(pc_ref, halo_ref, cw_ref, cs_ref, ext_scr, seq_row0, emit):
    tm = pc_ref.shape[0]
    ext_scr[0:POOL_HALO, :] = jnp.where(seq_row0 == 0, jnp.zeros_like(halo_ref), halo_ref[...])
    ext_scr[POOL_HALO:, :] = pc_ref[...]
    src_rows = BLOCK + POOL_HALO
    lag = (lax.broadcasted_iota(jnp.int32, (BLOCK, src_rows), 0) + POOL_HALO
           - lax.broadcasted_iota(jnp.int32, (BLOCK, src_rows), 1))
    t_seq = seq_row0 + lax.broadcasted_iota(jnp.int32, (BLOCK, POOL_GROUP_DIM), 0)
    for gi, win in enumerate(POOL_WINDOWS):
        cols = slice(gi * POOL_GROUP_DIM, (gi + 1) * POOL_GROUP_DIM)
        band = jnp.where((lag >= 0) & (lag < win), 1.0 / win, 0.0).astype(BF16)
        short = jnp.where(t_seq + 1 < win, win / (t_seq + 1).astype(F32), 1.0)
        pooled = []
        for c in range(tm // BLOCK):
            mean = _dot(band, ext_scr[c * BLOCK:c * BLOCK + src_rows, cols])
            if c == 0:
                mean = mean * short
            cur = pc_ref[c * BLOCK:(c + 1) * BLOCK, cols].astype(F32)
            pooled.append((mean - cur).astype(BF16))
        emit(gi, _dot(jnp.concatenate(pooled, axis=0), cw_ref[gi]) * cs_ref[:, cols])


def _odd_mixers_kernel(p_ref, idec_ref, qdec_ref, kdec_ref, cdec_ref,
                       x_ref, gate_ref, halo_ref, cw_ref, cs_ref, wout_ref,
                       o_ref, st_scr, ext_scr, y_scr):
    tm = x_ref.shape[0]
    c_width = halo_ref.shape[1]
    pc_ref, q_ref, k_ref, v_ref, za_ref, zb_ref = (
        p_ref.at[:, pl.ds(j * c_width, c_width)] for j in range(6))

    @pl.when(pl.program_id(1) == 0)
    def _():
        st_scr[...] = jnp.zeros_like(st_scr)

    def gated_retention(rows, h, out):
        cols = slice(h * HEAD_DIM, (h + 1) * HEAD_DIM)
        ycols = slice(c_width + h * HEAD_DIM, c_width + (h + 1) * HEAD_DIM)
        y_scr[rows, ycols] = (out * _silu(zb_ref[rows, cols].astype(F32))).astype(BF16)

    def gated_pool(gi, mixed):
        cols = slice(gi * POOL_GROUP_DIM, (gi + 1) * POOL_GROUP_DIM)
        y_scr[:, cols] = (mixed * _silu(za_ref[:, cols].astype(F32))).astype(BF16)

    _retention_rows(q_ref, k_ref, v_ref, idec_ref, qdec_ref, kdec_ref, cdec_ref, st_scr,
                    gated_retention)
    _pooled_rows(pc_ref, halo_ref, cw_ref, cs_ref, ext_scr, pl.program_id(1) * tm, gated_pool)
    o_ref[...] = x_ref[...] + gate_ref[...] * _dot(y_scr[...], wout_ref[...])


def _odd_mixers(x2, batch, seq, gate, p, c_w_bf, c_scale, w_out_bf):
    n, d = x2.shape
    tm = ROW_TILE
    cw = N_HEADS * HEAD_DIM
    steps = seq // tm
    tables = _retention_tables()
    tabs = [pl.BlockSpec(t.shape, lambda b, s: (0, 0, 0)) for t in tables]
    halo_blocks = tm // POOL_HALO
    halo = pl.BlockSpec(
        (POOL_HALO, cw), lambda b, s: (jnp.maximum((b * steps + s) * halo_blocks - 1, 0), 0))
    return pl.pallas_call(
        _odd_mixers_kernel,
        grid=(batch, steps),
        in_specs=[pl.BlockSpec((tm, p.shape[1]), lambda b, s: (b * steps + s, 0))] + tabs + [
            pl.BlockSpec((tm, d), lambda b, s: (b * steps + s, 0)),
            pl.BlockSpec((None, 1, d), lambda b, s: (b, 0, 0)),
            halo,
            pl.BlockSpec(c_w_bf.shape, lambda b, s: (0, 0, 0)),
            pl.BlockSpec((1, cw), lambda b, s: (0, 0)),
            pl.BlockSpec(w_out_bf.shape, lambda b, s: (0, 0))],
        out_specs=pl.BlockSpec((tm, d), lambda b, s: (b * steps + s, 0)),
        out_shape=jax.ShapeDtypeStruct((n, d), F32),
        scratch_shapes=[pltpu.VMEM((N_HEADS, HEAD_DIM, HEAD_DIM), F32),
                        pltpu.VMEM((POOL_HALO + tm, cw), BF16),
                        pltpu.VMEM((tm, 2 * cw), BF16)],
        compiler_params=pltpu.CompilerParams(
            dimension_semantics=("parallel", "arbitrary"), vmem_limit_bytes=VMEM_LIMIT),
        name="odd_mixers",
    )(p, *tables, x2, gate, p, c_w_bf, c_scale.reshape(1, cw), w_out_bf)


def _even_layer(x2, batch, seq, c, norm_g, w_mod, b_mod, w_in, a_vnorm_g, a_ws, a_bs,
                b_qnorm_g, b_knorm_g, w_out):
    d = x2.shape[1]
    shift, scale, gate = _modulation(c, w_mod, b_mod)
    ones = jnp.ones((d,), F32)
    ep_gain = jnp.concatenate([
        ones, a_vnorm_g.reshape(-1), jnp.tile(b_qnorm_g, N_HEADS), jnp.tile(b_knorm_g, N_HEADS),
        ones, ones, ones]).reshape(1, -1)
    modes = ("hold", "norm", "norm", "norm", "raw", "silu_gate", "raw")
    p = _in_projection(x2, seq, norm_g, scale, shift, w_in, ep_gain, modes)
    b_out = _stick_breaking(p, batch, seq, q_col=d, k_col=2 * d, v_col=3 * d)
    bias_full = jnp.repeat(a_bs.T, HEAD_DIM, axis=1)
    return _out_even(x2, seq, gate, p, b_out, a_ws, bias_full, w_out.astype(BF16))


def _odd_layer(x2, batch, seq, c, positions, norm_g, w_mod, b_mod, w_in, c_w, c_scale,
               d_qnorm_g, d_knorm_g, w_out):
    d = x2.shape[1]
    shift, scale, gate = _modulation(c, w_mod, b_mod)
    ones = jnp.ones((d,), F32)
    ep_gain = jnp.concatenate([
        ones, jnp.tile(d_qnorm_g, N_HEADS), jnp.tile(d_knorm_g, N_HEADS),
        ones, ones, ones]).reshape(1, -1)
    modes = ("raw", "rope", "rope_scaled", "raw", "raw", "raw")
    p = _in_projection(x2, seq, norm_g, scale, shift, w_in, ep_gain, modes,
                       rope=_rope_operands(positions), post_scale=HEAD_DIM ** -0.5)
    return _odd_mixers(x2, batch, seq, gate, p, c_w.astype(BF16), c_scale, w_out.astype(BF16))


def kernel(x, c, positions, even_norm_g, even_w_mod, even_b_mod, even_w_in, even_a_vnorm_g, even_a_ws, even_a_bs, even_b_qnorm_g, even_b_knorm_g, even_w_out, odd_norm_g, odd_w_mod, odd_b_mod, odd_w_in, odd_c_w, odd_c_scale, odd_d_qnorm_g, odd_d_knorm_g, odd_w_out):
    batch, seq, d = x.shape
    depth = even_norm_g.shape[0] + odd_norm_g.shape[0]
    x2 = x.reshape(batch * seq, d)
    for layer in range(depth):
        i = layer // 2
        if layer % 2 == 0:
            x2 = _even_layer(x2, batch, seq, c, even_norm_g[i], even_w_mod[i], even_b_mod[i],
                             even_w_in[i], even_a_vnorm_g[i], even_a_ws[i], even_a_bs[i],
                             even_b_qnorm_g[i], even_b_knorm_g[i], even_w_out[i])
        else:
            x2 = _odd_layer(x2, batch, seq, c, positions, odd_norm_g[i], odd_w_mod[i], odd_b_mod[i],
                            odd_w_in[i], odd_c_w[i], odd_c_scale[i], odd_d_qnorm_g[i],
                            odd_d_knorm_g[i], odd_w_out[i])
    return x2.reshape(batch, seq, d)
```

```python
import functools

import jax
import jax.numpy as jnp
from jax import lax
from jax.experimental import pallas as pl
from jax.experimental.pallas import tpu as pltpu

F32 = jnp.float32
BF16 = jnp.bfloat16

EPS = 1e-6
ROPE_BASE = 10000.0
BLOCK = 128
HEAD_DIM = 128
N_HEADS = 8
POOL_WINDOWS = (2, 4, 8, 16)
POOL_GROUP_DIM = 256
POOL_HALO = 16

MOD_ROWS = 8

ROW_TILE = 512
OUT_EVEN_ROWS = 1024
COL_TILE = 1024
RET_CHUNK = 128

ATT_Q = 64
ATT_WIN = 384
ATT_NEW = 256
ATT_EXCLUDED = -1e30
LANES = 128
ATT_SUB = 2
ATT_HEADS_PER_STEP = 4
ATT_SKIP_BELOW = -151.0
LOG2E = 1.4426950408889634
ATT_NO_LIMIT = 1 << 30

VMEM_LIMIT = 56 * 1024 * 1024


def _silu(v):
    return v * jax.nn.sigmoid(v)


def _dot(a, b):
    return jnp.dot(a, b, preferred_element_type=F32)


def _dot_nt(a, b):
    return lax.dot_general(a, b, (((1,), (1,)), ((), ())), preferred_element_type=F32)


def _dot_tn(a, b):
    return lax.dot_general(a, b, (((0,), (0,)), ((), ())), preferred_element_type=F32)


def _group_rms(a, gain):
    ms = jnp.mean(a * a, axis=-1, keepdims=True)
    return a * lax.rsqrt(ms + EPS) * gain


def _mod_kernel(c_ref, we_ref, be_ref, wo_ref, bo_ref, o_ref, *, n_even):
    a = _silu(c_ref[...]).astype(BF16)
    layer = pl.program_id(0)

    @pl.when(layer < n_even)
    def _():
        o_ref[...] = _dot(a, we_ref[...].astype(BF16)) + be_ref[...]

    @pl.when(layer >= n_even)
    def _():
        o_ref[...] = _dot(a, wo_ref[...].astype(BF16)) + bo_ref[...]


def _modulations(c, even_w, even_b, odd_w, odd_b):
    bsz, d = c.shape
    n_even, n_odd = even_w.shape[0], odd_w.shape[0]
    c8 = jnp.pad(c, ((0, MOD_ROWS - bsz), (0, 0)))
    even_at = lambda l, j: (jnp.minimum(l, n_even - 1), 0, jnp.where(l < n_even, j, 2))
    odd_at = lambda l, j: (jnp.maximum(l - n_even, 0), 0, jnp.where(l < n_even, 0, j))
    m = pl.pallas_call(
        functools.partial(_mod_kernel, n_even=n_even),
        grid=(n_even + n_odd, 3),
        in_specs=[pl.BlockSpec((MOD_ROWS, d), lambda l, j: (0, 0)),
                  pl.BlockSpec((None, d, d), even_at),
                  pl.BlockSpec((None, 1, d), even_at),
                  pl.BlockSpec((None, d, d), odd_at),
                  pl.BlockSpec((None, 1, d), odd_at)],
        out_specs=pl.BlockSpec((None, MOD_ROWS, d), lambda l, j: (l, 0, j)),
        out_shape=jax.ShapeDtypeStruct((n_even + n_odd, MOD_ROWS, 3 * d), F32),
        name="modulation",
    )(c8, even_w, even_b.reshape(n_even, 1, 3 * d), odd_w, odd_b.reshape(n_odd, 1, 3 * d))
    return m[:, :bsz]


def _split_modulation(m):
    d = m.shape[1] // 3
    return m[:, None, :d], m[:, None, d:2 * d], m[:, None, 2 * d:]


def _inproj_kernel(*refs, modes, out_tiles, post_scale, use_rope):
    *refs, h_scr, w_ref, stage, sem, held = refs
    w_hbm = refs[4]
    tn = COL_TILE

    def weight_copy(j):
        return pltpu.make_async_copy(w_hbm.at[:, pl.ds(j * tn, tn)], stage.at[j % 2], sem.at[j % 2])

    @pl.when(pl.program_id(0) == 0)
    def _():
        weight_copy(0).start()
        for j in range(len(modes)):
            if j + 1 < len(modes):
                weight_copy(j + 1).start()
            weight_copy(j).wait()
            w_ref[:, j * tn:(j + 1) * tn] = stage[j % 2].astype(BF16)

    if use_rope:
        x_ref, g_ref, sc_ref, sh_ref, _, eg_ref, pos_ref, freq_ref, sign_ref, o_ref = refs
        ang = pos_ref[...] * freq_ref[...]
        cos_t = jnp.cos(ang)
        sin_t = jnp.sin(ang) * sign_ref[...]
    else:
        x_ref, g_ref, sc_ref, sh_ref, _, eg_ref, o_ref = refs
    x = x_ref[...]
    ms = jnp.mean(x * x, axis=-1, keepdims=True)
    y = x * lax.rsqrt(ms + EPS) * g_ref[...]
    h_scr[...] = (y * (1.0 + sc_ref[...]) + sh_ref[...]).astype(BF16)

    cross_lane = ("norm", "norm_scaled", "rope", "rope_scaled")
    heavy = [jj for jj, m in enumerate(modes) if m in cross_lane]
    plain = [jj for jj, m in enumerate(modes) if m not in cross_lane]
    order = []
    while heavy or plain:
        order += heavy[:1] + plain[:1]
        heavy, plain = heavy[1:], plain[1:]
    for jj in order:
        mode = modes[jj]
        out0 = None if out_tiles[jj] is None else out_tiles[jj] * tn
        acc = _dot(h_scr[...], w_ref[:, jj * tn:(jj + 1) * tn])
        if mode == "hold":
            held[...] = acc
            continue
        if mode == "raw":
            o_ref[:, out0:out0 + tn] = acc.astype(BF16)
            continue
        if mode == "silu_gate":
            o_ref[:, out0:out0 + tn] = (held[...] * _silu(acc)).astype(BF16)
            continue
        for g in range(tn // HEAD_DIM):
            cols = slice(jj * tn + g * HEAD_DIM, jj * tn + (g + 1) * HEAD_DIM)
            t = _group_rms(acc[:, g * HEAD_DIM:(g + 1) * HEAD_DIM], eg_ref[:, cols])
            if mode in ("rope", "rope_scaled"):
                t = t * cos_t + pltpu.roll(t, HEAD_DIM // 2, axis=1) * sin_t
            if mode in ("norm_scaled", "rope_scaled"):
                t = t * post_scale
            o_ref[:, out0 + g * HEAD_DIM:out0 + (g + 1) * HEAD_DIM] = t.astype(BF16)


def _in_projection(x2, seq, norm_g, scale, shift, w_in, ep_gain, modes, rope=None, post_scale=1.0):
    n, d = x2.shape
    tm, tn = ROW_TILE, COL_TILE
    ncols = len(modes) * tn
    out_tiles, n_out = [], 0
    for m in modes:
        out_tiles.append(None if m == "hold" else n_out)
        n_out += m != "hold"
    assert modes.count("hold") == modes.count("silu_gate") <= 1
    assert "hold" not in modes or modes.index("hold") < modes.index("silu_gate")
    use_rope = rope is not None
    in_specs = [
        pl.BlockSpec((tm, d), lambda i: (i, 0)),
        pl.BlockSpec((1, d), lambda i: (0, 0)),
        pl.BlockSpec((None, 1, d), lambda i: ((i * tm) // seq, 0, 0)),
        pl.BlockSpec((None, 1, d), lambda i: ((i * tm) // seq, 0, 0)),
        pl.BlockSpec(memory_space=pl.ANY),
        pl.BlockSpec((1, ncols), lambda i: (0, 0)),
    ]
    args = [x2, norm_g.reshape(1, d), scale, shift, w_in, ep_gain]
    if use_rope:
        vec = pl.BlockSpec((1, HEAD_DIM), lambda i: (0, 0))
        in_specs += [pl.BlockSpec((tm, 1), lambda i: (i, 0)), vec, vec]
        args += list(rope)
    return pl.pallas_call(
        functools.partial(_inproj_kernel, modes=tuple(modes), out_tiles=tuple(out_tiles),
                          post_scale=post_scale, use_rope=use_rope),
        grid=(n // tm,),
        in_specs=in_specs,
        out_specs=pl.BlockSpec((tm, n_out * tn), lambda i: (i, 0)),
        out_shape=jax.ShapeDtypeStruct((n, n_out * tn), BF16),
        scratch_shapes=[pltpu.VMEM((tm, d), BF16),
                        pltpu.VMEM((d, ncols), BF16),
                        pltpu.VMEM((2, d, tn), F32),
                        pltpu.SemaphoreType.DMA((2,)),
                        pltpu.VMEM((tm, tn), F32)],
        compiler_params=pltpu.CompilerParams(
            dimension_semantics=("arbitrary",), vmem_limit_bytes=VMEM_LIMIT),
        name="in_projection",
    )(*args)


def _attn_kernel(q_ref, k_ref, v_ref, o_ref, u_scr, *, seq):
    hb, tq, win, nsub = ATT_HEADS_PER_STEP, ATT_Q, ATT_WIN, ATT_SUB
    chains = nsub * hb

    u_scr[...] = jnp.where(lax.broadcasted_iota(jnp.int32, (ATT_NEW, ATT_NEW), 0)
                           > lax.broadcasted_iota(jnp.int32, (ATT_NEW, ATT_NEW), 1),
                           1.0, 0.0).astype(BF16)
    old = win - ATT_NEW
    key_idx = lax.broadcasted_iota(jnp.int32, (tq, win), 1)
    row_idx = lax.broadcasted_iota(jnp.int32, (tq, win), 0)
    tail = win - LANES
    tail_ok = (lax.broadcasted_iota(jnp.int32, (tq, LANES), 1) + tail
               < lax.broadcasted_iota(jnp.int32, (tq, LANES), 0) + (win - tq))

    def head_cols(h):
        return slice(h * HEAD_DIM, (h + 1) * HEAD_DIM)

    def tile_logits(rows0, keys0):
        return [_dot_nt(q_ref[pl.ds(rows0[a], tq), head_cols(h)],
                        k_ref[pl.ds(keys0[a], win), head_cols(h)])
                for a in range(nsub) for h in range(hb)]

    def tile_weights(logits, rows0, keys0, limits, totals):
        log_beta, terms = [], []
        for a in range(nsub):
            if limits is not None:
                ok = key_idx < (jnp.minimum(rows0[a] + row_idx, limits[a]) - keys0[a])
            for h in range(hb):
                z = logits[a * hb + h]
                if limits is None:
                    z = jnp.concatenate(
                        [z[:, :tail], jnp.where(tail_ok, z[:, tail:], ATT_EXCLUDED)], axis=1)
                else:
                    z = jnp.where(ok, z, ATT_EXCLUDED)
                nz = -z
                lsm = (jnp.minimum(nz, 0.0)
                       - jnp.log(1.0 + jnp.exp2(jnp.minimum(z, nz))) * LOG2E)
                log_beta.append(z + lsm)
                terms.append(lsm)
        sums_new = _dot(jnp.concatenate([t[:, old:].astype(BF16) for t in terms], axis=0),
                        u_scr[...])
        sums_old = _dot(jnp.concatenate([t[:, :old].astype(BF16) for t in terms], axis=0),
                        u_scr[0:old, 0:old])
        weights, tile_totals = [], []
        for c in range(chains):
            newer = sums_new[c * tq:(c + 1) * tq]
            older = sums_old[c * tq:(c + 1) * tq]
            total_new = newer[:, 0:1] + terms[c][:, old:old + 1]
            expo = log_beta[c] + jnp.concatenate([older + total_new, newer], axis=1)
            if totals is not None:
                expo = expo + totals[c]
            weights.append(jnp.exp2(expo).astype(BF16))
            tile_totals.append(total_new + older[:, 0:1] + terms[c][:, 0:1])
        return weights, tile_totals

    def tile_values(weights, keys0):
        return [_dot(weights[a * hb + h], v_ref[pl.ds(keys0[a], win), head_cols(h)])
                for a in range(nsub) for h in range(hb)]

    def slowest(totals):
        m = totals[0]
        for t in totals[1:]:
            m = jnp.maximum(m, t)
        return jnp.max(m)

    def qblock(i, carry, *, near_start):
        r0 = i * (tq * nsub)
        rows0 = [pl.multiple_of(r0 + a * tq, tq) for a in range(nsub)]
        if near_start:
            starts = [jnp.maximum(r - (win - tq), 0) for r in rows0]
        else:
            starts = [r - (win - tq) for r in rows0]
        keys0 = [pl.multiple_of(s, tq) for s in starts]
        weights, totals = tile_weights(tile_logits(rows0, keys0), rows0, keys0,
                                       [ATT_NO_LIMIT] * nsub if near_start else None, None)
        accs = tile_values(weights, keys0)

        def more(st):
            m, worst, _, _ = st
            return (starts[-1] - m * win > 0) & (worst > ATT_SKIP_BELOW)

        def older(st):
            m, _, totals, accs = st
            limits = [s - m * win for s in starts]
            keys0 = [pl.multiple_of(jnp.maximum(lim - win, 0), tq) for lim in limits]
            weights, tile_totals = tile_weights(tile_logits(rows0, keys0), rows0, keys0,
                                                limits, totals)
            outs = tile_values(weights, keys0)
            totals = tuple(t + d for t, d in zip(totals, tile_totals))
            accs = tuple(acc + o for acc, o in zip(accs, outs))
            return (m + 1, slowest(totals), totals, accs)

        _, _, _, accs = lax.while_loop(
            more, older, (jnp.int32(0), slowest(totals), tuple(totals), tuple(accs)))
        for a in range(nsub):
            for h in range(hb):
                o_ref[pl.ds(rows0[a], tq), head_cols(h)] = accs[a * hb + h].astype(BF16)
        return carry

    n_blocks = seq // (tq * nsub)
    n_clipped = min(-(-(win - tq) // (tq * nsub)), n_blocks)
    lax.fori_loop(0, n_clipped, functools.partial(qblock, near_start=True), 0)
    lax.fori_loop(n_clipped, n_blocks, functools.partial(qblock, near_start=False), 0)


def _stick_breaking(p, batch, seq, q_col, k_col, v_col):
    n = p.shape[0]
    width = ATT_HEADS_PER_STEP * HEAD_DIM
    n_hg = N_HEADS // ATT_HEADS_PER_STEP

    def spec(col):
        return pl.BlockSpec((seq, width), lambda b, hg: (b, col // width + hg))

    return pl.pallas_call(
        functools.partial(_attn_kernel, seq=seq),
        grid=(batch, n_hg),
        in_specs=[spec(q_col), spec(k_col), spec(v_col)],
        out_specs=pl.BlockSpec((seq, width), lambda b, hg: (b, hg)),
        out_shape=jax.ShapeDtypeStruct((n, N_HEADS * HEAD_DIM), BF16),
        scratch_shapes=[pltpu.VMEM((ATT_NEW, ATT_NEW), BF16)],
        compiler_params=pltpu.CompilerParams(
            dimension_semantics=("parallel", "parallel"), vmem_limit_bytes=VMEM_LIMIT),
        name="stick_breaking",
    )(p, p, p)


def _out_even_kernel(x_ref, gate_ref, ug_ref, vg_ref, zb_ref, bo_ref, ws_ref, bias_ref, wout_ref,
                     o_ref, y_scr):
    tm = x_ref.shape[0]
    a_width = bo_ref.shape[1]
    row = lax.broadcasted_iota(jnp.int32, (BLOCK, BLOCK), 0)
    col = lax.broadcasted_iota(jnp.int32, (BLOCK, BLOCK), 1)
    for g in range(a_width // HEAD_DIM):
        cols = slice(g * HEAD_DIM, (g + 1) * HEAD_DIM)
        wg = jnp.where(col <= row, ws_ref[g], 0.0).astype(BF16)
        for c in range(tm // BLOCK):
            rows = slice(c * BLOCK, (c + 1) * BLOCK)
            mixed = _dot(wg, vg_ref[rows, cols]) + bias_ref[:, cols]
            y_scr[rows, cols] = (ug_ref[rows, cols].astype(F32) * mixed).astype(BF16)
    y_scr[:, a_width:] = (bo_ref[...].astype(F32) * _silu(zb_ref[...].astype(F32))).astype(BF16)
    o_ref[...] = x_ref[...] + gate_ref[...] * _dot(y_scr[...], wout_ref[...])


def _out_even(x2, seq, gate, p, b_out, a_ws, bias_full, w_out_bf):
    n, d = x2.shape
    tm = OUT_EVEN_ROWS
    aw = b_out.shape[1]
    row = lambda c: pl.BlockSpec((tm, aw), lambda i: (i, c))
    const = lambda a: pl.BlockSpec(a.shape, lambda i: (0,) * a.ndim, pipeline_mode=pl.Buffered(1))
    return pl.pallas_call(
        _out_even_kernel,
        grid=(n // tm,),
        in_specs=[pl.BlockSpec((tm, d), lambda i: (i, 0)),
                  pl.BlockSpec((None, 1, d), lambda i: ((i * tm) // seq, 0, 0)),
                  row(4), row(0), row(5),
                  row(0),
                  const(a_ws), const(bias_full), const(w_out_bf)],
        out_specs=pl.BlockSpec((tm, d), lambda i: (i, 0)),
        out_shape=jax.ShapeDtypeStruct((n, d), F32),
        scratch_shapes=[pltpu.VMEM((tm, 2 * aw), BF16)],
        compiler_params=pltpu.CompilerParams(
            dimension_semantics=("parallel",), vmem_limit_bytes=VMEM_LIMIT),
        name="out_even",
    )(x2, gate, p, p, p, b_out, a_ws, bias_full, w_out_bf)


def _rope_operands(positions):
    half = HEAD_DIM // 2
    inv_freq = ROPE_BASE ** (-jnp.arange(half, dtype=F32) / half)
    freq = jnp.concatenate([inv_freq, inv_freq]).reshape(1, HEAD_DIM)
    sign = jnp.concatenate([-jnp.ones((half,), F32), jnp.ones((half,), F32)]).reshape(1, HEAD_DIM)
    return positions.reshape(-1, 1).astype(F32), freq, sign


def _retention_tables():
    chunk = RET_CHUNK
    log_gamma = jnp.log1p(-jnp.exp2(-5.0 - jnp.arange(N_HEADS, dtype=F32)))
    idx = jnp.arange(chunk, dtype=F32)
    diff = idx[:, None] - idx[None, :]
    intra = jnp.where(diff >= 0, jnp.exp(log_gamma[:, None, None] * jnp.maximum(diff, 0.0)), 0.0)
    q_decay = jnp.exp(log_gamma[:, None] * (idx + 1.0))
    k_decay = jnp.exp(log_gamma[:, None] * (chunk - 1.0 - idx))
    chunk_decay = jnp.exp(log_gamma * chunk)
    return (intra,
            jnp.broadcast_to(q_decay[:, :, None], (N_HEADS, chunk, HEAD_DIM)),
            jnp.broadcast_to(k_decay[:, :, None], (N_HEADS, chunk, HEAD_DIM)),
            jnp.broadcast_to(chunk_decay[:, None, None], (N_HEADS, HEAD_DIM, HEAD_DIM)))


def _retention_rows(q_ref, k_ref, v_ref, idec_ref, qdec_ref, kdec_ref, cdec_ref, st_scr, emit):
    chunk = RET_CHUNK
    n_chunks = q_ref.shape[0] // chunk
    cols = [slice(h * HEAD_DIM, (h + 1) * HEAD_DIM) for h in range(N_HEADS)]
    rows = [slice(c * chunk, (c + 1) * chunk) for c in range(n_chunks)]

    kv = {}
    for c in range(n_chunks):
        for h in range(N_HEADS):
            k_decayed = (k_ref[rows[c], cols[h]].astype(F32) * kdec_ref[h]).astype(BF16)
            kv[h, c] = _dot_tn(k_decayed, v_ref[rows[c], cols[h]])
    state = {}
    for h in range(N_HEADS):
        s = st_scr[h]
        for c in range(n_chunks):
            state[h, c] = s.astype(BF16)
            s = s * cdec_ref[h] + kv[h, c]
        st_scr[h] = s
    for c in range(n_chunks):
        for h in range(N_HEADS):
            q = q_ref[rows[c], cols[h]]
            scores = _dot_nt(q, k_ref[rows[c], cols[h]]) * idec_ref[h]
            out = (_dot(scores.astype(BF16), v_ref[rows[c], cols[h]])
                   + _dot(q, state[h, c]) * qdec_ref[h])
            ms = jnp.mean(out * out, axis=-1, keepdims=True)
            emit(rows[c], h, out * lax.rsqrt(ms + EPS))


def _pooled_rows(pc_ref, halo_ref, cw_ref, cs_ref, ext_scr, seq_row0, emit):
    tm = pc_ref.shape[0]
    ext_scr[0:POOL_HALO, :] = jnp.where(seq_row0 == 0, jnp.zeros_like(halo_ref), halo_ref[...])
    ext_scr[POOL_HALO:, :] = pc_ref[...]
    src_rows = BLOCK + POOL_HALO
    lag = (lax.broadcasted_iota(jnp.int32, (BLOCK, src_rows), 0) + POOL_HALO
           - lax.broadcasted_iota(jnp.int32, (BLOCK, src_rows), 1))
    t_seq = seq_row0 + lax.broadcasted_iota(jnp.int32, (BLOCK, POOL_GROUP_DIM), 0)
    for gi, win in enumerate(POOL_WINDOWS):
        cols = slice(gi * POOL_GROUP_DIM, (gi + 1) * POOL_GROUP_DIM)
        band = jnp.where((lag >= 0) & (lag < win), 1.0 / win, 0.0).astype(BF16)
        short = jnp.where(t_seq + 1 < win, win / (t_seq + 1).astype(F32), 1.0)
        pooled = []
        for c in range(tm // BLOCK):
            mean = _dot(band, ext_scr[c * BLOCK:c * BLOCK + src_rows, cols])
            if c == 0:
                mean = mean * short
            cur = pc_ref[c * BLOCK:(c + 1) * BLOCK, cols].astype(F32)
            pooled.append((mean - cur).astype(BF16))
        emit(gi, _dot(jnp.concatenate(pooled, axis=0), cw_ref[gi]) * cs_ref[:, cols])


def _odd_mixers_kernel(p_ref, idec_ref, qdec_ref, kdec_ref, cdec_ref,
                       x_ref, gate_ref, halo_ref, cw_ref, cs_ref, wout_ref,
                       o_ref, st_scr, ext_scr, y_scr):
    tm = x_ref.shape[0]
    c_width = halo_ref.shape[1]
    pc_ref, q_ref, k_ref, v_ref, za_ref, zb_ref = (
        p_ref.at[:, pl.ds(j * c_width, c_width)] for j in range(6))

    @pl.when(pl.program_id(1) == 0)
    def _():
        st_scr[...] = jnp.zeros_like(st_scr)

    def gated_retention(rows, h, out):
        cols = slice(h * HEAD_DIM, (h + 1) * HEAD_DIM)
        ycols = slice(c_width + h * HEAD_DIM, c_width + (h + 1) * HEAD_DIM)
        y_scr[rows, ycols] = (out * _silu(zb_ref[rows, cols].astype(F32))).astype(BF16)

    def gated_pool(gi, mixed):
        cols = slice(gi * POOL_GROUP_DIM, (gi + 1) * POOL_GROUP_DIM)
        y_scr[:, cols] = (mixed * _silu(za_ref[:, cols].astype(F32))).astype(BF16)

    _retention_rows(q_ref, k_ref, v_ref, idec_ref, qdec_ref, kdec_ref, cdec_ref, st_scr,
                    gated_retention)
    _pooled_rows(pc_ref, halo_ref, cw_ref, cs_ref, ext_scr, pl.program_id(1) * tm, gated_pool)
    o_ref[...] = x_ref[...] + gate_ref[...] * _dot(y_scr[...], wout_ref[...])


def _odd_mixers(x2, batch, seq, gate, p, c_w_bf, c_scale, w_out_bf):
    n, d = x2.shape
    tm = ROW_TILE
    cw = N_HEADS * HEAD_DIM
    steps = seq // tm
    tables = _retention_tables()
    tabs = [pl.BlockSpec(t.shape, lambda b, s: (0, 0, 0)) for t in tables]
    halo_blocks = tm // POOL_HALO
    halo = pl.BlockSpec(
        (POOL_HALO, cw), lambda b, s: (jnp.maximum((b * steps + s) * halo_blocks - 1, 0), 0))
    return pl.pallas_call(
        _odd_mixers_kernel,
        grid=(batch, steps),
        in_specs=[pl.BlockSpec((tm, p.shape[1]), lambda b, s: (b * steps + s, 0))] + tabs + [
            pl.BlockSpec((tm, d), lambda b, s: (b * steps + s, 0)),
            pl.BlockSpec((None, 1, d), lambda b, s: (b, 0, 0)),
            halo,
            pl.BlockSpec(c_w_bf.shape, lambda b, s: (0, 0, 0)),
            pl.BlockSpec((1, cw), lambda b, s: (0, 0)),
            pl.BlockSpec(w_out_bf.shape, lambda b, s: (0, 0))],
        out_specs=pl.BlockSpec((tm, d), lambda b, s: (b * steps + s, 0)),
        out_shape=jax.ShapeDtypeStruct((n, d), F32),
        scratch_shapes=[pltpu.VMEM((N_HEADS, HEAD_DIM, HEAD_DIM), F32),
                        pltpu.VMEM((POOL_HALO + tm, cw), BF16),
                        pltpu.VMEM((tm, 2 * cw), BF16)],
        compiler_params=pltpu.CompilerParams(
            dimension_semantics=("parallel", "arbitrary"), vmem_limit_bytes=VMEM_LIMIT),
        name="odd_mixers",
    )(p, *tables, x2, gate, p, c_w_bf, c_scale.reshape(1, cw), w_out_bf)


def _even_layer(x2, batch, seq, mod, norm_g, w_in, a_vnorm_g, a_ws, a_bs,
                b_qnorm_g, b_knorm_g, w_out):
    d = x2.shape[1]
    shift, scale, gate = _split_modulation(mod)
    ones = jnp.ones((d,), F32)
    ep_gain = jnp.concatenate([
        ones, a_vnorm_g.reshape(-1), jnp.tile(b_qnorm_g, N_HEADS), jnp.tile(b_knorm_g, N_HEADS),
        ones, ones, ones]).reshape(1, -1)
    modes = ("hold", "norm", "norm_scaled", "norm", "raw", "silu_gate", "raw")
    p = _in_projection(x2, seq, norm_g, scale, shift, w_in, ep_gain, modes,
                       post_scale=LOG2E * HEAD_DIM ** -0.5)
    b_out = _stick_breaking(p, batch, seq, q_col=d, k_col=2 * d, v_col=3 * d)
    bias_full = jnp.repeat(a_bs.T, HEAD_DIM, axis=1)
    return _out_even(x2, seq, gate, p, b_out, a_ws, bias_full, w_out.astype(BF16))


def _odd_layer(x2, batch, seq, mod, positions, norm_g, w_in, c_w, c_scale,
               d_qnorm_g, d_knorm_g, w_out):
    d = x2.shape[1]
    shift, scale, gate = _split_modulation(mod)
    ones = jnp.ones((d,), F32)
    ep_gain = jnp.concatenate([
        ones, jnp.tile(d_qnorm_g, N_HEADS), jnp.tile(d_knorm_g, N_HEADS),
        ones, ones, ones]).reshape(1, -1)
    modes = ("raw", "rope", "rope_scaled", "raw", "raw", "raw")
    p = _in_projection(x2, seq, norm_g, scale, shift, w_in, ep_gain, modes,
                       rope=_rope_operands(positions), post_scale=HEAD_DIM ** -0.5)
    return _odd_mixers(x2, batch, seq, gate, p, c_w.astype(BF16), c_scale, w_out.astype(BF16))


def kernel(x, c, positions, even_norm_g, even_w_mod, even_b_mod, even_w_in, even_a_vnorm_g, even_a_ws, even_a_bs, even_b_qnorm_g, even_b_knorm_g, even_w_out, odd_norm_g, odd_w_mod, odd_b_mod, odd_w_in, odd_c_w, odd_c_scale, odd_d_qnorm_g, odd_d_knorm_g, odd_w_out):
    batch, seq, d = x.shape
    depth = even_norm_g.shape[0] + odd_norm_g.shape[0]
    x2 = x.reshape(batch * seq, d)
    n_even = even_norm_g.shape[0]
    mods = _modulations(c, even_w_mod, even_b_mod, odd_w_mod, odd_b_mod)
    for layer in range(depth):
        i = layer // 2
        if layer % 2 == 0:
            x2 = _even_layer(x2, batch, seq, mods[i], even_norm_g[i],
                             even_w_in[i], even_a_vnorm_g[i], even_a_ws[i], even_a_bs[i],
                             even_b_qnorm_g[i], even_b_knorm_g[i], even_w_out[i])
        else:
            x2 = _odd_layer(x2, batch, seq, mods[n_even + i], positions, odd_norm_g[i],
                            odd_w_in[i], odd_c_w[i], odd_c_scale[i], odd_d_qnorm_g[i],
                            odd_d_knorm_g[i], odd_w_out[i])
    return x2.reshape(batch, seq, d)
```

```python
import functools

import jax
import jax.numpy as jnp
import numpy as np
from jax import lax
from jax.experimental import pallas as pl
from jax.experimental.pallas import tpu as pltpu

F32 = jnp.float32
BF16 = jnp.bfloat16

EPS = 1e-6
ROPE_BASE = 10000.0
BLOCK = 128
HEAD_DIM = 128
N_HEADS = 8
POOL_WINDOWS = (2, 4, 8, 16)
POOL_GROUP_DIM = 256
POOL_HALO = 16

MOD_ROWS = 8

ROW_TILE = 512
OUT_EVEN_ROWS = 1024
COL_TILE = 1024
RET_CHUNK = 128

ATT_Q = 64
ATT_WIN = 384
ATT_NEW = 256
ATT_EXCLUDED = -1e30
LANES = 128
ATT_SUB = 2
ATT_HEADS_PER_STEP = 4
ATT_SKIP_BELOW = -151.0
LOG2E = 1.4426950408889634
ATT_NO_LIMIT = 1 << 30

VMEM_LIMIT = 56 * 1024 * 1024


def _silu(v):
    return v * jax.nn.sigmoid(v)


def _dot(a, b):
    return jnp.dot(a, b, preferred_element_type=F32)


def _dot_nt(a, b):
    return lax.dot_general(a, b, (((1,), (1,)), ((), ())), preferred_element_type=F32)


def _dot_tn(a, b):
    return lax.dot_general(a, b, (((0,), (0,)), ((), ())), preferred_element_type=F32)


def _group_rms(a, gain):
    ms = jnp.mean(a * a, axis=-1, keepdims=True)
    return a * lax.rsqrt(ms + EPS) * gain


def _mod_kernel(c_ref, we_ref, be_ref, wo_ref, bo_ref, o_ref, *, n_even):
    a = _silu(c_ref[...]).astype(BF16)
    layer = pl.program_id(0)

    @pl.when(layer < n_even)
    def _():
        o_ref[...] = _dot(a, we_ref[...].astype(BF16)) + be_ref[...]

    @pl.when(layer >= n_even)
    def _():
        o_ref[...] = _dot(a, wo_ref[...].astype(BF16)) + bo_ref[...]


def _modulations(c, even_w, even_b, odd_w, odd_b):
    bsz, d = c.shape
    n_even, n_odd = even_w.shape[0], odd_w.shape[0]
    c8 = jnp.pad(c, ((0, MOD_ROWS - bsz), (0, 0)))
    even_at = lambda l, j: (jnp.minimum(l, n_even - 1), 0, jnp.where(l < n_even, j, 2))
    odd_at = lambda l, j: (jnp.maximum(l - n_even, 0), 0, jnp.where(l < n_even, 0, j))
    m = pl.pallas_call(
        functools.partial(_mod_kernel, n_even=n_even),
        grid=(n_even + n_odd, 3),
        in_specs=[pl.BlockSpec((MOD_ROWS, d), lambda l, j: (0, 0)),
                  pl.BlockSpec((None, d, d), even_at),
                  pl.BlockSpec((None, 1, d), even_at),
                  pl.BlockSpec((None, d, d), odd_at),
                  pl.BlockSpec((None, 1, d), odd_at)],
        out_specs=pl.BlockSpec((None, MOD_ROWS, d), lambda l, j: (l, 0, j)),
        out_shape=jax.ShapeDtypeStruct((n_even + n_odd, MOD_ROWS, 3 * d), F32),
        name="modulation",
    )(c8, even_w, even_b.reshape(n_even, 1, 3 * d), odd_w, odd_b.reshape(n_odd, 1, 3 * d))
    return m[:, :bsz]


def _split_modulation(m):
    d = m.shape[1] // 3
    return m[:, None, :d], m[:, None, d:2 * d], m[:, None, 2 * d:]


def _inproj_kernel(*refs, modes, out_tiles, gain_of, n_gains, post_scale, use_rope):
    *refs, h_scr, w_ref, stage, sem, held = refs
    x_ref, g_ref, sc_ref, sh_ref, w_hbm = refs[:5]
    gain_refs = refs[5:5 + n_gains]
    o_ref = refs[-1]
    tn = COL_TILE

    def weight_copy(j):
        return pltpu.make_async_copy(w_hbm.at[:, pl.ds(j * tn, tn)], stage.at[j % 2], sem.at[j % 2])

    @pl.when(pl.program_id(0) == 0)
    def _():
        weight_copy(0).start()
        for j in range(len(modes)):
            if j + 1 < len(modes):
                weight_copy(j + 1).start()
            weight_copy(j).wait()
            w_ref[:, j * tn:(j + 1) * tn] = stage[j % 2].astype(BF16)

    if use_rope:
        pos_ref, freq_ref, sign_ref = refs[5 + n_gains:8 + n_gains]
        on_diag = (lax.broadcasted_iota(jnp.int32, (LANES, LANES), 0)
                   == lax.broadcasted_iota(jnp.int32, (LANES, LANES), 1))
        angles = []
        for a in range(pos_ref.shape[0]):
            spread = jnp.where(on_diag, jnp.broadcast_to(pos_ref[a:a + 1, :], (LANES, LANES)), 0.0)
            angles.append(jnp.sum(spread, axis=1, keepdims=True) * freq_ref[...])
        ang = jnp.concatenate(angles, axis=0)
        cos_t = jnp.cos(ang)
        sin_t = jnp.sin(ang) * sign_ref[...]
    x = x_ref[...]
    ms = jnp.mean(x * x, axis=-1, keepdims=True)
    y = x * lax.rsqrt(ms + EPS) * g_ref[...]
    h_scr[...] = (y * (1.0 + sc_ref[...]) + sh_ref[...]).astype(BF16)

    cross_lane = ("norm", "norm_scaled", "rope", "rope_scaled")
    heavy = [jj for jj, m in enumerate(modes) if m in cross_lane]
    plain = [jj for jj, m in enumerate(modes) if m not in cross_lane]
    order = []
    while heavy or plain:
        order += heavy[:1] + plain[:1]
        heavy, plain = heavy[1:], plain[1:]
    for jj in order:
        mode = modes[jj]
        out0 = None if out_tiles[jj] is None else out_tiles[jj] * tn
        acc = _dot(h_scr[...], w_ref[:, jj * tn:(jj + 1) * tn])
        if mode == "hold":
            held[...] = acc
            continue
        if mode == "raw":
            o_ref[:, out0:out0 + tn] = acc.astype(BF16)
            continue
        if mode == "silu_gate":
            o_ref[:, out0:out0 + tn] = (held[...] * _silu(acc)).astype(BF16)
            continue
        gain_ref = gain_refs[gain_of[jj]]
        for g in range(tn // HEAD_DIM):
            row = g if gain_ref.shape[0] > 1 else 0
            t = _group_rms(acc[:, g * HEAD_DIM:(g + 1) * HEAD_DIM], gain_ref[row:row + 1, :])
            if mode in ("rope", "rope_scaled"):
                t = t * cos_t + pltpu.roll(t, HEAD_DIM // 2, axis=1) * sin_t
            if mode in ("norm_scaled", "rope_scaled"):
                t = t * post_scale
            o_ref[:, out0 + g * HEAD_DIM:out0 + (g + 1) * HEAD_DIM] = t.astype(BF16)


def _in_projection(x2, seq, norm_g, scale, shift, w_in, gains, modes, rope=None, post_scale=1.0):
    n, d = x2.shape
    tm, tn = ROW_TILE, COL_TILE
    ncols = len(modes) * tn
    out_tiles, n_out = [], 0
    for m in modes:
        out_tiles.append(None if m == "hold" else n_out)
        n_out += m != "hold"
    assert modes.count("hold") == modes.count("silu_gate") <= 1
    assert "hold" not in modes or modes.index("hold") < modes.index("silu_gate")
    use_rope = rope is not None
    normalising = [jj for jj, m in enumerate(modes) if m in ("norm", "norm_scaled", "rope", "rope_scaled")]
    assert len(gains) == len(normalising)
    gain_of = tuple(normalising.index(jj) if jj in normalising else None for jj in range(len(modes)))
    in_specs = [
        pl.BlockSpec((tm, d), lambda i: (i, 0)),
        pl.BlockSpec((1, d), lambda i: (0, 0)),
        pl.BlockSpec((None, 1, d), lambda i: ((i * tm) // seq, 0, 0)),
        pl.BlockSpec((None, 1, d), lambda i: ((i * tm) // seq, 0, 0)),
        pl.BlockSpec(memory_space=pl.ANY),
    ] + [pl.BlockSpec(gn.shape, lambda i: (0, 0)) for gn in gains]
    args = [x2, norm_g.reshape(1, d), scale, shift, w_in, *gains]
    if use_rope:
        vec = pl.BlockSpec((1, HEAD_DIM), lambda i: (0, 0))
        pos, freq, sign = rope
        in_specs += [pl.BlockSpec((None, tm // LANES, LANES), lambda i: (i, 0, 0)), vec, vec]
        args += [pos.reshape(n // tm, tm // LANES, LANES), freq, sign]
    return pl.pallas_call(
        functools.partial(_inproj_kernel, modes=tuple(modes), out_tiles=tuple(out_tiles),
                          gain_of=gain_of, n_gains=len(gains), post_scale=post_scale,
                          use_rope=use_rope),
        grid=(n // tm,),
        in_specs=in_specs,
        out_specs=pl.BlockSpec((tm, n_out * tn), lambda i: (i, 0)),
        out_shape=jax.ShapeDtypeStruct((n, n_out * tn), BF16),
        scratch_shapes=[pltpu.VMEM((tm, d), BF16),
                        pltpu.VMEM((d, ncols), BF16),
                        pltpu.VMEM((2, d, tn), F32),
                        pltpu.SemaphoreType.DMA((2,)),
                        pltpu.VMEM((tm, tn), F32)],
        compiler_params=pltpu.CompilerParams(
            dimension_semantics=("arbitrary",), vmem_limit_bytes=VMEM_LIMIT),
        name="in_projection",
    )(*args)


def _attn_kernel(q_ref, k_ref, v_ref, o_ref, u_scr, *, seq):
    hb, tq, win, nsub = ATT_HEADS_PER_STEP, ATT_Q, ATT_WIN, ATT_SUB
    chains = nsub * hb

    u_scr[...] = jnp.where(lax.broadcasted_iota(jnp.int32, (ATT_NEW, ATT_NEW), 0)
                           > lax.broadcasted_iota(jnp.int32, (ATT_NEW, ATT_NEW), 1),
                           1.0, 0.0).astype(BF16)
    old = win - ATT_NEW
    key_idx = lax.broadcasted_iota(jnp.int32, (tq, win), 1)
    row_idx = lax.broadcasted_iota(jnp.int32, (tq, win), 0)
    tail = win - LANES
    tail_ok = (lax.broadcasted_iota(jnp.int32, (tq, LANES), 1) + tail
               < lax.broadcasted_iota(jnp.int32, (tq, LANES), 0) + (win - tq))

    def head_cols(h):
        return slice(h * HEAD_DIM, (h + 1) * HEAD_DIM)

    def tile_logits(rows0, keys0):
        return [_dot_nt(q_ref[pl.ds(rows0[a], tq), head_cols(h)],
                        k_ref[pl.ds(keys0[a], win), head_cols(h)])
                for a in range(nsub) for h in range(hb)]

    def tile_weights(logits, rows0, keys0, limits, totals):
        log_beta, terms = [], []
        for a in range(nsub):
            if limits is not None:
                ok = key_idx < (jnp.minimum(rows0[a] + row_idx, limits[a]) - keys0[a])
            for h in range(hb):
                z = logits[a * hb + h]
                if limits is None:
                    z = jnp.concatenate(
                        [z[:, :tail], jnp.where(tail_ok, z[:, tail:], ATT_EXCLUDED)], axis=1)
                else:
                    z = jnp.where(ok, z, ATT_EXCLUDED)
                nz = -z
                lsm = (jnp.minimum(nz, 0.0)
                       - jnp.log(1.0 + jnp.exp2(jnp.minimum(z, nz))) * LOG2E)
                log_beta.append(z + lsm)
                terms.append(lsm)
        sums_new = _dot(jnp.concatenate([t[:, old:].astype(BF16) for t in terms], axis=0),
                        u_scr[...])
        sums_old = _dot(jnp.concatenate([t[:, :old].astype(BF16) for t in terms], axis=0),
                        u_scr[0:old, 0:old])
        weights, tile_totals = [], []
        for c in range(chains):
            newer = sums_new[c * tq:(c + 1) * tq]
            older = sums_old[c * tq:(c + 1) * tq]
            total_new = newer[:, 0:1] + terms[c][:, old:old + 1]
            expo = log_beta[c] + jnp.concatenate([older + total_new, newer], axis=1)
            if totals is not None:
                expo = expo + totals[c]
            weights.append(jnp.exp2(expo).astype(BF16))
            tile_totals.append(total_new + older[:, 0:1] + terms[c][:, 0:1])
        return weights, tile_totals

    def tile_values(weights, keys0):
        return [_dot(weights[a * hb + h], v_ref[pl.ds(keys0[a], win), head_cols(h)])
                for a in range(nsub) for h in range(hb)]

    def slowest(totals):
        m = totals[0]
        for t in totals[1:]:
            m = jnp.maximum(m, t)
        return jnp.max(m)

    def qblock(i, carry, *, near_start):
        r0 = i * (tq * nsub)
        rows0 = [pl.multiple_of(r0 + a * tq, tq) for a in range(nsub)]
        if near_start:
            starts = [jnp.maximum(r - (win - tq), 0) for r in rows0]
        else:
            starts = [r - (win - tq) for r in rows0]
        keys0 = [pl.multiple_of(s, tq) for s in starts]
        weights, totals = tile_weights(tile_logits(rows0, keys0), rows0, keys0,
                                       [ATT_NO_LIMIT] * nsub if near_start else None, None)
        accs = tile_values(weights, keys0)

        def more(st):
            m, worst, _, _ = st
            return (starts[-1] - m * win > 0) & (worst > ATT_SKIP_BELOW)

        def older(st):
            m, _, totals, accs = st
            limits = [s - m * win for s in starts]
            keys0 = [pl.multiple_of(jnp.maximum(lim - win, 0), tq) for lim in limits]
            weights, tile_totals = tile_weights(tile_logits(rows0, keys0), rows0, keys0,
                                                limits, totals)
            outs = tile_values(weights, keys0)
            totals = tuple(t + d for t, d in zip(totals, tile_totals))
            accs = tuple(acc + o for acc, o in zip(accs, outs))
            return (m + 1, slowest(totals), totals, accs)

        _, _, _, accs = lax.while_loop(
            more, older, (jnp.int32(0), slowest(totals), tuple(totals), tuple(accs)))
        for a in range(nsub):
            for h in range(hb):
                o_ref[pl.ds(rows0[a], tq), head_cols(h)] = accs[a * hb + h].astype(BF16)
        return carry

    n_blocks = seq // (tq * nsub)
    n_clipped = min(-(-(win - tq) // (tq * nsub)), n_blocks)
    lax.fori_loop(0, n_clipped, functools.partial(qblock, near_start=True), 0)
    lax.fori_loop(n_clipped, n_blocks, functools.partial(qblock, near_start=False), 0)


def _stick_breaking(p, batch, seq, q_col, k_col, v_col):
    n = p.shape[0]
    width = ATT_HEADS_PER_STEP * HEAD_DIM
    n_hg = N_HEADS // ATT_HEADS_PER_STEP

    def spec(col):
        return pl.BlockSpec((seq, width), lambda b, hg: (b, col // width + hg))

    return pl.pallas_call(
        functools.partial(_attn_kernel, seq=seq),
        grid=(batch, n_hg),
        in_specs=[spec(q_col), spec(k_col), spec(v_col)],
        out_specs=pl.BlockSpec((seq, width), lambda b, hg: (b, hg)),
        out_shape=jax.ShapeDtypeStruct((n, N_HEADS * HEAD_DIM), BF16),
        scratch_shapes=[pltpu.VMEM((ATT_NEW, ATT_NEW), BF16)],
        compiler_params=pltpu.CompilerParams(
            dimension_semantics=("parallel", "parallel"), vmem_limit_bytes=VMEM_LIMIT),
        name="stick_breaking",
    )(p, p, p)


def _out_even_kernel(x_ref, gate_ref, ug_ref, vg_ref, zb_ref, bo_ref, ws_ref, bias_ref, wout_ref,
                     o_ref, y_scr):
    tm = x_ref.shape[0]
    a_width = bo_ref.shape[1]
    row = lax.broadcasted_iota(jnp.int32, (BLOCK, BLOCK), 0)
    col = lax.broadcasted_iota(jnp.int32, (BLOCK, BLOCK), 1)
    for g in range(a_width // HEAD_DIM):
        cols = slice(g * HEAD_DIM, (g + 1) * HEAD_DIM)
        wg = jnp.where(col <= row, ws_ref[g], 0.0).astype(BF16)
        for c in range(tm // BLOCK):
            rows = slice(c * BLOCK, (c + 1) * BLOCK)
            mixed = _dot(wg, vg_ref[rows, cols]) + bias_ref[:, cols]
            y_scr[rows, cols] = (ug_ref[rows, cols].astype(F32) * mixed).astype(BF16)
    y_scr[:, a_width:] = (bo_ref[...].astype(F32) * _silu(zb_ref[...].astype(F32))).astype(BF16)
    o_ref[...] = x_ref[...] + gate_ref[...] * _dot(y_scr[...], wout_ref[...])


def _out_even(x2, seq, gate, p, b_out, a_ws, bias_full, w_out_bf):
    n, d = x2.shape
    tm = OUT_EVEN_ROWS
    aw = b_out.shape[1]
    row = lambda c: pl.BlockSpec((tm, aw), lambda i: (i, c))
    const = lambda a: pl.BlockSpec(a.shape, lambda i: (0,) * a.ndim, pipeline_mode=pl.Buffered(1))
    return pl.pallas_call(
        _out_even_kernel,
        grid=(n // tm,),
        in_specs=[pl.BlockSpec((tm, d), lambda i: (i, 0)),
                  pl.BlockSpec((None, 1, d), lambda i: ((i * tm) // seq, 0, 0)),
                  row(4), row(0), row(5),
                  row(0),
                  const(a_ws), const(bias_full), const(w_out_bf)],
        out_specs=pl.BlockSpec((tm, d), lambda i: (i, 0)),
        out_shape=jax.ShapeDtypeStruct((n, d), F32),
        scratch_shapes=[pltpu.VMEM((tm, 2 * aw), BF16)],
        compiler_params=pltpu.CompilerParams(
            dimension_semantics=("parallel",), vmem_limit_bytes=VMEM_LIMIT),
        name="out_even",
    )(x2, gate, p, p, p, b_out, a_ws, bias_full, w_out_bf)


def _rope_operands(positions):
    half = HEAD_DIM // 2
    inv_freq = ROPE_BASE ** (-jnp.arange(half, dtype=F32) / half)
    freq = jnp.concatenate([inv_freq, inv_freq]).reshape(1, HEAD_DIM)
    sign = jnp.concatenate([-jnp.ones((half,), F32), jnp.ones((half,), F32)]).reshape(1, HEAD_DIM)
    return positions.reshape(-1, LANES).astype(F32), freq, sign


def _retention_tables():
    chunk = RET_CHUNK
    f32 = np.float32
    log_gamma = np.log1p(-np.exp2(-5.0 - np.arange(N_HEADS, dtype=f32))).astype(f32)
    idx = np.arange(chunk, dtype=f32)
    diff = idx[:, None] - idx[None, :]
    intra = np.where(diff >= 0, np.exp(log_gamma[:, None, None] * np.maximum(diff, 0.0)), 0.0)
    q_decay = np.exp(log_gamma[:, None] * (idx + 1.0))
    k_decay = np.exp(log_gamma[:, None] * (chunk - 1.0 - idx))
    chunk_decay = np.exp(log_gamma * chunk)
    tables = (intra,
              np.broadcast_to(q_decay[:, :, None], (N_HEADS, chunk, HEAD_DIM)),
              np.broadcast_to(k_decay[:, :, None], (N_HEADS, chunk, HEAD_DIM)),
              np.broadcast_to(chunk_decay[:, None, None], (N_HEADS, HEAD_DIM, HEAD_DIM)))
    return tuple(jnp.asarray(np.ascontiguousarray(t, dtype=f32)) for t in tables)


def _retention_rows(q_ref, k_ref, v_ref, idec_ref, qdec_ref, kdec_ref, cdec_ref, st_scr, emit):
    chunk = RET_CHUNK
    n_chunks = q_ref.shape[0] // chunk
    cols = [slice(h * HEAD_DIM, (h + 1) * HEAD_DIM) for h in range(N_HEADS)]
    rows = [slice(c * chunk, (c + 1) * chunk) for c in range(n_chunks)]

    kv = {}
    for c in range(n_chunks):
        for h in range(N_HEADS):
            k_decayed = (k_ref[rows[c], cols[h]].astype(F32) * kdec_ref[h]).astype(BF16)
            kv[h, c] = _dot_tn(k_decayed, v_ref[rows[c], cols[h]])
    state = {}
    for h in range(N_HEADS):
        s = st_scr[h]
        for c in range(n_chunks):
            state[h, c] = s.astype(BF16)
            s = s * cdec_ref[h] + kv[h, c]
        st_scr[h] = s
    for c in range(n_chunks):
        for h in range(N_HEADS):
            q = q_ref[rows[c], cols[h]]
            scores = _dot_nt(q, k_ref[rows[c], cols[h]]) * idec_ref[h]
            out = (_dot(scores.astype(BF16), v_ref[rows[c], cols[h]])
                   + _dot(q, state[h, c]) * qdec_ref[h])
            ms = jnp.mean(out * out, axis=-1, keepdims=True)
            emit(rows[c], h, out * lax.rsqrt(ms + EPS))


def _pooled_rows(pc_ref, halo_ref, cw_ref, cs_ref, ext_scr, seq_row0, emit):
    tm = pc_ref.shape[0]
    ext_scr[0:POOL_HALO, :] = jnp.where(seq_row0 == 0, jnp.zeros_like(halo_ref), halo_ref[...])
    ext_scr[POOL_HALO:, :] = pc_ref[...]
    src_rows = BLOCK + POOL_HALO
    lag = (lax.broadcasted_iota(jnp.int32, (BLOCK, src_rows), 0) + POOL_HALO
           - lax.broadcasted_iota(jnp.int32, (BLOCK, src_rows), 1))
    t_seq = seq_row0 + lax.broadcasted_iota(jnp.int32, (BLOCK, POOL_GROUP_DIM), 0)
    for gi, win in enumerate(POOL_WINDOWS):
        cols = slice(gi * POOL_GROUP_DIM, (gi + 1) * POOL_GROUP_DIM)
        band = jnp.where((lag >= 0) & (lag < win), 1.0 / win, 0.0).astype(BF16)
        short = jnp.where(t_seq + 1 < win, win / (t_seq + 1).astype(F32), 1.0)
        pooled = []
        for c in range(tm // BLOCK):
            mean = _dot(band, ext_scr[c * BLOCK:c * BLOCK + src_rows, cols])
            if c == 0:
                mean = mean * short
            cur = pc_ref[c * BLOCK:(c + 1) * BLOCK, cols].astype(F32)
            pooled.append((mean - cur).astype(BF16))
        emit(gi, _dot(jnp.concatenate(pooled, axis=0), cw_ref[gi]) * cs_ref[:, cols])


def _odd_mixers_kernel(p_ref, idec_ref, qdec_ref, kdec_ref, cdec_ref,
                       x_ref, gate_ref, halo_ref, cw_ref, cs_ref, wout_ref,
                       o_ref, st_scr, ext_scr, y_scr):
    tm = x_ref.shape[0]
    c_width = halo_ref.shape[1]
    pc_ref, q_ref, k_ref, v_ref, za_ref, zb_ref = (
        p_ref.at[:, pl.ds(j * c_width, c_width)] for j in range(6))

    @pl.when(pl.program_id(1) == 0)
    def _():
        st_scr[...] = jnp.zeros_like(st_scr)

    def gated_retention(rows, h, out):
        cols = slice(h * HEAD_DIM, (h + 1) * HEAD_DIM)
        ycols = slice(c_width + h * HEAD_DIM, c_width + (h + 1) * HEAD_DIM)
        y_scr[rows, ycols] = (out * _silu(zb_ref[rows, cols].astype(F32))).astype(BF16)

    def gated_pool(gi, mixed):
        cols = slice(gi * POOL_GROUP_DIM, (gi + 1) * POOL_GROUP_DIM)
        y_scr[:, cols] = (mixed * _silu(za_ref[:, cols].astype(F32))).astype(BF16)

    _retention_rows(q_ref, k_ref, v_ref, idec_ref, qdec_ref, kdec_ref, cdec_ref, st_scr,
                    gated_retention)
    _pooled_rows(pc_ref, halo_ref, cw_ref, cs_ref, ext_scr, pl.program_id(1) * tm, gated_pool)
    o_ref[...] = x_ref[...] + gate_ref[...] * _dot(y_scr[...], wout_ref[...])


def _odd_mixers(x2, batch, seq, gate, p, c_w_bf, c_scale, w_out_bf):
    n, d = x2.shape
    tm = ROW_TILE
    cw = N_HEADS * HEAD_DIM
    steps = seq // tm
    tables = _retention_tables()
    tabs = [pl.BlockSpec(t.shape, lambda b, s: (0, 0, 0)) for t in tables]
    halo_blocks = tm // POOL_HALO
    halo = pl.BlockSpec(
        (POOL_HALO, cw), lambda b, s: (jnp.maximum((b * steps + s) * halo_blocks - 1, 0), 0))
    return pl.pallas_call(
        _odd_mixers_kernel,
        grid=(batch, steps),
        in_specs=[pl.BlockSpec((tm, p.shape[1]), lambda b, s: (b * steps + s, 0))] + tabs + [
            pl.BlockSpec((tm, d), lambda b, s: (b * steps + s, 0)),
            pl.BlockSpec((None, 1, d), lambda b, s: (b, 0, 0)),
            halo,
            pl.BlockSpec(c_w_bf.shape, lambda b, s: (0, 0, 0)),
            pl.BlockSpec((1, cw), lambda b, s: (0, 0)),
            pl.BlockSpec(w_out_bf.shape, lambda b, s: (0, 0))],
        out_specs=pl.BlockSpec((tm, d), lambda b, s: (b * steps + s, 0)),
        out_shape=jax.ShapeDtypeStruct((n, d), F32),
        scratch_shapes=[pltpu.VMEM((N_HEADS, HEAD_DIM, HEAD_DIM), F32),
                        pltpu.VMEM((POOL_HALO + tm, cw), BF16),
                        pltpu.VMEM((tm, 2 * cw), BF16)],
        compiler_params=pltpu.CompilerParams(
            dimension_semantics=("parallel", "arbitrary"), vmem_limit_bytes=VMEM_LIMIT),
        name="odd_mixers",
    )(p, *tables, x2, gate, p, c_w_bf, c_scale.reshape(1, cw), w_out_bf)


def _even_layer(x2, batch, seq, mod, norm_g, w_in, a_vnorm_g, a_ws, a_bs,
                b_qnorm_g, b_knorm_g, w_out):
    d = x2.shape[1]
    shift, scale, gate = _split_modulation(mod)
    gains = [a_vnorm_g, b_qnorm_g.reshape(1, -1), b_knorm_g.reshape(1, -1)]
    modes = ("hold", "norm", "norm_scaled", "norm", "raw", "silu_gate", "raw")
    p = _in_projection(x2, seq, norm_g, scale, shift, w_in, gains, modes,
                       post_scale=LOG2E * HEAD_DIM ** -0.5)
    b_out = _stick_breaking(p, batch, seq, q_col=d, k_col=2 * d, v_col=3 * d)
    bias_full = jnp.repeat(a_bs.T, HEAD_DIM, axis=1)
    return _out_even(x2, seq, gate, p, b_out, a_ws, bias_full, w_out.astype(BF16))


def _odd_layer(x2, batch, seq, mod, positions, norm_g, w_in, c_w, c_scale,
               d_qnorm_g, d_knorm_g, w_out):
    shift, scale, gate = _split_modulation(mod)
    gains = [d_qnorm_g.reshape(1, -1), d_knorm_g.reshape(1, -1)]
    modes = ("raw", "rope", "rope_scaled", "raw", "raw", "raw")
    p = _in_projection(x2, seq, norm_g, scale, shift, w_in, gains, modes,
                       rope=_rope_operands(positions), post_scale=HEAD_DIM ** -0.5)
    return _odd_mixers(x2, batch, seq, gate, p, c_w.astype(BF16), c_scale, w_out.astype(BF16))


def kernel(x, c, positions, even_norm_g, even_w_mod, even_b_mod, even_w_in, even_a_vnorm_g, even_a_ws, even_a_bs, even_b_qnorm_g, even_b_knorm_g, even_w_out, odd_norm_g, odd_w_mod, odd_b_mod, odd_w_in, odd_c_w, odd_c_scale, odd_d_qnorm_g, odd_d_knorm_g, odd_w_out):
    batch, seq, d = x.shape
    depth = even_norm_g.shape[0] + odd_norm_g.shape[0]
    x2 = x.reshape(batch * seq, d)
    n_even = even_norm_g.shape[0]
    mods = _modulations(c, even_w_mod, even_b_mod, odd_w_mod, odd_b_mod)
    for layer in range(depth):
        i = layer // 2
        if layer % 2 == 0:
            x2 = _even_layer(x2, batch, seq, mods[i], even_norm_g[i],
                             even_w_in[i], even_a_vnorm_g[i], even_a_ws[i], even_a_bs[i],
                             even_b_qnorm_g[i], even_b_knorm_g[i], even_w_out[i])
        else:
            x2 = _odd_layer(x2, batch, seq, mods[n_even + i], positions, odd_norm_g[i],
                            odd_w_in[i], odd_c_w[i], odd_c_scale[i], odd_d_qnorm_g[i],
                            odd_d_knorm_g[i], odd_w_out[i])
    return x2.reshape(batch, seq, d)
```

```python
import functools

import jax
import jax.numpy as jnp
import numpy as np
from jax import lax
from jax.experimental import pallas as pl
from jax.experimental.pallas import tpu as pltpu

F32 = jnp.float32
BF16 = jnp.bfloat16

EPS = 1e-6
ROPE_BASE = 10000.0
BLOCK = 128
HEAD_DIM = 128
N_HEADS = 8
POOL_WINDOWS = (2, 4, 8, 16)
POOL_GROUP_DIM = 256
POOL_HALO = 16

MOD_ROWS = 8

ROW_TILE = 512
OUT_EVEN_ROWS = 1024
COL_TILE = 1024
RET_CHUNK = 128

ATT_Q = 64
ATT_WIN = 384
ATT_NEW = 256
ATT_EXCLUDED = -1e30
LANES = 128
ATT_SUB = 2
ATT_HEADS_PER_STEP = 4
ATT_SKIP_BELOW = -151.0
LOG2E = 1.4426950408889634
ATT_NO_LIMIT = 1 << 30

VMEM_LIMIT = 56 * 1024 * 1024


def _silu(v):
    return v * jax.nn.sigmoid(v)


def _dot(a, b):
    return jnp.dot(a, b, preferred_element_type=F32)


def _dot_nt(a, b):
    return lax.dot_general(a, b, (((1,), (1,)), ((), ())), preferred_element_type=F32)


def _dot_tn(a, b):
    return lax.dot_general(a, b, (((0,), (0,)), ((), ())), preferred_element_type=F32)


def _group_rms(a, gain):
    ms = jnp.mean(a * a, axis=-1, keepdims=True)
    return a * lax.rsqrt(ms + EPS) * gain


def _mod_kernel(c_ref, we_ref, be_ref, wo_ref, bo_ref, o_ref, *, n_even):
    a = _silu(c_ref[...]).astype(BF16)
    layer = pl.program_id(0)

    @pl.when(layer < n_even)
    def _():
        o_ref[...] = _dot(a, we_ref[...].astype(BF16)) + be_ref[...]

    @pl.when(layer >= n_even)
    def _():
        o_ref[...] = _dot(a, wo_ref[...].astype(BF16)) + bo_ref[...]


def _modulations(c, even_w, even_b, odd_w, odd_b):
    bsz, d = c.shape
    n_even, n_odd = even_w.shape[0], odd_w.shape[0]
    c8 = jnp.pad(c, ((0, MOD_ROWS - bsz), (0, 0)))
    even_at = lambda l, j: (jnp.minimum(l, n_even - 1), 0, jnp.where(l < n_even, j, 2))
    odd_at = lambda l, j: (jnp.maximum(l - n_even, 0), 0, jnp.where(l < n_even, 0, j))
    m = pl.pallas_call(
        functools.partial(_mod_kernel, n_even=n_even),
        grid=(n_even + n_odd, 3),
        in_specs=[pl.BlockSpec((MOD_ROWS, d), lambda l, j: (0, 0)),
                  pl.BlockSpec((None, d, d), even_at),
                  pl.BlockSpec((None, 1, d), even_at),
                  pl.BlockSpec((None, d, d), odd_at),
                  pl.BlockSpec((None, 1, d), odd_at)],
        out_specs=pl.BlockSpec((None, MOD_ROWS, d), lambda l, j: (l, 0, j)),
        out_shape=jax.ShapeDtypeStruct((n_even + n_odd, MOD_ROWS, 3 * d), F32),
        name="modulation",
    )(c8, even_w, even_b.reshape(n_even, 1, 3 * d), odd_w, odd_b.reshape(n_odd, 1, 3 * d))
    return m[:, :bsz]


def _split_modulation(m):
    d = m.shape[1] // 3
    return m[:, None, :d], m[:, None, d:2 * d], m[:, None, 2 * d:]


def _inproj_kernel(*refs, modes, out_tiles, gain_of, n_gains, post_scale, use_rope):
    *refs, h_scr, w_ref, stage, sem, held = refs
    x_ref, g_ref, sc_ref, sh_ref, w_hbm = refs[:5]
    gain_refs = refs[5:5 + n_gains]
    o_ref = refs[-1]
    tn = COL_TILE

    def weight_copy(j):
        return pltpu.make_async_copy(w_hbm.at[:, pl.ds(j * tn, tn)], stage.at[j % 2], sem.at[j % 2])

    @pl.when(pl.program_id(0) == 0)
    def _():
        weight_copy(0).start()
        for j in range(len(modes)):
            if j + 1 < len(modes):
                weight_copy(j + 1).start()
            weight_copy(j).wait()
            w_ref[:, j * tn:(j + 1) * tn] = stage[j % 2].astype(BF16)

    if use_rope:
        pos_ref, freq_ref, sign_ref = refs[5 + n_gains:8 + n_gains]
        on_diag = (lax.broadcasted_iota(jnp.int32, (LANES, LANES), 0)
                   == lax.broadcasted_iota(jnp.int32, (LANES, LANES), 1))
        angles = []
        for a in range(pos_ref.shape[0]):
            spread = jnp.where(on_diag, jnp.broadcast_to(pos_ref[a:a + 1, :], (LANES, LANES)), 0.0)
            angles.append(jnp.sum(spread, axis=1, keepdims=True) * freq_ref[...])
        ang = jnp.concatenate(angles, axis=0)
        cos_t = jnp.cos(ang)
        sin_t = jnp.sin(ang) * sign_ref[...]
    x = x_ref[...]
    ms = jnp.mean(x * x, axis=-1, keepdims=True)
    y = x * lax.rsqrt(ms + EPS) * g_ref[...]
    h_scr[...] = (y * (1.0 + sc_ref[...]) + sh_ref[...]).astype(BF16)

    cross_lane = ("norm", "norm_scaled", "rope", "rope_scaled")
    heavy = [jj for jj, m in enumerate(modes) if m in cross_lane]
    plain = [jj for jj, m in enumerate(modes) if m not in cross_lane]
    order = []
    while heavy or plain:
        order += heavy[:1] + plain[:1]
        heavy, plain = heavy[1:], plain[1:]
    for jj in order:
        mode = modes[jj]
        out0 = None if out_tiles[jj] is None else out_tiles[jj] * tn
        acc = _dot(h_scr[...], w_ref[:, jj * tn:(jj + 1) * tn])
        if mode == "hold":
            held[...] = acc
            continue
        if mode == "raw":
            o_ref[:, out0:out0 + tn] = acc.astype(BF16)
            continue
        if mode == "silu_gate":
            o_ref[:, out0:out0 + tn] = (held[...] * _silu(acc)).astype(BF16)
            continue
        gain_ref = gain_refs[gain_of[jj]]
        for g in range(tn // HEAD_DIM):
            row = g if gain_ref.shape[0] > 1 else 0
            t = _group_rms(acc[:, g * HEAD_DIM:(g + 1) * HEAD_DIM], gain_ref[row:row + 1, :])
            if mode in ("rope", "rope_scaled"):
                t = t * cos_t + pltpu.roll(t, HEAD_DIM // 2, axis=1) * sin_t
            if mode in ("norm_scaled", "rope_scaled"):
                t = t * post_scale
            o_ref[:, out0 + g * HEAD_DIM:out0 + (g + 1) * HEAD_DIM] = t.astype(BF16)


def _in_projection(x2, seq, norm_g, scale, shift, w_in, gains, modes, rope=None, post_scale=1.0):
    n, d = x2.shape
    tm, tn = ROW_TILE, COL_TILE
    ncols = len(modes) * tn
    out_tiles, n_out = [], 0
    for m in modes:
        out_tiles.append(None if m == "hold" else n_out)
        n_out += m != "hold"
    assert modes.count("hold") == modes.count("silu_gate") <= 1
    assert "hold" not in modes or modes.index("hold") < modes.index("silu_gate")
    use_rope = rope is not None
    normalising = [jj for jj, m in enumerate(modes) if m in ("norm", "norm_scaled", "rope", "rope_scaled")]
    assert len(gains) == len(normalising)
    gain_of = tuple(normalising.index(jj) if jj in normalising else None for jj in range(len(modes)))
    in_specs = [
        pl.BlockSpec((tm, d), lambda i: (i, 0)),
        pl.BlockSpec((1, d), lambda i: (0, 0)),
        pl.BlockSpec((None, 1, d), lambda i: ((i * tm) // seq, 0, 0)),
        pl.BlockSpec((None, 1, d), lambda i: ((i * tm) // seq, 0, 0)),
        pl.BlockSpec(memory_space=pl.ANY),
    ] + [pl.BlockSpec(gn.shape, lambda i: (0, 0)) for gn in gains]
    args = [x2, norm_g.reshape(1, d), scale, shift, w_in, *gains]
    if use_rope:
        vec = pl.BlockSpec((1, HEAD_DIM), lambda i: (0, 0))
        pos, freq, sign = rope
        in_specs += [pl.BlockSpec((None, tm // LANES, LANES), lambda i: (i, 0, 0)), vec, vec]
        args += [pos.reshape(n // tm, tm // LANES, LANES), freq, sign]
    return pl.pallas_call(
        functools.partial(_inproj_kernel, modes=tuple(modes), out_tiles=tuple(out_tiles),
                          gain_of=gain_of, n_gains=len(gains), post_scale=post_scale,
                          use_rope=use_rope),
        grid=(n // tm,),
        in_specs=in_specs,
        out_specs=pl.BlockSpec((tm, n_out * tn), lambda i: (i, 0)),
        out_shape=jax.ShapeDtypeStruct((n, n_out * tn), BF16),
        scratch_shapes=[pltpu.VMEM((tm, d), BF16),
                        pltpu.VMEM((d, ncols), BF16),
                        pltpu.VMEM((2, d, tn), F32),
                        pltpu.SemaphoreType.DMA((2,)),
                        pltpu.VMEM((tm, tn), F32)],
        compiler_params=pltpu.CompilerParams(
            dimension_semantics=("arbitrary",), vmem_limit_bytes=VMEM_LIMIT),
        name="in_projection",
    )(*args)


def _attn_kernel(q_ref, k_ref, v_ref, o_ref, u_scr, *, seq):
    hb, tq, win, nsub = ATT_HEADS_PER_STEP, ATT_Q, ATT_WIN, ATT_SUB
    chains = nsub * hb

    u_scr[...] = jnp.where(lax.broadcasted_iota(jnp.int32, (ATT_NEW, ATT_NEW), 0)
                           > lax.broadcasted_iota(jnp.int32, (ATT_NEW, ATT_NEW), 1),
                           1.0, 0.0).astype(BF16)
    old = win - ATT_NEW
    key_idx = lax.broadcasted_iota(jnp.int32, (tq, win), 1)
    row_idx = lax.broadcasted_iota(jnp.int32, (tq, win), 0)
    tail = win - LANES
    tail_ok = (lax.broadcasted_iota(jnp.int32, (tq, LANES), 1) + tail
               < lax.broadcasted_iota(jnp.int32, (tq, LANES), 0) + (win - tq))

    def head_cols(h):
        return slice(h * HEAD_DIM, (h + 1) * HEAD_DIM)

    def tile_logits(rows0, keys0):
        return [_dot_nt(q_ref[pl.ds(rows0[a], tq), head_cols(h)],
                        k_ref[pl.ds(keys0[a], win), head_cols(h)])
                for a in range(nsub) for h in range(hb)]

    def tile_weights(logits, rows0, keys0, limits, totals):
        log_beta, terms = [], []
        for a in range(nsub):
            if limits is not None:
                ok = key_idx < (jnp.minimum(rows0[a] + row_idx, limits[a]) - keys0[a])
            for h in range(hb):
                z = logits[a * hb + h]
                if limits is None:
                    z = jnp.concatenate(
                        [z[:, :tail], jnp.where(tail_ok, z[:, tail:], ATT_EXCLUDED)], axis=1)
                else:
                    z = jnp.where(ok, z, ATT_EXCLUDED)
                nz = -z
                lsm = (jnp.minimum(nz, 0.0)
                       - jnp.log(1.0 + jnp.exp2(jnp.minimum(z, nz))) * LOG2E)
                log_beta.append(z + lsm)
                terms.append(lsm)
        sums_new = _dot(jnp.concatenate([t[:, old:].astype(BF16) for t in terms], axis=0),
                        u_scr[...])
        sums_old = _dot(jnp.concatenate([t[:, :old].astype(BF16) for t in terms], axis=0),
                        u_scr[0:old, 0:old])
        weights, tile_totals = [], []
        for c in range(chains):
            newer = sums_new[c * tq:(c + 1) * tq]
            older = sums_old[c * tq:(c + 1) * tq]
            total_new = newer[:, 0:1] + terms[c][:, old:old + 1]
            expo = log_beta[c] + jnp.concatenate([older + total_new, newer], axis=1)
            if totals is not None:
                expo = expo + totals[c]
            weights.append(jnp.exp2(expo).astype(BF16))
            tile_totals.append(total_new + older[:, 0:1] + terms[c][:, 0:1])
        return weights, tile_totals

    def tile_values(weights, keys0):
        return [_dot(weights[a * hb + h], v_ref[pl.ds(keys0[a], win), head_cols(h)])
                for a in range(nsub) for h in range(hb)]

    def slowest(totals):
        m = totals[0]
        for t in totals[1:]:
            m = jnp.maximum(m, t)
        return jnp.max(m)

    def qblock(i, carry, *, near_start):
        r0 = i * (tq * nsub)
        rows0 = [pl.multiple_of(r0 + a * tq, tq) for a in range(nsub)]
        if near_start:
            starts = [jnp.maximum(r - (win - tq), 0) for r in rows0]
        else:
            starts = [r - (win - tq) for r in rows0]
        keys0 = [pl.multiple_of(s, tq) for s in starts]
        weights, totals = tile_weights(tile_logits(rows0, keys0), rows0, keys0,
                                       [ATT_NO_LIMIT] * nsub if near_start else None, None)
        accs = tile_values(weights, keys0)

        def more(st):
            m, worst, _, _ = st
            return (starts[-1] - m * win > 0) & (worst > ATT_SKIP_BELOW)

        def older(st):
            m, _, totals, accs = st
            limits = [s - m * win for s in starts]
            keys0 = [pl.multiple_of(jnp.maximum(lim - win, 0), tq) for lim in limits]
            weights, tile_totals = tile_weights(tile_logits(rows0, keys0), rows0, keys0,
                                                limits, totals)
            outs = tile_values(weights, keys0)
            totals = tuple(t + d for t, d in zip(totals, tile_totals))
            accs = tuple(acc + o for acc, o in zip(accs, outs))
            return (m + 1, slowest(totals), totals, accs)

        _, _, _, accs = lax.while_loop(
            more, older, (jnp.int32(0), slowest(totals), tuple(totals), tuple(accs)))
        for a in range(nsub):
            for h in range(hb):
                o_ref[pl.ds(rows0[a], tq), head_cols(h)] = accs[a * hb + h].astype(BF16)
        return carry

    n_blocks = seq // (tq * nsub)
    n_clipped = min(-(-(win - tq) // (tq * nsub)), n_blocks)
    lax.fori_loop(0, n_clipped, functools.partial(qblock, near_start=True), 0)
    lax.fori_loop(n_clipped, n_blocks, functools.partial(qblock, near_start=False), 0)


def _stick_breaking(p, batch, seq, q_col, k_col, v_col):
    n = p.shape[0]
    width = ATT_HEADS_PER_STEP * HEAD_DIM
    n_hg = N_HEADS // ATT_HEADS_PER_STEP

    def spec(col):
        return pl.BlockSpec((seq, width), lambda b, hg: (b, col // width + hg))

    return pl.pallas_call(
        functools.partial(_attn_kernel, seq=seq),
        grid=(batch, n_hg),
        in_specs=[spec(q_col), spec(k_col), spec(v_col)],
        out_specs=pl.BlockSpec((seq, width), lambda b, hg: (b, hg)),
        out_shape=jax.ShapeDtypeStruct((n, N_HEADS * HEAD_DIM), BF16),
        scratch_shapes=[pltpu.VMEM((ATT_NEW, ATT_NEW), BF16)],
        compiler_params=pltpu.CompilerParams(
            dimension_semantics=("parallel", "parallel"), vmem_limit_bytes=VMEM_LIMIT),
        name="stick_breaking",
    )(p, p, p)


def _out_even_kernel(x_ref, gate_ref, ug_ref, vg_ref, zb_ref, bo_ref, ws_ref, bias_ref, wout_ref,
                     o_ref, y_scr):
    tm = x_ref.shape[0]
    a_width = bo_ref.shape[1]
    row = lax.broadcasted_iota(jnp.int32, (BLOCK, BLOCK), 0)
    col = lax.broadcasted_iota(jnp.int32, (BLOCK, BLOCK), 1)
    for g in range(a_width // HEAD_DIM):
        cols = slice(g * HEAD_DIM, (g + 1) * HEAD_DIM)
        wg = jnp.where(col <= row, ws_ref[g], 0.0).astype(BF16)
        for c in range(tm // BLOCK):
            rows = slice(c * BLOCK, (c + 1) * BLOCK)
            mixed = _dot(wg, vg_ref[rows, cols]) + bias_ref[:, cols]
            y_scr[rows, cols] = (ug_ref[rows, cols].astype(F32) * mixed).astype(BF16)
    y_scr[:, a_width:] = (bo_ref[...].astype(F32) * _silu(zb_ref[...].astype(F32))).astype(BF16)
    o_ref[...] = x_ref[...] + gate_ref[...] * _dot(y_scr[...], wout_ref[...])


def _out_even(x2, seq, gate, p, b_out, a_ws, bias_full, w_out_bf):
    n, d = x2.shape
    tm = OUT_EVEN_ROWS
    aw = b_out.shape[1]
    row = lambda c: pl.BlockSpec((tm, aw), lambda i: (i, c))
    const = lambda a: pl.BlockSpec(a.shape, lambda i: (0,) * a.ndim, pipeline_mode=pl.Buffered(1))
    return pl.pallas_call(
        _out_even_kernel,
        grid=(n // tm,),
        in_specs=[pl.BlockSpec((tm, d), lambda i: (i, 0)),
                  pl.BlockSpec((None, 1, d), lambda i: ((i * tm) // seq, 0, 0)),
                  row(4), row(0), row(5),
                  row(0),
                  const(a_ws), const(bias_full), const(w_out_bf)],
        out_specs=pl.BlockSpec((tm, d), lambda i: (i, 0)),
        out_shape=jax.ShapeDtypeStruct((n, d), F32),
        scratch_shapes=[pltpu.VMEM((tm, 2 * aw), BF16)],
        compiler_params=pltpu.CompilerParams(
            dimension_semantics=("parallel",), vmem_limit_bytes=VMEM_LIMIT),
        name="out_even",
    )(x2, gate, p, p, p, b_out, a_ws, bias_full, w_out_bf)


def _rope_operands(positions):
    half = HEAD_DIM // 2
    inv_freq = ROPE_BASE ** (-jnp.arange(half, dtype=F32) / half)
    freq = jnp.concatenate([inv_freq, inv_freq]).reshape(1, HEAD_DIM)
    sign = jnp.concatenate([-jnp.ones((half,), F32), jnp.ones((half,), F32)]).reshape(1, HEAD_DIM)
    return positions.reshape(-1, LANES).astype(F32), freq, sign


def _retention_tables():
    chunk = RET_CHUNK
    f32 = np.float32
    log_gamma = np.log1p(-np.exp2(-5.0 - np.arange(N_HEADS, dtype=f32))).astype(f32)
    idx = np.arange(chunk, dtype=f32)
    diff = idx[:, None] - idx[None, :]
    intra = np.where(diff >= 0, np.exp(log_gamma[:, None, None] * np.maximum(diff, 0.0)), 0.0)
    q_decay = np.exp(log_gamma[:, None] * (idx + 1.0))
    k_decay = np.exp(log_gamma[:, None] * (chunk - 1.0 - idx))
    chunk_decay = np.exp(log_gamma * chunk)
    tables = (intra,
              np.broadcast_to(q_decay[:, :, None], (N_HEADS, chunk, HEAD_DIM)),
              np.broadcast_to(k_decay[:, :, None], (N_HEADS, chunk, HEAD_DIM)),
              np.broadcast_to(chunk_decay[:, None, None], (N_HEADS, HEAD_DIM, HEAD_DIM)))
    return tuple(jnp.asarray(np.ascontiguousarray(t, dtype=f32)) for t in tables)


def _retention_rows(q_ref, k_ref, v_ref, idec_ref, qdec_ref, kdec_ref, cdec_ref, st_scr, emit,
                    between=None):
    chunk = RET_CHUNK
    n_chunks = q_ref.shape[0] // chunk
    cols = [slice(h * HEAD_DIM, (h + 1) * HEAD_DIM) for h in range(N_HEADS)]
    rows = [slice(c * chunk, (c + 1) * chunk) for c in range(n_chunks)]

    kv = {}
    for c in range(n_chunks):
        for h in range(N_HEADS):
            k_decayed = (k_ref[rows[c], cols[h]].astype(F32) * kdec_ref[h]).astype(BF16)
            kv[h, c] = _dot_tn(k_decayed, v_ref[rows[c], cols[h]])
    state = {}
    for h in range(N_HEADS):
        s = st_scr[h]
        for c in range(n_chunks):
            state[h, c] = s.astype(BF16)
            s = s * cdec_ref[h] + kv[h, c]
        st_scr[h] = s
    if between is not None:
        between()
    for c in range(n_chunks):
        for h in range(N_HEADS):
            q = q_ref[rows[c], cols[h]]
            scores = _dot_nt(q, k_ref[rows[c], cols[h]]) * idec_ref[h]
            out = (_dot(scores.astype(BF16), v_ref[rows[c], cols[h]])
                   + _dot(q, state[h, c]) * qdec_ref[h])
            ms = jnp.mean(out * out, axis=-1, keepdims=True)
            emit(rows[c], h, out * lax.rsqrt(ms + EPS))


def _pooled_rows(pc_ref, halo_ref, cw_ref, cs_ref, ext_scr, seq_row0, emit):
    tm = pc_ref.shape[0]
    ext_scr[0:POOL_HALO, :] = jnp.where(seq_row0 == 0, jnp.zeros_like(halo_ref), halo_ref[...])
    ext_scr[POOL_HALO:, :] = pc_ref[...]
    src_rows = BLOCK + POOL_HALO
    lag = (lax.broadcasted_iota(jnp.int32, (BLOCK, src_rows), 0) + POOL_HALO
           - lax.broadcasted_iota(jnp.int32, (BLOCK, src_rows), 1))
    t_seq = seq_row0 + lax.broadcasted_iota(jnp.int32, (BLOCK, POOL_GROUP_DIM), 0)
    for gi, win in enumerate(POOL_WINDOWS):
        cols = slice(gi * POOL_GROUP_DIM, (gi + 1) * POOL_GROUP_DIM)
        band = jnp.where((lag >= 0) & (lag < win), 1.0 / win, 0.0).astype(BF16)
        short = jnp.where(t_seq + 1 < win, win / (t_seq + 1).astype(F32), 1.0)
        pooled = []
        for c in range(tm // BLOCK):
            mean = _dot(band, ext_scr[c * BLOCK:c * BLOCK + src_rows, cols])
            if c == 0:
                mean = mean * short
            cur = pc_ref[c * BLOCK:(c + 1) * BLOCK, cols].astype(F32)
            pooled.append((mean - cur).astype(BF16))
        emit(gi, _dot(jnp.concatenate(pooled, axis=0), cw_ref[gi]) * cs_ref[:, cols])


def _odd_mixers_kernel(p_ref, idec_ref, qdec_ref, kdec_ref, cdec_ref,
                       x_ref, gate_ref, halo_ref, cw_ref, cs_ref, wout_ref,
                       o_ref, st_scr, ext_scr, y_scr):
    tm = x_ref.shape[0]
    c_width = halo_ref.shape[1]
    pc_ref, q_ref, k_ref, v_ref, za_ref, zb_ref = (
        p_ref.at[:, pl.ds(j * c_width, c_width)] for j in range(6))

    @pl.when(pl.program_id(1) == 0)
    def _():
        st_scr[...] = jnp.zeros_like(st_scr)

    def gated_retention(rows, h, out):
        cols = slice(h * HEAD_DIM, (h + 1) * HEAD_DIM)
        ycols = slice(c_width + h * HEAD_DIM, c_width + (h + 1) * HEAD_DIM)
        y_scr[rows, ycols] = (out * _silu(zb_ref[rows, cols].astype(F32))).astype(BF16)

    def gated_pool(gi, mixed):
        cols = slice(gi * POOL_GROUP_DIM, (gi + 1) * POOL_GROUP_DIM)
        y_scr[:, cols] = (mixed * _silu(za_ref[:, cols].astype(F32))).astype(BF16)

    _retention_rows(q_ref, k_ref, v_ref, idec_ref, qdec_ref, kdec_ref, cdec_ref, st_scr,
                    gated_retention,
                    between=lambda: _pooled_rows(pc_ref, halo_ref, cw_ref, cs_ref, ext_scr,
                                                 pl.program_id(1) * tm, gated_pool))
    o_ref[...] = x_ref[...] + gate_ref[...] * _dot(y_scr[...], wout_ref[...])


def _odd_mixers(x2, batch, seq, gate, p, c_w_bf, c_scale, w_out_bf):
    n, d = x2.shape
    tm = ROW_TILE
    cw = N_HEADS * HEAD_DIM
    steps = seq // tm
    tables = _retention_tables()
    tabs = [pl.BlockSpec(t.shape, lambda b, s: (0, 0, 0)) for t in tables]
    halo_blocks = tm // POOL_HALO
    halo = pl.BlockSpec(
        (POOL_HALO, cw), lambda b, s: (jnp.maximum((b * steps + s) * halo_blocks - 1, 0), 0))
    return pl.pallas_call(
        _odd_mixers_kernel,
        grid=(batch, steps),
        in_specs=[pl.BlockSpec((tm, p.shape[1]), lambda b, s: (b * steps + s, 0))] + tabs + [
            pl.BlockSpec((tm, d), lambda b, s: (b * steps + s, 0)),
            pl.BlockSpec((None, 1, d), lambda b, s: (b, 0, 0)),
            halo,
            pl.BlockSpec(c_w_bf.shape, lambda b, s: (0, 0, 0)),
            pl.BlockSpec((1, cw), lambda b, s: (0, 0)),
            pl.BlockSpec(w_out_bf.shape, lambda b, s: (0, 0))],
        out_specs=pl.BlockSpec((tm, d), lambda b, s: (b * steps + s, 0)),
        out_shape=jax.ShapeDtypeStruct((n, d), F32),
        scratch_shapes=[pltpu.VMEM((N_HEADS, HEAD_DIM, HEAD_DIM), F32),
                        pltpu.VMEM((POOL_HALO + tm, cw), BF16),
                        pltpu.VMEM((tm, 2 * cw), BF16)],
        compiler_params=pltpu.CompilerParams(
            dimension_semantics=("parallel", "arbitrary"), vmem_limit_bytes=VMEM_LIMIT),
        name="odd_mixers",
    )(p, *tables, x2, gate, p, c_w_bf, c_scale.reshape(1, cw), w_out_bf)


def _even_layer(x2, batch, seq, mod, norm_g, w_in, a_vnorm_g, a_ws, a_bs,
                b_qnorm_g, b_knorm_g, w_out):
    d = x2.shape[1]
    shift, scale, gate = _split_modulation(mod)
    gains = [a_vnorm_g, b_qnorm_g.reshape(1, -1), b_knorm_g.reshape(1, -1)]
    modes = ("hold", "norm", "norm_scaled", "norm", "raw", "silu_gate", "raw")
    p = _in_projection(x2, seq, norm_g, scale, shift, w_in, gains, modes,
                       post_scale=LOG2E * HEAD_DIM ** -0.5)
    b_out = _stick_breaking(p, batch, seq, q_col=d, k_col=2 * d, v_col=3 * d)
    bias_full = jnp.repeat(a_bs.T, HEAD_DIM, axis=1)
    return _out_even(x2, seq, gate, p, b_out, a_ws, bias_full, w_out.astype(BF16))


def _odd_layer(x2, batch, seq, mod, positions, norm_g, w_in, c_w, c_scale,
               d_qnorm_g, d_knorm_g, w_out):
    shift, scale, gate = _split_modulation(mod)
    gains = [d_qnorm_g.reshape(1, -1), d_knorm_g.reshape(1, -1)]
    modes = ("raw", "rope", "rope_scaled", "raw", "raw", "raw")
    p = _in_projection(x2, seq, norm_g, scale, shift, w_in, gains, modes,
                       rope=_rope_operands(positions), post_scale=HEAD_DIM ** -0.5)
    return _odd_mixers(x2, batch, seq, gate, p, c_w.astype(BF16), c_scale, w_out.astype(BF16))


def kernel(x, c, positions, even_norm_g, even_w_mod, even_b_mod, even_w_in, even_a_vnorm_g, even_a_ws, even_a_bs, even_b_qnorm_g, even_b_knorm_g, even_w_out, odd_norm_g, odd_w_mod, odd_b_mod, odd_w_in, odd_c_w, odd_c_scale, odd_d_qnorm_g, odd_d_knorm_g, odd_w_out):
    batch, seq, d = x.shape
    depth = even_norm_g.shape[0] + odd_norm_g.shape[0]
    x2 = x.reshape(batch * seq, d)
    n_even = even_norm_g.shape[0]
    mods = _modulations(c, even_w_mod, even_b_mod, odd_w_mod, odd_b_mod)
    for layer in range(depth):
        i = layer // 2
        if layer % 2 == 0:
            x2 = _even_layer(x2, batch, seq, mods[i], even_norm_g[i],
                             even_w_in[i], even_a_vnorm_g[i], even_a_ws[i], even_a_bs[i],
                             even_b_qnorm_g[i], even_b_knorm_g[i], even_w_out[i])
        else:
            x2 = _odd_layer(x2, batch, seq, mods[n_even + i], positions, odd_norm_g[i],
                            odd_w_in[i], odd_c_w[i], odd_c_scale[i], odd_d_qnorm_g[i],
                            odd_d_knorm_g[i], odd_w_out[i])
    return x2.reshape(batch, seq, d)
```

```python
import functools

import jax
import jax.numpy as jnp
import numpy as np
from jax import lax
from jax.experimental import pallas as pl
from jax.experimental.pallas import tpu as pltpu

F32 = jnp.float32
BF16 = jnp.bfloat16

EPS = 1e-6
ROPE_BASE = 10000.0
BLOCK = 128
HEAD_DIM = 128
N_HEADS = 8
POOL_WINDOWS = (2, 4, 8, 16)
POOL_GROUP_DIM = 256
POOL_HALO = 16

MOD_ROWS = 8

ROW_TILE = 512
OUT_EVEN_ROWS = 1024
COL_TILE = 1024
RET_CHUNK = 256

ATT_Q = 64
ATT_WIN = 384
ATT_NEW = 256
ATT_EXCLUDED = -1e30
LANES = 128
ATT_SUB = 2
ATT_HEADS_PER_STEP = 4
ATT_SKIP_BELOW = -151.0
LOG2E = 1.4426950408889634
ATT_NO_LIMIT = 1 << 30

VMEM_LIMIT = 56 * 1024 * 1024


def _silu(v):
    return v * jax.nn.sigmoid(v)


def _dot(a, b):
    return jnp.dot(a, b, preferred_element_type=F32)


def _dot_nt(a, b):
    return lax.dot_general(a, b, (((1,), (1,)), ((), ())), preferred_element_type=F32)


def _dot_tn(a, b):
    return lax.dot_general(a, b, (((0,), (0,)), ((), ())), preferred_element_type=F32)


def _group_rms(a, gain):
    ms = jnp.mean(a * a, axis=-1, keepdims=True)
    return a * lax.rsqrt(ms + EPS) * gain


def _mod_kernel(c_ref, we_ref, be_ref, wo_ref, bo_ref, o_ref, *, n_even):
    a = _silu(c_ref[...]).astype(BF16)
    layer = pl.program_id(0)

    @pl.when(layer < n_even)
    def _():
        o_ref[...] = _dot(a, we_ref[...].astype(BF16)) + be_ref[...]

    @pl.when(layer >= n_even)
    def _():
        o_ref[...] = _dot(a, wo_ref[...].astype(BF16)) + bo_ref[...]


def _modulations(c, even_w, even_b, odd_w, odd_b):
    bsz, d = c.shape
    n_even, n_odd = even_w.shape[0], odd_w.shape[0]
    c8 = jnp.pad(c, ((0, MOD_ROWS - bsz), (0, 0)))
    even_at = lambda l, j: (jnp.minimum(l, n_even - 1), 0, jnp.where(l < n_even, j, 2))
    odd_at = lambda l, j: (jnp.maximum(l - n_even, 0), 0, jnp.where(l < n_even, 0, j))
    m = pl.pallas_call(
        functools.partial(_mod_kernel, n_even=n_even),
        grid=(n_even + n_odd, 3),
        in_specs=[pl.BlockSpec((MOD_ROWS, d), lambda l, j: (0, 0)),
                  pl.BlockSpec((None, d, d), even_at),
                  pl.BlockSpec((None, 1, d), even_at),
                  pl.BlockSpec((None, d, d), odd_at),
                  pl.BlockSpec((None, 1, d), odd_at)],
        out_specs=pl.BlockSpec((None, MOD_ROWS, d), lambda l, j: (l, 0, j)),
        out_shape=jax.ShapeDtypeStruct((n_even + n_odd, MOD_ROWS, 3 * d), F32),
        name="modulation",
    )(c8, even_w, even_b.reshape(n_even, 1, 3 * d), odd_w, odd_b.reshape(n_odd, 1, 3 * d))
    return m[:, :bsz]


def _split_modulation(m):
    d = m.shape[1] // 3
    return m[:, None, :d], m[:, None, d:2 * d], m[:, None, 2 * d:]


def _inproj_kernel(*refs, modes, out_tiles, gain_of, n_gains, post_scale, use_rope):
    *refs, h_scr, w_ref, stage, sem, held = refs
    x_ref, g_ref, sc_ref, sh_ref, w_hbm = refs[:5]
    gain_refs = refs[5:5 + n_gains]
    o_ref = refs[-1]
    tn = COL_TILE

    def weight_copy(j):
        return pltpu.make_async_copy(w_hbm.at[:, pl.ds(j * tn, tn)], stage.at[j % 2], sem.at[j % 2])

    @pl.when(pl.program_id(0) == 0)
    def _():
        weight_copy(0).start()
        for j in range(len(modes)):
            if j + 1 < len(modes):
                weight_copy(j + 1).start()
            weight_copy(j).wait()
            w_ref[:, j * tn:(j + 1) * tn] = stage[j % 2].astype(BF16)

    if use_rope:
        pos_ref, freq_ref, sign_ref = refs[5 + n_gains:8 + n_gains]
        on_diag = (lax.broadcasted_iota(jnp.int32, (LANES, LANES), 0)
                   == lax.broadcasted_iota(jnp.int32, (LANES, LANES), 1))
        angles = []
        for a in range(pos_ref.shape[0]):
            spread = jnp.where(on_diag, jnp.broadcast_to(pos_ref[a:a + 1, :], (LANES, LANES)), 0.0)
            angles.append(jnp.sum(spread, axis=1, keepdims=True) * freq_ref[...])
        ang = jnp.concatenate(angles, axis=0)
        cos_t = jnp.cos(ang)
        sin_t = jnp.sin(ang) * sign_ref[...]
    x = x_ref[...]
    ms = jnp.mean(x * x, axis=-1, keepdims=True)
    y = x * lax.rsqrt(ms + EPS) * g_ref[...]
    h_scr[...] = (y * (1.0 + sc_ref[...]) + sh_ref[...]).astype(BF16)

    cross_lane = ("norm", "norm_scaled", "rope", "rope_scaled")
    heavy = [jj for jj, m in enumerate(modes) if m in cross_lane]
    plain = [jj for jj, m in enumerate(modes) if m not in cross_lane]
    order = []
    while heavy or plain:
        order += heavy[:1] + plain[:1]
        heavy, plain = heavy[1:], plain[1:]
    for jj in order:
        mode = modes[jj]
        out0 = None if out_tiles[jj] is None else out_tiles[jj] * tn
        acc = _dot(h_scr[...], w_ref[:, jj * tn:(jj + 1) * tn])
        if mode == "hold":
            held[...] = acc
            continue
        if mode == "raw":
            o_ref[:, out0:out0 + tn] = acc.astype(BF16)
            continue
        if mode == "silu_gate":
            o_ref[:, out0:out0 + tn] = (held[...] * _silu(acc)).astype(BF16)
            continue
        gain_ref = gain_refs[gain_of[jj]]
        for g in range(tn // HEAD_DIM):
            row = g if gain_ref.shape[0] > 1 else 0
            t = _group_rms(acc[:, g * HEAD_DIM:(g + 1) * HEAD_DIM], gain_ref[row:row + 1, :])
            if mode in ("rope", "rope_scaled"):
                t = t * cos_t + pltpu.roll(t, HEAD_DIM // 2, axis=1) * sin_t
            if mode in ("norm_scaled", "rope_scaled"):
                t = t * post_scale
            o_ref[:, out0 + g * HEAD_DIM:out0 + (g + 1) * HEAD_DIM] = t.astype(BF16)


def _in_projection(x2, seq, norm_g, scale, shift, w_in, gains, modes, rope=None, post_scale=1.0):
    n, d = x2.shape
    tm, tn = ROW_TILE, COL_TILE
    ncols = len(modes) * tn
    out_tiles, n_out = [], 0
    for m in modes:
        out_tiles.append(None if m == "hold" else n_out)
        n_out += m != "hold"
    assert modes.count("hold") == modes.count("silu_gate") <= 1
    assert "hold" not in modes or modes.index("hold") < modes.index("silu_gate")
    use_rope = rope is not None
    normalising = [jj for jj, m in enumerate(modes) if m in ("norm", "norm_scaled", "rope", "rope_scaled")]
    assert len(gains) == len(normalising)
    gain_of = tuple(normalising.index(jj) if jj in normalising else None for jj in range(len(modes)))
    in_specs = [
        pl.BlockSpec((tm, d), lambda i: (i, 0)),
        pl.BlockSpec((1, d), lambda i: (0, 0)),
        pl.BlockSpec((None, 1, d), lambda i: ((i * tm) // seq, 0, 0)),
        pl.BlockSpec((None, 1, d), lambda i: ((i * tm) // seq, 0, 0)),
        pl.BlockSpec(memory_space=pl.ANY),
    ] + [pl.BlockSpec(gn.shape, lambda i: (0, 0)) for gn in gains]
    args = [x2, norm_g.reshape(1, d), scale, shift, w_in, *gains]
    if use_rope:
        vec = pl.BlockSpec((1, HEAD_DIM), lambda i: (0, 0))
        pos, freq, sign = rope
        in_specs += [pl.BlockSpec((None, tm // LANES, LANES), lambda i: (i, 0, 0)), vec, vec]
        args += [pos.reshape(n // tm, tm // LANES, LANES), freq, sign]
    return pl.pallas_call(
        functools.partial(_inproj_kernel, modes=tuple(modes), out_tiles=tuple(out_tiles),
                          gain_of=gain_of, n_gains=len(gains), post_scale=post_scale,
                          use_rope=use_rope),
        grid=(n // tm,),
        in_specs=in_specs,
        out_specs=pl.BlockSpec((tm, n_out * tn), lambda i: (i, 0)),
        out_shape=jax.ShapeDtypeStruct((n, n_out * tn), BF16),
        scratch_shapes=[pltpu.VMEM((tm, d), BF16),
                        pltpu.VMEM((d, ncols), BF16),
                        pltpu.VMEM((2, d, tn), F32),
                        pltpu.SemaphoreType.DMA((2,)),
                        pltpu.VMEM((tm, tn), F32)],
        compiler_params=pltpu.CompilerParams(
            dimension_semantics=("arbitrary",), vmem_limit_bytes=VMEM_LIMIT),
        name="in_projection",
    )(*args)


def _attn_kernel(q_ref, k_ref, v_ref, o_ref, u_scr, *, seq):
    hb, tq, win, nsub = ATT_HEADS_PER_STEP, ATT_Q, ATT_WIN, ATT_SUB
    chains = nsub * hb

    u_scr[...] = jnp.where(lax.broadcasted_iota(jnp.int32, (ATT_NEW, ATT_NEW), 0)
                           > lax.broadcasted_iota(jnp.int32, (ATT_NEW, ATT_NEW), 1),
                           1.0, 0.0).astype(BF16)
    old = win - ATT_NEW
    key_idx = lax.broadcasted_iota(jnp.int32, (tq, win), 1)
    row_idx = lax.broadcasted_iota(jnp.int32, (tq, win), 0)
    tail = win - LANES
    tail_ok = (lax.broadcasted_iota(jnp.int32, (tq, LANES), 1) + tail
               < lax.broadcasted_iota(jnp.int32, (tq, LANES), 0) + (win - tq))

    def head_cols(h):
        return slice(h * HEAD_DIM, (h + 1) * HEAD_DIM)

    def tile_logits(rows0, keys0):
        return [_dot_nt(q_ref[pl.ds(rows0[a], tq), head_cols(h)],
                        k_ref[pl.ds(keys0[a], win), head_cols(h)])
                for a in range(nsub) for h in range(hb)]

    def tile_weights(logits, rows0, keys0, limits, totals):
        log_beta, terms = [], []
        for a in range(nsub):
            if limits is not None:
                ok = key_idx < (jnp.minimum(rows0[a] + row_idx, limits[a]) - keys0[a])
            for h in range(hb):
                z = logits[a * hb + h]
                if limits is None:
                    z = jnp.concatenate(
                        [z[:, :tail], jnp.where(tail_ok, z[:, tail:], ATT_EXCLUDED)], axis=1)
                else:
                    z = jnp.where(ok, z, ATT_EXCLUDED)
                nz = -z
                lsm = (jnp.minimum(nz, 0.0)
                       - jnp.log(1.0 + jnp.exp2(jnp.minimum(z, nz))) * LOG2E)
                log_beta.append(z + lsm)
                terms.append(lsm)
        sums_new = _dot(jnp.concatenate([t[:, old:].astype(BF16) for t in terms], axis=0),
                        u_scr[...])
        sums_old = _dot(jnp.concatenate([t[:, :old].astype(BF16) for t in terms], axis=0),
                        u_scr[0:old, 0:old])
        weights, tile_totals = [], []
        for c in range(chains):
            newer = sums_new[c * tq:(c + 1) * tq]
            older = sums_old[c * tq:(c + 1) * tq]
            total_new = newer[:, 0:1] + terms[c][:, old:old + 1]
            expo = log_beta[c] + jnp.concatenate([older + total_new, newer], axis=1)
            if totals is not None:
                expo = expo + totals[c]
            weights.append(jnp.exp2(expo).astype(BF16))
            tile_totals.append(total_new + older[:, 0:1] + terms[c][:, 0:1])
        return weights, tile_totals

    def tile_values(weights, keys0):
        return [_dot(weights[a * hb + h], v_ref[pl.ds(keys0[a], win), head_cols(h)])
                for a in range(nsub) for h in range(hb)]

    def slowest(totals):
        m = totals[0]
        for t in totals[1:]:
            m = jnp.maximum(m, t)
        return jnp.max(m)

    def qblock(i, carry, *, near_start):
        r0 = i * (tq * nsub)
        rows0 = [pl.multiple_of(r0 + a * tq, tq) for a in range(nsub)]
        if near_start:
            starts = [jnp.maximum(r - (win - tq), 0) for r in rows0]
        else:
            starts = [r - (win - tq) for r in rows0]
        keys0 = [pl.multiple_of(s, tq) for s in starts]
        weights, totals = tile_weights(tile_logits(rows0, keys0), rows0, keys0,
                                       [ATT_NO_LIMIT] * nsub if near_start else None, None)
        accs = tile_values(weights, keys0)

        def more(st):
            m, worst, _, _ = st
            return (starts[-1] - m * win > 0) & (worst > ATT_SKIP_BELOW)

        def older(st):
            m, _, totals, accs = st
            limits = [s - m * win for s in starts]
            keys0 = [pl.multiple_of(jnp.maximum(lim - win, 0), tq) for lim in limits]
            weights, tile_totals = tile_weights(tile_logits(rows0, keys0), rows0, keys0,
                                                limits, totals)
            outs = tile_values(weights, keys0)
            totals = tuple(t + d for t, d in zip(totals, tile_totals))
            accs = tuple(acc + o for acc, o in zip(accs, outs))
            return (m + 1, slowest(totals), totals, accs)

        _, _, _, accs = lax.while_loop(
            more, older, (jnp.int32(0), slowest(totals), tuple(totals), tuple(accs)))
        for a in range(nsub):
            for h in range(hb):
                o_ref[pl.ds(rows0[a], tq), head_cols(h)] = accs[a * hb + h].astype(BF16)
        return carry

    n_blocks = seq // (tq * nsub)
    n_clipped = min(-(-(win - tq) // (tq * nsub)), n_blocks)
    lax.fori_loop(0, n_clipped, functools.partial(qblock, near_start=True), 0)
    lax.fori_loop(n_clipped, n_blocks, functools.partial(qblock, near_start=False), 0)


def _stick_breaking(p, batch, seq, q_col, k_col, v_col):
    n = p.shape[0]
    width = ATT_HEADS_PER_STEP * HEAD_DIM
    n_hg = N_HEADS // ATT_HEADS_PER_STEP

    def spec(col):
        return pl.BlockSpec((seq, width), lambda b, hg: (b, col // width + hg))

    return pl.pallas_call(
        functools.partial(_attn_kernel, seq=seq),
        grid=(batch, n_hg),
        in_specs=[spec(q_col), spec(k_col), spec(v_col)],
        out_specs=pl.BlockSpec((seq, width), lambda b, hg: (b, hg)),
        out_shape=jax.ShapeDtypeStruct((n, N_HEADS * HEAD_DIM), BF16),
        scratch_shapes=[pltpu.VMEM((ATT_NEW, ATT_NEW), BF16)],
        compiler_params=pltpu.CompilerParams(
            dimension_semantics=("parallel", "parallel"), vmem_limit_bytes=VMEM_LIMIT),
        name="stick_breaking",
    )(p, p, p)


def _out_even_kernel(x_ref, gate_ref, ug_ref, vg_ref, zb_ref, bo_ref, ws_ref, bias_ref, wout_ref,
                     o_ref, y_scr):
    tm = x_ref.shape[0]
    a_width = bo_ref.shape[1]
    row = lax.broadcasted_iota(jnp.int32, (BLOCK, BLOCK), 0)
    col = lax.broadcasted_iota(jnp.int32, (BLOCK, BLOCK), 1)
    for g in range(a_width // HEAD_DIM):
        cols = slice(g * HEAD_DIM, (g + 1) * HEAD_DIM)
        wg = jnp.where(col <= row, ws_ref[g], 0.0).astype(BF16)
        for c in range(tm // BLOCK):
            rows = slice(c * BLOCK, (c + 1) * BLOCK)
            mixed = _dot(wg, vg_ref[rows, cols]) + bias_ref[:, cols]
            y_scr[rows, cols] = (ug_ref[rows, cols].astype(F32) * mixed).astype(BF16)
    y_scr[:, a_width:] = (bo_ref[...].astype(F32) * _silu(zb_ref[...].astype(F32))).astype(BF16)
    o_ref[...] = x_ref[...] + gate_ref[...] * _dot(y_scr[...], wout_ref[...])


def _out_even(x2, seq, gate, p, b_out, a_ws, bias_full, w_out_bf):
    n, d = x2.shape
    tm = OUT_EVEN_ROWS
    aw = b_out.shape[1]
    row = lambda c: pl.BlockSpec((tm, aw), lambda i: (i, c))
    const = lambda a: pl.BlockSpec(a.shape, lambda i: (0,) * a.ndim, pipeline_mode=pl.Buffered(1))
    return pl.pallas_call(
        _out_even_kernel,
        grid=(n // tm,),
        in_specs=[pl.BlockSpec((tm, d), lambda i: (i, 0)),
                  pl.BlockSpec((None, 1, d), lambda i: ((i * tm) // seq, 0, 0)),
                  row(4), row(0), row(5),
                  row(0),
                  const(a_ws), const(bias_full), const(w_out_bf)],
        out_specs=pl.BlockSpec((tm, d), lambda i: (i, 0)),
        out_shape=jax.ShapeDtypeStruct((n, d), F32),
        scratch_shapes=[pltpu.VMEM((tm, 2 * aw), BF16)],
        compiler_params=pltpu.CompilerParams(
            dimension_semantics=("parallel",), vmem_limit_bytes=VMEM_LIMIT),
        name="out_even",
    )(x2, gate, p, p, p, b_out, a_ws, bias_full, w_out_bf)


def _rope_operands(positions):
    half = HEAD_DIM // 2
    inv_freq = ROPE_BASE ** (-jnp.arange(half, dtype=F32) / half)
    freq = jnp.concatenate([inv_freq, inv_freq]).reshape(1, HEAD_DIM)
    sign = jnp.concatenate([-jnp.ones((half,), F32), jnp.ones((half,), F32)]).reshape(1, HEAD_DIM)
    return positions.reshape(-1, LANES).astype(F32), freq, sign


def _retention_tables():
    chunk = RET_CHUNK
    f32 = np.float32
    log_gamma = np.log1p(-np.exp2(-5.0 - np.arange(N_HEADS, dtype=f32))).astype(f32)
    idx = np.arange(chunk, dtype=f32)
    diff = idx[:, None] - idx[None, :]
    intra = np.where(diff >= 0, np.exp(log_gamma[:, None, None] * np.maximum(diff, 0.0)), 0.0)
    q_decay = np.exp(log_gamma[:, None] * (idx + 1.0))
    k_decay = np.exp(log_gamma[:, None] * (chunk - 1.0 - idx))
    chunk_decay = np.exp(log_gamma * chunk)
    tables = (intra,
              np.broadcast_to(q_decay[:, :, None], (N_HEADS, chunk, HEAD_DIM)),
              np.broadcast_to(k_decay[:, :, None], (N_HEADS, chunk, HEAD_DIM)),
              np.broadcast_to(chunk_decay[:, None, None], (N_HEADS, HEAD_DIM, HEAD_DIM)))
    return tuple(jnp.asarray(np.ascontiguousarray(t, dtype=f32)) for t in tables)


def _retention_rows(q_ref, k_ref, v_ref, idec_ref, qdec_ref, kdec_ref, cdec_ref, st_scr, emit):
    chunk = RET_CHUNK
    n_chunks = q_ref.shape[0] // chunk
    cols = [slice(h * HEAD_DIM, (h + 1) * HEAD_DIM) for h in range(N_HEADS)]
    rows = [slice(c * chunk, (c + 1) * chunk) for c in range(n_chunks)]

    kv = {}
    for c in range(n_chunks):
        for h in range(N_HEADS):
            k_decayed = (k_ref[rows[c], cols[h]].astype(F32) * kdec_ref[h]).astype(BF16)
            kv[h, c] = _dot_tn(k_decayed, v_ref[rows[c], cols[h]])
    state = {}
    for h in range(N_HEADS):
        s = st_scr[h]
        for c in range(n_chunks):
            state[h, c] = s.astype(BF16)
            s = s * cdec_ref[h] + kv[h, c]
        st_scr[h] = s
    for c in range(n_chunks):
        for h in range(N_HEADS):
            q = q_ref[rows[c], cols[h]]
            scores = _dot_nt(q, k_ref[rows[c], cols[h]]) * idec_ref[h]
            out = (_dot(scores.astype(BF16), v_ref[rows[c], cols[h]])
                   + _dot(q, state[h, c]) * qdec_ref[h])
            ms = jnp.mean(out * out, axis=-1, keepdims=True)
            emit(rows[c], h, out * lax.rsqrt(ms + EPS))


def _pooled_rows(pc_ref, halo_ref, cw_ref, cs_ref, ext_scr, seq_row0, emit):
    tm = pc_ref.shape[0]
    ext_scr[0:POOL_HALO, :] = jnp.where(seq_row0 == 0, jnp.zeros_like(halo_ref), halo_ref[...])
    ext_scr[POOL_HALO:, :] = pc_ref[...]
    src_rows = BLOCK + POOL_HALO
    lag = (lax.broadcasted_iota(jnp.int32, (BLOCK, src_rows), 0) + POOL_HALO
           - lax.broadcasted_iota(jnp.int32, (BLOCK, src_rows), 1))
    t_seq = seq_row0 + lax.broadcasted_iota(jnp.int32, (BLOCK, POOL_GROUP_DIM), 0)
    for gi, win in enumerate(POOL_WINDOWS):
        cols = slice(gi * POOL_GROUP_DIM, (gi + 1) * POOL_GROUP_DIM)
        band = jnp.where((lag >= 0) & (lag < win), 1.0 / win, 0.0).astype(BF16)
        short = jnp.where(t_seq + 1 < win, win / (t_seq + 1).astype(F32), 1.0)
        pooled = []
        for c in range(tm // BLOCK):
            mean = _dot(band, ext_scr[c * BLOCK:c * BLOCK + src_rows, cols])
            if c == 0:
                mean = mean * short
            cur = pc_ref[c * BLOCK:(c + 1) * BLOCK, cols].astype(F32)
            pooled.append((mean - cur).astype(BF16))
        emit(gi, _dot(jnp.concatenate(pooled, axis=0), cw_ref[gi]) * cs_ref[:, cols])


def _odd_mixers_kernel(p_ref, idec_ref, qdec_ref, kdec_ref, cdec_ref,
                       x_ref, gate_ref, halo_ref, cw_ref, cs_ref, wout_ref,
                       o_ref, st_scr, ext_scr, y_scr):
    tm = x_ref.shape[0]
    c_width = halo_ref.shape[1]
    pc_ref, q_ref, k_ref, v_ref, za_ref, zb_ref = (
        p_ref.at[:, pl.ds(j * c_width, c_width)] for j in range(6))

    @pl.when(pl.program_id(1) == 0)
    def _():
        st_scr[...] = jnp.zeros_like(st_scr)

    def gated_retention(rows, h, out):
        cols = slice(h * HEAD_DIM, (h + 1) * HEAD_DIM)
        ycols = slice(c_width + h * HEAD_DIM, c_width + (h + 1) * HEAD_DIM)
        y_scr[rows, ycols] = (out * _silu(zb_ref[rows, cols].astype(F32))).astype(BF16)

    def gated_pool(gi, mixed):
        cols = slice(gi * POOL_GROUP_DIM, (gi + 1) * POOL_GROUP_DIM)
        y_scr[:, cols] = (mixed * _silu(za_ref[:, cols].astype(F32))).astype(BF16)

    _retention_rows(q_ref, k_ref, v_ref, idec_ref, qdec_ref, kdec_ref, cdec_ref, st_scr,
                    gated_retention)
    _pooled_rows(pc_ref, halo_ref, cw_ref, cs_ref, ext_scr, pl.program_id(1) * tm, gated_pool)
    o_ref[...] = x_ref[...] + gate_ref[...] * _dot(y_scr[...], wout_ref[...])


def _odd_mixers(x2, batch, seq, gate, p, c_w_bf, c_scale, w_out_bf):
    n, d = x2.shape
    tm = ROW_TILE
    cw = N_HEADS * HEAD_DIM
    steps = seq // tm
    tables = _retention_tables()
    tabs = [pl.BlockSpec(t.shape, lambda b, s: (0, 0, 0)) for t in tables]
    halo_blocks = tm // POOL_HALO
    halo = pl.BlockSpec(
        (POOL_HALO, cw), lambda b, s: (jnp.maximum((b * steps + s) * halo_blocks - 1, 0), 0))
    return pl.pallas_call(
        _odd_mixers_kernel,
        grid=(batch, steps),
        in_specs=[pl.BlockSpec((tm, p.shape[1]), lambda b, s: (b * steps + s, 0))] + tabs + [
            pl.BlockSpec((tm, d), lambda b, s: (b * steps + s, 0)),
            pl.BlockSpec((None, 1, d), lambda b, s: (b, 0, 0)),
            halo,
            pl.BlockSpec(c_w_bf.shape, lambda b, s: (0, 0, 0)),
            pl.BlockSpec((1, cw), lambda b, s: (0, 0)),
            pl.BlockSpec(w_out_bf.shape, lambda b, s: (0, 0))],
        out_specs=pl.BlockSpec((tm, d), lambda b, s: (b * steps + s, 0)),
        out_shape=jax.ShapeDtypeStruct((n, d), F32),
        scratch_shapes=[pltpu.VMEM((N_HEADS, HEAD_DIM, HEAD_DIM), F32),
                        pltpu.VMEM((POOL_HALO + tm, cw), BF16),
                        pltpu.VMEM((tm, 2 * cw), BF16)],
        compiler_params=pltpu.CompilerParams(
            dimension_semantics=("parallel", "arbitrary"), vmem_limit_bytes=VMEM_LIMIT),
        name="odd_mixers",
    )(p, *tables, x2, gate, p, c_w_bf, c_scale.reshape(1, cw), w_out_bf)


def _even_layer(x2, batch, seq, mod, norm_g, w_in, a_vnorm_g, a_ws, a_bs,
                b_qnorm_g, b_knorm_g, w_out):
    d = x2.shape[1]
    shift, scale, gate = _split_modulation(mod)
    gains = [a_vnorm_g, b_qnorm_g.reshape(1, -1), b_knorm_g.reshape(1, -1)]
    modes = ("hold", "norm", "norm_scaled", "norm", "raw", "silu_gate", "raw")
    p = _in_projection(x2, seq, norm_g, scale, shift, w_in, gains, modes,
                       post_scale=LOG2E * HEAD_DIM ** -0.5)
    b_out = _stick_breaking(p, batch, seq, q_col=d, k_col=2 * d, v_col=3 * d)
    bias_full = jnp.repeat(a_bs.T, HEAD_DIM, axis=1)
    return _out_even(x2, seq, gate, p, b_out, a_ws, bias_full, w_out.astype(BF16))


def _odd_layer(x2, batch, seq, mod, positions, norm_g, w_in, c_w, c_scale,
               d_qnorm_g, d_knorm_g, w_out):
    shift, scale, gate = _split_modulation(mod)
    gains = [d_qnorm_g.reshape(1, -1), d_knorm_g.reshape(1, -1)]
    modes = ("raw", "rope", "rope_scaled", "raw", "raw", "raw")
    p = _in_projection(x2, seq, norm_g, scale, shift, w_in, gains, modes,
                       rope=_rope_operands(positions), post_scale=HEAD_DIM ** -0.5)
    return _odd_mixers(x2, batch, seq, gate, p, c_w.astype(BF16), c_scale, w_out.astype(BF16))


def kernel(x, c, positions, even_norm_g, even_w_mod, even_b_mod, even_w_in, even_a_vnorm_g, even_a_ws, even_a_bs, even_b_qnorm_g, even_b_knorm_g, even_w_out, odd_norm_g, odd_w_mod, odd_b_mod, odd_w_in, odd_c_w, odd_c_scale, odd_d_qnorm_g, odd_d_knorm_g, odd_w_out):
    batch, seq, d = x.shape
    depth = even_norm_g.shape[0] + odd_norm_g.shape[0]
    x2 = x.reshape(batch * seq, d)
    n_even = even_norm_g.shape[0]
    mods = _modulations(c, even_w_mod, even_b_mod, odd_w_mod, odd_b_mod)
    for layer in range(depth):
        i = layer // 2
        if layer % 2 == 0:
            x2 = _even_layer(x2, batch, seq, mods[i], even_norm_g[i],
                             even_w_in[i], even_a_vnorm_g[i], even_a_ws[i], even_a_bs[i],
                             even_b_qnorm_g[i], even_b_knorm_g[i], even_w_out[i])
        else:
            x2 = _odd_layer(x2, batch, seq, mods[n_even + i], positions, odd_norm_g[i],
                            odd_w_in[i], odd_c_w[i], odd_c_scale[i], odd_d_qnorm_g[i],
                            odd_d_knorm_g[i], odd_w_out[i])
    return x2.reshape(batch, seq, d)
```

```python
import functools

import jax
import jax.numpy as jnp
import numpy as np
from jax import lax
from jax.experimental import pallas as pl
from jax.experimental.pallas import tpu as pltpu

F32 = jnp.float32
BF16 = jnp.bfloat16

EPS = 1e-6
ROPE_BASE = 10000.0
BLOCK = 128
HEAD_DIM = 128
N_HEADS = 8
POOL_WINDOWS = (2, 4, 8, 16)
POOL_GROUP_DIM = 256
POOL_HALO = 16

MOD_ROWS = 8

ROW_TILE = 512
OUT_EVEN_ROWS = 1024
COL_TILE = 1024
RET_CHUNK = 128
ODD_ROW_BUFFERS = 3

ATT_Q = 64
ATT_WIN = 384
ATT_NEW = 256
ATT_EXCLUDED = -1e30
LANES = 128
ATT_SUB = 2
ATT_HEADS_PER_STEP = 4
ATT_SKIP_BELOW = -151.0
LOG2E = 1.4426950408889634
ATT_NO_LIMIT = 1 << 30

VMEM_LIMIT = 56 * 1024 * 1024


def _silu(v):
    return v * jax.nn.sigmoid(v)


def _dot(a, b):
    return jnp.dot(a, b, preferred_element_type=F32)


def _dot_nt(a, b):
    return lax.dot_general(a, b, (((1,), (1,)), ((), ())), preferred_element_type=F32)


def _dot_tn(a, b):
    return lax.dot_general(a, b, (((0,), (0,)), ((), ())), preferred_element_type=F32)


def _group_rms(a, gain):
    ms = jnp.mean(a * a, axis=-1, keepdims=True)
    return a * lax.rsqrt(ms + EPS) * gain


def _mod_kernel(c_ref, we_ref, be_ref, wo_ref, bo_ref, o_ref, *, n_even):
    a = _silu(c_ref[...]).astype(BF16)
    layer = pl.program_id(0)

    @pl.when(layer < n_even)
    def _():
        o_ref[...] = _dot(a, we_ref[...].astype(BF16)) + be_ref[...]

    @pl.when(layer >= n_even)
    def _():
        o_ref[...] = _dot(a, wo_ref[...].astype(BF16)) + bo_ref[...]


def _modulations(c, even_w, even_b, odd_w, odd_b):
    bsz, d = c.shape
    n_even, n_odd = even_w.shape[0], odd_w.shape[0]
    c8 = jnp.pad(c, ((0, MOD_ROWS - bsz), (0, 0)))
    even_at = lambda l, j: (jnp.minimum(l, n_even - 1), 0, jnp.where(l < n_even, j, 2))
    odd_at = lambda l, j: (jnp.maximum(l - n_even, 0), 0, jnp.where(l < n_even, 0, j))
    m = pl.pallas_call(
        functools.partial(_mod_kernel, n_even=n_even),
        grid=(n_even + n_odd, 3),
        in_specs=[pl.BlockSpec((MOD_ROWS, d), lambda l, j: (0, 0)),
                  pl.BlockSpec((None, d, d), even_at),
                  pl.BlockSpec((None, 1, d), even_at),
                  pl.BlockSpec((None, d, d), odd_at),
                  pl.BlockSpec((None, 1, d), odd_at)],
        out_specs=pl.BlockSpec((None, MOD_ROWS, d), lambda l, j: (l, 0, j)),
        out_shape=jax.ShapeDtypeStruct((n_even + n_odd, MOD_ROWS, 3 * d), F32),
        name="modulation",
    )(c8, even_w, even_b.reshape(n_even, 1, 3 * d), odd_w, odd_b.reshape(n_odd, 1, 3 * d))
    return m[:, :bsz]


def _split_modulation(m):
    d = m.shape[1] // 3
    return m[:, None, :d], m[:, None, d:2 * d], m[:, None, 2 * d:]


def _inproj_kernel(*refs, modes, out_tiles, gain_of, n_gains, post_scale, use_rope):
    *refs, h_scr, w_ref, stage, sem, held = refs
    x_ref, g_ref, sc_ref, sh_ref, w_hbm = refs[:5]
    gain_refs = refs[5:5 + n_gains]
    o_ref = refs[-1]
    tn = COL_TILE

    def weight_copy(j):
        return pltpu.make_async_copy(w_hbm.at[:, pl.ds(j * tn, tn)], stage.at[j % 2], sem.at[j % 2])

    @pl.when(pl.program_id(0) == 0)
    def _():
        weight_copy(0).start()
        for j in range(len(modes)):
            if j + 1 < len(modes):
                weight_copy(j + 1).start()
            weight_copy(j).wait()
            w_ref[:, j * tn:(j + 1) * tn] = stage[j % 2].astype(BF16)

    if use_rope:
        pos_ref, freq_ref, sign_ref = refs[5 + n_gains:8 + n_gains]
        on_diag = (lax.broadcasted_iota(jnp.int32, (LANES, LANES), 0)
                   == lax.broadcasted_iota(jnp.int32, (LANES, LANES), 1))
        angles = []
        for a in range(pos_ref.shape[0]):
            spread = jnp.where(on_diag, jnp.broadcast_to(pos_ref[a:a + 1, :], (LANES, LANES)), 0.0)
            angles.append(jnp.sum(spread, axis=1, keepdims=True) * freq_ref[...])
        ang = jnp.concatenate(angles, axis=0)
        cos_t = jnp.cos(ang)
        sin_t = jnp.sin(ang) * sign_ref[...]
    x = x_ref[...]
    ms = jnp.mean(x * x, axis=-1, keepdims=True)
    y = x * lax.rsqrt(ms + EPS) * g_ref[...]
    h_scr[...] = (y * (1.0 + sc_ref[...]) + sh_ref[...]).astype(BF16)

    cross_lane = ("norm", "norm_scaled", "rope", "rope_scaled")
    heavy = [jj for jj, m in enumerate(modes) if m in cross_lane]
    plain = [jj for jj, m in enumerate(modes) if m not in cross_lane]
    order = []
    while heavy or plain:
        order += heavy[:1] + plain[:1]
        heavy, plain = heavy[1:], plain[1:]
    for jj in order:
        mode = modes[jj]
        out0 = None if out_tiles[jj] is None else out_tiles[jj] * tn
        acc = _dot(h_scr[...], w_ref[:, jj * tn:(jj + 1) * tn])
        if mode == "hold":
            held[...] = acc
            continue
        if mode == "raw":
            o_ref[:, out0:out0 + tn] = acc.astype(BF16)
            continue
        if mode == "silu_gate":
            o_ref[:, out0:out0 + tn] = (held[...] * _silu(acc)).astype(BF16)
            continue
        gain_ref = gain_refs[gain_of[jj]]
        for g in range(tn // HEAD_DIM):
            row = g if gain_ref.shape[0] > 1 else 0
            t = _group_rms(acc[:, g * HEAD_DIM:(g + 1) * HEAD_DIM], gain_ref[row:row + 1, :])
            if mode in ("rope", "rope_scaled"):
                t = t * cos_t + pltpu.roll(t, HEAD_DIM // 2, axis=1) * sin_t
            if mode in ("norm_scaled", "rope_scaled"):
                t = t * post_scale
            o_ref[:, out0 + g * HEAD_DIM:out0 + (g + 1) * HEAD_DIM] = t.astype(BF16)


def _in_projection(x2, seq, norm_g, scale, shift, w_in, gains, modes, rope=None, post_scale=1.0):
    n, d = x2.shape
    tm, tn = ROW_TILE, COL_TILE
    ncols = len(modes) * tn
    out_tiles, n_out = [], 0
    for m in modes:
        out_tiles.append(None if m == "hold" else n_out)
        n_out += m != "hold"
    assert modes.count("hold") == modes.count("silu_gate") <= 1
    assert "hold" not in modes or modes.index("hold") < modes.index("silu_gate")
    use_rope = rope is not None
    normalising = [jj for jj, m in enumerate(modes) if m in ("norm", "norm_scaled", "rope", "rope_scaled")]
    assert len(gains) == len(normalising)
    gain_of = tuple(normalising.index(jj) if jj in normalising else None for jj in range(len(modes)))
    in_specs = [
        pl.BlockSpec((tm, d), lambda i: (i, 0)),
        pl.BlockSpec((1, d), lambda i: (0, 0)),
        pl.BlockSpec((None, 1, d), lambda i: ((i * tm) // seq, 0, 0)),
        pl.BlockSpec((None, 1, d), lambda i: ((i * tm) // seq, 0, 0)),
        pl.BlockSpec(memory_space=pl.ANY),
    ] + [pl.BlockSpec(gn.shape, lambda i: (0, 0)) for gn in gains]
    args = [x2, norm_g.reshape(1, d), scale, shift, w_in, *gains]
    if use_rope:
        vec = pl.BlockSpec((1, HEAD_DIM), lambda i: (0, 0))
        pos, freq, sign = rope
        in_specs += [pl.BlockSpec((None, tm // LANES, LANES), lambda i: (i, 0, 0)), vec, vec]
        args += [pos.reshape(n // tm, tm // LANES, LANES), freq, sign]
    return pl.pallas_call(
        functools.partial(_inproj_kernel, modes=tuple(modes), out_tiles=tuple(out_tiles),
                          gain_of=gain_of, n_gains=len(gains), post_scale=post_scale,
                          use_rope=use_rope),
        grid=(n // tm,),
        in_specs=in_specs,
        out_specs=pl.BlockSpec((tm, n_out * tn), lambda i: (i, 0)),
        out_shape=jax.ShapeDtypeStruct((n, n_out * tn), BF16),
        scratch_shapes=[pltpu.VMEM((tm, d), BF16),
                        pltpu.VMEM((d, ncols), BF16),
                        pltpu.VMEM((2, d, tn), F32),
                        pltpu.SemaphoreType.DMA((2,)),
                        pltpu.VMEM((tm, tn), F32)],
        compiler_params=pltpu.CompilerParams(
            dimension_semantics=("arbitrary",), vmem_limit_bytes=VMEM_LIMIT),
        name="in_projection",
    )(*args)


def _attn_kernel(q_ref, k_ref, v_ref, o_ref, u_scr, *, seq):
    hb, tq, win, nsub = ATT_HEADS_PER_STEP, ATT_Q, ATT_WIN, ATT_SUB
    chains = nsub * hb

    u_scr[...] = jnp.where(lax.broadcasted_iota(jnp.int32, (ATT_NEW, ATT_NEW), 0)
                           > lax.broadcasted_iota(jnp.int32, (ATT_NEW, ATT_NEW), 1),
                           1.0, 0.0).astype(BF16)
    old = win - ATT_NEW
    key_idx = lax.broadcasted_iota(jnp.int32, (tq, win), 1)
    row_idx = lax.broadcasted_iota(jnp.int32, (tq, win), 0)
    tail = win - LANES
    tail_ok = (lax.broadcasted_iota(jnp.int32, (tq, LANES), 1) + tail
               < lax.broadcasted_iota(jnp.int32, (tq, LANES), 0) + (win - tq))

    def head_cols(h):
        return slice(h * HEAD_DIM, (h + 1) * HEAD_DIM)

    def tile_logits(rows0, keys0):
        return [_dot_nt(q_ref[pl.ds(rows0[a], tq), head_cols(h)],
                        k_ref[pl.ds(keys0[a], win), head_cols(h)])
                for a in range(nsub) for h in range(hb)]

    def tile_weights(logits, rows0, keys0, limits, totals):
        log_beta, terms = [], []
        for a in range(nsub):
            if limits is not None:
                ok = key_idx < (jnp.minimum(rows0[a] + row_idx, limits[a]) - keys0[a])
            for h in range(hb):
                z = logits[a * hb + h]
                if limits is None:
                    z = jnp.concatenate(
                        [z[:, :tail], jnp.where(tail_ok, z[:, tail:], ATT_EXCLUDED)], axis=1)
                else:
                    z = jnp.where(ok, z, ATT_EXCLUDED)
                nz = -z
                lsm = (jnp.minimum(nz, 0.0)
                       - jnp.log(1.0 + jnp.exp2(jnp.minimum(z, nz))) * LOG2E)
                log_beta.append(z + lsm)
                terms.append(lsm)
        sums_new = _dot(jnp.concatenate([t[:, old:].astype(BF16) for t in terms], axis=0),
                        u_scr[...])
        sums_old = _dot(jnp.concatenate([t[:, :old].astype(BF16) for t in terms], axis=0),
                        u_scr[0:old, 0:old])
        weights, tile_totals = [], []
        for c in range(chains):
            newer = sums_new[c * tq:(c + 1) * tq]
            older = sums_old[c * tq:(c + 1) * tq]
            total_new = newer[:, 0:1] + terms[c][:, old:old + 1]
            expo = log_beta[c] + jnp.concatenate([older + total_new, newer], axis=1)
            if totals is not None:
                expo = expo + totals[c]
            weights.append(jnp.exp2(expo).astype(BF16))
            tile_totals.append(total_new + older[:, 0:1] + terms[c][:, 0:1])
        return weights, tile_totals

    def tile_values(weights, keys0):
        return [_dot(weights[a * hb + h], v_ref[pl.ds(keys0[a], win), head_cols(h)])
                for a in range(nsub) for h in range(hb)]

    def slowest(totals):
        m = totals[0]
        for t in totals[1:]:
            m = jnp.maximum(m, t)
        return jnp.max(m)

    def qblock(i, carry, *, near_start):
        r0 = i * (tq * nsub)
        rows0 = [pl.multiple_of(r0 + a * tq, tq) for a in range(nsub)]
        if near_start:
            starts = [jnp.maximum(r - (win - tq), 0) for r in rows0]
        else:
            starts = [r - (win - tq) for r in rows0]
        keys0 = [pl.multiple_of(s, tq) for s in starts]
        weights, totals = tile_weights(tile_logits(rows0, keys0), rows0, keys0,
                                       [ATT_NO_LIMIT] * nsub if near_start else None, None)
        accs = tile_values(weights, keys0)

        def more(st):
            m, worst, _, _ = st
            return (starts[-1] - m * win > 0) & (worst > ATT_SKIP_BELOW)

        def older(st):
            m, _, totals, accs = st
            limits = [s - m * win for s in starts]
            keys0 = [pl.multiple_of(jnp.maximum(lim - win, 0), tq) for lim in limits]
            weights, tile_totals = tile_weights(tile_logits(rows0, keys0), rows0, keys0,
                                                limits, totals)
            outs = tile_values(weights, keys0)
            totals = tuple(t + d for t, d in zip(totals, tile_totals))
            accs = tuple(acc + o for acc, o in zip(accs, outs))
            return (m + 1, slowest(totals), totals, accs)

        _, _, _, accs = lax.while_loop(
            more, older, (jnp.int32(0), slowest(totals), tuple(totals), tuple(accs)))
        for a in range(nsub):
            for h in range(hb):
                o_ref[pl.ds(rows0[a], tq), head_cols(h)] = accs[a * hb + h].astype(BF16)
        return carry

    n_blocks = seq // (tq * nsub)
    n_clipped = min(-(-(win - tq) // (tq * nsub)), n_blocks)
    lax.fori_loop(0, n_clipped, functools.partial(qblock, near_start=True), 0)
    lax.fori_loop(n_clipped, n_blocks, functools.partial(qblock, near_start=False), 0)


def _stick_breaking(p, batch, seq, q_col, k_col, v_col):
    n = p.shape[0]
    width = ATT_HEADS_PER_STEP * HEAD_DIM
    n_hg = N_HEADS // ATT_HEADS_PER_STEP

    def spec(col):
        return pl.BlockSpec((seq, width), lambda b, hg: (b, col // width + hg))

    return pl.pallas_call(
        functools.partial(_attn_kernel, seq=seq),
        grid=(batch, n_hg),
        in_specs=[spec(q_col), spec(k_col), spec(v_col)],
        out_specs=pl.BlockSpec((seq, width), lambda b, hg: (b, hg)),
        out_shape=jax.ShapeDtypeStruct((n, N_HEADS * HEAD_DIM), BF16),
        scratch_shapes=[pltpu.VMEM((ATT_NEW, ATT_NEW), BF16)],
        compiler_params=pltpu.CompilerParams(
            dimension_semantics=("parallel", "parallel"), vmem_limit_bytes=VMEM_LIMIT),
        name="stick_breaking",
    )(p, p, p)


def _out_even_kernel(x_ref, gate_ref, ug_ref, vg_ref, zb_ref, bo_ref, ws_ref, bias_ref, wout_ref,
                     o_ref, y_scr):
    tm = x_ref.shape[0]
    a_width = bo_ref.shape[1]
    row = lax.broadcasted_iota(jnp.int32, (BLOCK, BLOCK), 0)
    col = lax.broadcasted_iota(jnp.int32, (BLOCK, BLOCK), 1)
    for g in range(a_width // HEAD_DIM):
        cols = slice(g * HEAD_DIM, (g + 1) * HEAD_DIM)
        wg = jnp.where(col <= row, ws_ref[g], 0.0).astype(BF16)
        for c in range(tm // BLOCK):
            rows = slice(c * BLOCK, (c + 1) * BLOCK)
            mixed = _dot(wg, vg_ref[rows, cols]) + bias_ref[:, cols]
            y_scr[rows, cols] = (ug_ref[rows, cols].astype(F32) * mixed).astype(BF16)
    y_scr[:, a_width:] = (bo_ref[...].astype(F32) * _silu(zb_ref[...].astype(F32))).astype(BF16)
    o_ref[...] = x_ref[...] + gate_ref[...] * _dot(y_scr[...], wout_ref[...])


def _out_even(x2, seq, gate, p, b_out, a_ws, bias_full, w_out_bf):
    n, d = x2.shape
    tm = OUT_EVEN_ROWS
    aw = b_out.shape[1]
    row = lambda c: pl.BlockSpec((tm, aw), lambda i: (i, c))
    const = lambda a: pl.BlockSpec(a.shape, lambda i: (0,) * a.ndim, pipeline_mode=pl.Buffered(1))
    return pl.pallas_call(
        _out_even_kernel,
        grid=(n // tm,),
        in_specs=[pl.BlockSpec((tm, d), lambda i: (i, 0)),
                  pl.BlockSpec((None, 1, d), lambda i: ((i * tm) // seq, 0, 0)),
                  row(4), row(0), row(5),
                  row(0),
                  const(a_ws), const(bias_full), const(w_out_bf)],
        out_specs=pl.BlockSpec((tm, d), lambda i: (i, 0)),
        out_shape=jax.ShapeDtypeStruct((n, d), F32),
        scratch_shapes=[pltpu.VMEM((tm, 2 * aw), BF16)],
        compiler_params=pltpu.CompilerParams(
            dimension_semantics=("parallel",), vmem_limit_bytes=VMEM_LIMIT),
        name="out_even",
    )(x2, gate, p, p, p, b_out, a_ws, bias_full, w_out_bf)


def _rope_operands(positions):
    half = HEAD_DIM // 2
    inv_freq = ROPE_BASE ** (-jnp.arange(half, dtype=F32) / half)
    freq = jnp.concatenate([inv_freq, inv_freq]).reshape(1, HEAD_DIM)
    sign = jnp.concatenate([-jnp.ones((half,), F32), jnp.ones((half,), F32)]).reshape(1, HEAD_DIM)
    return positions.reshape(-1, LANES).astype(F32), freq, sign


def _retention_tables():
    chunk = RET_CHUNK
    f32 = np.float32
    log_gamma = np.log1p(-np.exp2(-5.0 - np.arange(N_HEADS, dtype=f32))).astype(f32)
    idx = np.arange(chunk, dtype=f32)
    diff = idx[:, None] - idx[None, :]
    intra = np.where(diff >= 0, np.exp(log_gamma[:, None, None] * np.maximum(diff, 0.0)), 0.0)
    q_decay = np.exp(log_gamma[:, None] * (idx + 1.0))
    k_decay = np.exp(log_gamma[:, None] * (chunk - 1.0 - idx))
    chunk_decay = np.exp(log_gamma * chunk)
    tables = (intra,
              np.broadcast_to(q_decay[:, :, None], (N_HEADS, chunk, HEAD_DIM)),
              np.broadcast_to(k_decay[:, :, None], (N_HEADS, chunk, HEAD_DIM)),
              np.broadcast_to(chunk_decay[:, None, None], (N_HEADS, HEAD_DIM, HEAD_DIM)))
    return tuple(jnp.asarray(np.ascontiguousarray(t, dtype=f32)) for t in tables)


def _retention_rows(q_ref, k_ref, v_ref, idec_ref, qdec_ref, kdec_ref, cdec_ref, st_scr, emit):
    chunk = RET_CHUNK
    n_chunks = q_ref.shape[0] // chunk
    cols = [slice(h * HEAD_DIM, (h + 1) * HEAD_DIM) for h in range(N_HEADS)]
    rows = [slice(c * chunk, (c + 1) * chunk) for c in range(n_chunks)]

    kv = {}
    for c in range(n_chunks):
        for h in range(N_HEADS):
            k_decayed = (k_ref[rows[c], cols[h]].astype(F32) * kdec_ref[h]).astype(BF16)
            kv[h, c] = _dot_tn(k_decayed, v_ref[rows[c], cols[h]])
    state = {}
    for h in range(N_HEADS):
        s = st_scr[h]
        for c in range(n_chunks):
            state[h, c] = s.astype(BF16)
            s = s * cdec_ref[h] + kv[h, c]
        st_scr[h] = s
    for c in range(n_chunks):
        for h in range(N_HEADS):
            q = q_ref[rows[c], cols[h]]
            scores = _dot_nt(q, k_ref[rows[c], cols[h]]) * idec_ref[h]
            out = (_dot(scores.astype(BF16), v_ref[rows[c], cols[h]])
                   + _dot(q, state[h, c]) * qdec_ref[h])
            ms = jnp.mean(out * out, axis=-1, keepdims=True)
            emit(rows[c], h, out * lax.rsqrt(ms + EPS))


def _pooled_rows(pc_ref, halo_ref, cw_ref, cs_ref, ext_scr, seq_row0, emit):
    tm = pc_ref.shape[0]
    ext_scr[0:POOL_HALO, :] = jnp.where(seq_row0 == 0, jnp.zeros_like(halo_ref), halo_ref[...])
    ext_scr[POOL_HALO:, :] = pc_ref[...]
    src_rows = BLOCK + POOL_HALO
    lag = (lax.broadcasted_iota(jnp.int32, (BLOCK, src_rows), 0) + POOL_HALO
           - lax.broadcasted_iota(jnp.int32, (BLOCK, src_rows), 1))
    t_seq = seq_row0 + lax.broadcasted_iota(jnp.int32, (BLOCK, POOL_GROUP_DIM), 0)
    for gi, win in enumerate(POOL_WINDOWS):
        cols = slice(gi * POOL_GROUP_DIM, (gi + 1) * POOL_GROUP_DIM)
        band = jnp.where((lag >= 0) & (lag < win), 1.0 / win, 0.0).astype(BF16)
        short = jnp.where(t_seq + 1 < win, win / (t_seq + 1).astype(F32), 1.0)
        pooled = []
        for c in range(tm // BLOCK):
            mean = _dot(band, ext_scr[c * BLOCK:c * BLOCK + src_rows, cols])
            if c == 0:
                mean = mean * short
            cur = pc_ref[c * BLOCK:(c + 1) * BLOCK, cols].astype(F32)
            pooled.append((mean - cur).astype(BF16))
        emit(gi, _dot(jnp.concatenate(pooled, axis=0), cw_ref[gi]) * cs_ref[:, cols])


def _odd_mixers_kernel(p_hbm, idec_ref, qdec_ref, kdec_ref, cdec_ref,
                       x_ref, gate_ref, halo_ref, cw_ref, cs_ref, wout_ref,
                       o_ref, st_scr, ext_scr, y_scr, p_buf, p_sem):
    tm = x_ref.shape[0]
    c_width = halo_ref.shape[1]
    n_buf = p_buf.shape[0]
    step = pl.program_id(0) * pl.num_programs(1) + pl.program_id(1)
    n_steps = pl.num_programs(0) * pl.num_programs(1)

    def fetch(j):
        return pltpu.make_async_copy(p_hbm.at[pl.ds(j * tm, tm), :], p_buf.at[j % n_buf],
                                     p_sem.at[j % n_buf])

    @pl.when(step == 0)
    def _():
        for j in range(n_buf - 1):
            fetch(j).start()

    @pl.when(step + (n_buf - 1) < n_steps)
    def _():
        fetch(step + (n_buf - 1)).start()

    fetch(step).wait()
    p_ref = p_buf.at[step % n_buf]
    pc_ref, q_ref, k_ref, v_ref, za_ref, zb_ref = (
        p_ref.at[:, pl.ds(j * c_width, c_width)] for j in range(6))

    @pl.when(pl.program_id(1) == 0)
    def _():
        st_scr[...] = jnp.zeros_like(st_scr)

    def gated_retention(rows, h, out):
        cols = slice(h * HEAD_DIM, (h + 1) * HEAD_DIM)
        ycols = slice(c_width + h * HEAD_DIM, c_width + (h + 1) * HEAD_DIM)
        y_scr[rows, ycols] = (out * _silu(zb_ref[rows, cols].astype(F32))).astype(BF16)

    def gated_pool(gi, mixed):
        cols = slice(gi * POOL_GROUP_DIM, (gi + 1) * POOL_GROUP_DIM)
        y_scr[:, cols] = (mixed * _silu(za_ref[:, cols].astype(F32))).astype(BF16)

    _retention_rows(q_ref, k_ref, v_ref, idec_ref, qdec_ref, kdec_ref, cdec_ref, st_scr,
                    gated_retention)
    _pooled_rows(pc_ref, halo_ref, cw_ref, cs_ref, ext_scr, pl.program_id(1) * tm, gated_pool)
    o_ref[...] = x_ref[...] + gate_ref[...] * _dot(y_scr[...], wout_ref[...])


def _odd_mixers(x2, batch, seq, gate, p, c_w_bf, c_scale, w_out_bf):
    n, d = x2.shape
    tm = ROW_TILE
    cw = N_HEADS * HEAD_DIM
    steps = seq // tm
    assert batch * steps >= ODD_ROW_BUFFERS - 1
    tables = _retention_tables()
    tabs = [pl.BlockSpec(t.shape, lambda b, s: (0, 0, 0)) for t in tables]
    halo_blocks = tm // POOL_HALO
    halo = pl.BlockSpec(
        (POOL_HALO, cw), lambda b, s: (jnp.maximum((b * steps + s) * halo_blocks - 1, 0), 0))
    return pl.pallas_call(
        _odd_mixers_kernel,
        grid=(batch, steps),
        in_specs=[pl.BlockSpec(memory_space=pl.ANY)] + tabs + [
            pl.BlockSpec((tm, d), lambda b, s: (b * steps + s, 0)),
            pl.BlockSpec((None, 1, d), lambda b, s: (b, 0, 0)),
            halo,
            pl.BlockSpec(c_w_bf.shape, lambda b, s: (0, 0, 0)),
            pl.BlockSpec((1, cw), lambda b, s: (0, 0)),
            pl.BlockSpec(w_out_bf.shape, lambda b, s: (0, 0))],
        out_specs=pl.BlockSpec((tm, d), lambda b, s: (b * steps + s, 0)),
        out_shape=jax.ShapeDtypeStruct((n, d), F32),
        scratch_shapes=[pltpu.VMEM((N_HEADS, HEAD_DIM, HEAD_DIM), F32),
                        pltpu.VMEM((POOL_HALO + tm, cw), BF16),
                        pltpu.VMEM((tm, 2 * cw), BF16),
                        pltpu.VMEM((ODD_ROW_BUFFERS, tm, p.shape[1]), BF16),
                        pltpu.SemaphoreType.DMA((ODD_ROW_BUFFERS,))],
        compiler_params=pltpu.CompilerParams(
            dimension_semantics=("arbitrary", "arbitrary"), vmem_limit_bytes=VMEM_LIMIT),
        name="odd_mixers",
    )(p, *tables, x2, gate, p, c_w_bf, c_scale.reshape(1, cw), w_out_bf)


def _even_layer(x2, batch, seq, mod, norm_g, w_in, a_vnorm_g, a_ws, a_bs,
                b_qnorm_g, b_knorm_g, w_out):
    d = x2.shape[1]
    shift, scale, gate = _split_modulation(mod)
    gains = [a_vnorm_g, b_qnorm_g.reshape(1, -1), b_knorm_g.reshape(1, -1)]
    modes = ("hold", "norm", "norm_scaled", "norm", "raw", "silu_gate", "raw")
    p = _in_projection(x2, seq, norm_g, scale, shift, w_in, gains, modes,
                       post_scale=LOG2E * HEAD_DIM ** -0.5)
    b_out = _stick_breaking(p, batch, seq, q_col=d, k_col=2 * d, v_col=3 * d)
    bias_full = jnp.repeat(a_bs.T, HEAD_DIM, axis=1)
    return _out_even(x2, seq, gate, p, b_out, a_ws, bias_full, w_out.astype(BF16))


def _odd_layer(x2, batch, seq, mod, positions, norm_g, w_in, c_w, c_scale,
               d_qnorm_g, d_knorm_g, w_out):
    shift, scale, gate = _split_modulation(mod)
    gains = [d_qnorm_g.reshape(1, -1), d_knorm_g.reshape(1, -1)]
    modes = ("raw", "rope", "rope_scaled", "raw", "raw", "raw")
    p = _in_projection(x2, seq, norm_g, scale, shift, w_in, gains, modes,
                       rope=_rope_operands(positions), post_scale=HEAD_DIM ** -0.5)
    return _odd_mixers(x2, batch, seq, gate, p, c_w.astype(BF16), c_scale, w_out.astype(BF16))


def kernel(x, c, positions, even_norm_g, even_w_mod, even_b_mod, even_w_in, even_a_vnorm_g, even_a_ws, even_a_bs, even_b_qnorm_g, even_b_knorm_g, even_w_out, odd_norm_g, odd_w_mod, odd_b_mod, odd_w_in, odd_c_w, odd_c_scale, odd_d_qnorm_g, odd_d_knorm_g, odd_w_out):
    batch, seq, d = x.shape
    depth = even_norm_g.shape[0] + odd_norm_g.shape[0]
    x2 = x.reshape(batch * seq, d)
    n_even = even_norm_g.shape[0]
    mods = _modulations(c, even_w_mod, even_b_mod, odd_w_mod, odd_b_mod)
    for layer in range(depth):
        i = layer // 2
        if layer % 2 == 0:
            x2 = _even_layer(x2, batch, seq, mods[i], even_norm_g[i],
                             even_w_in[i], even_a_vnorm_g[i], even_a_ws[i], even_a_bs[i],
                             even_b_qnorm_g[i], even_b_knorm_g[i], even_w_out[i])
        else:
            x2 = _odd_layer(x2, batch, seq, mods[n_even + i], positions, odd_norm_g[i],
                            odd_w_in[i], odd_c_w[i], odd_c_scale[i], odd_d_qnorm_g[i],
                            odd_d_knorm_g[i], odd_w_out[i])
    return x2.reshape(batch, seq, d)
```

```python
import functools

import jax
import jax.numpy as jnp
import numpy as np
from jax import lax
from jax.experimental import pallas as pl
from jax.experimental.pallas import tpu as pltpu

F32 = jnp.float32
BF16 = jnp.bfloat16

EPS = 1e-6
ROPE_BASE = 10000.0
BLOCK = 128
HEAD_DIM = 128
N_HEADS = 8
POOL_WINDOWS = (2, 4, 8, 16)
POOL_GROUP_DIM = 256
POOL_HALO = 16

MOD_ROWS = 8

ROW_TILE = 512
OUT_EVEN_ROWS = 1024
COL_TILE = 1024
RET_CHUNK = 128

ATT_Q = 64
ATT_WIN = 384
ATT_NEW = 256
ATT_EXCLUDED = -1e30
LANES = 128
ATT_SUB = 4
ATT_HEADS_PER_STEP = 4
ATT_SKIP_BELOW = -151.0
LOG2E = 1.4426950408889634
ATT_NO_LIMIT = 1 << 30

VMEM_LIMIT = 56 * 1024 * 1024


def _silu(v):
    return v * jax.nn.sigmoid(v)


def _dot(a, b):
    return jnp.dot(a, b, preferred_element_type=F32)


def _dot_nt(a, b):
    return lax.dot_general(a, b, (((1,), (1,)), ((), ())), preferred_element_type=F32)


def _dot_tn(a, b):
    return lax.dot_general(a, b, (((0,), (0,)), ((), ())), preferred_element_type=F32)


def _group_rms(a, gain):
    ms = jnp.mean(a * a, axis=-1, keepdims=True)
    return a * lax.rsqrt(ms + EPS) * gain


def _mod_kernel(c_ref, we_ref, be_ref, wo_ref, bo_ref, o_ref, *, n_even):
    a = _silu(c_ref[...]).astype(BF16)
    layer = pl.program_id(0)

    @pl.when(layer < n_even)
    def _():
        o_ref[...] = _dot(a, we_ref[...].astype(BF16)) + be_ref[...]

    @pl.when(layer >= n_even)
    def _():
        o_ref[...] = _dot(a, wo_ref[...].astype(BF16)) + bo_ref[...]


def _modulations(c, even_w, even_b, odd_w, odd_b):
    bsz, d = c.shape
    n_even, n_odd = even_w.shape[0], odd_w.shape[0]
    c8 = jnp.pad(c, ((0, MOD_ROWS - bsz), (0, 0)))
    even_at = lambda l, j: (jnp.minimum(l, n_even - 1), 0, jnp.where(l < n_even, j, 2))
    odd_at = lambda l, j: (jnp.maximum(l - n_even, 0), 0, jnp.where(l < n_even, 0, j))
    m = pl.pallas_call(
        functools.partial(_mod_kernel, n_even=n_even),
        grid=(n_even + n_odd, 3),
        in_specs=[pl.BlockSpec((MOD_ROWS, d), lambda l, j: (0, 0)),
                  pl.BlockSpec((None, d, d), even_at),
                  pl.BlockSpec((None, 1, d), even_at),
                  pl.BlockSpec((None, d, d), odd_at),
                  pl.BlockSpec((None, 1, d), odd_at)],
        out_specs=pl.BlockSpec((None, MOD_ROWS, d), lambda l, j: (l, 0, j)),
        out_shape=jax.ShapeDtypeStruct((n_even + n_odd, MOD_ROWS, 3 * d), F32),
        name="modulation",
    )(c8, even_w, even_b.reshape(n_even, 1, 3 * d), odd_w, odd_b.reshape(n_odd, 1, 3 * d))
    return m[:, :bsz]


def _split_modulation(m):
    d = m.shape[1] // 3
    return m[:, None, :d], m[:, None, d:2 * d], m[:, None, 2 * d:]


def _inproj_kernel(*refs, modes, out_tiles, gain_of, n_gains, post_scale, use_rope):
    *refs, h_scr, w_ref, stage, sem, held = refs
    x_ref, g_ref, sc_ref, sh_ref, w_hbm = refs[:5]
    gain_refs = refs[5:5 + n_gains]
    o_ref = refs[-1]
    tn = COL_TILE

    def weight_copy(j):
        return pltpu.make_async_copy(w_hbm.at[:, pl.ds(j * tn, tn)], stage.at[j % 2], sem.at[j % 2])

    @pl.when(pl.program_id(0) == 0)
    def _():
        weight_copy(0).start()
        for j in range(len(modes)):
            if j + 1 < len(modes):
                weight_copy(j + 1).start()
            weight_copy(j).wait()
            w_ref[:, j * tn:(j + 1) * tn] = stage[j % 2].astype(BF16)

    if use_rope:
        pos_ref, freq_ref, sign_ref = refs[5 + n_gains:8 + n_gains]
        on_diag = (lax.broadcasted_iota(jnp.int32, (LANES, LANES), 0)
                   == lax.broadcasted_iota(jnp.int32, (LANES, LANES), 1))
        angles = []
        for a in range(pos_ref.shape[0]):
            spread = jnp.where(on_diag, jnp.broadcast_to(pos_ref[a:a + 1, :], (LANES, LANES)), 0.0)
            angles.append(jnp.sum(spread, axis=1, keepdims=True) * freq_ref[...])
        ang = jnp.concatenate(angles, axis=0)
        cos_t = jnp.cos(ang)
        sin_t = jnp.sin(ang) * sign_ref[...]
    x = x_ref[...]
    ms = jnp.mean(x * x, axis=-1, keepdims=True)
    y = x * lax.rsqrt(ms + EPS) * g_ref[...]
    h_scr[...] = (y * (1.0 + sc_ref[...]) + sh_ref[...]).astype(BF16)

    cross_lane = ("norm", "norm_scaled", "rope", "rope_scaled")
    heavy = [jj for jj, m in enumerate(modes) if m in cross_lane]
    plain = [jj for jj, m in enumerate(modes) if m not in cross_lane]
    order = []
    while heavy or plain:
        order += heavy[:1] + plain[:1]
        heavy, plain = heavy[1:], plain[1:]
    for jj in order:
        mode = modes[jj]
        out0 = None if out_tiles[jj] is None else out_tiles[jj] * tn
        acc = _dot(h_scr[...], w_ref[:, jj * tn:(jj + 1) * tn])
        if mode == "hold":
            held[...] = acc
            continue
        if mode == "raw":
            o_ref[:, out0:out0 + tn] = acc.astype(BF16)
            continue
        if mode == "silu_gate":
            o_ref[:, out0:out0 + tn] = (held[...] * _silu(acc)).astype(BF16)
            continue
        gain_ref = gain_refs[gain_of[jj]]
        for g in range(tn // HEAD_DIM):
            row = g if gain_ref.shape[0] > 1 else 0
            t = _group_rms(acc[:, g * HEAD_DIM:(g + 1) * HEAD_DIM], gain_ref[row:row + 1, :])
            if mode in ("rope", "rope_scaled"):
                t = t * cos_t + pltpu.roll(t, HEAD_DIM // 2, axis=1) * sin_t
            if mode in ("norm_scaled", "rope_scaled"):
                t = t * post_scale
            o_ref[:, out0 + g * HEAD_DIM:out0 + (g + 1) * HEAD_DIM] = t.astype(BF16)


def _in_projection(x2, seq, norm_g, scale, shift, w_in, gains, modes, rope=None, post_scale=1.0):
    n, d = x2.shape
    tm, tn = ROW_TILE, COL_TILE
    ncols = len(modes) * tn
    out_tiles, n_out = [], 0
    for m in modes:
        out_tiles.append(None if m == "hold" else n_out)
        n_out += m != "hold"
    assert modes.count("hold") == modes.count("silu_gate") <= 1
    assert "hold" not in modes or modes.index("hold") < modes.index("silu_gate")
    use_rope = rope is not None
    normalising = [jj for jj, m in enumerate(modes) if m in ("norm", "norm_scaled", "rope", "rope_scaled")]
    assert len(gains) == len(normalising)
    gain_of = tuple(normalising.index(jj) if jj in normalising else None for jj in range(len(modes)))
    in_specs = [
        pl.BlockSpec((tm, d), lambda i: (i, 0)),
        pl.BlockSpec((1, d), lambda i: (0, 0)),
        pl.BlockSpec((None, 1, d), lambda i: ((i * tm) // seq, 0, 0)),
        pl.BlockSpec((None, 1, d), lambda i: ((i * tm) // seq, 0, 0)),
        pl.BlockSpec(memory_space=pl.ANY),
    ] + [pl.BlockSpec(gn.shape, lambda i: (0, 0)) for gn in gains]
    args = [x2, norm_g.reshape(1, d), scale, shift, w_in, *gains]
    if use_rope:
        vec = pl.BlockSpec((1, HEAD_DIM), lambda i: (0, 0))
        pos, freq, sign = rope
        in_specs += [pl.BlockSpec((None, tm // LANES, LANES), lambda i: (i, 0, 0)), vec, vec]
        args += [pos.reshape(n // tm, tm // LANES, LANES), freq, sign]
    return pl.pallas_call(
        functools.partial(_inproj_kernel, modes=tuple(modes), out_tiles=tuple(out_tiles),
                          gain_of=gain_of, n_gains=len(gains), post_scale=post_scale,
                          use_rope=use_rope),
        grid=(n // tm,),
        in_specs=in_specs,
        out_specs=pl.BlockSpec((tm, n_out * tn), lambda i: (i, 0)),
        out_shape=jax.ShapeDtypeStruct((n, n_out * tn), BF16),
        scratch_shapes=[pltpu.VMEM((tm, d), BF16),
                        pltpu.VMEM((d, ncols), BF16),
                        pltpu.VMEM((2, d, tn), F32),
                        pltpu.SemaphoreType.DMA((2,)),
                        pltpu.VMEM((tm, tn), F32)],
        compiler_params=pltpu.CompilerParams(
            dimension_semantics=("arbitrary",), vmem_limit_bytes=VMEM_LIMIT),
        name="in_projection",
    )(*args)


def _attn_kernel(q_ref, k_ref, v_ref, o_ref, u_scr, *, seq):
    hb, tq, win, nsub = ATT_HEADS_PER_STEP, ATT_Q, ATT_WIN, ATT_SUB
    chains = nsub * hb

    u_scr[...] = jnp.where(lax.broadcasted_iota(jnp.int32, (ATT_NEW, ATT_NEW), 0)
                           > lax.broadcasted_iota(jnp.int32, (ATT_NEW, ATT_NEW), 1),
                           1.0, 0.0).astype(BF16)
    old = win - ATT_NEW
    key_idx = lax.broadcasted_iota(jnp.int32, (tq, win), 1)
    row_idx = lax.broadcasted_iota(jnp.int32, (tq, win), 0)
    tail = win - LANES
    tail_ok = (lax.broadcasted_iota(jnp.int32, (tq, LANES), 1) + tail
               < lax.broadcasted_iota(jnp.int32, (tq, LANES), 0) + (win - tq))

    def head_cols(h):
        return slice(h * HEAD_DIM, (h + 1) * HEAD_DIM)

    def tile_logits(rows0, keys0):
        return [_dot_nt(q_ref[pl.ds(rows0[a], tq), head_cols(h)],
                        k_ref[pl.ds(keys0[a], win), head_cols(h)])
                for a in range(nsub) for h in range(hb)]

    def tile_weights(logits, rows0, keys0, limits, totals):
        log_beta, terms = [], []
        for a in range(nsub):
            if limits is not None:
                ok = key_idx < (jnp.minimum(rows0[a] + row_idx, limits[a]) - keys0[a])
            for h in range(hb):
                z = logits[a * hb + h]
                if limits is None:
                    z = jnp.concatenate(
                        [z[:, :tail], jnp.where(tail_ok, z[:, tail:], ATT_EXCLUDED)], axis=1)
                else:
                    z = jnp.where(ok, z, ATT_EXCLUDED)
                nz = -z
                lsm = (jnp.minimum(nz, 0.0)
                       - jnp.log(1.0 + jnp.exp2(jnp.minimum(z, nz))) * LOG2E)
                log_beta.append(z + lsm)
                terms.append(lsm)
        sums_new = _dot(jnp.concatenate([t[:, old:].astype(BF16) for t in terms], axis=0),
                        u_scr[...])
        sums_old = _dot(jnp.concatenate([t[:, :old].astype(BF16) for t in terms], axis=0),
                        u_scr[0:old, 0:old])
        weights, tile_totals = [], []
        for c in range(chains):
            newer = sums_new[c * tq:(c + 1) * tq]
            older = sums_old[c * tq:(c + 1) * tq]
            total_new = newer[:, 0:1] + terms[c][:, old:old + 1]
            expo = log_beta[c] + jnp.concatenate([older + total_new, newer], axis=1)
            if totals is not None:
                expo = expo + totals[c]
            weights.append(jnp.exp2(expo).astype(BF16))
            tile_totals.append(total_new + older[:, 0:1] + terms[c][:, 0:1])
        return weights, tile_totals

    def tile_values(weights, keys0):
        return [_dot(weights[a * hb + h], v_ref[pl.ds(keys0[a], win), head_cols(h)])
                for a in range(nsub) for h in range(hb)]

    def slowest(totals):
        m = totals[0]
        for t in totals[1:]:
            m = jnp.maximum(m, t)
        return jnp.max(m)

    def qblock(i, carry, *, near_start):
        r0 = i * (tq * nsub)
        rows0 = [pl.multiple_of(r0 + a * tq, tq) for a in range(nsub)]
        if near_start:
            starts = [jnp.maximum(r - (win - tq), 0) for r in rows0]
        else:
            starts = [r - (win - tq) for r in rows0]
        keys0 = [pl.multiple_of(s, tq) for s in starts]
        weights, totals = tile_weights(tile_logits(rows0, keys0), rows0, keys0,
                                       [ATT_NO_LIMIT] * nsub if near_start else None, None)
        accs = tile_values(weights, keys0)

        def more(st):
            m, worst, _, _ = st
            return (starts[-1] - m * win > 0) & (worst > ATT_SKIP_BELOW)

        def older(st):
            m, _, totals, accs = st
            limits = [s - m * win for s in starts]
            keys0 = [pl.multiple_of(jnp.maximum(lim - win, 0), tq) for lim in limits]
            weights, tile_totals = tile_weights(tile_logits(rows0, keys0), rows0, keys0,
                                                limits, totals)
            outs = tile_values(weights, keys0)
            totals = tuple(t + d for t, d in zip(totals, tile_totals))
            accs = tuple(acc + o for acc, o in zip(accs, outs))
            return (m + 1, slowest(totals), totals, accs)

        _, _, _, accs = lax.while_loop(
            more, older, (jnp.int32(0), slowest(totals), tuple(totals), tuple(accs)))
        for a in range(nsub):
            for h in range(hb):
                o_ref[pl.ds(rows0[a], tq), head_cols(h)] = accs[a * hb + h].astype(BF16)
        return carry

    n_blocks = seq // (tq * nsub)
    n_clipped = min(-(-(win - tq) // (tq * nsub)), n_blocks)
    lax.fori_loop(0, n_clipped, functools.partial(qblock, near_start=True), 0)
    lax.fori_loop(n_clipped, n_blocks, functools.partial(qblock, near_start=False), 0)


def _stick_breaking(p, batch, seq, q_col, k_col, v_col):
    n = p.shape[0]
    width = ATT_HEADS_PER_STEP * HEAD_DIM
    n_hg = N_HEADS // ATT_HEADS_PER_STEP

    def spec(col):
        return pl.BlockSpec((seq, width), lambda b, hg: (b, col // width + hg))

    return pl.pallas_call(
        functools.partial(_attn_kernel, seq=seq),
        grid=(batch, n_hg),
        in_specs=[spec(q_col), spec(k_col), spec(v_col)],
        out_specs=pl.BlockSpec((seq, width), lambda b, hg: (b, hg)),
        out_shape=jax.ShapeDtypeStruct((n, N_HEADS * HEAD_DIM), BF16),
        scratch_shapes=[pltpu.VMEM((ATT_NEW, ATT_NEW), BF16)],
        compiler_params=pltpu.CompilerParams(
            dimension_semantics=("parallel", "parallel"), vmem_limit_bytes=VMEM_LIMIT),
        name="stick_breaking",
    )(p, p, p)


def _out_even_kernel(x_ref, gate_ref, ug_ref, vg_ref, zb_ref, bo_ref, ws_ref, bias_ref, wout_ref,
                     o_ref, y_scr):
    tm = x_ref.shape[0]
    a_width = bo_ref.shape[1]
    row = lax.broadcasted_iota(jnp.int32, (BLOCK, BLOCK), 0)
    col = lax.broadcasted_iota(jnp.int32, (BLOCK, BLOCK), 1)
    for g in range(a_width // HEAD_DIM):
        cols = slice(g * HEAD_DIM, (g + 1) * HEAD_DIM)
        wg = jnp.where(col <= row, ws_ref[g], 0.0).astype(BF16)
        for c in range(tm // BLOCK):
            rows = slice(c * BLOCK, (c + 1) * BLOCK)
            mixed = _dot(wg, vg_ref[rows, cols]) + bias_ref[:, cols]
            y_scr[rows, cols] = (ug_ref[rows, cols].astype(F32) * mixed).astype(BF16)
    y_scr[:, a_width:] = (bo_ref[...].astype(F32) * _silu(zb_ref[...].astype(F32))).astype(BF16)
    o_ref[...] = x_ref[...] + gate_ref[...] * _dot(y_scr[...], wout_ref[...])


def _out_even(x2, seq, gate, p, b_out, a_ws, bias_full, w_out_bf):
    n, d = x2.shape
    tm = OUT_EVEN_ROWS
    aw = b_out.shape[1]
    row = lambda c: pl.BlockSpec((tm, aw), lambda i: (i, c))
    const = lambda a: pl.BlockSpec(a.shape, lambda i: (0,) * a.ndim, pipeline_mode=pl.Buffered(1))
    return pl.pallas_call(
        _out_even_kernel,
        grid=(n // tm,),
        in_specs=[pl.BlockSpec((tm, d), lambda i: (i, 0)),
                  pl.BlockSpec((None, 1, d), lambda i: ((i * tm) // seq, 0, 0)),
                  row(4), row(0), row(5),
                  row(0),
                  const(a_ws), const(bias_full), const(w_out_bf)],
        out_specs=pl.BlockSpec((tm, d), lambda i: (i, 0)),
        out_shape=jax.ShapeDtypeStruct((n, d), F32),
        scratch_shapes=[pltpu.VMEM((tm, 2 * aw), BF16)],
        compiler_params=pltpu.CompilerParams(
            dimension_semantics=("parallel",), vmem_limit_bytes=VMEM_LIMIT),
        name="out_even",
    )(x2, gate, p, p, p, b_out, a_ws, bias_full, w_out_bf)


def _rope_operands(positions):
    half = HEAD_DIM // 2
    inv_freq = ROPE_BASE ** (-jnp.arange(half, dtype=F32) / half)
    freq = jnp.concatenate([inv_freq, inv_freq]).reshape(1, HEAD_DIM)
    sign = jnp.concatenate([-jnp.ones((half,), F32), jnp.ones((half,), F32)]).reshape(1, HEAD_DIM)
    return positions.reshape(-1, LANES).astype(F32), freq, sign


def _retention_tables():
    chunk = RET_CHUNK
    f32 = np.float32
    log_gamma = np.log1p(-np.exp2(-5.0 - np.arange(N_HEADS, dtype=f32))).astype(f32)
    idx = np.arange(chunk, dtype=f32)
    diff = idx[:, None] - idx[None, :]
    intra = np.where(diff >= 0, np.exp(log_gamma[:, None, None] * np.maximum(diff, 0.0)), 0.0)
    q_decay = np.exp(log_gamma[:, None] * (idx + 1.0))
    k_decay = np.exp(log_gamma[:, None] * (chunk - 1.0 - idx))
    chunk_decay = np.exp(log_gamma * chunk)
    tables = (intra,
              np.broadcast_to(q_decay[:, :, None], (N_HEADS, chunk, HEAD_DIM)),
              np.broadcast_to(k_decay[:, :, None], (N_HEADS, chunk, HEAD_DIM)),
              np.broadcast_to(chunk_decay[:, None, None], (N_HEADS, HEAD_DIM, HEAD_DIM)))
    return tuple(jnp.asarray(np.ascontiguousarray(t, dtype=f32)) for t in tables)


def _retention_rows(q_ref, k_ref, v_ref, idec_ref, qdec_ref, kdec_ref, cdec_ref, st_scr, emit):
    chunk = RET_CHUNK
    n_chunks = q_ref.shape[0] // chunk
    cols = [slice(h * HEAD_DIM, (h + 1) * HEAD_DIM) for h in range(N_HEADS)]
    rows = [slice(c * chunk, (c + 1) * chunk) for c in range(n_chunks)]

    kv = {}
    for c in range(n_chunks):
        for h in range(N_HEADS):
            k_decayed = (k_ref[rows[c], cols[h]].astype(F32) * kdec_ref[h]).astype(BF16)
            kv[h, c] = _dot_tn(k_decayed, v_ref[rows[c], cols[h]])
    state = {}
    for h in range(N_HEADS):
        s = st_scr[h]
        for c in range(n_chunks):
            state[h, c] = s.astype(BF16)
            s = s * cdec_ref[h] + kv[h, c]
        st_scr[h] = s
    for c in range(n_chunks):
        for h in range(N_HEADS):
            q = q_ref[rows[c], cols[h]]
            scores = _dot_nt(q, k_ref[rows[c], cols[h]]) * idec_ref[h]
            out = (_dot(scores.astype(BF16), v_ref[rows[c], cols[h]])
                   + _dot(q, state[h, c]) * qdec_ref[h])
            ms = jnp.mean(out * out, axis=-1, keepdims=True)
            emit(rows[c], h, out * lax.rsqrt(ms + EPS))


def _pooled_rows(pc_ref, halo_ref, cw_ref, cs_ref, ext_scr, seq_row0, emit):
    tm = pc_ref.shape[0]
    ext_scr[0:POOL_HALO, :] = jnp.where(seq_row0 == 0, jnp.zeros_like(halo_ref), halo_ref[...])
    ext_scr[POOL_HALO:, :] = pc_ref[...]
    src_rows = BLOCK + POOL_HALO
    lag = (lax.broadcasted_iota(jnp.int32, (BLOCK, src_rows), 0) + POOL_HALO
           - lax.broadcasted_iota(jnp.int32, (BLOCK, src_rows), 1))
    t_seq = seq_row0 + lax.broadcasted_iota(jnp.int32, (BLOCK, POOL_GROUP_DIM), 0)
    for gi, win in enumerate(POOL_WINDOWS):
        cols = slice(gi * POOL_GROUP_DIM, (gi + 1) * POOL_GROUP_DIM)
        band = jnp.where((lag >= 0) & (lag < win), 1.0 / win, 0.0).astype(BF16)
        short = jnp.where(t_seq + 1 < win, win / (t_seq + 1).astype(F32), 1.0)
        pooled = []
        for c in range(tm // BLOCK):
            mean = _dot(band, ext_scr[c * BLOCK:c * BLOCK + src_rows, cols])
            if c == 0:
                mean = mean * short
            cur = pc_ref[c * BLOCK:(c + 1) * BLOCK, cols].astype(F32)
            pooled.append((mean - cur).astype(BF16))
        emit(gi, _dot(jnp.concatenate(pooled, axis=0), cw_ref[gi]) * cs_ref[:, cols])


def _odd_mixers_kernel(p_ref, idec_ref, qdec_ref, kdec_ref, cdec_ref,
                       x_ref, gate_ref, halo_ref, cw_ref, cs_ref, wout_ref,
                       o_ref, st_scr, ext_scr, y_scr):
    tm = x_ref.shape[0]
    c_width = halo_ref.shape[1]
    pc_ref, q_ref, k_ref, v_ref, za_ref, zb_ref = (
        p_ref.at[:, pl.ds(j * c_width, c_width)] for j in range(6))

    @pl.when(pl.program_id(1) == 0)
    def _():
        st_scr[...] = jnp.zeros_like(st_scr)

    def gated_retention(rows, h, out):
        cols = slice(h * HEAD_DIM, (h + 1) * HEAD_DIM)
        ycols = slice(c_width + h * HEAD_DIM, c_width + (h + 1) * HEAD_DIM)
        y_scr[rows, ycols] = (out * _silu(zb_ref[rows, cols].astype(F32))).astype(BF16)

    def gated_pool(gi, mixed):
        cols = slice(gi * POOL_GROUP_DIM, (gi + 1) * POOL_GROUP_DIM)
        y_scr[:, cols] = (mixed * _silu(za_ref[:, cols].astype(F32))).astype(BF16)

    _retention_rows(q_ref, k_ref, v_ref, idec_ref, qdec_ref, kdec_ref, cdec_ref, st_scr,
                    gated_retention)
    _pooled_rows(pc_ref, halo_ref, cw_ref, cs_ref, ext_scr, pl.program_id(1) * tm, gated_pool)
    o_ref[...] = x_ref[...] + gate_ref[...] * _dot(y_scr[...], wout_ref[...])


def _odd_mixers(x2, batch, seq, gate, p, c_w_bf, c_scale, w_out_bf):
    n, d = x2.shape
    tm = ROW_TILE
    cw = N_HEADS * HEAD_DIM
    steps = seq // tm
    tables = _retention_tables()
    tabs = [pl.BlockSpec(t.shape, lambda b, s: (0, 0, 0)) for t in tables]
    halo_blocks = tm // POOL_HALO
    halo = pl.BlockSpec(
        (POOL_HALO, cw), lambda b, s: (jnp.maximum((b * steps + s) * halo_blocks - 1, 0), 0))
    return pl.pallas_call(
        _odd_mixers_kernel,
        grid=(batch, steps),
        in_specs=[pl.BlockSpec((tm, p.shape[1]), lambda b, s: (b * steps + s, 0))] + tabs + [
            pl.BlockSpec((tm, d), lambda b, s: (b * steps + s, 0)),
            pl.BlockSpec((None, 1, d), lambda b, s: (b, 0, 0)),
            halo,
            pl.BlockSpec(c_w_bf.shape, lambda b, s: (0, 0, 0)),
            pl.BlockSpec((1, cw), lambda b, s: (0, 0)),
            pl.BlockSpec(w_out_bf.shape, lambda b, s: (0, 0))],
        out_specs=pl.BlockSpec((tm, d), lambda b, s: (b * steps + s, 0)),
        out_shape=jax.ShapeDtypeStruct((n, d), F32),
        scratch_shapes=[pltpu.VMEM((N_HEADS, HEAD_DIM, HEAD_DIM), F32),
                        pltpu.VMEM((POOL_HALO + tm, cw), BF16),
                        pltpu.VMEM((tm, 2 * cw), BF16)],
        compiler_params=pltpu.CompilerParams(
            dimension_semantics=("parallel", "arbitrary"), vmem_limit_bytes=VMEM_LIMIT),
        name="odd_mixers",
    )(p, *tables, x2, gate, p, c_w_bf, c_scale.reshape(1, cw), w_out_bf)


def _even_layer(x2, batch, seq, mod, norm_g, w_in, a_vnorm_g, a_ws, a_bs,
                b_qnorm_g, b_knorm_g, w_out):
    d = x2.shape[1]
    shift, scale, gate = _split_modulation(mod)
    gains = [a_vnorm_g, b_qnorm_g.reshape(1, -1), b_knorm_g.reshape(1, -1)]
    modes = ("hold", "norm", "norm_scaled", "norm", "raw", "silu_gate", "raw")
    p = _in_projection(x2, seq, norm_g, scale, shift, w_in, gains, modes,
                       post_scale=LOG2E * HEAD_DIM ** -0.5)
    b_out = _stick_breaking(p, batch, seq, q_col=d, k_col=2 * d, v_col=3 * d)
    bias_full = jnp.repeat(a_bs.T, HEAD_DIM, axis=1)
    return _out_even(x2, seq, gate, p, b_out, a_ws, bias_full, w_out.astype(BF16))


def _odd_layer(x2, batch, seq, mod, positions, norm_g, w_in, c_w, c_scale,
               d_qnorm_g, d_knorm_g, w_out):
    shift, scale, gate = _split_modulation(mod)
    gains = [d_qnorm_g.reshape(1, -1), d_knorm_g.reshape(1, -1)]
    modes = ("raw", "rope", "rope_scaled", "raw", "raw", "raw")
    p = _in_projection(x2, seq, norm_g, scale, shift, w_in, gains, modes,
                       rope=_rope_operands(positions), post_scale=HEAD_DIM ** -0.5)
    return _odd_mixers(x2, batch, seq, gate, p, c_w.astype(BF16), c_scale, w_out.astype(BF16))


def kernel(x, c, positions, even_norm_g, even_w_mod, even_b_mod, even_w_in, even_a_vnorm_g, even_a_ws, even_a_bs, even_b_qnorm_g, even_b_knorm_g, even_w_out, odd_norm_g, odd_w_mod, odd_b_mod, odd_w_in, odd_c_w, odd_c_scale, odd_d_qnorm_g, odd_d_knorm_g, odd_w_out):
    batch, seq, d = x.shape
    depth = even_norm_g.shape[0] + odd_norm_g.shape[0]
    x2 = x.reshape(batch * seq, d)
    n_even = even_norm_g.shape[0]
    mods = _modulations(c, even_w_mod, even_b_mod, odd_w_mod, odd_b_mod)
    for layer in range(depth):
        i = layer // 2
        if layer % 2 == 0:
            x2 = _even_layer(x2, batch, seq, mods[i], even_norm_g[i],
                             even_w_in[i], even_a_vnorm_g[i], even_a_ws[i], even_a_bs[i],
                             even_b_qnorm_g[i], even_b_knorm_g[i], even_w_out[i])
        else:
            x2 = _odd_layer(x2, batch, seq, mods[n_even + i], positions, odd_norm_g[i],
                            odd_w_in[i], odd_c_w[i], odd_c_scale[i], odd_d_qnorm_g[i],
                            odd_d_knorm_g[i], odd_w_out[i])
    return x2.reshape(batch, seq, d)
```

```python
import functools

import jax
import jax.numpy as jnp
import numpy as np
from jax import lax
from jax.experimental import pallas as pl
from jax.experimental.pallas import tpu as pltpu

F32 = jnp.float32
BF16 = jnp.bfloat16

EPS = 1e-6
ROPE_BASE = 10000.0
BLOCK = 128
HEAD_DIM = 128
N_HEADS = 8
POOL_WINDOWS = (2, 4, 8, 16)
POOL_GROUP_DIM = 256
POOL_HALO = 16

MOD_ROWS = 8

ROW_TILE = 512
OUT_EVEN_ROWS = 1024
COL_TILE = 1024
RET_CHUNK = 128

ATT_Q = 64
ATT_WIN = 384
ATT_NEW = 256
ATT_EXCLUDED = -1e30
LANES = 128
ATT_SUB = 8
ATT_HEADS_PER_STEP = 4
ATT_SKIP_BELOW = -151.0
LOG2E = 1.4426950408889634
ATT_NO_LIMIT = 1 << 30

VMEM_LIMIT = 56 * 1024 * 1024


def _silu(v):
    return v * jax.nn.sigmoid(v)


def _dot(a, b):
    return jnp.dot(a, b, preferred_element_type=F32)


def _dot_nt(a, b):
    return lax.dot_general(a, b, (((1,), (1,)), ((), ())), preferred_element_type=F32)


def _dot_tn(a, b):
    return lax.dot_general(a, b, (((0,), (0,)), ((), ())), preferred_element_type=F32)


def _group_rms(a, gain):
    ms = jnp.mean(a * a, axis=-1, keepdims=True)
    return a * lax.rsqrt(ms + EPS) * gain


def _mod_kernel(c_ref, we_ref, be_ref, wo_ref, bo_ref, o_ref, *, n_even):
    a = _silu(c_ref[...]).astype(BF16)
    layer = pl.program_id(0)

    @pl.when(layer < n_even)
    def _():
        o_ref[...] = _dot(a, we_ref[...].astype(BF16)) + be_ref[...]

    @pl.when(layer >= n_even)
    def _():
        o_ref[...] = _dot(a, wo_ref[...].astype(BF16)) + bo_ref[...]


def _modulations(c, even_w, even_b, odd_w, odd_b):
    bsz, d = c.shape
    n_even, n_odd = even_w.shape[0], odd_w.shape[0]
    c8 = jnp.pad(c, ((0, MOD_ROWS - bsz), (0, 0)))
    even_at = lambda l, j: (jnp.minimum(l, n_even - 1), 0, jnp.where(l < n_even, j, 2))
    odd_at = lambda l, j: (jnp.maximum(l - n_even, 0), 0, jnp.where(l < n_even, 0, j))
    m = pl.pallas_call(
        functools.partial(_mod_kernel, n_even=n_even),
        grid=(n_even + n_odd, 3),
        in_specs=[pl.BlockSpec((MOD_ROWS, d), lambda l, j: (0, 0)),
                  pl.BlockSpec((None, d, d), even_at),
                  pl.BlockSpec((None, 1, d), even_at),
                  pl.BlockSpec((None, d, d), odd_at),
                  pl.BlockSpec((None, 1, d), odd_at)],
        out_specs=pl.BlockSpec((None, MOD_ROWS, d), lambda l, j: (l, 0, j)),
        out_shape=jax.ShapeDtypeStruct((n_even + n_odd, MOD_ROWS, 3 * d), F32),
        name="modulation",
    )(c8, even_w, even_b.reshape(n_even, 1, 3 * d), odd_w, odd_b.reshape(n_odd, 1, 3 * d))
    return m[:, :bsz]


def _split_modulation(m):
    d = m.shape[1] // 3
    return m[:, None, :d], m[:, None, d:2 * d], m[:, None, 2 * d:]


def _inproj_kernel(*refs, modes, out_tiles, gain_of, n_gains, post_scale, use_rope):
    *refs, h_scr, w_ref, stage, sem, held = refs
    x_ref, g_ref, sc_ref, sh_ref, w_hbm = refs[:5]
    gain_refs = refs[5:5 + n_gains]
    o_ref = refs[-1]
    tn = COL_TILE

    def weight_copy(j):
        return pltpu.make_async_copy(w_hbm.at[:, pl.ds(j * tn, tn)], stage.at[j % 2], sem.at[j % 2])

    @pl.when(pl.program_id(0) == 0)
    def _():
        weight_copy(0).start()
        for j in range(len(modes)):
            if j + 1 < len(modes):
                weight_copy(j + 1).start()
            weight_copy(j).wait()
            w_ref[:, j * tn:(j + 1) * tn] = stage[j % 2].astype(BF16)

    if use_rope:
        pos_ref, freq_ref, sign_ref = refs[5 + n_gains:8 + n_gains]
        on_diag = (lax.broadcasted_iota(jnp.int32, (LANES, LANES), 0)
                   == lax.broadcasted_iota(jnp.int32, (LANES, LANES), 1))
        angles = []
        for a in range(pos_ref.shape[0]):
            spread = jnp.where(on_diag, jnp.broadcast_to(pos_ref[a:a + 1, :], (LANES, LANES)), 0.0)
            angles.append(jnp.sum(spread, axis=1, keepdims=True) * freq_ref[...])
        ang = jnp.concatenate(angles, axis=0)
        cos_t = jnp.cos(ang)
        sin_t = jnp.sin(ang) * sign_ref[...]
    x = x_ref[...]
    ms = jnp.mean(x * x, axis=-1, keepdims=True)
    y = x * lax.rsqrt(ms + EPS) * g_ref[...]
    h_scr[...] = (y * (1.0 + sc_ref[...]) + sh_ref[...]).astype(BF16)

    cross_lane = ("norm", "norm_scaled", "rope", "rope_scaled")
    heavy = [jj for jj, m in enumerate(modes) if m in cross_lane]
    plain = [jj for jj, m in enumerate(modes) if m not in cross_lane]
    order = []
    while heavy or plain:
        order += heavy[:1] + plain[:1]
        heavy, plain = heavy[1:], plain[1:]
    for jj in order:
        mode = modes[jj]
        out0 = None if out_tiles[jj] is None else out_tiles[jj] * tn
        acc = _dot(h_scr[...], w_ref[:, jj * tn:(jj + 1) * tn])
        if mode == "hold":
            held[...] = acc
            continue
        if mode == "raw":
            o_ref[:, out0:out0 + tn] = acc.astype(BF16)
            continue
        if mode == "silu_gate":
            o_ref[:, out0:out0 + tn] = (held[...] * _silu(acc)).astype(BF16)
            continue
        gain_ref = gain_refs[gain_of[jj]]
        for g in range(tn // HEAD_DIM):
            row = g if gain_ref.shape[0] > 1 else 0
            t = _group_rms(acc[:, g * HEAD_DIM:(g + 1) * HEAD_DIM], gain_ref[row:row + 1, :])
            if mode in ("rope", "rope_scaled"):
                t = t * cos_t + pltpu.roll(t, HEAD_DIM // 2, axis=1) * sin_t
            if mode in ("norm_scaled", "rope_scaled"):
                t = t * post_scale
            o_ref[:, out0 + g * HEAD_DIM:out0 + (g + 1) * HEAD_DIM] = t.astype(BF16)


def _in_projection(x2, seq, norm_g, scale, shift, w_in, gains, modes, rope=None, post_scale=1.0):
    n, d = x2.shape
    tm, tn = ROW_TILE, COL_TILE
    ncols = len(modes) * tn
    out_tiles, n_out = [], 0
    for m in modes:
        out_tiles.append(None if m == "hold" else n_out)
        n_out += m != "hold"
    assert modes.count("hold") == modes.count("silu_gate") <= 1
    assert "hold" not in modes or modes.index("hold") < modes.index("silu_gate")
    use_rope = rope is not None
    normalising = [jj for jj, m in enumerate(modes) if m in ("norm", "norm_scaled", "rope", "rope_scaled")]
    assert len(gains) == len(normalising)
    gain_of = tuple(normalising.index(jj) if jj in normalising else None for jj in range(len(modes)))
    in_specs = [
        pl.BlockSpec((tm, d), lambda i: (i, 0)),
        pl.BlockSpec((1, d), lambda i: (0, 0)),
        pl.BlockSpec((None, 1, d), lambda i: ((i * tm) // seq, 0, 0)),
        pl.BlockSpec((None, 1, d), lambda i: ((i * tm) // seq, 0, 0)),
        pl.BlockSpec(memory_space=pl.ANY),
    ] + [pl.BlockSpec(gn.shape, lambda i: (0, 0)) for gn in gains]
    args = [x2, norm_g.reshape(1, d), scale, shift, w_in, *gains]
    if use_rope:
        vec = pl.BlockSpec((1, HEAD_DIM), lambda i: (0, 0))
        pos, freq, sign = rope
        in_specs += [pl.BlockSpec((None, tm // LANES, LANES), lambda i: (i, 0, 0)), vec, vec]
        args += [pos.reshape(n // tm, tm // LANES, LANES), freq, sign]
    return pl.pallas_call(
        functools.partial(_inproj_kernel, modes=tuple(modes), out_tiles=tuple(out_tiles),
                          gain_of=gain_of, n_gains=len(gains), post_scale=post_scale,
                          use_rope=use_rope),
        grid=(n // tm,),
        in_specs=in_specs,
        out_specs=pl.BlockSpec((tm, n_out * tn), lambda i: (i, 0)),
        out_shape=jax.ShapeDtypeStruct((n, n_out * tn), BF16),
        scratch_shapes=[pltpu.VMEM((tm, d), BF16),
                        pltpu.VMEM((d, ncols), BF16),
                        pltpu.VMEM((2, d, tn), F32),
                        pltpu.SemaphoreType.DMA((2,)),
                        pltpu.VMEM((tm, tn), F32)],
        compiler_params=pltpu.CompilerParams(
            dimension_semantics=("arbitrary",), vmem_limit_bytes=VMEM_LIMIT),
        name="in_projection",
    )(*args)


def _attn_kernel(q_ref, k_ref, v_ref, o_ref, u_scr, *, seq):
    hb, tq, win, nsub = ATT_HEADS_PER_STEP, ATT_Q, ATT_WIN, ATT_SUB
    chains = nsub * hb

    u_scr[...] = jnp.where(lax.broadcasted_iota(jnp.int32, (ATT_NEW, ATT_NEW), 0)
                           > lax.broadcasted_iota(jnp.int32, (ATT_NEW, ATT_NEW), 1),
                           1.0, 0.0).astype(BF16)
    old = win - ATT_NEW
    key_idx = lax.broadcasted_iota(jnp.int32, (tq, win), 1)
    row_idx = lax.broadcasted_iota(jnp.int32, (tq, win), 0)
    tail = win - LANES
    tail_ok = (lax.broadcasted_iota(jnp.int32, (tq, LANES), 1) + tail
               < lax.broadcasted_iota(jnp.int32, (tq, LANES), 0) + (win - tq))

    def head_cols(h):
        return slice(h * HEAD_DIM, (h + 1) * HEAD_DIM)

    def tile_logits(rows0, keys0):
        return [_dot_nt(q_ref[pl.ds(rows0[a], tq), head_cols(h)],
                        k_ref[pl.ds(keys0[a], win), head_cols(h)])
                for a in range(nsub) for h in range(hb)]

    def tile_weights(logits, rows0, keys0, limits, totals):
        log_beta, terms = [], []
        for a in range(nsub):
            if limits is not None:
                ok = key_idx < (jnp.minimum(rows0[a] + row_idx, limits[a]) - keys0[a])
            for h in range(hb):
                z = logits[a * hb + h]
                if limits is None:
                    z = jnp.concatenate(
                        [z[:, :tail], jnp.where(tail_ok, z[:, tail:], ATT_EXCLUDED)], axis=1)
                else:
                    z = jnp.where(ok, z, ATT_EXCLUDED)
                nz = -z
                lsm = (jnp.minimum(nz, 0.0)
                       - jnp.log(1.0 + jnp.exp2(jnp.minimum(z, nz))) * LOG2E)
                log_beta.append(z + lsm)
                terms.append(lsm)
        sums_new = _dot(jnp.concatenate([t[:, old:].astype(BF16) for t in terms], axis=0),
                        u_scr[...])
        sums_old = _dot(jnp.concatenate([t[:, :old].astype(BF16) for t in terms], axis=0),
                        u_scr[0:old, 0:old])
        weights, tile_totals = [], []
        for c in range(chains):
            newer = sums_new[c * tq:(c + 1) * tq]
            older = sums_old[c * tq:(c + 1) * tq]
            total_new = newer[:, 0:1] + terms[c][:, old:old + 1]
            expo = log_beta[c] + jnp.concatenate([older + total_new, newer], axis=1)
            if totals is not None:
                expo = expo + totals[c]
            weights.append(jnp.exp2(expo).astype(BF16))
            tile_totals.append(total_new + older[:, 0:1] + terms[c][:, 0:1])
        return weights, tile_totals

    def tile_values(weights, keys0):
        return [_dot(weights[a * hb + h], v_ref[pl.ds(keys0[a], win), head_cols(h)])
                for a in range(nsub) for h in range(hb)]

    def slowest(totals):
        m = totals[0]
        for t in totals[1:]:
            m = jnp.maximum(m, t)
        return jnp.max(m)

    def qblock(i, carry, *, near_start):
        r0 = i * (tq * nsub)
        rows0 = [pl.multiple_of(r0 + a * tq, tq) for a in range(nsub)]
        if near_start:
            starts = [jnp.maximum(r - (win - tq), 0) for r in rows0]
        else:
            starts = [r - (win - tq) for r in rows0]
        keys0 = [pl.multiple_of(s, tq) for s in starts]
        weights, totals = tile_weights(tile_logits(rows0, keys0), rows0, keys0,
                                       [ATT_NO_LIMIT] * nsub if near_start else None, None)
        accs = tile_values(weights, keys0)

        def more(st):
            m, worst, _, _ = st
            return (starts[-1] - m * win > 0) & (worst > ATT_SKIP_BELOW)

        def older(st):
            m, _, totals, accs = st
            limits = [s - m * win for s in starts]
            keys0 = [pl.multiple_of(jnp.maximum(lim - win, 0), tq) for lim in limits]
            weights, tile_totals = tile_weights(tile_logits(rows0, keys0), rows0, keys0,
                                                limits, totals)
            outs = tile_values(weights, keys0)
            totals = tuple(t + d for t, d in zip(totals, tile_totals))
            accs = tuple(acc + o for acc, o in zip(accs, outs))
            return (m + 1, slowest(totals), totals, accs)

        _, _, _, accs = lax.while_loop(
            more, older, (jnp.int32(0), slowest(totals), tuple(totals), tuple(accs)))
        for a in range(nsub):
            for h in range(hb):
                o_ref[pl.ds(rows0[a], tq), head_cols(h)] = accs[a * hb + h].astype(BF16)
        return carry

    n_blocks = seq // (tq * nsub)
    n_clipped = min(-(-(win - tq) // (tq * nsub)), n_blocks)
    lax.fori_loop(0, n_clipped, functools.partial(qblock, near_start=True), 0)
    lax.fori_loop(n_clipped, n_blocks, functools.partial(qblock, near_start=False), 0)


def _stick_breaking(p, batch, seq, q_col, k_col, v_col):
    n = p.shape[0]
    width = ATT_HEADS_PER_STEP * HEAD_DIM
    n_hg = N_HEADS // ATT_HEADS_PER_STEP

    def spec(col):
        return pl.BlockSpec((seq, width), lambda b, hg: (b, col // width + hg))

    return pl.pallas_call(
        functools.partial(_attn_kernel, seq=seq),
        grid=(batch, n_hg),
        in_specs=[spec(q_col), spec(k_col), spec(v_col)],
        out_specs=pl.BlockSpec((seq, width), lambda b, hg: (b, hg)),
        out_shape=jax.ShapeDtypeStruct((n, N_HEADS * HEAD_DIM), BF16),
        scratch_shapes=[pltpu.VMEM((ATT_NEW, ATT_NEW), BF16)],
        compiler_params=pltpu.CompilerParams(
            dimension_semantics=("parallel", "parallel"), vmem_limit_bytes=VMEM_LIMIT),
        name="stick_breaking",
    )(p, p, p)


def _out_even_kernel(x_ref, gate_ref, ug_ref, vg_ref, zb_ref, bo_ref, ws_ref, bias_ref, wout_ref,
                     o_ref, y_scr):
    tm = x_ref.shape[0]
    a_width = bo_ref.shape[1]
    row = lax.broadcasted_iota(jnp.int32, (BLOCK, BLOCK), 0)
    col = lax.broadcasted_iota(jnp.int32, (BLOCK, BLOCK), 1)
    for g in range(a_width // HEAD_DIM):
        cols = slice(g * HEAD_DIM, (g + 1) * HEAD_DIM)
        wg = jnp.where(col <= row, ws_ref[g], 0.0).astype(BF16)
        for c in range(tm // BLOCK):
            rows = slice(c * BLOCK, (c + 1) * BLOCK)
            mixed = _dot(wg, vg_ref[rows, cols]) + bias_ref[:, cols]
            y_scr[rows, cols] = (ug_ref[rows, cols].astype(F32) * mixed).astype(BF16)
    y_scr[:, a_width:] = (bo_ref[...].astype(F32) * _silu(zb_ref[...].astype(F32))).astype(BF16)
    o_ref[...] = x_ref[...] + gate_ref[...] * _dot(y_scr[...], wout_ref[...])


def _out_even(x2, seq, gate, p, b_out, a_ws, bias_full, w_out_bf):
    n, d = x2.shape
    tm = OUT_EVEN_ROWS
    aw = b_out.shape[1]
    row = lambda c: pl.BlockSpec((tm, aw), lambda i: (i, c))
    const = lambda a: pl.BlockSpec(a.shape, lambda i: (0,) * a.ndim, pipeline_mode=pl.Buffered(1))
    return pl.pallas_call(
        _out_even_kernel,
        grid=(n // tm,),
        in_specs=[pl.BlockSpec((tm, d), lambda i: (i, 0)),
                  pl.BlockSpec((None, 1, d), lambda i: ((i * tm) // seq, 0, 0)),
                  row(4), row(0), row(5),
                  row(0),
                  const(a_ws), const(bias_full), const(w_out_bf)],
        out_specs=pl.BlockSpec((tm, d), lambda i: (i, 0)),
        out_shape=jax.ShapeDtypeStruct((n, d), F32),
        scratch_shapes=[pltpu.VMEM((tm, 2 * aw), BF16)],
        compiler_params=pltpu.CompilerParams(
            dimension_semantics=("parallel",), vmem_limit_bytes=VMEM_LIMIT),
        name="out_even",
    )(x2, gate, p, p, p, b_out, a_ws, bias_full, w_out_bf)


def _rope_operands(positions):
    half = HEAD_DIM // 2
    inv_freq = ROPE_BASE ** (-jnp.arange(half, dtype=F32) / half)
    freq = jnp.concatenate([inv_freq, inv_freq]).reshape(1, HEAD_DIM)
    sign = jnp.concatenate([-jnp.ones((half,), F32), jnp.ones((half,), F32)]).reshape(1, HEAD_DIM)
    return positions.reshape(-1, LANES).astype(F32), freq, sign


def _retention_tables():
    chunk = RET_CHUNK
    f32 = np.float32
    log_gamma = np.log1p(-np.exp2(-5.0 - np.arange(N_HEADS, dtype=f32))).astype(f32)
    idx = np.arange(chunk, dtype=f32)
    diff = idx[:, None] - idx[None, :]
    intra = np.where(diff >= 0, np.exp(log_gamma[:, None, None] * np.maximum(diff, 0.0)), 0.0)
    q_decay = np.exp(log_gamma[:, None] * (idx + 1.0))
    k_decay = np.exp(log_gamma[:, None] * (chunk - 1.0 - idx))
    chunk_decay = np.exp(log_gamma * chunk)
    tables = (intra,
              np.broadcast_to(q_decay[:, :, None], (N_HEADS, chunk, HEAD_DIM)),
              np.broadcast_to(k_decay[:, :, None], (N_HEADS, chunk, HEAD_DIM)),
              np.broadcast_to(chunk_decay[:, None, None], (N_HEADS, HEAD_DIM, HEAD_DIM)))
    return tuple(jnp.asarray(np.ascontiguousarray(t, dtype=f32)) for t in tables)


def _retention_rows(q_ref, k_ref, v_ref, idec_ref, qdec_ref, kdec_ref, cdec_ref, st_scr, emit):
    chunk = RET_CHUNK
    n_chunks = q_ref.shape[0] // chunk
    cols = [slice(h * HEAD_DIM, (h + 1) * HEAD_DIM) for h in range(N_HEADS)]
    rows = [slice(c * chunk, (c + 1) * chunk) for c in range(n_chunks)]

    kv = {}
    for c in range(n_chunks):
        for h in range(N_HEADS):
            k_decayed = (k_ref[rows[c], cols[h]].astype(F32) * kdec_ref[h]).astype(BF16)
            kv[h, c] = _dot_tn(k_decayed, v_ref[rows[c], cols[h]])
    state = {}
    for h in range(N_HEADS):
        s = st_scr[h]
        for c in range(n_chunks):
            state[h, c] = s.astype(BF16)
            s = s * cdec_ref[h] + kv[h, c]
        st_scr[h] = s
    for c in range(n_chunks):
        for h in range(N_HEADS):
            q = q_ref[rows[c], cols[h]]
            scores = _dot_nt(q, k_ref[rows[c], cols[h]]) * idec_ref[h]
            out = (_dot(scores.astype(BF16), v_ref[rows[c], cols[h]])
                   + _dot(q, state[h, c]) * qdec_ref[h])
            ms = jnp.mean(out * out, axis=-1, keepdims=True)
            emit(rows[c], h, out * lax.rsqrt(ms + EPS))


def _pooled_rows(pc_ref, halo_ref, cw_ref, cs_ref, ext_scr, seq_row0, emit):
    tm = pc_ref.shape[0]
    ext_scr[0:POOL_HALO, :] = jnp.where(seq_row0 == 0, jnp.zeros_like(halo_ref), halo_ref[...])
    ext_scr[POOL_HALO:, :] = pc_ref[...]
    src_rows = BLOCK + POOL_HALO
    lag = (lax.broadcasted_iota(jnp.int32, (BLOCK, src_rows), 0) + POOL_HALO
           - lax.broadcasted_iota(jnp.int32, (BLOCK, src_rows), 1))
    t_seq = seq_row0 + lax.broadcasted_iota(jnp.int32, (BLOCK, POOL_GROUP_DIM), 0)
    for gi, win in enumerate(POOL_WINDOWS):
        cols = slice(gi * POOL_GROUP_DIM, (gi + 1) * POOL_GROUP_DIM)
        band = jnp.where((lag >= 0) & (lag < win), 1.0 / win, 0.0).astype(BF16)
        short = jnp.where(t_seq + 1 < win, win / (t_seq + 1).astype(F32), 1.0)
        pooled = []
        for c in range(tm // BLOCK):
            mean = _dot(band, ext_scr[c * BLOCK:c * BLOCK + src_rows, cols])
            if c == 0:
                mean = mean * short
            cur = pc_ref[c * BLOCK:(c + 1) * BLOCK, cols].astype(F32)
            pooled.append((mean - cur).astype(BF16))
        emit(gi, _dot(jnp.concatenate(pooled, axis=0), cw_ref[gi]) * cs_ref[:, cols])


def _odd_mixers_kernel(p_ref, idec_ref, qdec_ref, kdec_ref, cdec_ref,
                       x_ref, gate_ref, halo_ref, cw_ref, cs_ref, wout_ref,
                       o_ref, st_scr, ext_scr, y_scr):
    tm = x_ref.shape[0]
    c_width = halo_ref.shape[1]
    pc_ref, q_ref, k_ref, v_ref, za_ref, zb_ref = (
        p_ref.at[:, pl.ds(j * c_width, c_width)] for j in range(6))

    @pl.when(pl.program_id(1) == 0)
    def _():
        st_scr[...] = jnp.zeros_like(st_scr)

    def gated_retention(rows, h, out):
        cols = slice(h * HEAD_DIM, (h + 1) * HEAD_DIM)
        ycols = slice(c_width + h * HEAD_DIM, c_width + (h + 1) * HEAD_DIM)
        y_scr[rows, ycols] = (out * _silu(zb_ref[rows, cols].astype(F32))).astype(BF16)

    def gated_pool(gi, mixed):
        cols = slice(gi * POOL_GROUP_DIM, (gi + 1) * POOL_GROUP_DIM)
        y_scr[:, cols] = (mixed * _silu(za_ref[:, cols].astype(F32))).astype(BF16)

    _retention_rows(q_ref, k_ref, v_ref, idec_ref, qdec_ref, kdec_ref, cdec_ref, st_scr,
                    gated_retention)
    _pooled_rows(pc_ref, halo_ref, cw_ref, cs_ref, ext_scr, pl.program_id(1) * tm, gated_pool)
    o_ref[...] = x_ref[...] + gate_ref[...] * _dot(y_scr[...], wout_ref[...])


def _odd_mixers(x2, batch, seq, gate, p, c_w_bf, c_scale, w_out_bf):
    n, d = x2.shape
    tm = ROW_TILE
    cw = N_HEADS * HEAD_DIM
    steps = seq // tm
    tables = _retention_tables()
    tabs = [pl.BlockSpec(t.shape, lambda b, s: (0, 0, 0)) for t in tables]
    halo_blocks = tm // POOL_HALO
    halo = pl.BlockSpec(
        (POOL_HALO, cw), lambda b, s: (jnp.maximum((b * steps + s) * halo_blocks - 1, 0), 0))
    return pl.pallas_call(
        _odd_mixers_kernel,
        grid=(batch, steps),
        in_specs=[pl.BlockSpec((tm, p.shape[1]), lambda b, s: (b * steps + s, 0))] + tabs + [
            pl.BlockSpec((tm, d), lambda b, s: (b * steps + s, 0)),
            pl.BlockSpec((None, 1, d), lambda b, s: (b, 0, 0)),
            halo,
            pl.BlockSpec(c_w_bf.shape, lambda b, s: (0, 0, 0)),
            pl.BlockSpec((1, cw), lambda b, s: (0, 0)),
            pl.BlockSpec(w_out_bf.shape, lambda b, s: (0, 0))],
        out_specs=pl.BlockSpec((tm, d), lambda b, s: (b * steps + s, 0)),
        out_shape=jax.ShapeDtypeStruct((n, d), F32),
        scratch_shapes=[pltpu.VMEM((N_HEADS, HEAD_DIM, HEAD_DIM), F32),
                        pltpu.VMEM((POOL_HALO + tm, cw), BF16),
                        pltpu.VMEM((tm, 2 * cw), BF16)],
        compiler_params=pltpu.CompilerParams(
            dimension_semantics=("parallel", "arbitrary"), vmem_limit_bytes=VMEM_LIMIT),
        name="odd_mixers",
    )(p, *tables, x2, gate, p, c_w_bf, c_scale.reshape(1, cw), w_out_bf)


def _even_layer(x2, batch, seq, mod, norm_g, w_in, a_vnorm_g, a_ws, a_bs,
                b_qnorm_g, b_knorm_g, w_out):
    d = x2.shape[1]
    shift, scale, gate = _split_modulation(mod)
    gains = [a_vnorm_g, b_qnorm_g.reshape(1, -1), b_knorm_g.reshape(1, -1)]
    modes = ("hold", "norm", "norm_scaled", "norm", "raw", "silu_gate", "raw")
    p = _in_projection(x2, seq, norm_g, scale, shift, w_in, gains, modes,
                       post_scale=LOG2E * HEAD_DIM ** -0.5)
    b_out = _stick_breaking(p, batch, seq, q_col=d, k_col=2 * d, v_col=3 * d)
    bias_full = jnp.repeat(a_bs.T, HEAD_DIM, axis=1)
    return _out_even(x2, seq, gate, p, b_out, a_ws, bias_full, w_out.astype(BF16))


def _odd_layer(x2, batch, seq, mod, positions, norm_g, w_in, c_w, c_scale,
               d_qnorm_g, d_knorm_g, w_out):
    shift, scale, gate = _split_modulation(mod)
    gains = [d_qnorm_g.reshape(1, -1), d_knorm_g.reshape(1, -1)]
    modes = ("raw", "rope", "rope_scaled", "raw", "raw", "raw")
    p = _in_projection(x2, seq, norm_g, scale, shift, w_in, gains, modes,
                       rope=_rope_operands(positions), post_scale=HEAD_DIM ** -0.5)
    return _odd_mixers(x2, batch, seq, gate, p, c_w.astype(BF16), c_scale, w_out.astype(BF16))


def kernel(x, c, positions, even_norm_g, even_w_mod, even_b_mod, even_w_in, even_a_vnorm_g, even_a_ws, even_a_bs, even_b_qnorm_g, even_b_knorm_g, even_w_out, odd_norm_g, odd_w_mod, odd_b_mod, odd_w_in, odd_c_w, odd_c_scale, odd_d_qnorm_g, odd_d_knorm_g, odd_w_out):
    batch, seq, d = x.shape
    depth = even_norm_g.shape[0] + odd_norm_g.shape[0]
    x2 = x.reshape(batch * seq, d)
    n_even = even_norm_g.shape[0]
    mods = _modulations(c, even_w_mod, even_b_mod, odd_w_mod, odd_b_mod)
    for layer in range(depth):
        i = layer // 2
        if layer % 2 == 0:
            x2 = _even_layer(x2, batch, seq, mods[i], even_norm_g[i],
                             even_w_in[i], even_a_vnorm_g[i], even_a_ws[i], even_a_bs[i],
                             even_b_qnorm_g[i], even_b_knorm_g[i], even_w_out[i])
        else:
            x2 = _odd_layer(x2, batch, seq, mods[n_even + i], positions, odd_norm_g[i],
                            odd_w_in[i], odd_c_w[i], odd_c_scale[i], odd_d_qnorm_g[i],
                            odd_d_knorm_g[i], odd_w_out[i])
    return x2.reshape(batch, seq, d)
```

```python
import functools

import jax
import jax.numpy as jnp
import numpy as np
from jax import lax
from jax.experimental import pallas as pl
from jax.experimental.pallas import tpu as pltpu

F32 = jnp.float32
BF16 = jnp.bfloat16

EPS = 1e-6
ROPE_BASE = 10000.0
BLOCK = 128
HEAD_DIM = 128
N_HEADS = 8
POOL_WINDOWS = (2, 4, 8, 16)
POOL_GROUP_DIM = 256
POOL_HALO = 16

MOD_ROWS = 8

ROW_TILE = 512
OUT_EVEN_ROWS = 1024
COL_TILE = 1024
RET_CHUNK = 128

ATT_Q = 128
ATT_WIN = 384
ATT_NEW = 256
ATT_EXCLUDED = -1e30
LANES = 128
ATT_SUB = 2
ATT_HEADS_PER_STEP = 4
ATT_SKIP_BELOW = -151.0
LOG2E = 1.4426950408889634
ATT_NO_LIMIT = 1 << 30

VMEM_LIMIT = 56 * 1024 * 1024


def _silu(v):
    return v * jax.nn.sigmoid(v)


def _dot(a, b):
    return jnp.dot(a, b, preferred_element_type=F32)


def _dot_nt(a, b):
    return lax.dot_general(a, b, (((1,), (1,)), ((), ())), preferred_element_type=F32)


def _dot_tn(a, b):
    return lax.dot_general(a, b, (((0,), (0,)), ((), ())), preferred_element_type=F32)


def _group_rms(a, gain):
    ms = jnp.mean(a * a, axis=-1, keepdims=True)
    return a * lax.rsqrt(ms + EPS) * gain


def _mod_kernel(c_ref, we_ref, be_ref, wo_ref, bo_ref, o_ref, *, n_even):
    a = _silu(c_ref[...]).astype(BF16)
    layer = pl.program_id(0)

    @pl.when(layer < n_even)
    def _():
        o_ref[...] = _dot(a, we_ref[...].astype(BF16)) + be_ref[...]

    @pl.when(layer >= n_even)
    def _():
        o_ref[...] = _dot(a, wo_ref[...].astype(BF16)) + bo_ref[...]


def _modulations(c, even_w, even_b, odd_w, odd_b):
    bsz, d = c.shape
    n_even, n_odd = even_w.shape[0], odd_w.shape[0]
    c8 = jnp.pad(c, ((0, MOD_ROWS - bsz), (0, 0)))
    even_at = lambda l, j: (jnp.minimum(l, n_even - 1), 0, jnp.where(l < n_even, j, 2))
    odd_at = lambda l, j: (jnp.maximum(l - n_even, 0), 0, jnp.where(l < n_even, 0, j))
    m = pl.pallas_call(
        functools.partial(_mod_kernel, n_even=n_even),
        grid=(n_even + n_odd, 3),
        in_specs=[pl.BlockSpec((MOD_ROWS, d), lambda l, j: (0, 0)),
                  pl.BlockSpec((None, d, d), even_at),
                  pl.BlockSpec((None, 1, d), even_at),
                  pl.BlockSpec((None, d, d), odd_at),
                  pl.BlockSpec((None, 1, d), odd_at)],
        out_specs=pl.BlockSpec((None, MOD_ROWS, d), lambda l, j: (l, 0, j)),
        out_shape=jax.ShapeDtypeStruct((n_even + n_odd, MOD_ROWS, 3 * d), F32),
        name="modulation",
    )(c8, even_w, even_b.reshape(n_even, 1, 3 * d), odd_w, odd_b.reshape(n_odd, 1, 3 * d))
    return m[:, :bsz]


def _split_modulation(m):
    d = m.shape[1] // 3
    return m[:, None, :d], m[:, None, d:2 * d], m[:, None, 2 * d:]


def _inproj_kernel(*refs, modes, out_tiles, gain_of, n_gains, post_scale, use_rope):
    *refs, h_scr, w_ref, stage, sem, held = refs
    x_ref, g_ref, sc_ref, sh_ref, w_hbm = refs[:5]
    gain_refs = refs[5:5 + n_gains]
    o_ref = refs[-1]
    tn = COL_TILE

    def weight_copy(j):
        return pltpu.make_async_copy(w_hbm.at[:, pl.ds(j * tn, tn)], stage.at[j % 2], sem.at[j % 2])

    @pl.when(pl.program_id(0) == 0)
    def _():
        weight_copy(0).start()
        for j in range(len(modes)):
            if j + 1 < len(modes):
                weight_copy(j + 1).start()
            weight_copy(j).wait()
            w_ref[:, j * tn:(j + 1) * tn] = stage[j % 2].astype(BF16)

    if use_rope:
        pos_ref, freq_ref, sign_ref = refs[5 + n_gains:8 + n_gains]
        on_diag = (lax.broadcasted_iota(jnp.int32, (LANES, LANES), 0)
                   == lax.broadcasted_iota(jnp.int32, (LANES, LANES), 1))
        angles = []
        for a in range(pos_ref.shape[0]):
            spread = jnp.where(on_diag, jnp.broadcast_to(pos_ref[a:a + 1, :], (LANES, LANES)), 0.0)
            angles.append(jnp.sum(spread, axis=1, keepdims=True) * freq_ref[...])
        ang = jnp.concatenate(angles, axis=0)
        cos_t = jnp.cos(ang)
        sin_t = jnp.sin(ang) * sign_ref[...]
    x = x_ref[...]
    ms = jnp.mean(x * x, axis=-1, keepdims=True)
    y = x * lax.rsqrt(ms + EPS) * g_ref[...]
    h_scr[...] = (y * (1.0 + sc_ref[...]) + sh_ref[...]).astype(BF16)

    cross_lane = ("norm", "norm_scaled", "rope", "rope_scaled")
    heavy = [jj for jj, m in enumerate(modes) if m in cross_lane]
    plain = [jj for jj, m in enumerate(modes) if m not in cross_lane]
    order = []
    while heavy or plain:
        order += heavy[:1] + plain[:1]
        heavy, plain = heavy[1:], plain[1:]
    for jj in order:
        mode = modes[jj]
        out0 = None if out_tiles[jj] is None else out_tiles[jj] * tn
        acc = _dot(h_scr[...], w_ref[:, jj * tn:(jj + 1) * tn])
        if mode == "hold":
            held[...] = acc
            continue
        if mode == "raw":
            o_ref[:, out0:out0 + tn] = acc.astype(BF16)
            continue
        if mode == "silu_gate":
            o_ref[:, out0:out0 + tn] = (held[...] * _silu(acc)).astype(BF16)
            continue
        gain_ref = gain_refs[gain_of[jj]]
        for g in range(tn // HEAD_DIM):
            row = g if gain_ref.shape[0] > 1 else 0
            t = _group_rms(acc[:, g * HEAD_DIM:(g + 1) * HEAD_DIM], gain_ref[row:row + 1, :])
            if mode in ("rope", "rope_scaled"):
                t = t * cos_t + pltpu.roll(t, HEAD_DIM // 2, axis=1) * sin_t
            if mode in ("norm_scaled", "rope_scaled"):
                t = t * post_scale
            o_ref[:, out0 + g * HEAD_DIM:out0 + (g + 1) * HEAD_DIM] = t.astype(BF16)


def _in_projection(x2, seq, norm_g, scale, shift, w_in, gains, modes, rope=None, post_scale=1.0):
    n, d = x2.shape
    tm, tn = ROW_TILE, COL_TILE
    ncols = len(modes) * tn
    out_tiles, n_out = [], 0
    for m in modes:
        out_tiles.append(None if m == "hold" else n_out)
        n_out += m != "hold"
    assert modes.count("hold") == modes.count("silu_gate") <= 1
    assert "hold" not in modes or modes.index("hold") < modes.index("silu_gate")
    use_rope = rope is not None
    normalising = [jj for jj, m in enumerate(modes) if m in ("norm", "norm_scaled", "rope", "rope_scaled")]
    assert len(gains) == len(normalising)
    gain_of = tuple(normalising.index(jj) if jj in normalising else None for jj in range(len(modes)))
    in_specs = [
        pl.BlockSpec((tm, d), lambda i: (i, 0)),
        pl.BlockSpec((1, d), lambda i: (0, 0)),
        pl.BlockSpec((None, 1, d), lambda i: ((i * tm) // seq, 0, 0)),
        pl.BlockSpec((None, 1, d), lambda i: ((i * tm) // seq, 0, 0)),
        pl.BlockSpec(memory_space=pl.ANY),
    ] + [pl.BlockSpec(gn.shape, lambda i: (0, 0)) for gn in gains]
    args = [x2, norm_g.reshape(1, d), scale, shift, w_in, *gains]
    if use_rope:
        vec = pl.BlockSpec((1, HEAD_DIM), lambda i: (0, 0))
        pos, freq, sign = rope
        in_specs += [pl.BlockSpec((None, tm // LANES, LANES), lambda i: (i, 0, 0)), vec, vec]
        args += [pos.reshape(n // tm, tm // LANES, LANES), freq, sign]
    return pl.pallas_call(
        functools.partial(_inproj_kernel, modes=tuple(modes), out_tiles=tuple(out_tiles),
                          gain_of=gain_of, n_gains=len(gains), post_scale=post_scale,
                          use_rope=use_rope),
        grid=(n // tm,),
        in_specs=in_specs,
        out_specs=pl.BlockSpec((tm, n_out * tn), lambda i: (i, 0)),
        out_shape=jax.ShapeDtypeStruct((n, n_out * tn), BF16),
        scratch_shapes=[pltpu.VMEM((tm, d), BF16),
                        pltpu.VMEM((d, ncols), BF16),
                        pltpu.VMEM((2, d, tn), F32),
                        pltpu.SemaphoreType.DMA((2,)),
                        pltpu.VMEM((tm, tn), F32)],
        compiler_params=pltpu.CompilerParams(
            dimension_semantics=("arbitrary",), vmem_limit_bytes=VMEM_LIMIT),
        name="in_projection",
    )(*args)


def _attn_kernel(q_ref, k_ref, v_ref, o_ref, u_scr, *, seq):
    hb, tq, win, nsub = ATT_HEADS_PER_STEP, ATT_Q, ATT_WIN, ATT_SUB
    chains = nsub * hb

    u_scr[...] = jnp.where(lax.broadcasted_iota(jnp.int32, (ATT_NEW, ATT_NEW), 0)
                           > lax.broadcasted_iota(jnp.int32, (ATT_NEW, ATT_NEW), 1),
                           1.0, 0.0).astype(BF16)
    old = win - ATT_NEW
    key_idx = lax.broadcasted_iota(jnp.int32, (tq, win), 1)
    row_idx = lax.broadcasted_iota(jnp.int32, (tq, win), 0)
    tail = win - LANES
    tail_ok = (lax.broadcasted_iota(jnp.int32, (tq, LANES), 1) + tail
               < lax.broadcasted_iota(jnp.int32, (tq, LANES), 0) + (win - tq))

    def head_cols(h):
        return slice(h * HEAD_DIM, (h + 1) * HEAD_DIM)

    def tile_logits(rows0, keys0):
        return [_dot_nt(q_ref[pl.ds(rows0[a], tq), head_cols(h)],
                        k_ref[pl.ds(keys0[a], win), head_cols(h)])
                for a in range(nsub) for h in range(hb)]

    def tile_weights(logits, rows0, keys0, limits, totals):
        log_beta, terms = [], []
        for a in range(nsub):
            if limits is not None:
                ok = key_idx < (jnp.minimum(rows0[a] + row_idx, limits[a]) - keys0[a])
            for h in range(hb):
                z = logits[a * hb + h]
                if limits is None:
                    z = jnp.concatenate(
                        [z[:, :tail], jnp.where(tail_ok, z[:, tail:], ATT_EXCLUDED)], axis=1)
                else:
                    z = jnp.where(ok, z, ATT_EXCLUDED)
                nz = -z
                lsm = (jnp.minimum(nz, 0.0)
                       - jnp.log(1.0 + jnp.exp2(jnp.minimum(z, nz))) * LOG2E)
                log_beta.append(z + lsm)
                terms.append(lsm)
        sums_new = _dot(jnp.concatenate([t[:, old:].astype(BF16) for t in terms], axis=0),
                        u_scr[...])
        sums_old = _dot(jnp.concatenate([t[:, :old].astype(BF16) for t in terms], axis=0),
                        u_scr[0:old, 0:old])
        weights, tile_totals = [], []
        for c in range(chains):
            newer = sums_new[c * tq:(c + 1) * tq]
            older = sums_old[c * tq:(c + 1) * tq]
            total_new = newer[:, 0:1] + terms[c][:, old:old + 1]
            expo = log_beta[c] + jnp.concatenate([older + total_new, newer], axis=1)
            if totals is not None:
                expo = expo + totals[c]
            weights.append(jnp.exp2(expo).astype(BF16))
            tile_totals.append(total_new + older[:, 0:1] + terms[c][:, 0:1])
        return weights, tile_totals

    def tile_values(weights, keys0):
        return [_dot(weights[a * hb + h], v_ref[pl.ds(keys0[a], win), head_cols(h)])
                for a in range(nsub) for h in range(hb)]

    def slowest(totals):
        m = totals[0]
        for t in totals[1:]:
            m = jnp.maximum(m, t)
        return jnp.max(m)

    def qblock(i, carry, *, near_start):
        r0 = i * (tq * nsub)
        rows0 = [pl.multiple_of(r0 + a * tq, tq) for a in range(nsub)]
        if near_start:
            starts = [jnp.maximum(r - (win - tq), 0) for r in rows0]
        else:
            starts = [r - (win - tq) for r in rows0]
        keys0 = [pl.multiple_of(s, tq) for s in starts]
        weights, totals = tile_weights(tile_logits(rows0, keys0), rows0, keys0,
                                       [ATT_NO_LIMIT] * nsub if near_start else None, None)
        accs = tile_values(weights, keys0)

        def more(st):
            m, worst, _, _ = st
            return (starts[-1] - m * win > 0) & (worst > ATT_SKIP_BELOW)

        def older(st):
            m, _, totals, accs = st
            limits = [s - m * win for s in starts]
            keys0 = [pl.multiple_of(jnp.maximum(lim - win, 0), tq) for lim in limits]
            weights, tile_totals = tile_weights(tile_logits(rows0, keys0), rows0, keys0,
                                                limits, totals)
            outs = tile_values(weights, keys0)
            totals = tuple(t + d for t, d in zip(totals, tile_totals))
            accs = tuple(acc + o for acc, o in zip(accs, outs))
            return (m + 1, slowest(totals), totals, accs)

        _, _, _, accs = lax.while_loop(
            more, older, (jnp.int32(0), slowest(totals), tuple(totals), tuple(accs)))
        for a in range(nsub):
            for h in range(hb):
                o_ref[pl.ds(rows0[a], tq), head_cols(h)] = accs[a * hb + h].astype(BF16)
        return carry

    n_blocks = seq // (tq * nsub)
    n_clipped = min(-(-(win - tq) // (tq * nsub)), n_blocks)
    lax.fori_loop(0, n_clipped, functools.partial(qblock, near_start=True), 0)
    lax.fori_loop(n_clipped, n_blocks, functools.partial(qblock, near_start=False), 0)


def _stick_breaking(p, batch, seq, q_col, k_col, v_col):
    n = p.shape[0]
    width = ATT_HEADS_PER_STEP * HEAD_DIM
    n_hg = N_HEADS // ATT_HEADS_PER_STEP

    def spec(col):
        return pl.BlockSpec((seq, width), lambda b, hg: (b, col // width + hg))

    return pl.pallas_call(
        functools.partial(_attn_kernel, seq=seq),
        grid=(batch, n_hg),
        in_specs=[spec(q_col), spec(k_col), spec(v_col)],
        out_specs=pl.BlockSpec((seq, width), lambda b, hg: (b, hg)),
        out_shape=jax.ShapeDtypeStruct((n, N_HEADS * HEAD_DIM), BF16),
        scratch_shapes=[pltpu.VMEM((ATT_NEW, ATT_NEW), BF16)],
        compiler_params=pltpu.CompilerParams(
            dimension_semantics=("parallel", "parallel"), vmem_limit_bytes=VMEM_LIMIT),
        name="stick_breaking",
    )(p, p, p)


def _out_even_kernel(x_ref, gate_ref, ug_ref, vg_ref, zb_ref, bo_ref, ws_ref, bias_ref, wout_ref,
                     o_ref, y_scr):
    tm = x_ref.shape[0]
    a_width = bo_ref.shape[1]
    row = lax.broadcasted_iota(jnp.int32, (BLOCK, BLOCK), 0)
    col = lax.broadcasted_iota(jnp.int32, (BLOCK, BLOCK), 1)
    for g in range(a_width // HEAD_DIM):
        cols = slice(g * HEAD_DIM, (g + 1) * HEAD_DIM)
        wg = jnp.where(col <= row, ws_ref[g], 0.0).astype(BF16)
        for c in range(tm // BLOCK):
            rows = slice(c * BLOCK, (c + 1) * BLOCK)
            mixed = _dot(wg, vg_ref[rows, cols]) + bias_ref[:, cols]
            y_scr[rows, cols] = (ug_ref[rows, cols].astype(F32) * mixed).astype(BF16)
    y_scr[:, a_width:] = (bo_ref[...].astype(F32) * _silu(zb_ref[...].astype(F32))).astype(BF16)
    o_ref[...] = x_ref[...] + gate_ref[...] * _dot(y_scr[...], wout_ref[...])


def _out_even(x2, seq, gate, p, b_out, a_ws, bias_full, w_out_bf):
    n, d = x2.shape
    tm = OUT_EVEN_ROWS
    aw = b_out.shape[1]
    row = lambda c: pl.BlockSpec((tm, aw), lambda i: (i, c))
    const = lambda a: pl.BlockSpec(a.shape, lambda i: (0,) * a.ndim, pipeline_mode=pl.Buffered(1))
    return pl.pallas_call(
        _out_even_kernel,
        grid=(n // tm,),
        in_specs=[pl.BlockSpec((tm, d), lambda i: (i, 0)),
                  pl.BlockSpec((None, 1, d), lambda i: ((i * tm) // seq, 0, 0)),
                  row(4), row(0), row(5),
                  row(0),
                  const(a_ws), const(bias_full), const(w_out_bf)],
        out_specs=pl.BlockSpec((tm, d), lambda i: (i, 0)),
        out_shape=jax.ShapeDtypeStruct((n, d), F32),
        scratch_shapes=[pltpu.VMEM((tm, 2 * aw), BF16)],
        compiler_params=pltpu.CompilerParams(
            dimension_semantics=("parallel",), vmem_limit_bytes=VMEM_LIMIT),
        name="out_even",
    )(x2, gate, p, p, p, b_out, a_ws, bias_full, w_out_bf)


def _rope_operands(positions):
    half = HEAD_DIM // 2
    inv_freq = ROPE_BASE ** (-jnp.arange(half, dtype=F32) / half)
    freq = jnp.concatenate([inv_freq, inv_freq]).reshape(1, HEAD_DIM)
    sign = jnp.concatenate([-jnp.ones((half,), F32), jnp.ones((half,), F32)]).reshape(1, HEAD_DIM)
    return positions.reshape(-1, LANES).astype(F32), freq, sign


def _retention_tables():
    chunk = RET_CHUNK
    f32 = np.float32
    log_gamma = np.log1p(-np.exp2(-5.0 - np.arange(N_HEADS, dtype=f32))).astype(f32)
    idx = np.arange(chunk, dtype=f32)
    diff = idx[:, None] - idx[None, :]
    intra = np.where(diff >= 0, np.exp(log_gamma[:, None, None] * np.maximum(diff, 0.0)), 0.0)
    q_decay = np.exp(log_gamma[:, None] * (idx + 1.0))
    k_decay = np.exp(log_gamma[:, None] * (chunk - 1.0 - idx))
    chunk_decay = np.exp(log_gamma * chunk)
    tables = (intra,
              np.broadcast_to(q_decay[:, :, None], (N_HEADS, chunk, HEAD_DIM)),
              np.broadcast_to(k_decay[:, :, None], (N_HEADS, chunk, HEAD_DIM)),
              np.broadcast_to(chunk_decay[:, None, None], (N_HEADS, HEAD_DIM, HEAD_DIM)))
    return tuple(jnp.asarray(np.ascontiguousarray(t, dtype=f32)) for t in tables)


def _retention_rows(q_ref, k_ref, v_ref, idec_ref, qdec_ref, kdec_ref, cdec_ref, st_scr, emit):
    chunk = RET_CHUNK
    n_chunks = q_ref.shape[0] // chunk
    cols = [slice(h * HEAD_DIM, (h + 1) * HEAD_DIM) for h in range(N_HEADS)]
    rows = [slice(c * chunk, (c + 1) * chunk) for c in range(n_chunks)]

    kv = {}
    for c in range(n_chunks):
        for h in range(N_HEADS):
            k_decayed = (k_ref[rows[c], cols[h]].astype(F32) * kdec_ref[h]).astype(BF16)
            kv[h, c] = _dot_tn(k_decayed, v_ref[rows[c], cols[h]])
    state = {}
    for h in range(N_HEADS):
        s = st_scr[h]
        for c in range(n_chunks):
            state[h, c] = s.astype(BF16)
            s = s * cdec_ref[h] + kv[h, c]
        st_scr[h] = s
    for c in range(n_chunks):
        for h in range(N_HEADS):
            q = q_ref[rows[c], cols[h]]
            scores = _dot_nt(q, k_ref[rows[c], cols[h]]) * idec_ref[h]
            out = (_dot(scores.astype(BF16), v_ref[rows[c], cols[h]])
                   + _dot(q, state[h, c]) * qdec_ref[h])
            ms = jnp.mean(out * out, axis=-1, keepdims=True)
            emit(rows[c], h, out * lax.rsqrt(ms + EPS))


def _pooled_rows(pc_ref, halo_ref, cw_ref, cs_ref, ext_scr, seq_row0, emit):
    tm = pc_ref.shape[0]
    ext_scr[0:POOL_HALO, :] = jnp.where(seq_row0 == 0, jnp.zeros_like(halo_ref), halo_ref[...])
    ext_scr[POOL_HALO:, :] = pc_ref[...]
    src_rows = BLOCK + POOL_HALO
    lag = (lax.broadcasted_iota(jnp.int32, (BLOCK, src_rows), 0) + POOL_HALO
           - lax.broadcasted_iota(jnp.int32, (BLOCK, src_rows), 1))
    t_seq = seq_row0 + lax.broadcasted_iota(jnp.int32, (BLOCK, POOL_GROUP_DIM), 0)
    for gi, win in enumerate(POOL_WINDOWS):
        cols = slice(gi * POOL_GROUP_DIM, (gi + 1) * POOL_GROUP_DIM)
        band = jnp.where((lag >= 0) & (lag < win), 1.0 / win, 0.0).astype(BF16)
        short = jnp.where(t_seq + 1 < win, win / (t_seq + 1).astype(F32), 1.0)
        pooled = []
        for c in range(tm // BLOCK):
            mean = _dot(band, ext_scr[c * BLOCK:c * BLOCK + src_rows, cols])
            if c == 0:
                mean = mean * short
            cur = pc_ref[c * BLOCK:(c + 1) * BLOCK, cols].astype(F32)
            pooled.append((mean - cur).astype(BF16))
        emit(gi, _dot(jnp.concatenate(pooled, axis=0), cw_ref[gi]) * cs_ref[:, cols])


def _odd_mixers_kernel(p_ref, idec_ref, qdec_ref, kdec_ref, cdec_ref,
                       x_ref, gate_ref, halo_ref, cw_ref, cs_ref, wout_ref,
                       o_ref, st_scr, ext_scr, y_scr):
    tm = x_ref.shape[0]
    c_width = halo_ref.shape[1]
    pc_ref, q_ref, k_ref, v_ref, za_ref, zb_ref = (
        p_ref.at[:, pl.ds(j * c_width, c_width)] for j in range(6))

    @pl.when(pl.program_id(1) == 0)
    def _():
        st_scr[...] = jnp.zeros_like(st_scr)

    def gated_retention(rows, h, out):
        cols = slice(h * HEAD_DIM, (h + 1) * HEAD_DIM)
        ycols = slice(c_width + h * HEAD_DIM, c_width + (h + 1) * HEAD_DIM)
        y_scr[rows, ycols] = (out * _silu(zb_ref[rows, cols].astype(F32))).astype(BF16)

    def gated_pool(gi, mixed):
        cols = slice(gi * POOL_GROUP_DIM, (gi + 1) * POOL_GROUP_DIM)
        y_scr[:, cols] = (mixed * _silu(za_ref[:, cols].astype(F32))).astype(BF16)

    _retention_rows(q_ref, k_ref, v_ref, idec_ref, qdec_ref, kdec_ref, cdec_ref, st_scr,
                    gated_retention)
    _pooled_rows(pc_ref, halo_ref, cw_ref, cs_ref, ext_scr, pl.program_id(1) * tm, gated_pool)
    o_ref[...] = x_ref[...] + gate_ref[...] * _dot(y_scr[...], wout_ref[...])


def _odd_mixers(x2, batch, seq, gate, p, c_w_bf, c_scale, w_out_bf):
    n, d = x2.shape
    tm = ROW_TILE
    cw = N_HEADS * HEAD_DIM
    steps = seq // tm
    tables = _retention_tables()
    tabs = [pl.BlockSpec(t.shape, lambda b, s: (0, 0, 0)) for t in tables]
    halo_blocks = tm // POOL_HALO
    halo = pl.BlockSpec(
        (POOL_HALO, cw), lambda b, s: (jnp.maximum((b * steps + s) * halo_blocks - 1, 0), 0))
    return pl.pallas_call(
        _odd_mixers_kernel,
        grid=(batch, steps),
        in_specs=[pl.BlockSpec((tm, p.shape[1]), lambda b, s: (b * steps + s, 0))] + tabs + [
            pl.BlockSpec((tm, d), lambda b, s: (b * steps + s, 0)),
            pl.BlockSpec((None, 1, d), lambda b, s: (b, 0, 0)),
            halo,
            pl.BlockSpec(c_w_bf.shape, lambda b, s: (0, 0, 0)),
            pl.BlockSpec((1, cw), lambda b, s: (0, 0)),
            pl.BlockSpec(w_out_bf.shape, lambda b, s: (0, 0))],
        out_specs=pl.BlockSpec((tm, d), lambda b, s: (b * steps + s, 0)),
        out_shape=jax.ShapeDtypeStruct((n, d), F32),
        scratch_shapes=[pltpu.VMEM((N_HEADS, HEAD_DIM, HEAD_DIM), F32),
                        pltpu.VMEM((POOL_HALO + tm, cw), BF16),
                        pltpu.VMEM((tm, 2 * cw), BF16)],
        compiler_params=pltpu.CompilerParams(
            dimension_semantics=("parallel", "arbitrary"), vmem_limit_bytes=VMEM_LIMIT),
        name="odd_mixers",
    )(p, *tables, x2, gate, p, c_w_bf, c_scale.reshape(1, cw), w_out_bf)


def _even_layer(x2, batch, seq, mod, norm_g, w_in, a_vnorm_g, a_ws, a_bs,
                b_qnorm_g, b_knorm_g, w_out):
    d = x2.shape[1]
    shift, scale, gate = _split_modulation(mod)
    gains = [a_vnorm_g, b_qnorm_g.reshape(1, -1), b_knorm_g.reshape(1, -1)]
    modes = ("hold", "norm", "norm_scaled", "norm", "raw", "silu_gate", "raw")
    p = _in_projection(x2, seq, norm_g, scale, shift, w_in, gains, modes,
                       post_scale=LOG2E * HEAD_DIM ** -0.5)
    b_out = _stick_breaking(p, batch, seq, q_col=d, k_col=2 * d, v_col=3 * d)
    bias_full = jnp.repeat(a_bs.T, HEAD_DIM, axis=1)
    return _out_even(x2, seq, gate, p, b_out, a_ws, bias_full, w_out.astype(BF16))


def _odd_layer(x2, batch, seq, mod, positions, norm_g, w_in, c_w, c_scale,
               d_qnorm_g, d_knorm_g, w_out):
    shift, scale, gate = _split_modulation(mod)
    gains = [d_qnorm_g.reshape(1, -1), d_knorm_g.reshape(1, -1)]
    modes = ("raw", "rope", "rope_scaled", "raw", "raw", "raw")
    p = _in_projection(x2, seq, norm_g, scale, shift, w_in, gains, modes,
                       rope=_rope_operands(positions), post_scale=HEAD_DIM ** -0.5)
    return _odd_mixers(x2, batch, seq, gate, p, c_w.astype(BF16), c_scale, w_out.astype(BF16))


def kernel(x, c, positions, even_norm_g, even_w_mod, even_b_mod, even_w_in, even_a_vnorm_g, even_a_ws, even_a_bs, even_b_qnorm_g, even_b_knorm_g, even_w_out, odd_norm_g, odd_w_mod, odd_b_mod, odd_w_in, odd_c_w, odd_c_scale, odd_d_qnorm_g, odd_d_knorm_g, odd_w_out):
    batch, seq, d = x.shape
    depth = even_norm_g.shape[0] + odd_norm_g.shape[0]
    x2 = x.reshape(batch * seq, d)
    n_even = even_norm_g.shape[0]
    mods = _modulations(c, even_w_mod, even_b_mod, odd_w_mod, odd_b_mod)
    for layer in range(depth):
        i = layer // 2
        if layer % 2 == 0:
            x2 = _even_layer(x2, batch, seq, mods[i], even_norm_g[i],
                             even_w_in[i], even_a_vnorm_g[i], even_a_ws[i], even_a_bs[i],
                             even_b_qnorm_g[i], even_b_knorm_g[i], even_w_out[i])
        else:
            x2 = _odd_layer(x2, batch, seq, mods[n_even + i], positions, odd_norm_g[i],
                            odd_w_in[i], odd_c_w[i], odd_c_scale[i], odd_d_qnorm_g[i],
                            odd_d_knorm_g[i], odd_w_out[i])
    return x2.reshape(batch, seq, d)
```

```python
import functools

import jax
import jax.numpy as jnp
import numpy as np
from jax import lax
from jax.experimental import pallas as pl
from jax.experimental.pallas import tpu as pltpu

F32 = jnp.float32
BF16 = jnp.bfloat16

EPS = 1e-6
ROPE_BASE = 10000.0
BLOCK = 128
HEAD_DIM = 128
N_HEADS = 8
POOL_WINDOWS = (2, 4, 8, 16)
POOL_GROUP_DIM = 256
POOL_HALO = 16

MOD_ROWS = 8

ROW_TILE = 512
OUT_EVEN_ROWS = 1024
COL_TILE = 1024
RET_CHUNK = 128

ATT_Q = 128
ATT_WIN = 384
ATT_NEW = 256
ATT_EXCLUDED = -1e30
LANES = 128
ATT_SUB = 2
ATT_HEADS_PER_STEP = 4
ATT_SKIP_BELOW = -151.0
LOG2E = 1.4426950408889634
ATT_NO_LIMIT = 1 << 30

VMEM_LIMIT = 56 * 1024 * 1024


def _silu(v):
    return v * jax.nn.sigmoid(v)


def _dot(a, b):
    return jnp.dot(a, b, preferred_element_type=F32)


def _dot_nt(a, b):
    return lax.dot_general(a, b, (((1,), (1,)), ((), ())), preferred_element_type=F32)


def _dot_tn(a, b):
    return lax.dot_general(a, b, (((0,), (0,)), ((), ())), preferred_element_type=F32)


def _group_rms(a, gain):
    ms = jnp.mean(a * a, axis=-1, keepdims=True)
    return a * lax.rsqrt(ms + EPS) * gain


def _mod_kernel(c_ref, we_ref, be_ref, wo_ref, bo_ref, o_ref, *, n_even):
    a = _silu(c_ref[...]).astype(BF16)
    layer = pl.program_id(0)

    @pl.when(layer < n_even)
    def _():
        o_ref[...] = _dot(a, we_ref[...].astype(BF16)) + be_ref[...]

    @pl.when(layer >= n_even)
    def _():
        o_ref[...] = _dot(a, wo_ref[...].astype(BF16)) + bo_ref[...]


def _modulations(c, even_w, even_b, odd_w, odd_b):
    bsz, d = c.shape
    n_even, n_odd = even_w.shape[0], odd_w.shape[0]
    c8 = jnp.pad(c, ((0, MOD_ROWS - bsz), (0, 0)))
    even_at = lambda l, j: (jnp.minimum(l, n_even - 1), 0, jnp.where(l < n_even, j, 2))
    odd_at = lambda l, j: (jnp.maximum(l - n_even, 0), 0, jnp.where(l < n_even, 0, j))
    m = pl.pallas_call(
        functools.partial(_mod_kernel, n_even=n_even),
        grid=(n_even + n_odd, 3),
        in_specs=[pl.BlockSpec((MOD_ROWS, d), lambda l, j: (0, 0)),
                  pl.BlockSpec((None, d, d), even_at),
                  pl.BlockSpec((None, 1, d), even_at),
                  pl.BlockSpec((None, d, d), odd_at),
                  pl.BlockSpec((None, 1, d), odd_at)],
        out_specs=pl.BlockSpec((None, MOD_ROWS, d), lambda l, j: (l, 0, j)),
        out_shape=jax.ShapeDtypeStruct((n_even + n_odd, MOD_ROWS, 3 * d), F32),
        name="modulation",
    )(c8, even_w, even_b.reshape(n_even, 1, 3 * d), odd_w, odd_b.reshape(n_odd, 1, 3 * d))
    return m[:, :bsz]


def _split_modulation(m):
    d = m.shape[1] // 3
    return m[:, None, :d], m[:, None, d:2 * d], m[:, None, 2 * d:]


def _inproj_kernel(*refs, modes, out_tiles, gain_of, n_gains, post_scale, use_rope):
    *refs, h_scr, w_ref, stage, sem, held = refs
    x_ref, g_ref, sc_ref, sh_ref, w_hbm = refs[:5]
    gain_refs = refs[5:5 + n_gains]
    o_ref = refs[-1]
    tn = COL_TILE

    def weight_copy(j):
        return pltpu.make_async_copy(w_hbm.at[:, pl.ds(j * tn, tn)], stage.at[j % 2], sem.at[j % 2])

    @pl.when(pl.program_id(0) == 0)
    def _():
        weight_copy(0).start()
        for j in range(len(modes)):
            if j + 1 < len(modes):
                weight_copy(j + 1).start()
            weight_copy(j).wait()
            w_ref[:, j * tn:(j + 1) * tn] = stage[j % 2].astype(BF16)

    if use_rope:
        pos_ref, freq_ref, sign_ref = refs[5 + n_gains:8 + n_gains]
        on_diag = (lax.broadcasted_iota(jnp.int32, (LANES, LANES), 0)
                   == lax.broadcasted_iota(jnp.int32, (LANES, LANES), 1))
        angles = []
        for a in range(pos_ref.shape[0]):
            spread = jnp.where(on_diag, jnp.broadcast_to(pos_ref[a:a + 1, :], (LANES, LANES)), 0.0)
            angles.append(jnp.sum(spread, axis=1, keepdims=True) * freq_ref[...])
        ang = jnp.concatenate(angles, axis=0)
        cos_t = jnp.cos(ang)
        sin_t = jnp.sin(ang) * sign_ref[...]
    x = x_ref[...]
    ms = jnp.mean(x * x, axis=-1, keepdims=True)
    y = x * lax.rsqrt(ms + EPS) * g_ref[...]
    h_scr[...] = (y * (1.0 + sc_ref[...]) + sh_ref[...]).astype(BF16)

    cross_lane = ("norm", "norm_scaled", "rope", "rope_scaled")
    heavy = [jj for jj, m in enumerate(modes) if m in cross_lane]
    plain = [jj for jj, m in enumerate(modes) if m not in cross_lane]
    order = []
    while heavy or plain:
        order += heavy[:1] + plain[:1]
        heavy, plain = heavy[1:], plain[1:]
    for jj in order:
        mode = modes[jj]
        out0 = None if out_tiles[jj] is None else out_tiles[jj] * tn
        acc = _dot(h_scr[...], w_ref[:, jj * tn:(jj + 1) * tn])
        if mode == "hold":
            held[...] = acc
            continue
        if mode == "raw":
            o_ref[:, out0:out0 + tn] = acc.astype(BF16)
            continue
        if mode == "silu_gate":
            o_ref[:, out0:out0 + tn] = (held[...] * _silu(acc)).astype(BF16)
            continue
        gain_ref = gain_refs[gain_of[jj]]
        for g in range(tn // HEAD_DIM):
            row = g if gain_ref.shape[0] > 1 else 0
            t = _group_rms(acc[:, g * HEAD_DIM:(g + 1) * HEAD_DIM], gain_ref[row:row + 1, :])
            if mode in ("rope", "rope_scaled"):
                t = t * cos_t + pltpu.roll(t, HEAD_DIM // 2, axis=1) * sin_t
            if mode in ("norm_scaled", "rope_scaled"):
                t = t * post_scale
            o_ref[:, out0 + g * HEAD_DIM:out0 + (g + 1) * HEAD_DIM] = t.astype(BF16)


def _in_projection(x2, seq, norm_g, scale, shift, w_in, gains, modes, rope=None, post_scale=1.0):
    n, d = x2.shape
    tm, tn = ROW_TILE, COL_TILE
    ncols = len(modes) * tn
    out_tiles, n_out = [], 0
    for m in modes:
        out_tiles.append(None if m == "hold" else n_out)
        n_out += m != "hold"
    assert modes.count("hold") == modes.count("silu_gate") <= 1
    assert "hold" not in modes or modes.index("hold") < modes.index("silu_gate")
    use_rope = rope is not None
    normalising = [jj for jj, m in enumerate(modes) if m in ("norm", "norm_scaled", "rope", "rope_scaled")]
    assert len(gains) == len(normalising)
    gain_of = tuple(normalising.index(jj) if jj in normalising else None for jj in range(len(modes)))
    in_specs = [
        pl.BlockSpec((tm, d), lambda i: (i, 0)),
        pl.BlockSpec((1, d), lambda i: (0, 0)),
        pl.BlockSpec((None, 1, d), lambda i: ((i * tm) // seq, 0, 0)),
        pl.BlockSpec((None, 1, d), lambda i: ((i * tm) // seq, 0, 0)),
        pl.BlockSpec(memory_space=pl.ANY),
    ] + [pl.BlockSpec(gn.shape, lambda i: (0, 0)) for gn in gains]
    args = [x2, norm_g.reshape(1, d), scale, shift, w_in, *gains]
    if use_rope:
        vec = pl.BlockSpec((1, HEAD_DIM), lambda i: (0, 0))
        pos, freq, sign = rope
        in_specs += [pl.BlockSpec((None, tm // LANES, LANES), lambda i: (i, 0, 0)), vec, vec]
        args += [pos.reshape(n // tm, tm // LANES, LANES), freq, sign]
    return pl.pallas_call(
        functools.partial(_inproj_kernel, modes=tuple(modes), out_tiles=tuple(out_tiles),
                          gain_of=gain_of, n_gains=len(gains), post_scale=post_scale,
                          use_rope=use_rope),
        grid=(n // tm,),
        in_specs=in_specs,
        out_specs=pl.BlockSpec((tm, n_out * tn), lambda i: (i, 0)),
        out_shape=jax.ShapeDtypeStruct((n, n_out * tn), BF16),
        scratch_shapes=[pltpu.VMEM((tm, d), BF16),
                        pltpu.VMEM((d, ncols), BF16),
                        pltpu.VMEM((2, d, tn), F32),
                        pltpu.SemaphoreType.DMA((2,)),
                        pltpu.VMEM((tm, tn), F32)],
        compiler_params=pltpu.CompilerParams(
            dimension_semantics=("arbitrary",), vmem_limit_bytes=VMEM_LIMIT),
        name="in_projection",
    )(*args)


def _attn_kernel(q_ref, k_ref, v_ref, o_ref, u_scr, *, seq):
    hb, tq, win, nsub = ATT_HEADS_PER_STEP, ATT_Q, ATT_WIN, ATT_SUB
    chains = nsub * hb

    u_scr[...] = jnp.where(lax.broadcasted_iota(jnp.int32, (ATT_NEW, ATT_NEW), 0)
                           > lax.broadcasted_iota(jnp.int32, (ATT_NEW, ATT_NEW), 1),
                           1.0, 0.0).astype(BF16)
    old = win - ATT_NEW
    key_idx = lax.broadcasted_iota(jnp.int32, (tq, win), 1)
    row_idx = lax.broadcasted_iota(jnp.int32, (tq, win), 0)
    tail = win - LANES
    tail_ok = (lax.broadcasted_iota(jnp.int32, (tq, LANES), 1) + tail
               < lax.broadcasted_iota(jnp.int32, (tq, LANES), 0) + (win - tq))

    def head_cols(h):
        return slice(h * HEAD_DIM, (h + 1) * HEAD_DIM)

    def tile_logits(rows0, keys0, which):
        return [_dot_nt(q_ref[pl.ds(rows0[c // hb], tq), head_cols(c % hb)],
                        k_ref[pl.ds(keys0[c // hb], win), head_cols(c % hb)]) for c in which]

    def tile_weights(logits, rows0, keys0, limits, totals, which):
        log_beta, terms = [], []
        for z, c in zip(logits, which):
            a = c // hb
            if limits is None:
                z = jnp.concatenate(
                    [z[:, :tail], jnp.where(tail_ok, z[:, tail:], ATT_EXCLUDED)], axis=1)
            else:
                ok = key_idx < (jnp.minimum(rows0[a] + row_idx, limits[a]) - keys0[a])
                z = jnp.where(ok, z, ATT_EXCLUDED)
            nz = -z
            lsm = (jnp.minimum(nz, 0.0)
                   - jnp.log(1.0 + jnp.exp2(jnp.minimum(z, nz))) * LOG2E)
            log_beta.append(z + lsm)
            terms.append(lsm)
        sums_new = _dot(jnp.concatenate([t[:, old:].astype(BF16) for t in terms], axis=0),
                        u_scr[...])
        sums_old = _dot(jnp.concatenate([t[:, :old].astype(BF16) for t in terms], axis=0),
                        u_scr[0:old, 0:old])
        weights, tile_totals = [], []
        for n in range(len(which)):
            newer = sums_new[n * tq:(n + 1) * tq]
            older = sums_old[n * tq:(n + 1) * tq]
            total_new = newer[:, 0:1] + terms[n][:, old:old + 1]
            expo = log_beta[n] + jnp.concatenate([older + total_new, newer], axis=1)
            if totals is not None:
                expo = expo + totals[n]
            weights.append(jnp.exp2(expo).astype(BF16))
            tile_totals.append(total_new + older[:, 0:1] + terms[n][:, 0:1])
        return weights, tile_totals

    def tile_values(weights, keys0, which):
        return [_dot(w, v_ref[pl.ds(keys0[c // hb], win), head_cols(c % hb)])
                for w, c in zip(weights, which)]

    def slowest(totals):
        m = totals[0]
        for t in totals[1:]:
            m = jnp.maximum(m, t)
        return jnp.max(m)

    def qblock(i, carry, *, near_start):
        r0 = i * (tq * nsub)
        rows0 = [pl.multiple_of(r0 + a * tq, tq) for a in range(nsub)]
        if near_start:
            starts = [jnp.maximum(r - (win - tq), 0) for r in rows0]
        else:
            starts = [r - (win - tq) for r in rows0]
        keys0 = [pl.multiple_of(s, tq) for s in starts]
        everyone = list(range(chains))
        weights, totals = tile_weights(tile_logits(rows0, keys0, everyone), rows0, keys0,
                                       [ATT_NO_LIMIT] * nsub if near_start else None, None,
                                       everyone)
        accs = tile_values(weights, keys0, everyone)

        def more(st):
            m, worst, _, _ = st
            return (starts[-1] - m * win > 0) & (worst > ATT_SKIP_BELOW)

        def older(st):
            m, _, totals, accs = st
            limits = [s - m * win for s in starts]
            keys0 = [pl.multiple_of(jnp.maximum(lim - win, 0), tq) for lim in limits]
            new_totals, new_accs = [], []
            needs = [jnp.max(t) > ATT_SKIP_BELOW for t in totals]
            for c in range(chains):
                def add_tile(c=c):
                    weights, tile_totals = tile_weights(tile_logits(rows0, keys0, [c]), rows0,
                                                        keys0, limits, [totals[c]], [c])
                    out, = tile_values(weights, keys0, [c])
                    return totals[c] + tile_totals[0], accs[c] + out

                total, acc = lax.cond(needs[c], add_tile, lambda c=c: (totals[c], accs[c]))
                new_totals.append(total)
                new_accs.append(acc)
            return (m + 1, slowest(new_totals), tuple(new_totals), tuple(new_accs))

        _, _, _, accs = lax.while_loop(
            more, older, (jnp.int32(0), slowest(totals), tuple(totals), tuple(accs)))
        for a in range(nsub):
            for h in range(hb):
                o_ref[pl.ds(rows0[a], tq), head_cols(h)] = accs[a * hb + h].astype(BF16)
        return carry

    n_blocks = seq // (tq * nsub)
    n_clipped = min(-(-(win - tq) // (tq * nsub)), n_blocks)
    lax.fori_loop(0, n_clipped, functools.partial(qblock, near_start=True), 0)
    lax.fori_loop(n_clipped, n_blocks, functools.partial(qblock, near_start=False), 0)


def _stick_breaking(p, batch, seq, q_col, k_col, v_col):
    n = p.shape[0]
    width = ATT_HEADS_PER_STEP * HEAD_DIM
    n_hg = N_HEADS // ATT_HEADS_PER_STEP

    def spec(col):
        return pl.BlockSpec((seq, width), lambda b, hg: (b, col // width + hg))

    return pl.pallas_call(
        functools.partial(_attn_kernel, seq=seq),
        grid=(batch, n_hg),
        in_specs=[spec(q_col), spec(k_col), spec(v_col)],
        out_specs=pl.BlockSpec((seq, width), lambda b, hg: (b, hg)),
        out_shape=jax.ShapeDtypeStruct((n, N_HEADS * HEAD_DIM), BF16),
        scratch_shapes=[pltpu.VMEM((ATT_NEW, ATT_NEW), BF16)],
        compiler_params=pltpu.CompilerParams(
            dimension_semantics=("parallel", "parallel"), vmem_limit_bytes=VMEM_LIMIT),
        name="stick_breaking",
    )(p, p, p)


def _out_even_kernel(x_ref, gate_ref, ug_ref, vg_ref, zb_ref, bo_ref, ws_ref, bias_ref, wout_ref,
                     o_ref, y_scr):
    tm = x_ref.shape[0]
    a_width = bo_ref.shape[1]
    row = lax.broadcasted_iota(jnp.int32, (BLOCK, BLOCK), 0)
    col = lax.broadcasted_iota(jnp.int32, (BLOCK, BLOCK), 1)
    for g in range(a_width // HEAD_DIM):
        cols = slice(g * HEAD_DIM, (g + 1) * HEAD_DIM)
        wg = jnp.where(col <= row, ws_ref[g], 0.0).astype(BF16)
        for c in range(tm // BLOCK):
            rows = slice(c * BLOCK, (c + 1) * BLOCK)
            mixed = _dot(wg, vg_ref[rows, cols]) + bias_ref[:, cols]
            y_scr[rows, cols] = (ug_ref[rows, cols].astype(F32) * mixed).astype(BF16)
    y_scr[:, a_width:] = (bo_ref[...].astype(F32) * _silu(zb_ref[...].astype(F32))).astype(BF16)
    o_ref[...] = x_ref[...] + gate_ref[...] * _dot(y_scr[...], wout_ref[...])


def _out_even(x2, seq, gate, p, b_out, a_ws, bias_full, w_out_bf):
    n, d = x2.shape
    tm = OUT_EVEN_ROWS
    aw = b_out.shape[1]
    row = lambda c: pl.BlockSpec((tm, aw), lambda i: (i, c))
    const = lambda a: pl.BlockSpec(a.shape, lambda i: (0,) * a.ndim, pipeline_mode=pl.Buffered(1))
    return pl.pallas_call(
        _out_even_kernel,
        grid=(n // tm,),
        in_specs=[pl.BlockSpec((tm, d), lambda i: (i, 0)),
                  pl.BlockSpec((None, 1, d), lambda i: ((i * tm) // seq, 0, 0)),
                  row(4), row(0), row(5),
                  row(0),
                  const(a_ws), const(bias_full), const(w_out_bf)],
        out_specs=pl.BlockSpec((tm, d), lambda i: (i, 0)),
        out_shape=jax.ShapeDtypeStruct((n, d), F32),
        scratch_shapes=[pltpu.VMEM((tm, 2 * aw), BF16)],
        compiler_params=pltpu.CompilerParams(
            dimension_semantics=("parallel",), vmem_limit_bytes=VMEM_LIMIT),
        name="out_even",
    )(x2, gate, p, p, p, b_out, a_ws, bias_full, w_out_bf)


def _rope_operands(positions):
    half = HEAD_DIM // 2
    inv_freq = ROPE_BASE ** (-jnp.arange(half, dtype=F32) / half)
    freq = jnp.concatenate([inv_freq, inv_freq]).reshape(1, HEAD_DIM)
    sign = jnp.concatenate([-jnp.ones((half,), F32), jnp.ones((half,), F32)]).reshape(1, HEAD_DIM)
    return positions.reshape(-1, LANES).astype(F32), freq, sign


def _retention_tables():
    chunk = RET_CHUNK
    f32 = np.float32
    log_gamma = np.log1p(-np.exp2(-5.0 - np.arange(N_HEADS, dtype=f32))).astype(f32)
    idx = np.arange(chunk, dtype=f32)
    diff = idx[:, None] - idx[None, :]
    intra = np.where(diff >= 0, np.exp(log_gamma[:, None, None] * np.maximum(diff, 0.0)), 0.0)
    q_decay = np.exp(log_gamma[:, None] * (idx + 1.0))
    k_decay = np.exp(log_gamma[:, None] * (chunk - 1.0 - idx))
    chunk_decay = np.exp(log_gamma * chunk)
    tables = (intra,
              np.broadcast_to(q_decay[:, :, None], (N_HEADS, chunk, HEAD_DIM)),
              np.broadcast_to(k_decay[:, :, None], (N_HEADS, chunk, HEAD_DIM)),
              np.broadcast_to(chunk_decay[:, None, None], (N_HEADS, HEAD_DIM, HEAD_DIM)))
    return tuple(jnp.asarray(np.ascontiguousarray(t, dtype=f32)) for t in tables)


def _retention_rows(q_ref, k_ref, v_ref, idec_ref, qdec_ref, kdec_ref, cdec_ref, st_scr, emit):
    chunk = RET_CHUNK
    n_chunks = q_ref.shape[0] // chunk
    cols = [slice(h * HEAD_DIM, (h + 1) * HEAD_DIM) for h in range(N_HEADS)]
    rows = [slice(c * chunk, (c + 1) * chunk) for c in range(n_chunks)]

    kv = {}
    for c in range(n_chunks):
        for h in range(N_HEADS):
            k_decayed = (k_ref[rows[c], cols[h]].astype(F32) * kdec_ref[h]).astype(BF16)
            kv[h, c] = _dot_tn(k_decayed, v_ref[rows[c], cols[h]])
    state = {}
    for h in range(N_HEADS):
        s = st_scr[h]
        for c in range(n_chunks):
            state[h, c] = s.astype(BF16)
            s = s * cdec_ref[h] + kv[h, c]
        st_scr[h] = s
    for c in range(n_chunks):
        for h in range(N_HEADS):
            q = q_ref[rows[c], cols[h]]
            scores = _dot_nt(q, k_ref[rows[c], cols[h]]) * idec_ref[h]
            out = (_dot(scores.astype(BF16), v_ref[rows[c], cols[h]])
                   + _dot(q, state[h, c]) * qdec_ref[h])
            ms = jnp.mean(out * out, axis=-1, keepdims=True)
            emit(rows[c], h, out * lax.rsqrt(ms + EPS))


def _pooled_rows(pc_ref, halo_ref, cw_ref, cs_ref, ext_scr, seq_row0, emit):
    tm = pc_ref.shape[0]
    ext_scr[0:POOL_HALO, :] = jnp.where(seq_row0 == 0, jnp.zeros_like(halo_ref), halo_ref[...])
    ext_scr[POOL_HALO:, :] = pc_ref[...]
    src_rows = BLOCK + POOL_HALO
    lag = (lax.broadcasted_iota(jnp.int32, (BLOCK, src_rows), 0) + POOL_HALO
           - lax.broadcasted_iota(jnp.int32, (BLOCK, src_rows), 1))
    t_seq = seq_row0 + lax.broadcasted_iota(jnp.int32, (BLOCK, POOL_GROUP_DIM), 0)
    for gi, win in enumerate(POOL_WINDOWS):
        cols = slice(gi * POOL_GROUP_DIM, (gi + 1) * POOL_GROUP_DIM)
        band = jnp.where((lag >= 0) & (lag < win), 1.0 / win, 0.0).astype(BF16)
        short = jnp.where(t_seq + 1 < win, win / (t_seq + 1).astype(F32), 1.0)
        pooled = []
        for c in range(tm // BLOCK):
            mean = _dot(band, ext_scr[c * BLOCK:c * BLOCK + src_rows, cols])
            if c == 0:
                mean = mean * short
            cur = pc_ref[c * BLOCK:(c + 1) * BLOCK, cols].astype(F32)
            pooled.append((mean - cur).astype(BF16))
        emit(gi, _dot(jnp.concatenate(pooled, axis=0), cw_ref[gi]) * cs_ref[:, cols])


def _odd_mixers_kernel(p_ref, idec_ref, qdec_ref, kdec_ref, cdec_ref,
                       x_ref, gate_ref, halo_ref, cw_ref, cs_ref, wout_ref,
                       o_ref, st_scr, ext_scr, y_scr):
    tm = x_ref.shape[0]
    c_width = halo_ref.shape[1]
    pc_ref, q_ref, k_ref, v_ref, za_ref, zb_ref = (
        p_ref.at[:, pl.ds(j * c_width, c_width)] for j in range(6))

    @pl.when(pl.program_id(1) == 0)
    def _():
        st_scr[...] = jnp.zeros_like(st_scr)

    def gated_retention(rows, h, out):
        cols = slice(h * HEAD_DIM, (h + 1) * HEAD_DIM)
        ycols = slice(c_width + h * HEAD_DIM, c_width + (h + 1) * HEAD_DIM)
        y_scr[rows, ycols] = (out * _silu(zb_ref[rows, cols].astype(F32))).astype(BF16)

    def gated_pool(gi, mixed):
        cols = slice(gi * POOL_GROUP_DIM, (gi + 1) * POOL_GROUP_DIM)
        y_scr[:, cols] = (mixed * _silu(za_ref[:, cols].astype(F32))).astype(BF16)

    _retention_rows(q_ref, k_ref, v_ref, idec_ref, qdec_ref, kdec_ref, cdec_ref, st_scr,
                    gated_retention)
    _pooled_rows(pc_ref, halo_ref, cw_ref, cs_ref, ext_scr, pl.program_id(1) * tm, gated_pool)
    o_ref[...] = x_ref[...] + gate_ref[...] * _dot(y_scr[...], wout_ref[...])


def _odd_mixers(x2, batch, seq, gate, p, c_w_bf, c_scale, w_out_bf):
    n, d = x2.shape
    tm = ROW_TILE
    cw = N_HEADS * HEAD_DIM
    steps = seq // tm
    tables = _retention_tables()
    tabs = [pl.BlockSpec(t.shape, lambda b, s: (0, 0, 0)) for t in tables]
    halo_blocks = tm // POOL_HALO
    halo = pl.BlockSpec(
        (POOL_HALO, cw), lambda b, s: (jnp.maximum((b * steps + s) * halo_blocks - 1, 0), 0))
    return pl.pallas_call(
        _odd_mixers_kernel,
        grid=(batch, steps),
        in_specs=[pl.BlockSpec((tm, p.shape[1]), lambda b, s: (b * steps + s, 0))] + tabs + [
            pl.BlockSpec((tm, d), lambda b, s: (b * steps + s, 0)),
            pl.BlockSpec((None, 1, d), lambda b, s: (b, 0, 0)),
            halo,
            pl.BlockSpec(c_w_bf.shape, lambda b, s: (0, 0, 0)),
            pl.BlockSpec((1, cw), lambda b, s: (0, 0)),
            pl.BlockSpec(w_out_bf.shape, lambda b, s: (0, 0))],
        out_specs=pl.BlockSpec((tm, d), lambda b, s: (b * steps + s, 0)),
        out_shape=jax.ShapeDtypeStruct((n, d), F32),
        scratch_shapes=[pltpu.VMEM((N_HEADS, HEAD_DIM, HEAD_DIM), F32),
                        pltpu.VMEM((POOL_HALO + tm, cw), BF16),
                        pltpu.VMEM((tm, 2 * cw), BF16)],
        compiler_params=pltpu.CompilerParams(
            dimension_semantics=("parallel", "arbitrary"), vmem_limit_bytes=VMEM_LIMIT),
        name="odd_mixers",
    )(p, *tables, x2, gate, p, c_w_bf, c_scale.reshape(1, cw), w_out_bf)


def _even_layer(x2, batch, seq, mod, norm_g, w_in, a_vnorm_g, a_ws, a_bs,
                b_qnorm_g, b_knorm_g, w_out):
    d = x2.shape[1]
    shift, scale, gate = _split_modulation(mod)
    gains = [a_vnorm_g, b_qnorm_g.reshape(1, -1), b_knorm_g.reshape(1, -1)]
    modes = ("hold", "norm", "norm_scaled", "norm", "raw", "silu_gate", "raw")
    p = _in_projection(x2, seq, norm_g, scale, shift, w_in, gains, modes,
                       post_scale=LOG2E * HEAD_DIM ** -0.5)
    b_out = _stick_breaking(p, batch, seq, q_col=d, k_col=2 * d, v_col=3 * d)
    bias_full = jnp.repeat(a_bs.T, HEAD_DIM, axis=1)
    return _out_even(x2, seq, gate, p, b_out, a_ws, bias_full, w_out.astype(BF16))


def _odd_layer(x2, batch, seq, mod, positions, norm_g, w_in, c_w, c_scale,
               d_qnorm_g, d_knorm_g, w_out):
    shift, scale, gate = _split_modulation(mod)
    gains = [d_qnorm_g.reshape(1, -1), d_knorm_g.reshape(1, -1)]
    modes = ("raw", "rope", "rope_scaled", "raw", "raw", "raw")
    p = _in_projection(x2, seq, norm_g, scale, shift, w_in, gains, modes,
                       rope=_rope_operands(positions), post_scale=HEAD_DIM ** -0.5)
    return _odd_mixers(x2, batch, seq, gate, p, c_w.astype(BF16), c_scale, w_out.astype(BF16))


def kernel(x, c, positions, even_norm_g, even_w_mod, even_b_mod, even_w_in, even_a_vnorm_g, even_a_ws, even_a_bs, even_b_qnorm_g, even_b_knorm_g, even_w_out, odd_norm_g, odd_w_mod, odd_b_mod, odd_w_in, odd_c_w, odd_c_scale, odd_d_qnorm_g, odd_d_knorm_g, odd_w_out):
    batch, seq, d = x.shape
    depth = even_norm_g.shape[0] + odd_norm_g.shape[0]
    x2 = x.reshape(batch * seq, d)
    n_even = even_norm_g.shape[0]
    mods = _modulations(c, even_w_mod, even_b_mod, odd_w_mod, odd_b_mod)
    for layer in range(depth):
        i = layer // 2
        if layer % 2 == 0:
            x2 = _even_layer(x2, batch, seq, mods[i], even_norm_g[i],
                             even_w_in[i], even_a_vnorm_g[i], even_a_ws[i], even_a_bs[i],
                             even_b_qnorm_g[i], even_b_knorm_g[i], even_w_out[i])
        else:
            x2 = _odd_layer(x2, batch, seq, mods[n_even + i], positions, odd_norm_g[i],
                            odd_w_in[i], odd_c_w[i], odd_c_scale[i], odd_d_qnorm_g[i],
                            odd_d_knorm_g[i], odd_w_out[i])
    return x2.reshape(batch, seq, d)
```

```python
import functools

import jax
import jax.numpy as jnp
import numpy as np
from jax import lax
from jax.experimental import pallas as pl
from jax.experimental.pallas import tpu as pltpu

F32 = jnp.float32
BF16 = jnp.bfloat16

EPS = 1e-6
ROPE_BASE = 10000.0
BLOCK = 128
HEAD_DIM = 128
N_HEADS = 8
POOL_WINDOWS = (2, 4, 8, 16)
POOL_GROUP_DIM = 256
POOL_HALO = 16

MOD_ROWS = 8

ROW_TILE = 512
OUT_EVEN_ROWS = 1024
COL_TILE = 1024
RET_CHUNK = 128

ATT_Q = 128
ATT_WIN = 384
ATT_NEW = 256
ATT_EXCLUDED = -1e30
LANES = 128
ATT_SUB = 2
ATT_HEADS_PER_STEP = 4
ATT_SKIP_BELOW = -151.0
LOG2E = 1.4426950408889634
ATT_NO_LIMIT = 1 << 30

VMEM_LIMIT = 56 * 1024 * 1024


def _silu(v):
    return v * jax.nn.sigmoid(v)


def _dot(a, b):
    return jnp.dot(a, b, preferred_element_type=F32)


def _dot_nt(a, b):
    return lax.dot_general(a, b, (((1,), (1,)), ((), ())), preferred_element_type=F32)


def _dot_tn(a, b):
    return lax.dot_general(a, b, (((0,), (0,)), ((), ())), preferred_element_type=F32)


def _group_rms(a, gain):
    ms = jnp.mean(a * a, axis=-1, keepdims=True)
    return a * lax.rsqrt(ms + EPS) * gain


def _mod_kernel(c_ref, we_ref, be_ref, wo_ref, bo_ref, o_ref, *, n_even):
    a = _silu(c_ref[...]).astype(BF16)
    layer = pl.program_id(0)

    @pl.when(layer < n_even)
    def _():
        o_ref[...] = _dot(a, we_ref[...].astype(BF16)) + be_ref[...]

    @pl.when(layer >= n_even)
    def _():
        o_ref[...] = _dot(a, wo_ref[...].astype(BF16)) + bo_ref[...]


def _modulations(c, even_w, even_b, odd_w, odd_b):
    bsz, d = c.shape
    n_even, n_odd = even_w.shape[0], odd_w.shape[0]
    c8 = jnp.pad(c, ((0, MOD_ROWS - bsz), (0, 0)))
    even_at = lambda l, j: (jnp.minimum(l, n_even - 1), 0, jnp.where(l < n_even, j, 2))
    odd_at = lambda l, j: (jnp.maximum(l - n_even, 0), 0, jnp.where(l < n_even, 0, j))
    m = pl.pallas_call(
        functools.partial(_mod_kernel, n_even=n_even),
        grid=(n_even + n_odd, 3),
        in_specs=[pl.BlockSpec((MOD_ROWS, d), lambda l, j: (0, 0)),
                  pl.BlockSpec((None, d, d), even_at),
                  pl.BlockSpec((None, 1, d), even_at),
                  pl.BlockSpec((None, d, d), odd_at),
                  pl.BlockSpec((None, 1, d), odd_at)],
        out_specs=pl.BlockSpec((None, MOD_ROWS, d), lambda l, j: (l, 0, j)),
        out_shape=jax.ShapeDtypeStruct((n_even + n_odd, MOD_ROWS, 3 * d), F32),
        name="modulation",
    )(c8, even_w, even_b.reshape(n_even, 1, 3 * d), odd_w, odd_b.reshape(n_odd, 1, 3 * d))
    return m[:, :bsz]


def _split_modulation(m):
    d = m.shape[1] // 3
    return m[:, None, :d], m[:, None, d:2 * d], m[:, None, 2 * d:]


def _inproj_kernel(*refs, modes, out_tiles, gain_of, n_gains, post_scale, use_rope):
    *refs, h_scr, w_ref, stage, sem, held = refs
    x_ref, g_ref, sc_ref, sh_ref, w_hbm = refs[:5]
    gain_refs = refs[5:5 + n_gains]
    o_ref = refs[-1]
    tn = COL_TILE

    def weight_copy(j):
        return pltpu.make_async_copy(w_hbm.at[:, pl.ds(j * tn, tn)], stage.at[j % 2], sem.at[j % 2])

    @pl.when(pl.program_id(0) == 0)
    def _():
        weight_copy(0).start()
        for j in range(len(modes)):
            if j + 1 < len(modes):
                weight_copy(j + 1).start()
            weight_copy(j).wait()
            w_ref[:, j * tn:(j + 1) * tn] = stage[j % 2].astype(BF16)

    if use_rope:
        pos_ref, freq_ref, sign_ref = refs[5 + n_gains:8 + n_gains]
        on_diag = (lax.broadcasted_iota(jnp.int32, (LANES, LANES), 0)
                   == lax.broadcasted_iota(jnp.int32, (LANES, LANES), 1))
        angles = []
        for a in range(pos_ref.shape[0]):
            spread = jnp.where(on_diag, jnp.broadcast_to(pos_ref[a:a + 1, :], (LANES, LANES)), 0.0)
            angles.append(jnp.sum(spread, axis=1, keepdims=True) * freq_ref[...])
        ang = jnp.concatenate(angles, axis=0)
        cos_t = jnp.cos(ang)
        sin_t = jnp.sin(ang) * sign_ref[...]
    x = x_ref[...]
    ms = jnp.mean(x * x, axis=-1, keepdims=True)
    gain = g_ref[...] * (1.0 + sc_ref[...])
    h_scr[...] = (x * lax.rsqrt(ms + EPS) * gain + sh_ref[...]).astype(BF16)

    cross_lane = ("norm", "norm_scaled", "rope", "rope_scaled")
    heavy = [jj for jj, m in enumerate(modes) if m in cross_lane]
    plain = [jj for jj, m in enumerate(modes) if m not in cross_lane]
    order = []
    while heavy or plain:
        order += heavy[:1] + plain[:1]
        heavy, plain = heavy[1:], plain[1:]
    for jj in order:
        mode = modes[jj]
        out0 = None if out_tiles[jj] is None else out_tiles[jj] * tn
        acc = _dot(h_scr[...], w_ref[:, jj * tn:(jj + 1) * tn])
        if mode == "hold":
            held[...] = acc
            continue
        if mode == "raw":
            o_ref[:, out0:out0 + tn] = acc.astype(BF16)
            continue
        if mode == "silu_gate":
            o_ref[:, out0:out0 + tn] = (held[...] * _silu(acc)).astype(BF16)
            continue
        gain_ref = gain_refs[gain_of[jj]]
        for g in range(tn // HEAD_DIM):
            row = g if gain_ref.shape[0] > 1 else 0
            t = _group_rms(acc[:, g * HEAD_DIM:(g + 1) * HEAD_DIM], gain_ref[row:row + 1, :])
            if mode in ("rope", "rope_scaled"):
                t = t * cos_t + pltpu.roll(t, HEAD_DIM // 2, axis=1) * sin_t
            if mode in ("norm_scaled", "rope_scaled"):
                t = t * post_scale
            o_ref[:, out0 + g * HEAD_DIM:out0 + (g + 1) * HEAD_DIM] = t.astype(BF16)


def _in_projection(x2, seq, norm_g, scale, shift, w_in, gains, modes, rope=None, post_scale=1.0):
    n, d = x2.shape
    tm, tn = ROW_TILE, COL_TILE
    ncols = len(modes) * tn
    out_tiles, n_out = [], 0
    for m in modes:
        out_tiles.append(None if m == "hold" else n_out)
        n_out += m != "hold"
    assert modes.count("hold") == modes.count("silu_gate") <= 1
    assert "hold" not in modes or modes.index("hold") < modes.index("silu_gate")
    use_rope = rope is not None
    normalising = [jj for jj, m in enumerate(modes) if m in ("norm", "norm_scaled", "rope", "rope_scaled")]
    assert len(gains) == len(normalising)
    gain_of = tuple(normalising.index(jj) if jj in normalising else None for jj in range(len(modes)))
    in_specs = [
        pl.BlockSpec((tm, d), lambda i: (i, 0)),
        pl.BlockSpec((1, d), lambda i: (0, 0)),
        pl.BlockSpec((None, 1, d), lambda i: ((i * tm) // seq, 0, 0)),
        pl.BlockSpec((None, 1, d), lambda i: ((i * tm) // seq, 0, 0)),
        pl.BlockSpec(memory_space=pl.ANY),
    ] + [pl.BlockSpec(gn.shape, lambda i: (0, 0)) for gn in gains]
    args = [x2, norm_g.reshape(1, d), scale, shift, w_in, *gains]
    if use_rope:
        vec = pl.BlockSpec((1, HEAD_DIM), lambda i: (0, 0))
        pos, freq, sign = rope
        in_specs += [pl.BlockSpec((None, tm // LANES, LANES), lambda i: (i, 0, 0)), vec, vec]
        args += [pos.reshape(n // tm, tm // LANES, LANES), freq, sign]
    return pl.pallas_call(
        functools.partial(_inproj_kernel, modes=tuple(modes), out_tiles=tuple(out_tiles),
                          gain_of=gain_of, n_gains=len(gains), post_scale=post_scale,
                          use_rope=use_rope),
        grid=(n // tm,),
        in_specs=in_specs,
        out_specs=pl.BlockSpec((tm, n_out * tn), lambda i: (i, 0)),
        out_shape=jax.ShapeDtypeStruct((n, n_out * tn), BF16),
        scratch_shapes=[pltpu.VMEM((tm, d), BF16),
                        pltpu.VMEM((d, ncols), BF16),
                        pltpu.VMEM((2, d, tn), F32),
                        pltpu.SemaphoreType.DMA((2,)),
                        pltpu.VMEM((tm, tn), F32)],
        compiler_params=pltpu.CompilerParams(
            dimension_semantics=("arbitrary",), vmem_limit_bytes=VMEM_LIMIT),
        name="in_projection",
    )(*args)


def _attn_kernel(q_ref, k_ref, v_ref, o_ref, u_scr, *, seq):
    hb, tq, win, nsub = ATT_HEADS_PER_STEP, ATT_Q, ATT_WIN, ATT_SUB
    chains = nsub * hb

    u_scr[...] = jnp.where(lax.broadcasted_iota(jnp.int32, (ATT_NEW, ATT_NEW), 0)
                           > lax.broadcasted_iota(jnp.int32, (ATT_NEW, ATT_NEW), 1),
                           1.0, 0.0).astype(BF16)
    old = win - ATT_NEW
    key_idx = lax.broadcasted_iota(jnp.int32, (tq, win), 1)
    row_idx = lax.broadcasted_iota(jnp.int32, (tq, win), 0)
    tail = win - LANES
    tail_ok = (lax.broadcasted_iota(jnp.int32, (tq, LANES), 1) + tail
               < lax.broadcasted_iota(jnp.int32, (tq, LANES), 0) + (win - tq))

    def head_cols(h):
        return slice(h * HEAD_DIM, (h + 1) * HEAD_DIM)

    def tile_logits(rows0, keys0, which):
        return [_dot_nt(q_ref[pl.ds(rows0[c // hb], tq), head_cols(c % hb)],
                        k_ref[pl.ds(keys0[c // hb], win), head_cols(c % hb)]) for c in which]

    def tile_weights(logits, rows0, keys0, limits, totals, which):
        log_beta, terms = [], []
        for z, c in zip(logits, which):
            a = c // hb
            if limits is None:
                z = jnp.concatenate(
                    [z[:, :tail], jnp.where(tail_ok, z[:, tail:], ATT_EXCLUDED)], axis=1)
            else:
                ok = key_idx < (jnp.minimum(rows0[a] + row_idx, limits[a]) - keys0[a])
                z = jnp.where(ok, z, ATT_EXCLUDED)
            nz = -z
            lsm = (jnp.minimum(nz, 0.0)
                   - jnp.log(1.0 + jnp.exp2(jnp.minimum(z, nz))) * LOG2E)
            log_beta.append(z + lsm)
            terms.append(lsm)
        sums_new = _dot(jnp.concatenate([t[:, old:].astype(BF16) for t in terms], axis=0),
                        u_scr[...])
        sums_old = _dot(jnp.concatenate([t[:, :old].astype(BF16) for t in terms], axis=0),
                        u_scr[0:old, 0:old])
        weights, tile_totals = [], []
        for n in range(len(which)):
            newer = sums_new[n * tq:(n + 1) * tq]
            older = sums_old[n * tq:(n + 1) * tq]
            total_new = newer[:, 0:1] + terms[n][:, old:old + 1]
            expo = log_beta[n] + jnp.concatenate([older + total_new, newer], axis=1)
            if totals is not None:
                expo = expo + totals[n]
            weights.append(jnp.exp2(expo).astype(BF16))
            tile_totals.append(total_new + older[:, 0:1] + terms[n][:, 0:1])
        return weights, tile_totals

    def tile_values(weights, keys0, which):
        return [_dot(w, v_ref[pl.ds(keys0[c // hb], win), head_cols(c % hb)])
                for w, c in zip(weights, which)]

    def slowest(totals):
        m = totals[0]
        for t in totals[1:]:
            m = jnp.maximum(m, t)
        return jnp.max(m)

    def qblock(i, carry, *, near_start):
        r0 = i * (tq * nsub)
        rows0 = [pl.multiple_of(r0 + a * tq, tq) for a in range(nsub)]
        if near_start:
            starts = [jnp.maximum(r - (win - tq), 0) for r in rows0]
        else:
            starts = [r - (win - tq) for r in rows0]
        keys0 = [pl.multiple_of(s, tq) for s in starts]
        everyone = list(range(chains))
        weights, totals = tile_weights(tile_logits(rows0, keys0, everyone), rows0, keys0,
                                       [ATT_NO_LIMIT] * nsub if near_start else None, None,
                                       everyone)
        accs = tile_values(weights, keys0, everyone)

        def more(st):
            m, worst, _, _ = st
            return (starts[-1] - m * win > 0) & (worst > ATT_SKIP_BELOW)

        def older(st):
            m, _, totals, accs = st
            limits = [s - m * win for s in starts]
            keys0 = [pl.multiple_of(jnp.maximum(lim - win, 0), tq) for lim in limits]
            new_totals, new_accs = [], []
            needs = [jnp.max(t) > ATT_SKIP_BELOW for t in totals]
            for c in range(chains):
                def add_tile(c=c):
                    weights, tile_totals = tile_weights(tile_logits(rows0, keys0, [c]), rows0,
                                                        keys0, limits, [totals[c]], [c])
                    out, = tile_values(weights, keys0, [c])
                    return totals[c] + tile_totals[0], accs[c] + out

                total, acc = lax.cond(needs[c], add_tile, lambda c=c: (totals[c], accs[c]))
                new_totals.append(total)
                new_accs.append(acc)
            return (m + 1, slowest(new_totals), tuple(new_totals), tuple(new_accs))

        _, _, _, accs = lax.while_loop(
            more, older, (jnp.int32(0), slowest(totals), tuple(totals), tuple(accs)))
        for a in range(nsub):
            for h in range(hb):
                o_ref[pl.ds(rows0[a], tq), head_cols(h)] = accs[a * hb + h].astype(BF16)
        return carry

    n_blocks = seq // (tq * nsub)
    n_clipped = min(-(-(win - tq) // (tq * nsub)), n_blocks)
    lax.fori_loop(0, n_clipped, functools.partial(qblock, near_start=True), 0)
    lax.fori_loop(n_clipped, n_blocks, functools.partial(qblock, near_start=False), 0)


def _stick_breaking(p, batch, seq, q_col, k_col, v_col):
    n = p.shape[0]
    width = ATT_HEADS_PER_STEP * HEAD_DIM
    n_hg = N_HEADS // ATT_HEADS_PER_STEP

    def spec(col):
        return pl.BlockSpec((seq, width), lambda b, hg: (b, col // width + hg))

    return pl.pallas_call(
        functools.partial(_attn_kernel, seq=seq),
        grid=(batch, n_hg),
        in_specs=[spec(q_col), spec(k_col), spec(v_col)],
        out_specs=pl.BlockSpec((seq, width), lambda b, hg: (b, hg)),
        out_shape=jax.ShapeDtypeStruct((n, N_HEADS * HEAD_DIM), BF16),
        scratch_shapes=[pltpu.VMEM((ATT_NEW, ATT_NEW), BF16)],
        compiler_params=pltpu.CompilerParams(
            dimension_semantics=("parallel", "parallel"), vmem_limit_bytes=VMEM_LIMIT),
        name="stick_breaking",
    )(p, p, p)


def _out_even_kernel(x_ref, gate_ref, ug_ref, vg_ref, zb_ref, bo_ref, ws_ref, bias_ref, wout_ref,
                     o_ref, y_scr):
    tm = x_ref.shape[0]
    a_width = bo_ref.shape[1]
    row = lax.broadcasted_iota(jnp.int32, (BLOCK, BLOCK), 0)
    col = lax.broadcasted_iota(jnp.int32, (BLOCK, BLOCK), 1)
    for g in range(a_width // HEAD_DIM):
        cols = slice(g * HEAD_DIM, (g + 1) * HEAD_DIM)
        wg = jnp.where(col <= row, ws_ref[g], 0.0).astype(BF16)
        for c in range(tm // BLOCK):
            rows = slice(c * BLOCK, (c + 1) * BLOCK)
            mixed = _dot(wg, vg_ref[rows, cols]) + bias_ref[:, cols]
            y_scr[rows, cols] = (ug_ref[rows, cols].astype(F32) * mixed).astype(BF16)
    y_scr[:, a_width:] = (bo_ref[...].astype(F32) * _silu(zb_ref[...].astype(F32))).astype(BF16)
    o_ref[...] = x_ref[...] + gate_ref[...] * _dot(y_scr[...], wout_ref[...])


def _out_even(x2, seq, gate, p, b_out, a_ws, bias_full, w_out_bf):
    n, d = x2.shape
    tm = OUT_EVEN_ROWS
    aw = b_out.shape[1]
    row = lambda c: pl.BlockSpec((tm, aw), lambda i: (i, c))
    const = lambda a: pl.BlockSpec(a.shape, lambda i: (0,) * a.ndim, pipeline_mode=pl.Buffered(1))
    return pl.pallas_call(
        _out_even_kernel,
        grid=(n // tm,),
        in_specs=[pl.BlockSpec((tm, d), lambda i: (i, 0)),
                  pl.BlockSpec((None, 1, d), lambda i: ((i * tm) // seq, 0, 0)),
                  row(4), row(0), row(5),
                  row(0),
                  const(a_ws), const(bias_full), const(w_out_bf)],
        out_specs=pl.BlockSpec((tm, d), lambda i: (i, 0)),
        out_shape=jax.ShapeDtypeStruct((n, d), F32),
        scratch_shapes=[pltpu.VMEM((tm, 2 * aw), BF16)],
        compiler_params=pltpu.CompilerParams(
            dimension_semantics=("parallel",), vmem_limit_bytes=VMEM_LIMIT),
        name="out_even",
    )(x2, gate, p, p, p, b_out, a_ws, bias_full, w_out_bf)


def _rope_operands(positions):
    half = HEAD_DIM // 2
    inv_freq = ROPE_BASE ** (-jnp.arange(half, dtype=F32) / half)
    freq = jnp.concatenate([inv_freq, inv_freq]).reshape(1, HEAD_DIM)
    sign = jnp.concatenate([-jnp.ones((half,), F32), jnp.ones((half,), F32)]).reshape(1, HEAD_DIM)
    return positions.reshape(-1, LANES).astype(F32), freq, sign


def _retention_tables():
    chunk = RET_CHUNK
    f32 = np.float32
    log_gamma = np.log1p(-np.exp2(-5.0 - np.arange(N_HEADS, dtype=f32))).astype(f32)
    idx = np.arange(chunk, dtype=f32)
    diff = idx[:, None] - idx[None, :]
    intra = np.where(diff >= 0, np.exp(log_gamma[:, None, None] * np.maximum(diff, 0.0)), 0.0)
    q_decay = np.exp(log_gamma[:, None] * (idx + 1.0))
    k_decay = np.exp(log_gamma[:, None] * (chunk - 1.0 - idx))
    chunk_decay = np.exp(log_gamma * chunk)
    tables = (intra,
              np.broadcast_to(q_decay[:, :, None], (N_HEADS, chunk, HEAD_DIM)),
              np.broadcast_to(k_decay[:, :, None], (N_HEADS, chunk, HEAD_DIM)),
              np.broadcast_to(chunk_decay[:, None, None], (N_HEADS, HEAD_DIM, HEAD_DIM)))
    return tuple(jnp.asarray(np.ascontiguousarray(t, dtype=f32)) for t in tables)


def _retention_rows(q_ref, k_ref, v_ref, idec_ref, qdec_ref, kdec_ref, cdec_ref, st_scr, emit):
    chunk = RET_CHUNK
    n_chunks = q_ref.shape[0] // chunk
    cols = [slice(h * HEAD_DIM, (h + 1) * HEAD_DIM) for h in range(N_HEADS)]
    rows = [slice(c * chunk, (c + 1) * chunk) for c in range(n_chunks)]

    kv = {}
    for c in range(n_chunks):
        for h in range(N_HEADS):
            k_decayed = (k_ref[rows[c], cols[h]].astype(F32) * kdec_ref[h]).astype(BF16)
            kv[h, c] = _dot_tn(k_decayed, v_ref[rows[c], cols[h]])
    state = {}
    for h in range(N_HEADS):
        s = st_scr[h]
        for c in range(n_chunks):
            state[h, c] = s.astype(BF16)
            s = s * cdec_ref[h] + kv[h, c]
        st_scr[h] = s
    for c in range(n_chunks):
        for h in range(N_HEADS):
            q = q_ref[rows[c], cols[h]]
            scores = _dot_nt(q, k_ref[rows[c], cols[h]]) * idec_ref[h]
            out = (_dot(scores.astype(BF16), v_ref[rows[c], cols[h]])
                   + _dot(q, state[h, c]) * qdec_ref[h])
            ms = jnp.mean(out * out, axis=-1, keepdims=True)
            emit(rows[c], h, out * lax.rsqrt(ms + EPS))


def _pooled_rows(pc_ref, halo_ref, cw_ref, cs_ref, ext_scr, seq_row0, emit):
    tm = pc_ref.shape[0]
    ext_scr[0:POOL_HALO, :] = jnp.where(seq_row0 == 0, jnp.zeros_like(halo_ref), halo_ref[...])
    ext_scr[POOL_HALO:, :] = pc_ref[...]
    src_rows = BLOCK + POOL_HALO
    lag = (lax.broadcasted_iota(jnp.int32, (BLOCK, src_rows), 0) + POOL_HALO
           - lax.broadcasted_iota(jnp.int32, (BLOCK, src_rows), 1))
    t_seq = seq_row0 + lax.broadcasted_iota(jnp.int32, (BLOCK, POOL_GROUP_DIM), 0)
    for gi, win in enumerate(POOL_WINDOWS):
        cols = slice(gi * POOL_GROUP_DIM, (gi + 1) * POOL_GROUP_DIM)
        band = jnp.where((lag >= 0) & (lag < win), 1.0 / win, 0.0).astype(BF16)
        short = jnp.where(t_seq + 1 < win, win / (t_seq + 1).astype(F32), 1.0)
        pooled = []
        for c in range(tm // BLOCK):
            mean = _dot(band, ext_scr[c * BLOCK:c * BLOCK + src_rows, cols])
            if c == 0:
                mean = mean * short
            cur = pc_ref[c * BLOCK:(c + 1) * BLOCK, cols].astype(F32)
            pooled.append((mean - cur).astype(BF16))
        emit(gi, _dot(jnp.concatenate(pooled, axis=0), cw_ref[gi]) * cs_ref[:, cols])


def _odd_mixers_kernel(p_ref, idec_ref, qdec_ref, kdec_ref, cdec_ref,
                       x_ref, gate_ref, halo_ref, cw_ref, cs_ref, wout_ref,
                       o_ref, st_scr, ext_scr, y_scr):
    tm = x_ref.shape[0]
    c_width = halo_ref.shape[1]
    pc_ref, q_ref, k_ref, v_ref, za_ref, zb_ref = (
        p_ref.at[:, pl.ds(j * c_width, c_width)] for j in range(6))

    @pl.when(pl.program_id(1) == 0)
    def _():
        st_scr[...] = jnp.zeros_like(st_scr)

    def gated_retention(rows, h, out):
        cols = slice(h * HEAD_DIM, (h + 1) * HEAD_DIM)
        ycols = slice(c_width + h * HEAD_DIM, c_width + (h + 1) * HEAD_DIM)
        y_scr[rows, ycols] = (out * _silu(zb_ref[rows, cols].astype(F32))).astype(BF16)

    def gated_pool(gi, mixed):
        cols = slice(gi * POOL_GROUP_DIM, (gi + 1) * POOL_GROUP_DIM)
        y_scr[:, cols] = (mixed * _silu(za_ref[:, cols].astype(F32))).astype(BF16)

    _retention_rows(q_ref, k_ref, v_ref, idec_ref, qdec_ref, kdec_ref, cdec_ref, st_scr,
                    gated_retention)
    _pooled_rows(pc_ref, halo_ref, cw_ref, cs_ref, ext_scr, pl.program_id(1) * tm, gated_pool)
    o_ref[...] = x_ref[...] + gate_ref[...] * _dot(y_scr[...], wout_ref[...])


def _odd_mixers(x2, batch, seq, gate, p, c_w_bf, c_scale, w_out_bf):
    n, d = x2.shape
    tm = ROW_TILE
    cw = N_HEADS * HEAD_DIM
    steps = seq // tm
    tables = _retention_tables()
    tabs = [pl.BlockSpec(t.shape, lambda b, s: (0, 0, 0)) for t in tables]
    halo_blocks = tm // POOL_HALO
    halo = pl.BlockSpec(
        (POOL_HALO, cw), lambda b, s: (jnp.maximum((b * steps + s) * halo_blocks - 1, 0), 0))
    return pl.pallas_call(
        _odd_mixers_kernel,
        grid=(batch, steps),
        in_specs=[pl.BlockSpec((tm, p.shape[1]), lambda b, s: (b * steps + s, 0))] + tabs + [
            pl.BlockSpec((tm, d), lambda b, s: (b * steps + s, 0)),
            pl.BlockSpec((None, 1, d), lambda b, s: (b, 0, 0)),
            halo,
            pl.BlockSpec(c_w_bf.shape, lambda b, s: (0, 0, 0)),
            pl.BlockSpec((1, cw), lambda b, s: (0, 0)),
            pl.BlockSpec(w_out_bf.shape, lambda b, s: (0, 0))],
        out_specs=pl.BlockSpec((tm, d), lambda b, s: (b * steps + s, 0)),
        out_shape=jax.ShapeDtypeStruct((n, d), F32),
        scratch_shapes=[pltpu.VMEM((N_HEADS, HEAD_DIM, HEAD_DIM), F32),
                        pltpu.VMEM((POOL_HALO + tm, cw), BF16),
                        pltpu.VMEM((tm, 2 * cw), BF16)],
        compiler_params=pltpu.CompilerParams(
            dimension_semantics=("parallel", "arbitrary"), vmem_limit_bytes=VMEM_LIMIT),
        name="odd_mixers",
    )(p, *tables, x2, gate, p, c_w_bf, c_scale.reshape(1, cw), w_out_bf)


def _even_layer(x2, batch, seq, mod, norm_g, w_in, a_vnorm_g, a_ws, a_bs,
                b_qnorm_g, b_knorm_g, w_out):
    d = x2.shape[1]
    shift, scale, gate = _split_modulation(mod)
    gains = [a_vnorm_g, b_qnorm_g.reshape(1, -1), b_knorm_g.reshape(1, -1)]
    modes = ("hold", "norm", "norm_scaled", "norm", "raw", "silu_gate", "raw")
    p = _in_projection(x2, seq, norm_g, scale, shift, w_in, gains, modes,
                       post_scale=LOG2E * HEAD_DIM ** -0.5)
    b_out = _stick_breaking(p, batch, seq, q_col=d, k_col=2 * d, v_col=3 * d)
    bias_full = jnp.repeat(a_bs.T, HEAD_DIM, axis=1)
    return _out_even(x2, seq, gate, p, b_out, a_ws, bias_full, w_out.astype(BF16))


def _odd_layer(x2, batch, seq, mod, positions, norm_g, w_in, c_w, c_scale,
               d_qnorm_g, d_knorm_g, w_out):
    shift, scale, gate = _split_modulation(mod)
    gains = [d_qnorm_g.reshape(1, -1), d_knorm_g.reshape(1, -1)]
    modes = ("raw", "rope", "rope_scaled", "raw", "raw", "raw")
    p = _in_projection(x2, seq, norm_g, scale, shift, w_in, gains, modes,
                       rope=_rope_operands(positions), post_scale=HEAD_DIM ** -0.5)
    return _odd_mixers(x2, batch, seq, gate, p, c_w.astype(BF16), c_scale, w_out.astype(BF16))


def kernel(x, c, positions, even_norm_g, even_w_mod, even_b_mod, even_w_in, even_a_vnorm_g, even_a_ws, even_a_bs, even_b_qnorm_g, even_b_knorm_g, even_w_out, odd_norm_g, odd_w_mod, odd_b_mod, odd_w_in, odd_c_w, odd_c_scale, odd_d_qnorm_g, odd_d_knorm_g, odd_w_out):
    batch, seq, d = x.shape
    depth = even_norm_g.shape[0] + odd_norm_g.shape[0]
    x2 = x.reshape(batch * seq, d)
    n_even = even_norm_g.shape[0]
    mods = _modulations(c, even_w_mod, even_b_mod, odd_w_mod, odd_b_mod)
    for layer in range(depth):
        i = layer // 2
        if layer % 2 == 0:
            x2 = _even_layer(x2, batch, seq, mods[i], even_norm_g[i],
                             even_w_in[i], even_a_vnorm_g[i], even_a_ws[i], even_a_bs[i],
                             even_b_qnorm_g[i], even_b_knorm_g[i], even_w_out[i])
        else:
            x2 = _odd_layer(x2, batch, seq, mods[n_even + i], positions, odd_norm_g[i],
                            odd_w_in[i], odd_c_w[i], odd_c_scale[i], odd_d_qnorm_g[i],
                            odd_d_knorm_g[i], odd_w_out[i])
    return x2.reshape(batch, seq, d)
```

```python
import functools

import jax
import jax.numpy as jnp
import numpy as np
from jax import lax
from jax.experimental import pallas as pl
from jax.experimental.pallas import tpu as pltpu

F32 = jnp.float32
BF16 = jnp.bfloat16

EPS = 1e-6
ROPE_BASE = 10000.0
BLOCK = 128
HEAD_DIM = 128
N_HEADS = 8
POOL_WINDOWS = (2, 4, 8, 16)
POOL_GROUP_DIM = 256
POOL_HALO = 16

MOD_ROWS = 8

ROW_TILE = 512
OUT_EVEN_ROWS = 1024
COL_TILE = 1024
RET_CHUNK = 128

ATT_Q = 128
ATT_WIN = 384
ATT_NEW = 256
ATT_EXCLUDED = -1e30
LANES = 128
ATT_SUB = 4
ATT_HEADS_PER_STEP = 4
ATT_SKIP_BELOW = -151.0
LOG2E = 1.4426950408889634
ATT_NO_LIMIT = 1 << 30

VMEM_LIMIT = 56 * 1024 * 1024


def _silu(v):
    return v * jax.nn.sigmoid(v)


def _dot(a, b):
    return jnp.dot(a, b, preferred_element_type=F32)


def _dot_nt(a, b):
    return lax.dot_general(a, b, (((1,), (1,)), ((), ())), preferred_element_type=F32)


def _dot_tn(a, b):
    return lax.dot_general(a, b, (((0,), (0,)), ((), ())), preferred_element_type=F32)


def _group_rms(a, gain):
    ms = jnp.mean(a * a, axis=-1, keepdims=True)
    return a * lax.rsqrt(ms + EPS) * gain


def _mod_kernel(c_ref, we_ref, be_ref, wo_ref, bo_ref, o_ref, *, n_even):
    a = _silu(c_ref[...]).astype(BF16)
    layer = pl.program_id(0)

    @pl.when(layer < n_even)
    def _():
        o_ref[...] = _dot(a, we_ref[...].astype(BF16)) + be_ref[...]

    @pl.when(layer >= n_even)
    def _():
        o_ref[...] = _dot(a, wo_ref[...].astype(BF16)) + bo_ref[...]


def _modulations(c, even_w, even_b, odd_w, odd_b):
    bsz, d = c.shape
    n_even, n_odd = even_w.shape[0], odd_w.shape[0]
    c8 = jnp.pad(c, ((0, MOD_ROWS - bsz), (0, 0)))
    even_at = lambda l, j: (jnp.minimum(l, n_even - 1), 0, jnp.where(l < n_even, j, 2))
    odd_at = lambda l, j: (jnp.maximum(l - n_even, 0), 0, jnp.where(l < n_even, 0, j))
    m = pl.pallas_call(
        functools.partial(_mod_kernel, n_even=n_even),
        grid=(n_even + n_odd, 3),
        in_specs=[pl.BlockSpec((MOD_ROWS, d), lambda l, j: (0, 0)),
                  pl.BlockSpec((None, d, d), even_at),
                  pl.BlockSpec((None, 1, d), even_at),
                  pl.BlockSpec((None, d, d), odd_at),
                  pl.BlockSpec((None, 1, d), odd_at)],
        out_specs=pl.BlockSpec((None, MOD_ROWS, d), lambda l, j: (l, 0, j)),
        out_shape=jax.ShapeDtypeStruct((n_even + n_odd, MOD_ROWS, 3 * d), F32),
        name="modulation",
    )(c8, even_w, even_b.reshape(n_even, 1, 3 * d), odd_w, odd_b.reshape(n_odd, 1, 3 * d))
    return m[:, :bsz]


def _split_modulation(m):
    d = m.shape[1] // 3
    return m[:, None, :d], m[:, None, d:2 * d], m[:, None, 2 * d:]


def _inproj_kernel(*refs, modes, out_tiles, gain_of, n_gains, post_scale, use_rope):
    *refs, h_scr, w_ref, stage, sem, held = refs
    x_ref, g_ref, sc_ref, sh_ref, w_hbm = refs[:5]
    gain_refs = refs[5:5 + n_gains]
    o_ref = refs[-1]
    tn = COL_TILE

    def weight_copy(j):
        return pltpu.make_async_copy(w_hbm.at[:, pl.ds(j * tn, tn)], stage.at[j % 2], sem.at[j % 2])

    @pl.when(pl.program_id(0) == 0)
    def _():
        weight_copy(0).start()
        for j in range(len(modes)):
            if j + 1 < len(modes):
                weight_copy(j + 1).start()
            weight_copy(j).wait()
            w_ref[:, j * tn:(j + 1) * tn] = stage[j % 2].astype(BF16)

    if use_rope:
        pos_ref, freq_ref, sign_ref = refs[5 + n_gains:8 + n_gains]
        on_diag = (lax.broadcasted_iota(jnp.int32, (LANES, LANES), 0)
                   == lax.broadcasted_iota(jnp.int32, (LANES, LANES), 1))
        angles = []
        for a in range(pos_ref.shape[0]):
            spread = jnp.where(on_diag, jnp.broadcast_to(pos_ref[a:a + 1, :], (LANES, LANES)), 0.0)
            angles.append(jnp.sum(spread, axis=1, keepdims=True) * freq_ref[...])
        ang = jnp.concatenate(angles, axis=0)
        cos_t = jnp.cos(ang)
        sin_t = jnp.sin(ang) * sign_ref[...]
    x = x_ref[...]
    ms = jnp.mean(x * x, axis=-1, keepdims=True)
    gain = g_ref[...] * (1.0 + sc_ref[...])
    h_scr[...] = (x * lax.rsqrt(ms + EPS) * gain + sh_ref[...]).astype(BF16)

    cross_lane = ("norm", "norm_scaled", "rope", "rope_scaled")
    heavy = [jj for jj, m in enumerate(modes) if m in cross_lane]
    plain = [jj for jj, m in enumerate(modes) if m not in cross_lane]
    order = []
    while heavy or plain:
        order += heavy[:1] + plain[:1]
        heavy, plain = heavy[1:], plain[1:]
    for jj in order:
        mode = modes[jj]
        out0 = None if out_tiles[jj] is None else out_tiles[jj] * tn
        acc = _dot(h_scr[...], w_ref[:, jj * tn:(jj + 1) * tn])
        if mode == "hold":
            held[...] = acc
            continue
        if mode == "raw":
            o_ref[:, out0:out0 + tn] = acc.astype(BF16)
            continue
        if mode == "silu_gate":
            o_ref[:, out0:out0 + tn] = (held[...] * _silu(acc)).astype(BF16)
            continue
        gain_ref = gain_refs[gain_of[jj]]
        for g in range(tn // HEAD_DIM):
            row = g if gain_ref.shape[0] > 1 else 0
            t = _group_rms(acc[:, g * HEAD_DIM:(g + 1) * HEAD_DIM], gain_ref[row:row + 1, :])
            if mode in ("rope", "rope_scaled"):
                t = t * cos_t + pltpu.roll(t, HEAD_DIM // 2, axis=1) * sin_t
            if mode in ("norm_scaled", "rope_scaled"):
                t = t * post_scale
            o_ref[:, out0 + g * HEAD_DIM:out0 + (g + 1) * HEAD_DIM] = t.astype(BF16)


def _in_projection(x2, seq, norm_g, scale, shift, w_in, gains, modes, rope=None, post_scale=1.0):
    n, d = x2.shape
    tm, tn = ROW_TILE, COL_TILE
    ncols = len(modes) * tn
    out_tiles, n_out = [], 0
    for m in modes:
        out_tiles.append(None if m == "hold" else n_out)
        n_out += m != "hold"
    assert modes.count("hold") == modes.count("silu_gate") <= 1
    assert "hold" not in modes or modes.index("hold") < modes.index("silu_gate")
    use_rope = rope is not None
    normalising = [jj for jj, m in enumerate(modes) if m in ("norm", "norm_scaled", "rope", "rope_scaled")]
    assert len(gains) == len(normalising)
    gain_of = tuple(normalising.index(jj) if jj in normalising else None for jj in range(len(modes)))
    in_specs = [
        pl.BlockSpec((tm, d), lambda i: (i, 0)),
        pl.BlockSpec((1, d), lambda i: (0, 0)),
        pl.BlockSpec((None, 1, d), lambda i: ((i * tm) // seq, 0, 0)),
        pl.BlockSpec((None, 1, d), lambda i: ((i * tm) // seq, 0, 0)),
        pl.BlockSpec(memory_space=pl.ANY),
    ] + [pl.BlockSpec(gn.shape, lambda i: (0, 0)) for gn in gains]
    args = [x2, norm_g.reshape(1, d), scale, shift, w_in, *gains]
    if use_rope:
        vec = pl.BlockSpec((1, HEAD_DIM), lambda i: (0, 0))
        pos, freq, sign = rope
        in_specs += [pl.BlockSpec((None, tm // LANES, LANES), lambda i: (i, 0, 0)), vec, vec]
        args += [pos.reshape(n // tm, tm // LANES, LANES), freq, sign]
    return pl.pallas_call(
        functools.partial(_inproj_kernel, modes=tuple(modes), out_tiles=tuple(out_tiles),
                          gain_of=gain_of, n_gains=len(gains), post_scale=post_scale,
                          use_rope=use_rope),
        grid=(n // tm,),
        in_specs=in_specs,
        out_specs=pl.BlockSpec((tm, n_out * tn), lambda i: (i, 0)),
        out_shape=jax.ShapeDtypeStruct((n, n_out * tn), BF16),
        scratch_shapes=[pltpu.VMEM((tm, d), BF16),
                        pltpu.VMEM((d, ncols), BF16),
                        pltpu.VMEM((2, d, tn), F32),
                        pltpu.SemaphoreType.DMA((2,)),
                        pltpu.VMEM((tm, tn), F32)],
        compiler_params=pltpu.CompilerParams(
            dimension_semantics=("arbitrary",), vmem_limit_bytes=VMEM_LIMIT),
        name="in_projection",
    )(*args)


def _attn_kernel(q_ref, k_ref, v_ref, o_ref, u_scr, *, seq):
    hb, tq, win, nsub = ATT_HEADS_PER_STEP, ATT_Q, ATT_WIN, ATT_SUB
    chains = nsub * hb

    u_scr[...] = jnp.where(lax.broadcasted_iota(jnp.int32, (ATT_NEW, ATT_NEW), 0)
                           > lax.broadcasted_iota(jnp.int32, (ATT_NEW, ATT_NEW), 1),
                           1.0, 0.0).astype(BF16)
    old = win - ATT_NEW
    key_idx = lax.broadcasted_iota(jnp.int32, (tq, win), 1)
    row_idx = lax.broadcasted_iota(jnp.int32, (tq, win), 0)
    tail = win - LANES
    tail_ok = (lax.broadcasted_iota(jnp.int32, (tq, LANES), 1) + tail
               < lax.broadcasted_iota(jnp.int32, (tq, LANES), 0) + (win - tq))

    def head_cols(h):
        return slice(h * HEAD_DIM, (h + 1) * HEAD_DIM)

    def tile_logits(rows0, keys0, which):
        return [_dot_nt(q_ref[pl.ds(rows0[c // hb], tq), head_cols(c % hb)],
                        k_ref[pl.ds(keys0[c // hb], win), head_cols(c % hb)]) for c in which]

    def tile_weights(logits, rows0, keys0, limits, totals, which):
        log_beta, terms = [], []
        for z, c in zip(logits, which):
            a = c // hb
            if limits is None:
                z = jnp.concatenate(
                    [z[:, :tail], jnp.where(tail_ok, z[:, tail:], ATT_EXCLUDED)], axis=1)
            else:
                ok = key_idx < (jnp.minimum(rows0[a] + row_idx, limits[a]) - keys0[a])
                z = jnp.where(ok, z, ATT_EXCLUDED)
            nz = -z
            lsm = (jnp.minimum(nz, 0.0)
                   - jnp.log(1.0 + jnp.exp2(jnp.minimum(z, nz))) * LOG2E)
            log_beta.append(z + lsm)
            terms.append(lsm)
        sums_new = _dot(jnp.concatenate([t[:, old:].astype(BF16) for t in terms], axis=0),
                        u_scr[...])
        sums_old = _dot(jnp.concatenate([t[:, :old].astype(BF16) for t in terms], axis=0),
                        u_scr[0:old, 0:old])
        weights, tile_totals = [], []
        for n in range(len(which)):
            newer = sums_new[n * tq:(n + 1) * tq]
            older = sums_old[n * tq:(n + 1) * tq]
            total_new = newer[:, 0:1] + terms[n][:, old:old + 1]
            expo = log_beta[n] + jnp.concatenate([older + total_new, newer], axis=1)
            if totals is not None:
                expo = expo + totals[n]
            weights.append(jnp.exp2(expo).astype(BF16))
            tile_totals.append(total_new + older[:, 0:1] + terms[n][:, 0:1])
        return weights, tile_totals

    def tile_values(weights, keys0, which):
        return [_dot(w, v_ref[pl.ds(keys0[c // hb], win), head_cols(c % hb)])
                for w, c in zip(weights, which)]

    def slowest(totals):
        m = totals[0]
        for t in totals[1:]:
            m = jnp.maximum(m, t)
        return jnp.max(m)

    def qblock(i, carry, *, near_start):
        r0 = i * (tq * nsub)
        rows0 = [pl.multiple_of(r0 + a * tq, tq) for a in range(nsub)]
        if near_start:
            starts = [jnp.maximum(r - (win - tq), 0) for r in rows0]
        else:
            starts = [r - (win - tq) for r in rows0]
        keys0 = [pl.multiple_of(s, tq) for s in starts]
        everyone = list(range(chains))
        weights, totals = tile_weights(tile_logits(rows0, keys0, everyone), rows0, keys0,
                                       [ATT_NO_LIMIT] * nsub if near_start else None, None,
                                       everyone)
        accs = tile_values(weights, keys0, everyone)

        def more(st):
            m, worst, _, _ = st
            return (starts[-1] - m * win > 0) & (worst > ATT_SKIP_BELOW)

        def older(st):
            m, _, totals, accs = st
            limits = [s - m * win for s in starts]
            keys0 = [pl.multiple_of(jnp.maximum(lim - win, 0), tq) for lim in limits]
            new_totals, new_accs = [], []
            needs = [jnp.max(t) > ATT_SKIP_BELOW for t in totals]
            for c in range(chains):
                def add_tile(c=c):
                    weights, tile_totals = tile_weights(tile_logits(rows0, keys0, [c]), rows0,
                                                        keys0, limits, [totals[c]], [c])
                    out, = tile_values(weights, keys0, [c])
                    return totals[c] + tile_totals[0], accs[c] + out

                total, acc = lax.cond(needs[c], add_tile, lambda c=c: (totals[c], accs[c]))
                new_totals.append(total)
                new_accs.append(acc)
            return (m + 1, slowest(new_totals), tuple(new_totals), tuple(new_accs))

        _, _, _, accs = lax.while_loop(
            more, older, (jnp.int32(0), slowest(totals), tuple(totals), tuple(accs)))
        for a in range(nsub):
            for h in range(hb):
                o_ref[pl.ds(rows0[a], tq), head_cols(h)] = accs[a * hb + h].astype(BF16)
        return carry

    n_blocks = seq // (tq * nsub)
    n_clipped = min(-(-(win - tq) // (tq * nsub)), n_blocks)
    lax.fori_loop(0, n_clipped, functools.partial(qblock, near_start=True), 0)
    lax.fori_loop(n_clipped, n_blocks, functools.partial(qblock, near_start=False), 0)


def _stick_breaking(p, batch, seq, q_col, k_col, v_col):
    n = p.shape[0]
    width = ATT_HEADS_PER_STEP * HEAD_DIM
    n_hg = N_HEADS // ATT_HEADS_PER_STEP

    def spec(col):
        return pl.BlockSpec((seq, width), lambda b, hg: (b, col // width + hg))

    return pl.pallas_call(
        functools.partial(_attn_kernel, seq=seq),
        grid=(batch, n_hg),
        in_specs=[spec(q_col), spec(k_col), spec(v_col)],
        out_specs=pl.BlockSpec((seq, width), lambda b, hg: (b, hg)),
        out_shape=jax.ShapeDtypeStruct((n, N_HEADS * HEAD_DIM), BF16),
        scratch_shapes=[pltpu.VMEM((ATT_NEW, ATT_NEW), BF16)],
        compiler_params=pltpu.CompilerParams(
            dimension_semantics=("parallel", "parallel"), vmem_limit_bytes=VMEM_LIMIT),
        name="stick_breaking",
    )(p, p, p)


def _out_even_kernel(x_ref, gate_ref, ug_ref, vg_ref, zb_ref, bo_ref, ws_ref, bias_ref, wout_ref,
                     o_ref, y_scr):
    tm = x_ref.shape[0]
    a_width = bo_ref.shape[1]
    row = lax.broadcasted_iota(jnp.int32, (BLOCK, BLOCK), 0)
    col = lax.broadcasted_iota(jnp.int32, (BLOCK, BLOCK), 1)
    for g in range(a_width // HEAD_DIM):
        cols = slice(g * HEAD_DIM, (g + 1) * HEAD_DIM)
        wg = jnp.where(col <= row, ws_ref[g], 0.0).astype(BF16)
        for c in range(tm // BLOCK):
            rows = slice(c * BLOCK, (c + 1) * BLOCK)
            mixed = _dot(wg, vg_ref[rows, cols]) + bias_ref[:, cols]
            y_scr[rows, cols] = (ug_ref[rows, cols].astype(F32) * mixed).astype(BF16)
    y_scr[:, a_width:] = (bo_ref[...].astype(F32) * _silu(zb_ref[...].astype(F32))).astype(BF16)
    o_ref[...] = x_ref[...] + gate_ref[...] * _dot(y_scr[...], wout_ref[...])


def _out_even(x2, seq, gate, p, b_out, a_ws, bias_full, w_out_bf):
    n, d = x2.shape
    tm = OUT_EVEN_ROWS
    aw = b_out.shape[1]
    row = lambda c: pl.BlockSpec((tm, aw), lambda i: (i, c))
    const = lambda a: pl.BlockSpec(a.shape, lambda i: (0,) * a.ndim, pipeline_mode=pl.Buffered(1))
    return pl.pallas_call(
        _out_even_kernel,
        grid=(n // tm,),
        in_specs=[pl.BlockSpec((tm, d), lambda i: (i, 0)),
                  pl.BlockSpec((None, 1, d), lambda i: ((i * tm) // seq, 0, 0)),
                  row(4), row(0), row(5),
                  row(0),
                  const(a_ws), const(bias_full), const(w_out_bf)],
        out_specs=pl.BlockSpec((tm, d), lambda i: (i, 0)),
        out_shape=jax.ShapeDtypeStruct((n, d), F32),
        scratch_shapes=[pltpu.VMEM((tm, 2 * aw), BF16)],
        compiler_params=pltpu.CompilerParams(
            dimension_semantics=("parallel",), vmem_limit_bytes=VMEM_LIMIT),
        name="out_even",
    )(x2, gate, p, p, p, b_out, a_ws, bias_full, w_out_bf)


def _rope_operands(positions):
    half = HEAD_DIM // 2
    inv_freq = ROPE_BASE ** (-jnp.arange(half, dtype=F32) / half)
    freq = jnp.concatenate([inv_freq, inv_freq]).reshape(1, HEAD_DIM)
    sign = jnp.concatenate([-jnp.ones((half,), F32), jnp.ones((half,), F32)]).reshape(1, HEAD_DIM)
    return positions.reshape(-1, LANES).astype(F32), freq, sign


def _retention_tables():
    chunk = RET_CHUNK
    f32 = np.float32
    log_gamma = np.log1p(-np.exp2(-5.0 - np.arange(N_HEADS, dtype=f32))).astype(f32)
    idx = np.arange(chunk, dtype=f32)
    diff = idx[:, None] - idx[None, :]
    intra = np.where(diff >= 0, np.exp(log_gamma[:, None, None] * np.maximum(diff, 0.0)), 0.0)
    q_decay = np.exp(log_gamma[:, None] * (idx + 1.0))
    k_decay = np.exp(log_gamma[:, None] * (chunk - 1.0 - idx))
    chunk_decay = np.exp(log_gamma * chunk)
    tables = (intra,
              np.broadcast_to(q_decay[:, :, None], (N_HEADS, chunk, HEAD_DIM)),
              np.broadcast_to(k_decay[:, :, None], (N_HEADS, chunk, HEAD_DIM)),
              np.broadcast_to(chunk_decay[:, None, None], (N_HEADS, HEAD_DIM, HEAD_DIM)))
    return tuple(jnp.asarray(np.ascontiguousarray(t, dtype=f32)) for t in tables)


def _retention_rows(q_ref, k_ref, v_ref, idec_ref, qdec_ref, kdec_ref, cdec_ref, st_scr, emit):
    chunk = RET_CHUNK
    n_chunks = q_ref.shape[0] // chunk
    cols = [slice(h * HEAD_DIM, (h + 1) * HEAD_DIM) for h in range(N_HEADS)]
    rows = [slice(c * chunk, (c + 1) * chunk) for c in range(n_chunks)]

    kv = {}
    for c in range(n_chunks):
        for h in range(N_HEADS):
            k_decayed = (k_ref[rows[c], cols[h]].astype(F32) * kdec_ref[h]).astype(BF16)
            kv[h, c] = _dot_tn(k_decayed, v_ref[rows[c], cols[h]])
    state = {}
    for h in range(N_HEADS):
        s = st_scr[h]
        for c in range(n_chunks):
            state[h, c] = s.astype(BF16)
            s = s * cdec_ref[h] + kv[h, c]
        st_scr[h] = s
    for c in range(n_chunks):
        for h in range(N_HEADS):
            q = q_ref[rows[c], cols[h]]
            scores = _dot_nt(q, k_ref[rows[c], cols[h]]) * idec_ref[h]
            out = (_dot(scores.astype(BF16), v_ref[rows[c], cols[h]])
                   + _dot(q, state[h, c]) * qdec_ref[h])
            ms = jnp.mean(out * out, axis=-1, keepdims=True)
            emit(rows[c], h, out * lax.rsqrt(ms + EPS))


def _pooled_rows(pc_ref, halo_ref, cw_ref, cs_ref, ext_scr, seq_row0, emit):
    tm = pc_ref.shape[0]
    ext_scr[0:POOL_HALO, :] = jnp.where(seq_row0 == 0, jnp.zeros_like(halo_ref), halo_ref[...])
    ext_scr[POOL_HALO:, :] = pc_ref[...]
    src_rows = BLOCK + POOL_HALO
    lag = (lax.broadcasted_iota(jnp.int32, (BLOCK, src_rows), 0) + POOL_HALO
           - lax.broadcasted_iota(jnp.int32, (BLOCK, src_rows), 1))
    t_seq = seq_row0 + lax.broadcasted_iota(jnp.int32, (BLOCK, POOL_GROUP_DIM), 0)
    for gi, win in enumerate(POOL_WINDOWS):
        cols = slice(gi * POOL_GROUP_DIM, (gi + 1) * POOL_GROUP_DIM)
        band = jnp.where((lag >= 0) & (lag < win), 1.0 / win, 0.0).astype(BF16)
        short = jnp.where(t_seq + 1 < win, win / (t_seq + 1).astype(F32), 1.0)
        pooled = []
        for c in range(tm // BLOCK):
            mean = _dot(band, ext_scr[c * BLOCK:c * BLOCK + src_rows, cols])
            if c == 0:
                mean = mean * short
            cur = pc_ref[c * BLOCK:(c + 1) * BLOCK, cols].astype(F32)
            pooled.append((mean - cur).astype(BF16))
        emit(gi, _dot(jnp.concatenate(pooled, axis=0), cw_ref[gi]) * cs_ref[:, cols])


def _odd_mixers_kernel(p_ref, idec_ref, qdec_ref, kdec_ref, cdec_ref,
                       x_ref, gate_ref, halo_ref, cw_ref, cs_ref, wout_ref,
                       o_ref, st_scr, ext_scr, y_scr):
    tm = x_ref.shape[0]
    c_width = halo_ref.shape[1]
    pc_ref, q_ref, k_ref, v_ref, za_ref, zb_ref = (
        p_ref.at[:, pl.ds(j * c_width, c_width)] for j in range(6))

    @pl.when(pl.program_id(1) == 0)
    def _():
        st_scr[...] = jnp.zeros_like(st_scr)

    def gated_retention(rows, h, out):
        cols = slice(h * HEAD_DIM, (h + 1) * HEAD_DIM)
        ycols = slice(c_width + h * HEAD_DIM, c_width + (h + 1) * HEAD_DIM)
        y_scr[rows, ycols] = (out * _silu(zb_ref[rows, cols].astype(F32))).astype(BF16)

    def gated_pool(gi, mixed):
        cols = slice(gi * POOL_GROUP_DIM, (gi + 1) * POOL_GROUP_DIM)
        y_scr[:, cols] = (mixed * _silu(za_ref[:, cols].astype(F32))).astype(BF16)

    _retention_rows(q_ref, k_ref, v_ref, idec_ref, qdec_ref, kdec_ref, cdec_ref, st_scr,
                    gated_retention)
    _pooled_rows(pc_ref, halo_ref, cw_ref, cs_ref, ext_scr, pl.program_id(1) * tm, gated_pool)
    o_ref[...] = x_ref[...] + gate_ref[...] * _dot(y_scr[...], wout_ref[...])


def _odd_mixers(x2, batch, seq, gate, p, c_w_bf, c_scale, w_out_bf):
    n, d = x2.shape
    tm = ROW_TILE
    cw = N_HEADS * HEAD_DIM
    steps = seq // tm
    tables = _retention_tables()
    tabs = [pl.BlockSpec(t.shape, lambda b, s: (0, 0, 0)) for t in tables]
    halo_blocks = tm // POOL_HALO
    halo = pl.BlockSpec(
        (POOL_HALO, cw), lambda b, s: (jnp.maximum((b * steps + s) * halo_blocks - 1, 0), 0))
    return pl.pallas_call(
        _odd_mixers_kernel,
        grid=(batch, steps),
        in_specs=[pl.BlockSpec((tm, p.shape[1]), lambda b, s: (b * steps + s, 0))] + tabs + [
            pl.BlockSpec((tm, d), lambda b, s: (b * steps + s, 0)),
            pl.BlockSpec((None, 1, d), lambda b, s: (b, 0, 0)),
            halo,
            pl.BlockSpec(c_w_bf.shape, lambda b, s: (0, 0, 0)),
            pl.BlockSpec((1, cw), lambda b, s: (0, 0)),
            pl.BlockSpec(w_out_bf.shape, lambda b, s: (0, 0))],
        out_specs=pl.BlockSpec((tm, d), lambda b, s: (b * steps + s, 0)),
        out_shape=jax.ShapeDtypeStruct((n, d), F32),
        scratch_shapes=[pltpu.VMEM((N_HEADS, HEAD_DIM, HEAD_DIM), F32),
                        pltpu.VMEM((POOL_HALO + tm, cw), BF16),
                        pltpu.VMEM((tm, 2 * cw), BF16)],
        compiler_params=pltpu.CompilerParams(
            dimension_semantics=("parallel", "arbitrary"), vmem_limit_bytes=VMEM_LIMIT),
        name="odd_mixers",
    )(p, *tables, x2, gate, p, c_w_bf, c_scale.reshape(1, cw), w_out_bf)


def _even_layer(x2, batch, seq, mod, norm_g, w_in, a_vnorm_g, a_ws, a_bs,
                b_qnorm_g, b_knorm_g, w_out):
    d = x2.shape[1]
    shift, scale, gate = _split_modulation(mod)
    gains = [a_vnorm_g, b_qnorm_g.reshape(1, -1), b_knorm_g.reshape(1, -1)]
    modes = ("hold", "norm", "norm_scaled", "norm", "raw", "silu_gate", "raw")
    p = _in_projection(x2, seq, norm_g, scale, shift, w_in, gains, modes,
                       post_scale=LOG2E * HEAD_DIM ** -0.5)
    b_out = _stick_breaking(p, batch, seq, q_col=d, k_col=2 * d, v_col=3 * d)
    bias_full = jnp.repeat(a_bs.T, HEAD_DIM, axis=1)
    return _out_even(x2, seq, gate, p, b_out, a_ws, bias_full, w_out.astype(BF16))


def _odd_layer(x2, batch, seq, mod, positions, norm_g, w_in, c_w, c_scale,
               d_qnorm_g, d_knorm_g, w_out):
    shift, scale, gate = _split_modulation(mod)
    gains = [d_qnorm_g.reshape(1, -1), d_knorm_g.reshape(1, -1)]
    modes = ("raw", "rope", "rope_scaled", "raw", "raw", "raw")
    p = _in_projection(x2, seq, norm_g, scale, shift, w_in, gains, modes,
                       rope=_rope_operands(positions), post_scale=HEAD_DIM ** -0.5)
    return _odd_mixers(x2, batch, seq, gate, p, c_w.astype(BF16), c_scale, w_out.astype(BF16))


def kernel(x, c, positions, even_norm_g, even_w_mod, even_b_mod, even_w_in, even_a_vnorm_g, even_a_ws, even_a_bs, even_b_qnorm_g, even_b_knorm_g, even_w_out, odd_norm_g, odd_w_mod, odd_b_mod, odd_w_in, odd_c_w, odd_c_scale, odd_d_qnorm_g, odd_d_knorm_g, odd_w_out):
    batch, seq, d = x.shape
    depth = even_norm_g.shape[0] + odd_norm_g.shape[0]
    x2 = x.reshape(batch * seq, d)
    n_even = even_norm_g.shape[0]
    mods = _modulations(c, even_w_mod, even_b_mod, odd_w_mod, odd_b_mod)
    for layer in range(depth):
        i = layer // 2
        if layer % 2 == 0:
            x2 = _even_layer(x2, batch, seq, mods[i], even_norm_g[i],
                             even_w_in[i], even_a_vnorm_g[i], even_a_ws[i], even_a_bs[i],
                             even_b_qnorm_g[i], even_b_knorm_g[i], even_w_out[i])
        else:
            x2 = _odd_layer(x2, batch, seq, mods[n_even + i], positions, odd_norm_g[i],
                            odd_w_in[i], odd_c_w[i], odd_c_scale[i], odd_d_qnorm_g[i],
                            odd_d_knorm_g[i], odd_w_out[i])
    return x2.reshape(batch, seq, d)
```

```python
import functools

import jax
import jax.numpy as jnp
import numpy as np
from jax import lax
from jax.experimental import pallas as pl
from jax.experimental.pallas import tpu as pltpu

F32 = jnp.float32
BF16 = jnp.bfloat16

EPS = 1e-6
ROPE_BASE = 10000.0
BLOCK = 128
HEAD_DIM = 128
N_HEADS = 8
POOL_WINDOWS = (2, 4, 8, 16)
POOL_GROUP_DIM = 256
POOL_HALO = 16

MOD_ROWS = 8

ROW_TILE = 512
OUT_EVEN_ROWS = 1024
COL_TILE = 1024
RET_CHUNK = 128
RET_HEAD_GROUP = 4

ATT_Q = 128
ATT_WIN = 384
ATT_NEW = 256
ATT_EXCLUDED = -1e30
LANES = 128
ATT_SUB = 2
ATT_HEADS_PER_STEP = 4
ATT_SKIP_BELOW = -151.0
LOG2E = 1.4426950408889634
ATT_NO_LIMIT = 1 << 30

VMEM_LIMIT = 56 * 1024 * 1024


def _silu(v):
    return v * jax.nn.sigmoid(v)


def _dot(a, b):
    return jnp.dot(a, b, preferred_element_type=F32)


def _dot_nt(a, b):
    return lax.dot_general(a, b, (((1,), (1,)), ((), ())), preferred_element_type=F32)


def _dot_tn(a, b):
    return lax.dot_general(a, b, (((0,), (0,)), ((), ())), preferred_element_type=F32)


def _group_rms(a, gain):
    ms = jnp.mean(a * a, axis=-1, keepdims=True)
    return a * lax.rsqrt(ms + EPS) * gain


def _mod_kernel(c_ref, we_ref, be_ref, wo_ref, bo_ref, o_ref, *, n_even):
    a = _silu(c_ref[...]).astype(BF16)
    layer = pl.program_id(0)

    @pl.when(layer < n_even)
    def _():
        o_ref[...] = _dot(a, we_ref[...].astype(BF16)) + be_ref[...]

    @pl.when(layer >= n_even)
    def _():
        o_ref[...] = _dot(a, wo_ref[...].astype(BF16)) + bo_ref[...]


def _modulations(c, even_w, even_b, odd_w, odd_b):
    bsz, d = c.shape
    n_even, n_odd = even_w.shape[0], odd_w.shape[0]
    c8 = jnp.pad(c, ((0, MOD_ROWS - bsz), (0, 0)))
    even_at = lambda l, j: (jnp.minimum(l, n_even - 1), 0, jnp.where(l < n_even, j, 2))
    odd_at = lambda l, j: (jnp.maximum(l - n_even, 0), 0, jnp.where(l < n_even, 0, j))
    m = pl.pallas_call(
        functools.partial(_mod_kernel, n_even=n_even),
        grid=(n_even + n_odd, 3),
        in_specs=[pl.BlockSpec((MOD_ROWS, d), lambda l, j: (0, 0)),
                  pl.BlockSpec((None, d, d), even_at),
                  pl.BlockSpec((None, 1, d), even_at),
                  pl.BlockSpec((None, d, d), odd_at),
                  pl.BlockSpec((None, 1, d), odd_at)],
        out_specs=pl.BlockSpec((None, MOD_ROWS, d), lambda l, j: (l, 0, j)),
        out_shape=jax.ShapeDtypeStruct((n_even + n_odd, MOD_ROWS, 3 * d), F32),
        name="modulation",
    )(c8, even_w, even_b.reshape(n_even, 1, 3 * d), odd_w, odd_b.reshape(n_odd, 1, 3 * d))
    return m[:, :bsz]


def _split_modulation(m):
    d = m.shape[1] // 3
    return m[:, None, :d], m[:, None, d:2 * d], m[:, None, 2 * d:]


def _inproj_kernel(*refs, modes, out_tiles, gain_of, n_gains, post_scale, use_rope):
    *refs, h_scr, w_ref, stage, sem, held = refs
    x_ref, g_ref, sc_ref, sh_ref, w_hbm = refs[:5]
    gain_refs = refs[5:5 + n_gains]
    o_ref = refs[-1]
    tn = COL_TILE

    def weight_copy(j):
        return pltpu.make_async_copy(w_hbm.at[:, pl.ds(j * tn, tn)], stage.at[j % 2], sem.at[j % 2])

    @pl.when(pl.program_id(0) == 0)
    def _():
        weight_copy(0).start()
        for j in range(len(modes)):
            if j + 1 < len(modes):
                weight_copy(j + 1).start()
            weight_copy(j).wait()
            w_ref[:, j * tn:(j + 1) * tn] = stage[j % 2].astype(BF16)

    if use_rope:
        pos_ref, freq_ref, sign_ref = refs[5 + n_gains:8 + n_gains]
        on_diag = (lax.broadcasted_iota(jnp.int32, (LANES, LANES), 0)
                   == lax.broadcasted_iota(jnp.int32, (LANES, LANES), 1))
        angles = []
        for a in range(pos_ref.shape[0]):
            spread = jnp.where(on_diag, jnp.broadcast_to(pos_ref[a:a + 1, :], (LANES, LANES)), 0.0)
            angles.append(jnp.sum(spread, axis=1, keepdims=True) * freq_ref[...])
        ang = jnp.concatenate(angles, axis=0)
        cos_t = jnp.cos(ang)
        sin_t = jnp.sin(ang) * sign_ref[...]
    x = x_ref[...]
    ms = jnp.mean(x * x, axis=-1, keepdims=True)
    gain = g_ref[...] * (1.0 + sc_ref[...])
    h_scr[...] = (x * lax.rsqrt(ms + EPS) * gain + sh_ref[...]).astype(BF16)

    cross_lane = ("norm", "norm_scaled", "rope", "rope_scaled")
    heavy = [jj for jj, m in enumerate(modes) if m in cross_lane]
    plain = [jj for jj, m in enumerate(modes) if m not in cross_lane]
    order = []
    while heavy or plain:
        order += heavy[:1] + plain[:1]
        heavy, plain = heavy[1:], plain[1:]
    for jj in order:
        mode = modes[jj]
        out0 = None if out_tiles[jj] is None else out_tiles[jj] * tn
        acc = _dot(h_scr[...], w_ref[:, jj * tn:(jj + 1) * tn])
        if mode == "hold":
            held[...] = acc
            continue
        if mode == "raw":
            o_ref[:, out0:out0 + tn] = acc.astype(BF16)
            continue
        if mode == "silu_gate":
            o_ref[:, out0:out0 + tn] = (held[...] * _silu(acc)).astype(BF16)
            continue
        gain_ref = gain_refs[gain_of[jj]]
        for g in range(tn // HEAD_DIM):
            row = g if gain_ref.shape[0] > 1 else 0
            t = _group_rms(acc[:, g * HEAD_DIM:(g + 1) * HEAD_DIM], gain_ref[row:row + 1, :])
            if mode in ("rope", "rope_scaled"):
                t = t * cos_t + pltpu.roll(t, HEAD_DIM // 2, axis=1) * sin_t
            if mode in ("norm_scaled", "rope_scaled"):
                t = t * post_scale
            o_ref[:, out0 + g * HEAD_DIM:out0 + (g + 1) * HEAD_DIM] = t.astype(BF16)


def _in_projection(x2, seq, norm_g, scale, shift, w_in, gains, modes, rope=None, post_scale=1.0):
    n, d = x2.shape
    tm, tn = ROW_TILE, COL_TILE
    ncols = len(modes) * tn
    out_tiles, n_out = [], 0
    for m in modes:
        out_tiles.append(None if m == "hold" else n_out)
        n_out += m != "hold"
    assert modes.count("hold") == modes.count("silu_gate") <= 1
    assert "hold" not in modes or modes.index("hold") < modes.index("silu_gate")
    use_rope = rope is not None
    normalising = [jj for jj, m in enumerate(modes) if m in ("norm", "norm_scaled", "rope", "rope_scaled")]
    assert len(gains) == len(normalising)
    gain_of = tuple(normalising.index(jj) if jj in normalising else None for jj in range(len(modes)))
    in_specs = [
        pl.BlockSpec((tm, d), lambda i: (i, 0)),
        pl.BlockSpec((1, d), lambda i: (0, 0)),
        pl.BlockSpec((None, 1, d), lambda i: ((i * tm) // seq, 0, 0)),
        pl.BlockSpec((None, 1, d), lambda i: ((i * tm) // seq, 0, 0)),
        pl.BlockSpec(memory_space=pl.ANY),
    ] + [pl.BlockSpec(gn.shape, lambda i: (0, 0)) for gn in gains]
    args = [x2, norm_g.reshape(1, d), scale, shift, w_in, *gains]
    if use_rope:
        vec = pl.BlockSpec((1, HEAD_DIM), lambda i: (0, 0))
        pos, freq, sign = rope
        in_specs += [pl.BlockSpec((None, tm // LANES, LANES), lambda i: (i, 0, 0)), vec, vec]
        args += [pos.reshape(n // tm, tm // LANES, LANES), freq, sign]
    return pl.pallas_call(
        functools.partial(_inproj_kernel, modes=tuple(modes), out_tiles=tuple(out_tiles),
                          gain_of=gain_of, n_gains=len(gains), post_scale=post_scale,
                          use_rope=use_rope),
        grid=(n // tm,),
        in_specs=in_specs,
        out_specs=pl.BlockSpec((tm, n_out * tn), lambda i: (i, 0)),
        out_shape=jax.ShapeDtypeStruct((n, n_out * tn), BF16),
        scratch_shapes=[pltpu.VMEM((tm, d), BF16),
                        pltpu.VMEM((d, ncols), BF16),
                        pltpu.VMEM((2, d, tn), F32),
                        pltpu.SemaphoreType.DMA((2,)),
                        pltpu.VMEM((tm, tn), F32)],
        compiler_params=pltpu.CompilerParams(
            dimension_semantics=("arbitrary",), vmem_limit_bytes=VMEM_LIMIT),
        name="in_projection",
    )(*args)


def _attn_kernel(q_ref, k_ref, v_ref, o_ref, u_scr, *, seq):
    hb, tq, win, nsub = ATT_HEADS_PER_STEP, ATT_Q, ATT_WIN, ATT_SUB
    chains = nsub * hb

    u_scr[...] = jnp.where(lax.broadcasted_iota(jnp.int32, (ATT_NEW, ATT_NEW), 0)
                           > lax.broadcasted_iota(jnp.int32, (ATT_NEW, ATT_NEW), 1),
                           1.0, 0.0).astype(BF16)
    old = win - ATT_NEW
    key_idx = lax.broadcasted_iota(jnp.int32, (tq, win), 1)
    row_idx = lax.broadcasted_iota(jnp.int32, (tq, win), 0)
    tail = win - LANES
    tail_ok = (lax.broadcasted_iota(jnp.int32, (tq, LANES), 1) + tail
               < lax.broadcasted_iota(jnp.int32, (tq, LANES), 0) + (win - tq))

    def head_cols(h):
        return slice(h * HEAD_DIM, (h + 1) * HEAD_DIM)

    def tile_logits(rows0, keys0, which):
        return [_dot_nt(q_ref[pl.ds(rows0[c // hb], tq), head_cols(c % hb)],
                        k_ref[pl.ds(keys0[c // hb], win), head_cols(c % hb)]) for c in which]

    def tile_weights(logits, rows0, keys0, limits, totals, which):
        log_beta, terms = [], []
        for z, c in zip(logits, which):
            a = c // hb
            if limits is None:
                z = jnp.concatenate(
                    [z[:, :tail], jnp.where(tail_ok, z[:, tail:], ATT_EXCLUDED)], axis=1)
            else:
                ok = key_idx < (jnp.minimum(rows0[a] + row_idx, limits[a]) - keys0[a])
                z = jnp.where(ok, z, ATT_EXCLUDED)
            nz = -z
            lsm = (jnp.minimum(nz, 0.0)
                   - jnp.log(1.0 + jnp.exp2(jnp.minimum(z, nz))) * LOG2E)
            log_beta.append(z + lsm)
            terms.append(lsm)
        sums_new = _dot(jnp.concatenate([t[:, old:].astype(BF16) for t in terms], axis=0),
                        u_scr[...])
        sums_old = _dot(jnp.concatenate([t[:, :old].astype(BF16) for t in terms], axis=0),
                        u_scr[0:old, 0:old])
        weights, tile_totals = [], []
        for n in range(len(which)):
            newer = sums_new[n * tq:(n + 1) * tq]
            older = sums_old[n * tq:(n + 1) * tq]
            total_new = newer[:, 0:1] + terms[n][:, old:old + 1]
            expo = log_beta[n] + jnp.concatenate([older + total_new, newer], axis=1)
            if totals is not None:
                expo = expo + totals[n]
            weights.append(jnp.exp2(expo).astype(BF16))
            tile_totals.append(total_new + older[:, 0:1] + terms[n][:, 0:1])
        return weights, tile_totals

    def tile_values(weights, keys0, which):
        return [_dot(w, v_ref[pl.ds(keys0[c // hb], win), head_cols(c % hb)])
                for w, c in zip(weights, which)]

    def slowest(totals):
        m = totals[0]
        for t in totals[1:]:
            m = jnp.maximum(m, t)
        return jnp.max(m)

    def qblock(i, carry, *, near_start):
        r0 = i * (tq * nsub)
        rows0 = [pl.multiple_of(r0 + a * tq, tq) for a in range(nsub)]
        if near_start:
            starts = [jnp.maximum(r - (win - tq), 0) for r in rows0]
        else:
            starts = [r - (win - tq) for r in rows0]
        keys0 = [pl.multiple_of(s, tq) for s in starts]
        everyone = list(range(chains))
        weights, totals = tile_weights(tile_logits(rows0, keys0, everyone), rows0, keys0,
                                       [ATT_NO_LIMIT] * nsub if near_start else None, None,
                                       everyone)
        accs = tile_values(weights, keys0, everyone)

        def more(st):
            m, worst, _, _ = st
            return (starts[-1] - m * win > 0) & (worst > ATT_SKIP_BELOW)

        def older(st):
            m, _, totals, accs = st
            limits = [s - m * win for s in starts]
            keys0 = [pl.multiple_of(jnp.maximum(lim - win, 0), tq) for lim in limits]
            new_totals, new_accs = [], []
            needs = [jnp.max(t) > ATT_SKIP_BELOW for t in totals]
            for c in range(chains):
                def add_tile(c=c):
                    weights, tile_totals = tile_weights(tile_logits(rows0, keys0, [c]), rows0,
                                                        keys0, limits, [totals[c]], [c])
                    out, = tile_values(weights, keys0, [c])
                    return totals[c] + tile_totals[0], accs[c] + out

                total, acc = lax.cond(needs[c], add_tile, lambda c=c: (totals[c], accs[c]))
                new_totals.append(total)
                new_accs.append(acc)
            return (m + 1, slowest(new_totals), tuple(new_totals), tuple(new_accs))

        _, _, _, accs = lax.while_loop(
            more, older, (jnp.int32(0), slowest(totals), tuple(totals), tuple(accs)))
        for a in range(nsub):
            for h in range(hb):
                o_ref[pl.ds(rows0[a], tq), head_cols(h)] = accs[a * hb + h].astype(BF16)
        return carry

    n_blocks = seq // (tq * nsub)
    n_clipped = min(-(-(win - tq) // (tq * nsub)), n_blocks)
    lax.fori_loop(0, n_clipped, functools.partial(qblock, near_start=True), 0)
    lax.fori_loop(n_clipped, n_blocks, functools.partial(qblock, near_start=False), 0)


def _stick_breaking(p, batch, seq, q_col, k_col, v_col):
    n = p.shape[0]
    width = ATT_HEADS_PER_STEP * HEAD_DIM
    n_hg = N_HEADS // ATT_HEADS_PER_STEP

    def spec(col):
        return pl.BlockSpec((seq, width), lambda b, hg: (b, col // width + hg))

    return pl.pallas_call(
        functools.partial(_attn_kernel, seq=seq),
        grid=(batch, n_hg),
        in_specs=[spec(q_col), spec(k_col), spec(v_col)],
        out_specs=pl.BlockSpec((seq, width), lambda b, hg: (b, hg)),
        out_shape=jax.ShapeDtypeStruct((n, N_HEADS * HEAD_DIM), BF16),
        scratch_shapes=[pltpu.VMEM((ATT_NEW, ATT_NEW), BF16)],
        compiler_params=pltpu.CompilerParams(
            dimension_semantics=("parallel", "parallel"), vmem_limit_bytes=VMEM_LIMIT),
        name="stick_breaking",
    )(p, p, p)


def _out_even_kernel(x_ref, gate_ref, ug_ref, vg_ref, zb_ref, bo_ref, ws_ref, bias_ref, wout_ref,
                     o_ref, y_scr):
    tm = x_ref.shape[0]
    a_width = bo_ref.shape[1]
    row = lax.broadcasted_iota(jnp.int32, (BLOCK, BLOCK), 0)
    col = lax.broadcasted_iota(jnp.int32, (BLOCK, BLOCK), 1)
    for g in range(a_width // HEAD_DIM):
        cols = slice(g * HEAD_DIM, (g + 1) * HEAD_DIM)
        wg = jnp.where(col <= row, ws_ref[g], 0.0).astype(BF16)
        for c in range(tm // BLOCK):
            rows = slice(c * BLOCK, (c + 1) * BLOCK)
            mixed = _dot(wg, vg_ref[rows, cols]) + bias_ref[:, cols]
            y_scr[rows, cols] = (ug_ref[rows, cols].astype(F32) * mixed).astype(BF16)
    y_scr[:, a_width:] = (bo_ref[...].astype(F32) * _silu(zb_ref[...].astype(F32))).astype(BF16)
    o_ref[...] = x_ref[...] + gate_ref[...] * _dot(y_scr[...], wout_ref[...])


def _out_even(x2, seq, gate, p, b_out, a_ws, bias_full, w_out_bf):
    n, d = x2.shape
    tm = OUT_EVEN_ROWS
    aw = b_out.shape[1]
    row = lambda c: pl.BlockSpec((tm, aw), lambda i: (i, c))
    const = lambda a: pl.BlockSpec(a.shape, lambda i: (0,) * a.ndim, pipeline_mode=pl.Buffered(1))
    return pl.pallas_call(
        _out_even_kernel,
        grid=(n // tm,),
        in_specs=[pl.BlockSpec((tm, d), lambda i: (i, 0)),
                  pl.BlockSpec((None, 1, d), lambda i: ((i * tm) // seq, 0, 0)),
                  row(4), row(0), row(5),
                  row(0),
                  const(a_ws), const(bias_full), const(w_out_bf)],
        out_specs=pl.BlockSpec((tm, d), lambda i: (i, 0)),
        out_shape=jax.ShapeDtypeStruct((n, d), F32),
        scratch_shapes=[pltpu.VMEM((tm, 2 * aw), BF16)],
        compiler_params=pltpu.CompilerParams(
            dimension_semantics=("parallel",), vmem_limit_bytes=VMEM_LIMIT),
        name="out_even",
    )(x2, gate, p, p, p, b_out, a_ws, bias_full, w_out_bf)


def _rope_operands(positions):
    half = HEAD_DIM // 2
    inv_freq = ROPE_BASE ** (-jnp.arange(half, dtype=F32) / half)
    freq = jnp.concatenate([inv_freq, inv_freq]).reshape(1, HEAD_DIM)
    sign = jnp.concatenate([-jnp.ones((half,), F32), jnp.ones((half,), F32)]).reshape(1, HEAD_DIM)
    return positions.reshape(-1, LANES).astype(F32), freq, sign


def _retention_tables():
    chunk = RET_CHUNK
    f32 = np.float32
    log_gamma = np.log1p(-np.exp2(-5.0 - np.arange(N_HEADS, dtype=f32))).astype(f32)
    idx = np.arange(chunk, dtype=f32)
    diff = idx[:, None] - idx[None, :]
    intra = np.where(diff >= 0, np.exp(log_gamma[:, None, None] * np.maximum(diff, 0.0)), 0.0)
    q_decay = np.exp(log_gamma[:, None] * (idx + 1.0))
    k_decay = np.exp(log_gamma[:, None] * (chunk - 1.0 - idx))
    chunk_decay = np.exp(log_gamma * chunk)
    tables = (intra,
              np.broadcast_to(q_decay[:, :, None], (N_HEADS, chunk, HEAD_DIM)),
              np.broadcast_to(k_decay[:, :, None], (N_HEADS, chunk, HEAD_DIM)),
              np.broadcast_to(chunk_decay[:, None, None], (N_HEADS, HEAD_DIM, HEAD_DIM)))
    return tuple(jnp.asarray(np.ascontiguousarray(t, dtype=f32)) for t in tables)


def _retention_rows(q_ref, k_ref, v_ref, idec_ref, qdec_ref, kdec_ref, cdec_ref, st_scr, emit):
    chunk = RET_CHUNK
    n_chunks = q_ref.shape[0] // chunk
    cols = [slice(h * HEAD_DIM, (h + 1) * HEAD_DIM) for h in range(N_HEADS)]
    rows = [slice(c * chunk, (c + 1) * chunk) for c in range(n_chunks)]

    for h0 in range(0, N_HEADS, RET_HEAD_GROUP):
        heads = range(h0, h0 + RET_HEAD_GROUP)
        kv = {}
        for c in range(n_chunks):
            for h in heads:
                k_decayed = (k_ref[rows[c], cols[h]].astype(F32) * kdec_ref[h]).astype(BF16)
                kv[h, c] = _dot_tn(k_decayed, v_ref[rows[c], cols[h]])
        state = {}
        for h in heads:
            s = st_scr[h]
            for c in range(n_chunks):
                state[h, c] = s.astype(BF16)
                s = s * cdec_ref[h] + kv[h, c]
            st_scr[h] = s
        for c in range(n_chunks):
            for h in heads:
                q = q_ref[rows[c], cols[h]]
                scores = _dot_nt(q, k_ref[rows[c], cols[h]]) * idec_ref[h]
                out = (_dot(scores.astype(BF16), v_ref[rows[c], cols[h]])
                       + _dot(q, state[h, c]) * qdec_ref[h])
                ms = jnp.mean(out * out, axis=-1, keepdims=True)
                emit(rows[c], h, out * lax.rsqrt(ms + EPS))


def _pooled_rows(pc_ref, halo_ref, cw_ref, cs_ref, ext_scr, seq_row0, emit):
    tm = pc_ref.shape[0]
    ext_scr[0:POOL_HALO, :] = jnp.where(seq_row0 == 0, jnp.zeros_like(halo_ref), halo_ref[...])
    ext_scr[POOL_HALO:, :] = pc_ref[...]
    src_rows = BLOCK + POOL_HALO
    lag = (lax.broadcasted_iota(jnp.int32, (BLOCK, src_rows), 0) + POOL_HALO
           - lax.broadcasted_iota(jnp.int32, (BLOCK, src_rows), 1))
    t_seq = seq_row0 + lax.broadcasted_iota(jnp.int32, (BLOCK, POOL_GROUP_DIM), 0)
    for gi, win in enumerate(POOL_WINDOWS):
        cols = slice(gi * POOL_GROUP_DIM, (gi + 1) * POOL_GROUP_DIM)
        band = jnp.where((lag >= 0) & (lag < win), 1.0 / win, 0.0).astype(BF16)
        short = jnp.where(t_seq + 1 < win, win / (t_seq + 1).astype(F32), 1.0)
        pooled = []
        for c in range(tm // BLOCK):
            mean = _dot(band, ext_scr[c * BLOCK:c * BLOCK + src_rows, cols])
            if c == 0:
                mean = mean * short
            cur = pc_ref[c * BLOCK:(c + 1) * BLOCK, cols].astype(F32)
            pooled.append((mean - cur).astype(BF16))
        emit(gi, _dot(jnp.concatenate(pooled, axis=0), cw_ref[gi]) * cs_ref[:, cols])


def _odd_mixers_kernel(p_ref, idec_ref, qdec_ref, kdec_ref, cdec_ref,
                       x_ref, gate_ref, halo_ref, cw_ref, cs_ref, wout_ref,
                       o_ref, st_scr, ext_scr, y_scr):
    tm = x_ref.shape[0]
    c_width = halo_ref.shape[1]
    pc_ref, q_ref, k_ref, v_ref, za_ref, zb_ref = (
        p_ref.at[:, pl.ds(j * c_width, c_width)] for j in range(6))

    @pl.when(pl.program_id(1) == 0)
    def _():
        st_scr[...] = jnp.zeros_like(st_scr)

    def gated_retention(rows, h, out):
        cols = slice(h * HEAD_DIM, (h + 1) * HEAD_DIM)
        ycols = slice(c_width + h * HEAD_DIM, c_width + (h + 1) * HEAD_DIM)
        y_scr[rows, ycols] = (out * _silu(zb_ref[rows, cols].astype(F32))).astype(BF16)

    def gated_pool(gi, mixed):
        cols = slice(gi * POOL_GROUP_DIM, (gi + 1) * POOL_GROUP_DIM)
        y_scr[:, cols] = (mixed * _silu(za_ref[:, cols].astype(F32))).astype(BF16)

    _retention_rows(q_ref, k_ref, v_ref, idec_ref, qdec_ref, kdec_ref, cdec_ref, st_scr,
                    gated_retention)
    _pooled_rows(pc_ref, halo_ref, cw_ref, cs_ref, ext_scr, pl.program_id(1) * tm, gated_pool)
    o_ref[...] = x_ref[...] + gate_ref[...] * _dot(y_scr[...], wout_ref[...])


def _odd_mixers(x2, batch, seq, gate, p, c_w_bf, c_scale, w_out_bf):
    n, d = x2.shape
    tm = ROW_TILE
    cw = N_HEADS * HEAD_DIM
    steps = seq // tm
    tables = _retention_tables()
    tabs = [pl.BlockSpec(t.shape, lambda b, s: (0, 0, 0)) for t in tables]
    halo_blocks = tm // POOL_HALO
    halo = pl.BlockSpec(
        (POOL_HALO, cw), lambda b, s: (jnp.maximum((b * steps + s) * halo_blocks - 1, 0), 0))
    return pl.pallas_call(
        _odd_mixers_kernel,
        grid=(batch, steps),
        in_specs=[pl.BlockSpec((tm, p.shape[1]), lambda b, s: (b * steps + s, 0))] + tabs + [
            pl.BlockSpec((tm, d), lambda b, s: (b * steps + s, 0)),
            pl.BlockSpec((None, 1, d), lambda b, s: (b, 0, 0)),
            halo,
            pl.BlockSpec(c_w_bf.shape, lambda b, s: (0, 0, 0)),
            pl.BlockSpec((1, cw), lambda b, s: (0, 0)),
            pl.BlockSpec(w_out_bf.shape, lambda b, s: (0, 0))],
        out_specs=pl.BlockSpec((tm, d), lambda b, s: (b * steps + s, 0)),
        out_shape=jax.ShapeDtypeStruct((n, d), F32),
        scratch_shapes=[pltpu.VMEM((N_HEADS, HEAD_DIM, HEAD_DIM), F32),
                        pltpu.VMEM((POOL_HALO + tm, cw), BF16),
                        pltpu.VMEM((tm, 2 * cw), BF16)],
        compiler_params=pltpu.CompilerParams(
            dimension_semantics=("parallel", "arbitrary"), vmem_limit_bytes=VMEM_LIMIT),
        name="odd_mixers",
    )(p, *tables, x2, gate, p, c_w_bf, c_scale.reshape(1, cw), w_out_bf)


def _even_layer(x2, batch, seq, mod, norm_g, w_in, a_vnorm_g, a_ws, a_bs,
                b_qnorm_g, b_knorm_g, w_out):
    d = x2.shape[1]
    shift, scale, gate = _split_modulation(mod)
    gains = [a_vnorm_g, b_qnorm_g.reshape(1, -1), b_knorm_g.reshape(1, -1)]
    modes = ("hold", "norm", "norm_scaled", "norm", "raw", "silu_gate", "raw")
    p = _in_projection(x2, seq, norm_g, scale, shift, w_in, gains, modes,
                       post_scale=LOG2E * HEAD_DIM ** -0.5)
    b_out = _stick_breaking(p, batch, seq, q_col=d, k_col=2 * d, v_col=3 * d)
    bias_full = jnp.repeat(a_bs.T, HEAD_DIM, axis=1)
    return _out_even(x2, seq, gate, p, b_out, a_ws, bias_full, w_out.astype(BF16))


def _odd_layer(x2, batch, seq, mod, positions, norm_g, w_in, c_w, c_scale,
               d_qnorm_g, d_knorm_g, w_out):
    shift, scale, gate = _split_modulation(mod)
    gains = [d_qnorm_g.reshape(1, -1), d_knorm_g.reshape(1, -1)]
    modes = ("raw", "rope", "rope_scaled", "raw", "raw", "raw")
    p = _in_projection(x2, seq, norm_g, scale, shift, w_in, gains, modes,
                       rope=_rope_operands(positions), post_scale=HEAD_DIM ** -0.5)
    return _odd_mixers(x2, batch, seq, gate, p, c_w.astype(BF16), c_scale, w_out.astype(BF16))


def kernel(x, c, positions, even_norm_g, even_w_mod, even_b_mod, even_w_in, even_a_vnorm_g, even_a_ws, even_a_bs, even_b_qnorm_g, even_b_knorm_g, even_w_out, odd_norm_g, odd_w_mod, odd_b_mod, odd_w_in, odd_c_w, odd_c_scale, odd_d_qnorm_g, odd_d_knorm_g, odd_w_out):
    batch, seq, d = x.shape
    depth = even_norm_g.shape[0] + odd_norm_g.shape[0]
    x2 = x.reshape(batch * seq, d)
    n_even = even_norm_g.shape[0]
    mods = _modulations(c, even_w_mod, even_b_mod, odd_w_mod, odd_b_mod)
    for layer in range(depth):
        i = layer // 2
        if layer % 2 == 0:
            x2 = _even_layer(x2, batch, seq, mods[i], even_norm_g[i],
                             even_w_in[i], even_a_vnorm_g[i], even_a_ws[i], even_a_bs[i],
                             even_b_qnorm_g[i], even_b_knorm_g[i], even_w_out[i])
        else:
            x2 = _odd_layer(x2, batch, seq, mods[n_even + i], positions, odd_norm_g[i],
                            odd_w_in[i], odd_c_w[i], odd_c_scale[i], odd_d_qnorm_g[i],
                            odd_d_knorm_g[i], odd_w_out[i])
    return x2.reshape(batch, seq, d)
```

```python
import functools

import jax
import jax.numpy as jnp
import numpy as np
from jax import lax
from jax.experimental import pallas as pl
from jax.experimental.pallas import tpu as pltpu

F32 = jnp.float32
BF16 = jnp.bfloat16

EPS = 1e-6
ROPE_BASE = 10000.0
BLOCK = 128
HEAD_DIM = 128
N_HEADS = 8
POOL_WINDOWS = (2, 4, 8, 16)
POOL_GROUP_DIM = 256
POOL_HALO = 16

MOD_ROWS = 8

ROW_TILE = 512
OUT_EVEN_ROWS = 1024
COL_TILE = 1024
RET_CHUNK = 128
RET_HEAD_GROUP = 2

ATT_Q = 128
ATT_WIN = 384
ATT_NEW = 256
ATT_EXCLUDED = -1e30
LANES = 128
ATT_SUB = 2
ATT_HEADS_PER_STEP = 4
ATT_SKIP_BELOW = -151.0
LOG2E = 1.4426950408889634
ATT_NO_LIMIT = 1 << 30

VMEM_LIMIT = 56 * 1024 * 1024


def _silu(v):
    return v * jax.nn.sigmoid(v)


def _dot(a, b):
    return jnp.dot(a, b, preferred_element_type=F32)


def _dot_nt(a, b):
    return lax.dot_general(a, b, (((1,), (1,)), ((), ())), preferred_element_type=F32)


def _dot_tn(a, b):
    return lax.dot_general(a, b, (((0,), (0,)), ((), ())), preferred_element_type=F32)


def _group_rms(a, gain):
    ms = jnp.mean(a * a, axis=-1, keepdims=True)
    return a * lax.rsqrt(ms + EPS) * gain


def _mod_kernel(c_ref, we_ref, be_ref, wo_ref, bo_ref, o_ref, *, n_even):
    a = _silu(c_ref[...]).astype(BF16)
    layer = pl.program_id(0)

    @pl.when(layer < n_even)
    def _():
        o_ref[...] = _dot(a, we_ref[...].astype(BF16)) + be_ref[...]

    @pl.when(layer >= n_even)
    def _():
        o_ref[...] = _dot(a, wo_ref[...].astype(BF16)) + bo_ref[...]


def _modulations(c, even_w, even_b, odd_w, odd_b):
    bsz, d = c.shape
    n_even, n_odd = even_w.shape[0], odd_w.shape[0]
    c8 = jnp.pad(c, ((0, MOD_ROWS - bsz), (0, 0)))
    even_at = lambda l, j: (jnp.minimum(l, n_even - 1), 0, jnp.where(l < n_even, j, 2))
    odd_at = lambda l, j: (jnp.maximum(l - n_even, 0), 0, jnp.where(l < n_even, 0, j))
    m = pl.pallas_call(
        functools.partial(_mod_kernel, n_even=n_even),
        grid=(n_even + n_odd, 3),
        in_specs=[pl.BlockSpec((MOD_ROWS, d), lambda l, j: (0, 0)),
                  pl.BlockSpec((None, d, d), even_at),
                  pl.BlockSpec((None, 1, d), even_at),
                  pl.BlockSpec((None, d, d), odd_at),
                  pl.BlockSpec((None, 1, d), odd_at)],
        out_specs=pl.BlockSpec((None, MOD_ROWS, d), lambda l, j: (l, 0, j)),
        out_shape=jax.ShapeDtypeStruct((n_even + n_odd, MOD_ROWS, 3 * d), F32),
        name="modulation",
    )(c8, even_w, even_b.reshape(n_even, 1, 3 * d), odd_w, odd_b.reshape(n_odd, 1, 3 * d))
    return m[:, :bsz]


def _split_modulation(m):
    d = m.shape[1] // 3
    return m[:, None, :d], m[:, None, d:2 * d], m[:, None, 2 * d:]


def _inproj_kernel(*refs, modes, out_tiles, gain_of, n_gains, post_scale, use_rope):
    *refs, h_scr, w_ref, stage, sem, held = refs
    x_ref, g_ref, sc_ref, sh_ref, w_hbm = refs[:5]
    gain_refs = refs[5:5 + n_gains]
    o_ref = refs[-1]
    tn = COL_TILE

    def weight_copy(j):
        return pltpu.make_async_copy(w_hbm.at[:, pl.ds(j * tn, tn)], stage.at[j % 2], sem.at[j % 2])

    @pl.when(pl.program_id(0) == 0)
    def _():
        weight_copy(0).start()
        for j in range(len(modes)):
            if j + 1 < len(modes):
                weight_copy(j + 1).start()
            weight_copy(j).wait()
            w_ref[:, j * tn:(j + 1) * tn] = stage[j % 2].astype(BF16)

    if use_rope:
        pos_ref, freq_ref, sign_ref = refs[5 + n_gains:8 + n_gains]
        on_diag = (lax.broadcasted_iota(jnp.int32, (LANES, LANES), 0)
                   == lax.broadcasted_iota(jnp.int32, (LANES, LANES), 1))
        angles = []
        for a in range(pos_ref.shape[0]):
            spread = jnp.where(on_diag, jnp.broadcast_to(pos_ref[a:a + 1, :], (LANES, LANES)), 0.0)
            angles.append(jnp.sum(spread, axis=1, keepdims=True) * freq_ref[...])
        ang = jnp.concatenate(angles, axis=0)
        cos_t = jnp.cos(ang)
        sin_t = jnp.sin(ang) * sign_ref[...]
    x = x_ref[...]
    ms = jnp.mean(x * x, axis=-1, keepdims=True)
    gain = g_ref[...] * (1.0 + sc_ref[...])
    h_scr[...] = (x * lax.rsqrt(ms + EPS) * gain + sh_ref[...]).astype(BF16)

    cross_lane = ("norm", "norm_scaled", "rope", "rope_scaled")
    heavy = [jj for jj, m in enumerate(modes) if m in cross_lane]
    plain = [jj for jj, m in enumerate(modes) if m not in cross_lane]
    order = []
    while heavy or plain:
        order += heavy[:1] + plain[:1]
        heavy, plain = heavy[1:], plain[1:]
    for jj in order:
        mode = modes[jj]
        out0 = None if out_tiles[jj] is None else out_tiles[jj] * tn
        acc = _dot(h_scr[...], w_ref[:, jj * tn:(jj + 1) * tn])
        if mode == "hold":
            held[...] = acc
            continue
        if mode == "raw":
            o_ref[:, out0:out0 + tn] = acc.astype(BF16)
            continue
        if mode == "silu_gate":
            o_ref[:, out0:out0 + tn] = (held[...] * _silu(acc)).astype(BF16)
            continue
        gain_ref = gain_refs[gain_of[jj]]
        for g in range(tn // HEAD_DIM):
            row = g if gain_ref.shape[0] > 1 else 0
            t = _group_rms(acc[:, g * HEAD_DIM:(g + 1) * HEAD_DIM], gain_ref[row:row + 1, :])
            if mode in ("rope", "rope_scaled"):
                t = t * cos_t + pltpu.roll(t, HEAD_DIM // 2, axis=1) * sin_t
            if mode in ("norm_scaled", "rope_scaled"):
                t = t * post_scale
            o_ref[:, out0 + g * HEAD_DIM:out0 + (g + 1) * HEAD_DIM] = t.astype(BF16)


def _in_projection(x2, seq, norm_g, scale, shift, w_in, gains, modes, rope=None, post_scale=1.0):
    n, d = x2.shape
    tm, tn = ROW_TILE, COL_TILE
    ncols = len(modes) * tn
    out_tiles, n_out = [], 0
    for m in modes:
        out_tiles.append(None if m == "hold" else n_out)
        n_out += m != "hold"
    assert modes.count("hold") == modes.count("silu_gate") <= 1
    assert "hold" not in modes or modes.index("hold") < modes.index("silu_gate")
    use_rope = rope is not None
    normalising = [jj for jj, m in enumerate(modes) if m in ("norm", "norm_scaled", "rope", "rope_scaled")]
    assert len(gains) == len(normalising)
    gain_of = tuple(normalising.index(jj) if jj in normalising else None for jj in range(len(modes)))
    in_specs = [
        pl.BlockSpec((tm, d), lambda i: (i, 0)),
        pl.BlockSpec((1, d), lambda i: (0, 0)),
        pl.BlockSpec((None, 1, d), lambda i: ((i * tm) // seq, 0, 0)),
        pl.BlockSpec((None, 1, d), lambda i: ((i * tm) // seq, 0, 0)),
        pl.BlockSpec(memory_space=pl.ANY),
    ] + [pl.BlockSpec(gn.shape, lambda i: (0, 0)) for gn in gains]
    args = [x2, norm_g.reshape(1, d), scale, shift, w_in, *gains]
    if use_rope:
        vec = pl.BlockSpec((1, HEAD_DIM), lambda i: (0, 0))
        pos, freq, sign = rope
        in_specs += [pl.BlockSpec((None, tm // LANES, LANES), lambda i: (i, 0, 0)), vec, vec]
        args += [pos.reshape(n // tm, tm // LANES, LANES), freq, sign]
    return pl.pallas_call(
        functools.partial(_inproj_kernel, modes=tuple(modes), out_tiles=tuple(out_tiles),
                          gain_of=gain_of, n_gains=len(gains), post_scale=post_scale,
                          use_rope=use_rope),
        grid=(n // tm,),
        in_specs=in_specs,
        out_specs=pl.BlockSpec((tm, n_out * tn), lambda i: (i, 0)),
        out_shape=jax.ShapeDtypeStruct((n, n_out * tn), BF16),
        scratch_shapes=[pltpu.VMEM((tm, d), BF16),
                        pltpu.VMEM((d, ncols), BF16),
                        pltpu.VMEM((2, d, tn), F32),
                        pltpu.SemaphoreType.DMA((2,)),
                        pltpu.VMEM((tm, tn), F32)],
        compiler_params=pltpu.CompilerParams(
            dimension_semantics=("arbitrary",), vmem_limit_bytes=VMEM_LIMIT),
        name="in_projection",
    )(*args)


def _attn_kernel(q_ref, k_ref, v_ref, o_ref, u_scr, *, seq):
    hb, tq, win, nsub = ATT_HEADS_PER_STEP, ATT_Q, ATT_WIN, ATT_SUB
    chains = nsub * hb

    u_scr[...] = jnp.where(lax.broadcasted_iota(jnp.int32, (ATT_NEW, ATT_NEW), 0)
                           > lax.broadcasted_iota(jnp.int32, (ATT_NEW, ATT_NEW), 1),
                           1.0, 0.0).astype(BF16)
    old = win - ATT_NEW
    key_idx = lax.broadcasted_iota(jnp.int32, (tq, win), 1)
    row_idx = lax.broadcasted_iota(jnp.int32, (tq, win), 0)
    tail = win - LANES
    tail_ok = (lax.broadcasted_iota(jnp.int32, (tq, LANES), 1) + tail
               < lax.broadcasted_iota(jnp.int32, (tq, LANES), 0) + (win - tq))

    def head_cols(h):
        return slice(h * HEAD_DIM, (h + 1) * HEAD_DIM)

    def tile_logits(rows0, keys0, which):
        return [_dot_nt(q_ref[pl.ds(rows0[c // hb], tq), head_cols(c % hb)],
                        k_ref[pl.ds(keys0[c // hb], win), head_cols(c % hb)]) for c in which]

    def tile_weights(logits, rows0, keys0, limits, totals, which):
        log_beta, terms = [], []
        for z, c in zip(logits, which):
            a = c // hb
            if limits is None:
                z = jnp.concatenate(
                    [z[:, :tail], jnp.where(tail_ok, z[:, tail:], ATT_EXCLUDED)], axis=1)
            else:
                ok = key_idx < (jnp.minimum(rows0[a] + row_idx, limits[a]) - keys0[a])
                z = jnp.where(ok, z, ATT_EXCLUDED)
            nz = -z
            lsm = (jnp.minimum(nz, 0.0)
                   - jnp.log(1.0 + jnp.exp2(jnp.minimum(z, nz))) * LOG2E)
            log_beta.append(z + lsm)
            terms.append(lsm)
        sums_new = _dot(jnp.concatenate([t[:, old:].astype(BF16) for t in terms], axis=0),
                        u_scr[...])
        sums_old = _dot(jnp.concatenate([t[:, :old].astype(BF16) for t in terms], axis=0),
                        u_scr[0:old, 0:old])
        weights, tile_totals = [], []
        for n in range(len(which)):
            newer = sums_new[n * tq:(n + 1) * tq]
            older = sums_old[n * tq:(n + 1) * tq]
            total_new = newer[:, 0:1] + terms[n][:, old:old + 1]
            expo = log_beta[n] + jnp.concatenate([older + total_new, newer], axis=1)
            if totals is not None:
                expo = expo + totals[n]
            weights.append(jnp.exp2(expo).astype(BF16))
            tile_totals.append(total_new + older[:, 0:1] + terms[n][:, 0:1])
        return weights, tile_totals

    def tile_values(weights, keys0, which):
        return [_dot(w, v_ref[pl.ds(keys0[c // hb], win), head_cols(c % hb)])
                for w, c in zip(weights, which)]

    def slowest(totals):
        m = totals[0]
        for t in totals[1:]:
            m = jnp.maximum(m, t)
        return jnp.max(m)

    def qblock(i, carry, *, near_start):
        r0 = i * (tq * nsub)
        rows0 = [pl.multiple_of(r0 + a * tq, tq) for a in range(nsub)]
        if near_start:
            starts = [jnp.maximum(r - (win - tq), 0) for r in rows0]
        else:
            starts = [r - (win - tq) for r in rows0]
        keys0 = [pl.multiple_of(s, tq) for s in starts]
        everyone = list(range(chains))
        weights, totals = tile_weights(tile_logits(rows0, keys0, everyone), rows0, keys0,
                                       [ATT_NO_LIMIT] * nsub if near_start else None, None,
                                       everyone)
        accs = tile_values(weights, keys0, everyone)

        def more(st):
            m, worst, _, _ = st
            return (starts[-1] - m * win > 0) & (worst > ATT_SKIP_BELOW)

        def older(st):
            m, _, totals, accs = st
            limits = [s - m * win for s in starts]
            keys0 = [pl.multiple_of(jnp.maximum(lim - win, 0), tq) for lim in limits]
            new_totals, new_accs = [], []
            needs = [jnp.max(t) > ATT_SKIP_BELOW for t in totals]
            for c in range(chains):
                def add_tile(c=c):
                    weights, tile_totals = tile_weights(tile_logits(rows0, keys0, [c]), rows0,
                                                        keys0, limits, [totals[c]], [c])
                    out, = tile_values(weights, keys0, [c])
                    return totals[c] + tile_totals[0], accs[c] + out

                total, acc = lax.cond(needs[c], add_tile, lambda c=c: (totals[c], accs[c]))
                new_totals.append(total)
                new_accs.append(acc)
            return (m + 1, slowest(new_totals), tuple(new_totals), tuple(new_accs))

        _, _, _, accs = lax.while_loop(
            more, older, (jnp.int32(0), slowest(totals), tuple(totals), tuple(accs)))
        for a in range(nsub):
            for h in range(hb):
                o_ref[pl.ds(rows0[a], tq), head_cols(h)] = accs[a * hb + h].astype(BF16)
        return carry

    n_blocks = seq // (tq * nsub)
    n_clipped = min(-(-(win - tq) // (tq * nsub)), n_blocks)
    lax.fori_loop(0, n_clipped, functools.partial(qblock, near_start=True), 0)
    lax.fori_loop(n_clipped, n_blocks, functools.partial(qblock, near_start=False), 0)


def _stick_breaking(p, batch, seq, q_col, k_col, v_col):
    n = p.shape[0]
    width = ATT_HEADS_PER_STEP * HEAD_DIM
    n_hg = N_HEADS // ATT_HEADS_PER_STEP

    def spec(col):
        return pl.BlockSpec((seq, width), lambda b, hg: (b, col // width + hg))

    return pl.pallas_call(
        functools.partial(_attn_kernel, seq=seq),
        grid=(batch, n_hg),
        in_specs=[spec(q_col), spec(k_col), spec(v_col)],
        out_specs=pl.BlockSpec((seq, width), lambda b, hg: (b, hg)),
        out_shape=jax.ShapeDtypeStruct((n, N_HEADS * HEAD_DIM), BF16),
        scratch_shapes=[pltpu.VMEM((ATT_NEW, ATT_NEW), BF16)],
        compiler_params=pltpu.CompilerParams(
            dimension_semantics=("parallel", "parallel"), vmem_limit_bytes=VMEM_LIMIT),
        name="stick_breaking",
    )(p, p, p)


def _out_even_kernel(x_ref, gate_ref, ug_ref, vg_ref, zb_ref, bo_ref, ws_ref, bias_ref, wout_ref,
                     o_ref, y_scr):
    tm = x_ref.shape[0]
    a_width = bo_ref.shape[1]
    row = lax.broadcasted_iota(jnp.int32, (BLOCK, BLOCK), 0)
    col = lax.broadcasted_iota(jnp.int32, (BLOCK, BLOCK), 1)
    for g in range(a_width // HEAD_DIM):
        cols = slice(g * HEAD_DIM, (g + 1) * HEAD_DIM)
        wg = jnp.where(col <= row, ws_ref[g], 0.0).astype(BF16)
        for c in range(tm // BLOCK):
            rows = slice(c * BLOCK, (c + 1) * BLOCK)
            mixed = _dot(wg, vg_ref[rows, cols]) + bias_ref[:, cols]
            y_scr[rows, cols] = (ug_ref[rows, cols].astype(F32) * mixed).astype(BF16)
    y_scr[:, a_width:] = (bo_ref[...].astype(F32) * _silu(zb_ref[...].astype(F32))).astype(BF16)
    o_ref[...] = x_ref[...] + gate_ref[...] * _dot(y_scr[...], wout_ref[...])


def _out_even(x2, seq, gate, p, b_out, a_ws, bias_full, w_out_bf):
    n, d = x2.shape
    tm = OUT_EVEN_ROWS
    aw = b_out.shape[1]
    row = lambda c: pl.BlockSpec((tm, aw), lambda i: (i, c))
    const = lambda a: pl.BlockSpec(a.shape, lambda i: (0,) * a.ndim, pipeline_mode=pl.Buffered(1))
    return pl.pallas_call(
        _out_even_kernel,
        grid=(n // tm,),
        in_specs=[pl.BlockSpec((tm, d), lambda i: (i, 0)),
                  pl.BlockSpec((None, 1, d), lambda i: ((i * tm) // seq, 0, 0)),
                  row(4), row(0), row(5),
                  row(0),
                  const(a_ws), const(bias_full), const(w_out_bf)],
        out_specs=pl.BlockSpec((tm, d), lambda i: (i, 0)),
        out_shape=jax.ShapeDtypeStruct((n, d), F32),
        scratch_shapes=[pltpu.VMEM((tm, 2 * aw), BF16)],
        compiler_params=pltpu.CompilerParams(
            dimension_semantics=("parallel",), vmem_limit_bytes=VMEM_LIMIT),
        name="out_even",
    )(x2, gate, p, p, p, b_out, a_ws, bias_full, w_out_bf)


def _rope_operands(positions):
    half = HEAD_DIM // 2
    inv_freq = ROPE_BASE ** (-jnp.arange(half, dtype=F32) / half)
    freq = jnp.concatenate([inv_freq, inv_freq]).reshape(1, HEAD_DIM)
    sign = jnp.concatenate([-jnp.ones((half,), F32), jnp.ones((half,), F32)]).reshape(1, HEAD_DIM)
    return positions.reshape(-1, LANES).astype(F32), freq, sign


def _retention_tables():
    chunk = RET_CHUNK
    f32 = np.float32
    log_gamma = np.log1p(-np.exp2(-5.0 - np.arange(N_HEADS, dtype=f32))).astype(f32)
    idx = np.arange(chunk, dtype=f32)
    diff = idx[:, None] - idx[None, :]
    intra = np.where(diff >= 0, np.exp(log_gamma[:, None, None] * np.maximum(diff, 0.0)), 0.0)
    q_decay = np.exp(log_gamma[:, None] * (idx + 1.0))
    k_decay = np.exp(log_gamma[:, None] * (chunk - 1.0 - idx))
    chunk_decay = np.exp(log_gamma * chunk)
    tables = (intra,
              np.broadcast_to(q_decay[:, :, None], (N_HEADS, chunk, HEAD_DIM)),
              np.broadcast_to(k_decay[:, :, None], (N_HEADS, chunk, HEAD_DIM)),
              np.broadcast_to(chunk_decay[:, None, None], (N_HEADS, HEAD_DIM, HEAD_DIM)))
    return tuple(jnp.asarray(np.ascontiguousarray(t, dtype=f32)) for t in tables)


def _retention_rows(q_ref, k_ref, v_ref, idec_ref, qdec_ref, kdec_ref, cdec_ref, st_scr, emit):
    chunk = RET_CHUNK
    n_chunks = q_ref.shape[0] // chunk
    cols = [slice(h * HEAD_DIM, (h + 1) * HEAD_DIM) for h in range(N_HEADS)]
    rows = [slice(c * chunk, (c + 1) * chunk) for c in range(n_chunks)]

    for h0 in range(0, N_HEADS, RET_HEAD_GROUP):
        heads = range(h0, h0 + RET_HEAD_GROUP)
        kv = {}
        for c in range(n_chunks):
            for h in heads:
                k_decayed = (k_ref[rows[c], cols[h]].astype(F32) * kdec_ref[h]).astype(BF16)
                kv[h, c] = _dot_tn(k_decayed, v_ref[rows[c], cols[h]])
        state = {}
        for h in heads:
            s = st_scr[h]
            for c in range(n_chunks):
                state[h, c] = s.astype(BF16)
                s = s * cdec_ref[h] + kv[h, c]
            st_scr[h] = s
        for c in range(n_chunks):
            for h in heads:
                q = q_ref[rows[c], cols[h]]
                scores = _dot_nt(q, k_ref[rows[c], cols[h]]) * idec_ref[h]
                out = (_dot(scores.astype(BF16), v_ref[rows[c], cols[h]])
                       + _dot(q, state[h, c]) * qdec_ref[h])
                ms = jnp.mean(out * out, axis=-1, keepdims=True)
                emit(rows[c], h, out * lax.rsqrt(ms + EPS))


def _pooled_rows(pc_ref, halo_ref, cw_ref, cs_ref, ext_scr, seq_row0, emit):
    tm = pc_ref.shape[0]
    ext_scr[0:POOL_HALO, :] = jnp.where(seq_row0 == 0, jnp.zeros_like(halo_ref), halo_ref[...])
    ext_scr[POOL_HALO:, :] = pc_ref[...]
    src_rows = BLOCK + POOL_HALO
    lag = (lax.broadcasted_iota(jnp.int32, (BLOCK, src_rows), 0) + POOL_HALO
           - lax.broadcasted_iota(jnp.int32, (BLOCK, src_rows), 1))
    t_seq = seq_row0 + lax.broadcasted_iota(jnp.int32, (BLOCK, POOL_GROUP_DIM), 0)
    for gi, win in enumerate(POOL_WINDOWS):
        cols = slice(gi * POOL_GROUP_DIM, (gi + 1) * POOL_GROUP_DIM)
        band = jnp.where((lag >= 0) & (lag < win), 1.0 / win, 0.0).astype(BF16)
        short = jnp.where(t_seq + 1 < win, win / (t_seq + 1).astype(F32), 1.0)
        pooled = []
        for c in range(tm // BLOCK):
            mean = _dot(band, ext_scr[c * BLOCK:c * BLOCK + src_rows, cols])
            if c == 0:
                mean = mean * short
            cur = pc_ref[c * BLOCK:(c + 1) * BLOCK, cols].astype(F32)
            pooled.append((mean - cur).astype(BF16))
        emit(gi, _dot(jnp.concatenate(pooled, axis=0), cw_ref[gi]) * cs_ref[:, cols])


def _odd_mixers_kernel(p_ref, idec_ref, qdec_ref, kdec_ref, cdec_ref,
                       x_ref, gate_ref, halo_ref, cw_ref, cs_ref, wout_ref,
                       o_ref, st_scr, ext_scr, y_scr):
    tm = x_ref.shape[0]
    c_width = halo_ref.shape[1]
    pc_ref, q_ref, k_ref, v_ref, za_ref, zb_ref = (
        p_ref.at[:, pl.ds(j * c_width, c_width)] for j in range(6))

    @pl.when(pl.program_id(1) == 0)
    def _():
        st_scr[...] = jnp.zeros_like(st_scr)

    def gated_retention(rows, h, out):
        cols = slice(h * HEAD_DIM, (h + 1) * HEAD_DIM)
        ycols = slice(c_width + h * HEAD_DIM, c_width + (h + 1) * HEAD_DIM)
        y_scr[rows, ycols] = (out * _silu(zb_ref[rows, cols].astype(F32))).astype(BF16)

    def gated_pool(gi, mixed):
        cols = slice(gi * POOL_GROUP_DIM, (gi + 1) * POOL_GROUP_DIM)
        y_scr[:, cols] = (mixed * _silu(za_ref[:, cols].astype(F32))).astype(BF16)

    _retention_rows(q_ref, k_ref, v_ref, idec_ref, qdec_ref, kdec_ref, cdec_ref, st_scr,
                    gated_retention)
    _pooled_rows(pc_ref, halo_ref, cw_ref, cs_ref, ext_scr, pl.program_id(1) * tm, gated_pool)
    o_ref[...] = x_ref[...] + gate_ref[...] * _dot(y_scr[...], wout_ref[...])


def _odd_mixers(x2, batch, seq, gate, p, c_w_bf, c_scale, w_out_bf):
    n, d = x2.shape
    tm = ROW_TILE
    cw = N_HEADS * HEAD_DIM
    steps = seq // tm
    tables = _retention_tables()
    tabs = [pl.BlockSpec(t.shape, lambda b, s: (0, 0, 0)) for t in tables]
    halo_blocks = tm // POOL_HALO
    halo = pl.BlockSpec(
        (POOL_HALO, cw), lambda b, s: (jnp.maximum((b * steps + s) * halo_blocks - 1, 0), 0))
    return pl.pallas_call(
        _odd_mixers_kernel,
        grid=(batch, steps),
        in_specs=[pl.BlockSpec((tm, p.shape[1]), lambda b, s: (b * steps + s, 0))] + tabs + [
            pl.BlockSpec((tm, d), lambda b, s: (b * steps + s, 0)),
            pl.BlockSpec((None, 1, d), lambda b, s: (b, 0, 0)),
            halo,
            pl.BlockSpec(c_w_bf.shape, lambda b, s: (0, 0, 0)),
            pl.BlockSpec((1, cw), lambda b, s: (0, 0)),
            pl.BlockSpec(w_out_bf.shape, lambda b, s: (0, 0))],
        out_specs=pl.BlockSpec((tm, d), lambda b, s: (b * steps + s, 0)),
        out_shape=jax.ShapeDtypeStruct((n, d), F32),
        scratch_shapes=[pltpu.VMEM((N_HEADS, HEAD_DIM, HEAD_DIM), F32),
                        pltpu.VMEM((POOL_HALO + tm, cw), BF16),
                        pltpu.VMEM((tm, 2 * cw), BF16)],
        compiler_params=pltpu.CompilerParams(
            dimension_semantics=("parallel", "arbitrary"), vmem_limit_bytes=VMEM_LIMIT),
        name="odd_mixers",
    )(p, *tables, x2, gate, p, c_w_bf, c_scale.reshape(1, cw), w_out_bf)


def _even_layer(x2, batch, seq, mod, norm_g, w_in, a_vnorm_g, a_ws, a_bs,
                b_qnorm_g, b_knorm_g, w_out):
    d = x2.shape[1]
    shift, scale, gate = _split_modulation(mod)
    gains = [a_vnorm_g, b_qnorm_g.reshape(1, -1), b_knorm_g.reshape(1, -1)]
    modes = ("hold", "norm", "norm_scaled", "norm", "raw", "silu_gate", "raw")
    p = _in_projection(x2, seq, norm_g, scale, shift, w_in, gains, modes,
                       post_scale=LOG2E * HEAD_DIM ** -0.5)
    b_out = _stick_breaking(p, batch, seq, q_col=d, k_col=2 * d, v_col=3 * d)
    bias_full = jnp.repeat(a_bs.T, HEAD_DIM, axis=1)
    return _out_even(x2, seq, gate, p, b_out, a_ws, bias_full, w_out.astype(BF16))


def _odd_layer(x2, batch, seq, mod, positions, norm_g, w_in, c_w, c_scale,
               d_qnorm_g, d_knorm_g, w_out):
    shift, scale, gate = _split_modulation(mod)
    gains = [d_qnorm_g.reshape(1, -1), d_knorm_g.reshape(1, -1)]
    modes = ("raw", "rope", "rope_scaled", "raw", "raw", "raw")
    p = _in_projection(x2, seq, norm_g, scale, shift, w_in, gains, modes,
                       rope=_rope_operands(positions), post_scale=HEAD_DIM ** -0.5)
    return _odd_mixers(x2, batch, seq, gate, p, c_w.astype(BF16), c_scale, w_out.astype(BF16))


def kernel(x, c, positions, even_norm_g, even_w_mod, even_b_mod, even_w_in, even_a_vnorm_g, even_a_ws, even_a_bs, even_b_qnorm_g, even_b_knorm_g, even_w_out, odd_norm_g, odd_w_mod, odd_b_mod, odd_w_in, odd_c_w, odd_c_scale, odd_d_qnorm_g, odd_d_knorm_g, odd_w_out):
    batch, seq, d = x.shape
    depth = even_norm_g.shape[0] + odd_norm_g.shape[0]
    x2 = x.reshape(batch * seq, d)
    n_even = even_norm_g.shape[0]
    mods = _modulations(c, even_w_mod, even_b_mod, odd_w_mod, odd_b_mod)
    for layer in range(depth):
        i = layer // 2
        if layer % 2 == 0:
            x2 = _even_layer(x2, batch, seq, mods[i], even_norm_g[i],
                             even_w_in[i], even_a_vnorm_g[i], even_a_ws[i], even_a_bs[i],
                             even_b_qnorm_g[i], even_b_knorm_g[i], even_w_out[i])
        else:
            x2 = _odd_layer(x2, batch, seq, mods[n_even + i], positions, odd_norm_g[i],
                            odd_w_in[i], odd_c_w[i], odd_c_scale[i], odd_d_qnorm_g[i],
                            odd_d_knorm_g[i], odd_w_out[i])
    return x2.reshape(batch, seq, d)
```

```python
import functools

import jax
import jax.numpy as jnp
import numpy as np
from jax import lax
from jax.experimental import pallas as pl
from jax.experimental.pallas import tpu as pltpu

F32 = jnp.float32
BF16 = jnp.bfloat16

EPS = 1e-6
ROPE_BASE = 10000.0
BLOCK = 128
HEAD_DIM = 128
N_HEADS = 8
POOL_WINDOWS = (2, 4, 8, 16)
POOL_GROUP_DIM = 256
POOL_HALO = 16

MOD_ROWS = 8

ROW_TILE = 512
OUT_EVEN_ROWS = 1024
COL_TILE = 1024
RET_CHUNK = 128
RET_HEAD_GROUP = 2

ATT_Q = 128
ATT_WIN = 384
ATT_NEW = 256
ATT_EXCLUDED = -1e30
LANES = 128
ATT_SUB = 2
ATT_HEADS_PER_STEP = 4
ATT_SKIP_BELOW = -151.0
LOG2E = 1.4426950408889634
ATT_NO_LIMIT = 1 << 30

VMEM_LIMIT = 56 * 1024 * 1024


def _silu(v):
    return v * jax.nn.sigmoid(v)


def _dot(a, b):
    return jnp.dot(a, b, preferred_element_type=F32)


def _dot_nt(a, b):
    return lax.dot_general(a, b, (((1,), (1,)), ((), ())), preferred_element_type=F32)


def _dot_tn(a, b):
    return lax.dot_general(a, b, (((0,), (0,)), ((), ())), preferred_element_type=F32)


def _group_rms(a, gain):
    ms = jnp.mean(a * a, axis=-1, keepdims=True)
    return a * lax.rsqrt(ms + EPS) * gain


def _mod_kernel(c_ref, we_ref, be_ref, wo_ref, bo_ref, o_ref, *, n_even):
    a = _silu(c_ref[...]).astype(BF16)
    layer = pl.program_id(0)

    @pl.when(layer < n_even)
    def _():
        o_ref[...] = _dot(a, we_ref[...].astype(BF16)) + be_ref[...]

    @pl.when(layer >= n_even)
    def _():
        o_ref[...] = _dot(a, wo_ref[...].astype(BF16)) + bo_ref[...]


def _modulations(c, even_w, even_b, odd_w, odd_b):
    bsz, d = c.shape
    n_even, n_odd = even_w.shape[0], odd_w.shape[0]
    c8 = jnp.pad(c, ((0, MOD_ROWS - bsz), (0, 0)))
    even_at = lambda l, j: (jnp.minimum(l, n_even - 1), 0, jnp.where(l < n_even, j, 2))
    odd_at = lambda l, j: (jnp.maximum(l - n_even, 0), 0, jnp.where(l < n_even, 0, j))
    m = pl.pallas_call(
        functools.partial(_mod_kernel, n_even=n_even),
        grid=(n_even + n_odd, 3),
        in_specs=[pl.BlockSpec((MOD_ROWS, d), lambda l, j: (0, 0)),
                  pl.BlockSpec((None, d, d), even_at),
                  pl.BlockSpec((None, 1, d), even_at),
                  pl.BlockSpec((None, d, d), odd_at),
                  pl.BlockSpec((None, 1, d), odd_at)],
        out_specs=pl.BlockSpec((None, MOD_ROWS, d), lambda l, j: (l, 0, j)),
        out_shape=jax.ShapeDtypeStruct((n_even + n_odd, MOD_ROWS, 3 * d), F32),
        name="modulation",
    )(c8, even_w, even_b.reshape(n_even, 1, 3 * d), odd_w, odd_b.reshape(n_odd, 1, 3 * d))
    return m[:, :bsz]


def _split_modulation(m):
    d = m.shape[1] // 3
    return m[:, None, :d], m[:, None, d:2 * d], m[:, None, 2 * d:]


def _inproj_kernel(*refs, modes, out_tiles, gain_of, n_gains, post_scale, use_rope):
    *refs, h_scr, w_ref, stage, sem, held = refs
    x_ref, g_ref, sc_ref, sh_ref, w_hbm = refs[:5]
    gain_refs = refs[5:5 + n_gains]
    o_ref = refs[-1]
    tn = COL_TILE

    def weight_copy(j):
        return pltpu.make_async_copy(w_hbm.at[:, pl.ds(j * tn, tn)], stage.at[j % 2], sem.at[j % 2])

    @pl.when(pl.program_id(0) == 0)
    def _():
        weight_copy(0).start()
        for j in range(len(modes)):
            if j + 1 < len(modes):
                weight_copy(j + 1).start()
            weight_copy(j).wait()
            w_ref[:, j * tn:(j + 1) * tn] = stage[j % 2].astype(BF16)

    if use_rope:
        pos_ref, freq_ref, sign_ref = refs[5 + n_gains:8 + n_gains]
        on_diag = (lax.broadcasted_iota(jnp.int32, (LANES, LANES), 0)
                   == lax.broadcasted_iota(jnp.int32, (LANES, LANES), 1))
        angles = []
        for a in range(pos_ref.shape[0]):
            spread = jnp.where(on_diag, jnp.broadcast_to(pos_ref[a:a + 1, :], (LANES, LANES)), 0.0)
            angles.append(jnp.sum(spread, axis=1, keepdims=True) * freq_ref[...])
        ang = jnp.concatenate(angles, axis=0)
        cos_t = jnp.cos(ang)
        sin_t = jnp.sin(ang) * sign_ref[...]
    x = x_ref[...]
    ms = jnp.mean(x * x, axis=-1, keepdims=True)
    gain = g_ref[...] * (1.0 + sc_ref[...])
    h_scr[...] = (x * lax.rsqrt(ms + EPS) * gain + sh_ref[...]).astype(BF16)

    cross_lane = ("norm", "norm_scaled", "rope", "rope_scaled")
    heavy = [jj for jj, m in enumerate(modes) if m in cross_lane]
    plain = [jj for jj, m in enumerate(modes) if m not in cross_lane]
    order = []
    while heavy or plain:
        order += heavy[:1] + plain[:1]
        heavy, plain = heavy[1:], plain[1:]
    for jj in order:
        mode = modes[jj]
        out0 = None if out_tiles[jj] is None else out_tiles[jj] * tn
        acc = _dot(h_scr[...], w_ref[:, jj * tn:(jj + 1) * tn])
        if mode == "hold":
            held[...] = acc
            continue
        if mode == "raw":
            o_ref[:, out0:out0 + tn] = acc.astype(BF16)
            continue
        if mode == "silu_gate":
            o_ref[:, out0:out0 + tn] = (held[...] * _silu(acc)).astype(BF16)
            continue
        gain_ref = gain_refs[gain_of[jj]]
        for g in range(tn // HEAD_DIM):
            row = g if gain_ref.shape[0] > 1 else 0
            t = _group_rms(acc[:, g * HEAD_DIM:(g + 1) * HEAD_DIM], gain_ref[row:row + 1, :])
            if mode in ("rope", "rope_scaled"):
                t = t * cos_t + pltpu.roll(t, HEAD_DIM // 2, axis=1) * sin_t
            if mode in ("norm_scaled", "rope_scaled"):
                t = t * post_scale
            o_ref[:, out0 + g * HEAD_DIM:out0 + (g + 1) * HEAD_DIM] = t.astype(BF16)


def _in_projection(x2, seq, norm_g, scale, shift, w_in, gains, modes, rope=None, post_scale=1.0):
    n, d = x2.shape
    tm, tn = ROW_TILE, COL_TILE
    ncols = len(modes) * tn
    out_tiles, n_out = [], 0
    for m in modes:
        out_tiles.append(None if m == "hold" else n_out)
        n_out += m != "hold"
    assert modes.count("hold") == modes.count("silu_gate") <= 1
    assert "hold" not in modes or modes.index("hold") < modes.index("silu_gate")
    use_rope = rope is not None
    normalising = [jj for jj, m in enumerate(modes) if m in ("norm", "norm_scaled", "rope", "rope_scaled")]
    assert len(gains) == len(normalising)
    gain_of = tuple(normalising.index(jj) if jj in normalising else None for jj in range(len(modes)))
    in_specs = [
        pl.BlockSpec((tm, d), lambda i: (i, 0)),
        pl.BlockSpec((1, d), lambda i: (0, 0)),
        pl.BlockSpec((None, 1, d), lambda i: ((i * tm) // seq, 0, 0)),
        pl.BlockSpec((None, 1, d), lambda i: ((i * tm) // seq, 0, 0)),
        pl.BlockSpec(memory_space=pl.ANY),
    ] + [pl.BlockSpec(gn.shape, lambda i: (0, 0)) for gn in gains]
    args = [x2, norm_g.reshape(1, d), scale, shift, w_in, *gains]
    if use_rope:
        vec = pl.BlockSpec((1, HEAD_DIM), lambda i: (0, 0))
        pos, freq, sign = rope
        in_specs += [pl.BlockSpec((None, tm // LANES, LANES), lambda i: (i, 0, 0)), vec, vec]
        args += [pos.reshape(n // tm, tm // LANES, LANES), freq, sign]
    return pl.pallas_call(
        functools.partial(_inproj_kernel, modes=tuple(modes), out_tiles=tuple(out_tiles),
                          gain_of=gain_of, n_gains=len(gains), post_scale=post_scale,
                          use_rope=use_rope),
        grid=(n // tm,),
        in_specs=in_specs,
        out_specs=pl.BlockSpec((tm, n_out * tn), lambda i: (i, 0)),
        out_shape=jax.ShapeDtypeStruct((n, n_out * tn), BF16),
        scratch_shapes=[pltpu.VMEM((tm, d), BF16),
                        pltpu.VMEM((d, ncols), BF16),
                        pltpu.VMEM((2, d, tn), F32),
                        pltpu.SemaphoreType.DMA((2,)),
                        pltpu.VMEM((tm, tn), F32)],
        compiler_params=pltpu.CompilerParams(
            dimension_semantics=("arbitrary",), vmem_limit_bytes=VMEM_LIMIT),
        name="in_projection",
    )(*args)


def _attn_kernel(q_ref, k_ref, v_ref, o_ref, u_scr, *, seq):
    hb, tq, win, nsub = ATT_HEADS_PER_STEP, ATT_Q, ATT_WIN, ATT_SUB
    chains = nsub * hb

    u_scr[...] = jnp.where(lax.broadcasted_iota(jnp.int32, (ATT_NEW, ATT_NEW), 0)
                           > lax.broadcasted_iota(jnp.int32, (ATT_NEW, ATT_NEW), 1),
                           1.0, 0.0).astype(BF16)
    old = win - ATT_NEW
    key_idx = lax.broadcasted_iota(jnp.int32, (tq, win), 1)
    row_idx = lax.broadcasted_iota(jnp.int32, (tq, win), 0)
    tail = win - LANES
    tail_ok = (lax.broadcasted_iota(jnp.int32, (tq, LANES), 1) + tail
               < lax.broadcasted_iota(jnp.int32, (tq, LANES), 0) + (win - tq))

    def head_cols(h):
        return slice(h * HEAD_DIM, (h + 1) * HEAD_DIM)

    def tile_logits(rows0, keys0, which):
        return [_dot_nt(q_ref[pl.ds(rows0[c // hb], tq), head_cols(c % hb)],
                        k_ref[pl.ds(keys0[c // hb], win), head_cols(c % hb)]) for c in which]

    def tile_weights(logits, rows0, keys0, limits, totals, which):
        log_beta, terms = [], []
        for z, c in zip(logits, which):
            a = c // hb
            if limits is None:
                z = jnp.concatenate(
                    [z[:, :tail], jnp.where(tail_ok, z[:, tail:], ATT_EXCLUDED)], axis=1)
            else:
                ok = key_idx < (jnp.minimum(rows0[a] + row_idx, limits[a]) - keys0[a])
                z = jnp.where(ok, z, ATT_EXCLUDED)
            nz = -z
            lsm = (jnp.minimum(nz, 0.0)
                   - jnp.log(1.0 + jnp.exp2(jnp.minimum(z, nz))) * LOG2E)
            log_beta.append(z + lsm)
            terms.append(lsm)
        sums_new = _dot(jnp.concatenate([t[:, old:].astype(BF16) for t in terms], axis=0),
                        u_scr[...])
        sums_old = _dot(jnp.concatenate([t[:, :old].astype(BF16) for t in terms], axis=0),
                        u_scr[0:old, 0:old])
        weights, tile_totals = [], []
        for n in range(len(which)):
            newer = sums_new[n * tq:(n + 1) * tq]
            older = sums_old[n * tq:(n + 1) * tq]
            total_new = newer[:, 0:1] + terms[n][:, old:old + 1]
            expo = log_beta[n] + jnp.concatenate([older + total_new, newer], axis=1)
            if totals is not None:
                expo = expo + totals[n]
            weights.append(jnp.exp2(expo).astype(BF16))
            tile_totals.append(total_new + older[:, 0:1] + terms[n][:, 0:1])
        return weights, tile_totals

    def tile_values(weights, keys0, which):
        return [_dot(w, v_ref[pl.ds(keys0[c // hb], win), head_cols(c % hb)])
                for w, c in zip(weights, which)]

    def slowest(totals):
        m = totals[0]
        for t in totals[1:]:
            m = jnp.maximum(m, t)
        return jnp.max(m)

    def qblock(i, carry, *, near_start):
        r0 = i * (tq * nsub)
        rows0 = [pl.multiple_of(r0 + a * tq, tq) for a in range(nsub)]
        if near_start:
            starts = [jnp.maximum(r - (win - tq), 0) for r in rows0]
        else:
            starts = [r - (win - tq) for r in rows0]
        keys0 = [pl.multiple_of(s, tq) for s in starts]
        everyone = list(range(chains))
        weights, totals = tile_weights(tile_logits(rows0, keys0, everyone), rows0, keys0,
                                       [ATT_NO_LIMIT] * nsub if near_start else None, None,
                                       everyone)
        accs = tile_values(weights, keys0, everyone)

        def more(st):
            m, worst, _, _ = st
            return (starts[-1] - m * win > 0) & (worst > ATT_SKIP_BELOW)

        def older(st):
            m, _, totals, accs = st
            limits = [s - m * win for s in starts]
            keys0 = [pl.multiple_of(jnp.maximum(lim - win, 0), tq) for lim in limits]
            new_totals, new_accs = [], []
            needs = [jnp.max(t) > ATT_SKIP_BELOW for t in totals]
            for c in range(chains):
                def add_tile(c=c):
                    weights, tile_totals = tile_weights(tile_logits(rows0, keys0, [c]), rows0,
                                                        keys0, limits, [totals[c]], [c])
                    out, = tile_values(weights, keys0, [c])
                    return totals[c] + tile_totals[0], accs[c] + out

                total, acc = lax.cond(needs[c], add_tile, lambda c=c: (totals[c], accs[c]))
                new_totals.append(total)
                new_accs.append(acc)
            return (m + 1, slowest(new_totals), tuple(new_totals), tuple(new_accs))

        _, _, _, accs = lax.while_loop(
            more, older, (jnp.int32(0), slowest(totals), tuple(totals), tuple(accs)))
        for a in range(nsub):
            for h in range(hb):
                o_ref[pl.ds(rows0[a], tq), head_cols(h)] = accs[a * hb + h].astype(BF16)
        return carry

    n_blocks = seq // (tq * nsub)
    n_clipped = min(-(-(win - tq) // (tq * nsub)), n_blocks)
    lax.fori_loop(0, n_clipped, functools.partial(qblock, near_start=True), 0)
    lax.fori_loop(n_clipped, n_blocks, functools.partial(qblock, near_start=False), 0)


def _stick_breaking(p, batch, seq, q_col, k_col, v_col):
    n = p.shape[0]
    width = ATT_HEADS_PER_STEP * HEAD_DIM
    n_hg = N_HEADS // ATT_HEADS_PER_STEP

    def spec(col):
        return pl.BlockSpec((seq, width), lambda b, hg: (b, col // width + hg))

    return pl.pallas_call(
        functools.partial(_attn_kernel, seq=seq),
        grid=(batch, n_hg),
        in_specs=[spec(q_col), spec(k_col), spec(v_col)],
        out_specs=pl.BlockSpec((seq, width), lambda b, hg: (b, hg)),
        out_shape=jax.ShapeDtypeStruct((n, N_HEADS * HEAD_DIM), BF16),
        scratch_shapes=[pltpu.VMEM((ATT_NEW, ATT_NEW), BF16)],
        compiler_params=pltpu.CompilerParams(
            dimension_semantics=("parallel", "parallel"), vmem_limit_bytes=VMEM_LIMIT),
        name="stick_breaking",
    )(p, p, p)


def _out_even_kernel(x_ref, gate_ref, ug_ref, vg_ref, zb_ref, bo_ref, ws_ref, bias_ref, wout_ref,
                     o_ref, y_scr):
    tm = x_ref.shape[0]
    a_width = bo_ref.shape[1]
    row = lax.broadcasted_iota(jnp.int32, (BLOCK, BLOCK), 0)
    col = lax.broadcasted_iota(jnp.int32, (BLOCK, BLOCK), 1)
    for g in range(a_width // HEAD_DIM):
        cols = slice(g * HEAD_DIM, (g + 1) * HEAD_DIM)
        wg = jnp.where(col <= row, ws_ref[g], 0.0).astype(BF16)
        for c in range(tm // BLOCK):
            rows = slice(c * BLOCK, (c + 1) * BLOCK)
            mixed = _dot(wg, vg_ref[rows, cols]) + bias_ref[:, cols]
            y_scr[rows, cols] = (ug_ref[rows, cols].astype(F32) * mixed).astype(BF16)
    y_scr[:, a_width:] = (bo_ref[...].astype(F32) * _silu(zb_ref[...].astype(F32))).astype(BF16)
    o_ref[...] = x_ref[...] + gate_ref[...] * _dot(y_scr[...], wout_ref[...])


def _out_even(x2, seq, gate, p, b_out, a_ws, bias_full, w_out_bf):
    n, d = x2.shape
    tm = OUT_EVEN_ROWS
    aw = b_out.shape[1]
    row = lambda c: pl.BlockSpec((tm, aw), lambda i: (i, c))
    const = lambda a: pl.BlockSpec(a.shape, lambda i: (0,) * a.ndim, pipeline_mode=pl.Buffered(1))
    return pl.pallas_call(
        _out_even_kernel,
        grid=(n // tm,),
        in_specs=[pl.BlockSpec((tm, d), lambda i: (i, 0)),
                  pl.BlockSpec((None, 1, d), lambda i: ((i * tm) // seq, 0, 0)),
                  row(4), row(0), row(5),
                  row(0),
                  const(a_ws), const(bias_full), const(w_out_bf)],
        out_specs=pl.BlockSpec((tm, d), lambda i: (i, 0)),
        out_shape=jax.ShapeDtypeStruct((n, d), F32),
        scratch_shapes=[pltpu.VMEM((tm, 2 * aw), BF16)],
        compiler_params=pltpu.CompilerParams(
            dimension_semantics=("parallel",), vmem_limit_bytes=VMEM_LIMIT),
        name="out_even",
    )(x2, gate, p, p, p, b_out, a_ws, bias_full, w_out_bf)


def _rope_operands(positions):
    half = HEAD_DIM // 2
    inv_freq = ROPE_BASE ** (-jnp.arange(half, dtype=F32) / half)
    freq = jnp.concatenate([inv_freq, inv_freq]).reshape(1, HEAD_DIM)
    sign = jnp.concatenate([-jnp.ones((half,), F32), jnp.ones((half,), F32)]).reshape(1, HEAD_DIM)
    return positions.reshape(-1, LANES).astype(F32), freq, sign


def _retention_tables():
    chunk = RET_CHUNK
    f32 = np.float32
    log_gamma = np.log1p(-np.exp2(-5.0 - np.arange(N_HEADS, dtype=f32))).astype(f32)
    idx = np.arange(chunk, dtype=f32)
    diff = idx[:, None] - idx[None, :]
    intra = np.where(diff >= 0, np.exp(log_gamma[:, None, None] * np.maximum(diff, 0.0)), 0.0)
    q_decay = np.exp(log_gamma[:, None] * (idx + 1.0))
    k_decay = np.exp(log_gamma[:, None] * (chunk - 1.0 - idx))
    chunk_decay = np.exp(log_gamma * chunk)
    tables = (intra,
              np.broadcast_to(q_decay[:, :, None], (N_HEADS, chunk, HEAD_DIM)),
              np.broadcast_to(k_decay[:, :, None], (N_HEADS, chunk, HEAD_DIM)),
              np.broadcast_to(chunk_decay[:, None, None], (N_HEADS, HEAD_DIM, HEAD_DIM)))
    return tuple(jnp.asarray(np.ascontiguousarray(t, dtype=f32)) for t in tables)


def _retention_rows(q_ref, k_ref, v_ref, idec_ref, qdec_ref, kdec_ref, cdec_ref, st_scr, emit,
                    interleave):
    chunk = RET_CHUNK
    n_chunks = q_ref.shape[0] // chunk
    cols = [slice(h * HEAD_DIM, (h + 1) * HEAD_DIM) for h in range(N_HEADS)]
    rows = [slice(c * chunk, (c + 1) * chunk) for c in range(n_chunks)]

    for h0 in range(0, N_HEADS, RET_HEAD_GROUP):
        heads = range(h0, h0 + RET_HEAD_GROUP)
        kv = {}
        for c in range(n_chunks):
            for h in heads:
                k_decayed = (k_ref[rows[c], cols[h]].astype(F32) * kdec_ref[h]).astype(BF16)
                kv[h, c] = _dot_tn(k_decayed, v_ref[rows[c], cols[h]])
        state = {}
        for h in heads:
            s = st_scr[h]
            for c in range(n_chunks):
                state[h, c] = s.astype(BF16)
                s = s * cdec_ref[h] + kv[h, c]
            st_scr[h] = s
        for c in range(n_chunks):
            for h in heads:
                q = q_ref[rows[c], cols[h]]
                scores = _dot_nt(q, k_ref[rows[c], cols[h]]) * idec_ref[h]
                out = (_dot(scores.astype(BF16), v_ref[rows[c], cols[h]])
                       + _dot(q, state[h, c]) * qdec_ref[h])
                ms = jnp.mean(out * out, axis=-1, keepdims=True)
                emit(rows[c], h, out * lax.rsqrt(ms + EPS))
        next(interleave, None)


def _pooled_rows(pc_ref, halo_ref, cw_ref, cs_ref, ext_scr, seq_row0, emit):
    tm = pc_ref.shape[0]
    ext_scr[0:POOL_HALO, :] = jnp.where(seq_row0 == 0, jnp.zeros_like(halo_ref), halo_ref[...])
    ext_scr[POOL_HALO:, :] = pc_ref[...]
    src_rows = BLOCK + POOL_HALO
    lag = (lax.broadcasted_iota(jnp.int32, (BLOCK, src_rows), 0) + POOL_HALO
           - lax.broadcasted_iota(jnp.int32, (BLOCK, src_rows), 1))
    t_seq = seq_row0 + lax.broadcasted_iota(jnp.int32, (BLOCK, POOL_GROUP_DIM), 0)
    for gi, win in enumerate(POOL_WINDOWS):
        cols = slice(gi * POOL_GROUP_DIM, (gi + 1) * POOL_GROUP_DIM)
        band = jnp.where((lag >= 0) & (lag < win), 1.0 / win, 0.0).astype(BF16)
        short = jnp.where(t_seq + 1 < win, win / (t_seq + 1).astype(F32), 1.0)
        pooled = []
        for c in range(tm // BLOCK):
            mean = _dot(band, ext_scr[c * BLOCK:c * BLOCK + src_rows, cols])
            if c == 0:
                mean = mean * short
            cur = pc_ref[c * BLOCK:(c + 1) * BLOCK, cols].astype(F32)
            pooled.append((mean - cur).astype(BF16))
        emit(gi, _dot(jnp.concatenate(pooled, axis=0), cw_ref[gi]) * cs_ref[:, cols])
        yield


def _odd_mixers_kernel(p_ref, idec_ref, qdec_ref, kdec_ref, cdec_ref,
                       x_ref, gate_ref, halo_ref, cw_ref, cs_ref, wout_ref,
                       o_ref, st_scr, ext_scr, y_scr):
    tm = x_ref.shape[0]
    c_width = halo_ref.shape[1]
    pc_ref, q_ref, k_ref, v_ref, za_ref, zb_ref = (
        p_ref.at[:, pl.ds(j * c_width, c_width)] for j in range(6))

    @pl.when(pl.program_id(1) == 0)
    def _():
        st_scr[...] = jnp.zeros_like(st_scr)

    def gated_retention(rows, h, out):
        cols = slice(h * HEAD_DIM, (h + 1) * HEAD_DIM)
        ycols = slice(c_width + h * HEAD_DIM, c_width + (h + 1) * HEAD_DIM)
        y_scr[rows, ycols] = (out * _silu(zb_ref[rows, cols].astype(F32))).astype(BF16)

    def gated_pool(gi, mixed):
        cols = slice(gi * POOL_GROUP_DIM, (gi + 1) * POOL_GROUP_DIM)
        y_scr[:, cols] = (mixed * _silu(za_ref[:, cols].astype(F32))).astype(BF16)

    pooling = _pooled_rows(pc_ref, halo_ref, cw_ref, cs_ref, ext_scr, pl.program_id(1) * tm,
                           gated_pool)
    _retention_rows(q_ref, k_ref, v_ref, idec_ref, qdec_ref, kdec_ref, cdec_ref, st_scr,
                    gated_retention, pooling)
    for _ in pooling:
        pass
    o_ref[...] = x_ref[...] + gate_ref[...] * _dot(y_scr[...], wout_ref[...])


def _odd_mixers(x2, batch, seq, gate, p, c_w_bf, c_scale, w_out_bf):
    n, d = x2.shape
    tm = ROW_TILE
    cw = N_HEADS * HEAD_DIM
    steps = seq // tm
    tables = _retention_tables()
    tabs = [pl.BlockSpec(t.shape, lambda b, s: (0, 0, 0)) for t in tables]
    halo_blocks = tm // POOL_HALO
    halo = pl.BlockSpec(
        (POOL_HALO, cw), lambda b, s: (jnp.maximum((b * steps + s) * halo_blocks - 1, 0), 0))
    return pl.pallas_call(
        _odd_mixers_kernel,
        grid=(batch, steps),
        in_specs=[pl.BlockSpec((tm, p.shape[1]), lambda b, s: (b * steps + s, 0))] + tabs + [
            pl.BlockSpec((tm, d), lambda b, s: (b * steps + s, 0)),
            pl.BlockSpec((None, 1, d), lambda b, s: (b, 0, 0)),
            halo,
            pl.BlockSpec(c_w_bf.shape, lambda b, s: (0, 0, 0)),
            pl.BlockSpec((1, cw), lambda b, s: (0, 0)),
            pl.BlockSpec(w_out_bf.shape, lambda b, s: (0, 0))],
        out_specs=pl.BlockSpec((tm, d), lambda b, s: (b * steps + s, 0)),
        out_shape=jax.ShapeDtypeStruct((n, d), F32),
        scratch_shapes=[pltpu.VMEM((N_HEADS, HEAD_DIM, HEAD_DIM), F32),
                        pltpu.VMEM((POOL_HALO + tm, cw), BF16),
                        pltpu.VMEM((tm, 2 * cw), BF16)],
        compiler_params=pltpu.CompilerParams(
            dimension_semantics=("parallel", "arbitrary"), vmem_limit_bytes=VMEM_LIMIT),
        name="odd_mixers",
    )(p, *tables, x2, gate, p, c_w_bf, c_scale.reshape(1, cw), w_out_bf)


def _even_layer(x2, batch, seq, mod, norm_g, w_in, a_vnorm_g, a_ws, a_bs,
                b_qnorm_g, b_knorm_g, w_out):
    d = x2.shape[1]
    shift, scale, gate = _split_modulation(mod)
    gains = [a_vnorm_g, b_qnorm_g.reshape(1, -1), b_knorm_g.reshape(1, -1)]
    modes = ("hold", "norm", "norm_scaled", "norm", "raw", "silu_gate", "raw")
    p = _in_projection(x2, seq, norm_g, scale, shift, w_in, gains, modes,
                       post_scale=LOG2E * HEAD_DIM ** -0.5)
    b_out = _stick_breaking(p, batch, seq, q_col=d, k_col=2 * d, v_col=3 * d)
    bias_full = jnp.repeat(a_bs.T, HEAD_DIM, axis=1)
    return _out_even(x2, seq, gate, p, b_out, a_ws, bias_full, w_out.astype(BF16))


def _odd_layer(x2, batch, seq, mod, positions, norm_g, w_in, c_w, c_scale,
               d_qnorm_g, d_knorm_g, w_out):
    shift, scale, gate = _split_modulation(mod)
    gains = [d_qnorm_g.reshape(1, -1), d_knorm_g.reshape(1, -1)]
    modes = ("raw", "rope", "rope_scaled", "raw", "raw", "raw")
    p = _in_projection(x2, seq, norm_g, scale, shift, w_in, gains, modes,
                       rope=_rope_operands(positions), post_scale=HEAD_DIM ** -0.5)
    return _odd_mixers(x2, batch, seq, gate, p, c_w.astype(BF16), c_scale, w_out.astype(BF16))


def kernel(x, c, positions, even_norm_g, even_w_mod, even_b_mod, even_w_in, even_a_vnorm_g, even_a_ws, even_a_bs, even_b_qnorm_g, even_b_knorm_g, even_w_out, odd_norm_g, odd_w_mod, odd_b_mod, odd_w_in, odd_c_w, odd_c_scale, odd_d_qnorm_g, odd_d_knorm_g, odd_w_out):
    batch, seq, d = x.shape
    depth = even_norm_g.shape[0] + odd_norm_g.shape[0]
    x2 = x.reshape(batch * seq, d)
    n_even = even_norm_g.shape[0]
    mods = _modulations(c, even_w_mod, even_b_mod, odd_w_mod, odd_b_mod)
    for layer in range(depth):
        i = layer // 2
        if layer % 2 == 0:
            x2 = _even_layer(x2, batch, seq, mods[i], even_norm_g[i],
                             even_w_in[i], even_a_vnorm_g[i], even_a_ws[i], even_a_bs[i],
                             even_b_qnorm_g[i], even_b_knorm_g[i], even_w_out[i])
        else:
            x2 = _odd_layer(x2, batch, seq, mods[n_even + i], positions, odd_norm_g[i],
                            odd_w_in[i], odd_c_w[i], odd_c_scale[i], odd_d_qnorm_g[i],
                            odd_d_knorm_g[i], odd_w_out[i])
    return x2.reshape(batch, seq, d)
```
